```python
import math
import jax
import jax.numpy as jnp
from jax import lax
import numpy as np

D_MODEL = 2048
BATCH = 1
SEQ = 8192
DEPTH = 2
DEC_BATCH = 128
DEC_SEQ = 4
PAST_LEN = 8192
PAGE_SIZE = 128

N_EVEN = (DEPTH + 1) // 2
N_ODD = DEPTH // 2
MIX_WIDTH = D_MODEL
EPS = 1e-6
NEG_BIG = -1e30

A_WIDTH = MIX_WIDTH // 2
CHUNK = 128
A_GROUPS = 8
A_DH = A_WIDTH // A_GROUPS

B_WIDTH = MIX_WIDTH - A_WIDTH
B_DH = 64
B_HEADS = B_WIDTH // B_DH
B_KV_HEADS = 4
B_GQA = B_HEADS // B_KV_HEADS
WINDOW = 128
BLOCK = 128

C_WIDTH = MIX_WIDTH // 2
C_HEADS = 8
C_DV = C_WIDTH // C_HEADS
C_DK = C_DV // 2
C_CHUNK = 128
RET_DECAY_EXP0 = 5.0
ROPE_BASE = 10000.0

D_WIDTH = MIX_WIDTH - C_WIDTH
D_GROUP_CH = 16
D_GROUPS = D_WIDTH // D_GROUP_CH
D_STATE = 64

PEER_HEADS = 8
PEER_NKEYS = 128
PEER_EXPERTS = PEER_NKEYS * PEER_NKEYS
PEER_DKEY = 256
PEER_TOPK = 16
PEER_BLOCK = 128

EVEN_IN = 2 * A_WIDTH + B_HEADS * B_DH + 2 * B_KV_HEADS * B_DH
EVEN_SPLITS = [A_WIDTH, 2 * A_WIDTH, 2 * A_WIDTH + B_HEADS * B_DH,
               2 * A_WIDTH + B_HEADS * B_DH + B_KV_HEADS * B_DH]
ODD_IN = 2 * C_HEADS * C_DK + 2 * C_WIDTH + D_WIDTH
ODD_SPLITS = [C_HEADS * C_DK, 2 * C_HEADS * C_DK, 2 * C_HEADS * C_DK + C_WIDTH,
              2 * C_HEADS * C_DK + 2 * C_WIDTH]

kernel_name = 'hybrid_gmlp_swa_retnet_s5_peer_step'


def rmsnorm(x, g):
    xf = x.astype(jnp.float32)
    y = xf * lax.rsqrt(jnp.mean(xf * xf, axis=-1, keepdims=True) + EPS)
    return (y * g.astype(jnp.float32)).astype(x.dtype)


def gelu(x):
    return jax.nn.gelu(x, approximate=False)


def gmlp_prompt(u, v, ws, bs):
    b, s = u.shape[:2]
    nc = s // CHUNK
    vc = v.reshape(b, nc, CHUNK, A_GROUPS, A_DH)
    mixed = jnp.einsum('gij,bcjgd->bcigd', jnp.tril(ws), vc) + bs.T[:, :, None]
    return u * mixed.reshape(b, s, A_GROUPS, A_DH)


def gmlp_sample(u, v, ws, bs):
    t = v.shape[1]
    mixed = jnp.einsum('gij,bjgd->bigd', jnp.tril(ws[:, :t, :t]), v) + bs[:, :t].T[:, :, None]
    return u * mixed


def window_mask(qpos, kpos):
    d = qpos[..., :, None] - kpos[..., None, :]
    return (d >= 0) & (d < WINDOW) & (kpos[..., None, :] >= 0)


def sink_attention(q, k, v, allowed, sink):
    f32 = jnp.float32
    s = jnp.einsum('...qhgd,...mhd->...hgqm', q.astype(f32), k.astype(f32)) * (B_DH ** -0.5)
    s = jnp.where(allowed, s, NEG_BIG)
    sk = sink.astype(f32).reshape(B_KV_HEADS, B_GQA, 1, 1)
    mx = jnp.maximum(jnp.max(s, axis=-1, keepdims=True), sk)
    p = jnp.exp(s - mx)
    p = p / (jnp.sum(p, axis=-1, keepdims=True) + jnp.exp(sk - mx))
    o = jnp.einsum('...hgqm,...mhd->...qhgd', p, v.astype(f32))
    return o.astype(q.dtype)


def window_attn_prompt(q, k, v, sink):
    b, s = q.shape[:2]
    nb = s // BLOCK
    qb = q.reshape(b, nb, BLOCK, B_KV_HEADS, B_GQA, B_DH)
    kb = k.reshape(b, nb, BLOCK, B_KV_HEADS, B_DH)
    vb = v.reshape(b, nb, BLOCK, B_KV_HEADS, B_DH)
    pad = ((0, 0), (1, 0), (0, 0), (0, 0), (0, 0))
    kk = jnp.concatenate([jnp.pad(kb[:, :-1], pad), kb], axis=2)
    vv = jnp.concatenate([jnp.pad(vb[:, :-1], pad), vb], axis=2)
    blk = jnp.arange(nb)[:, None] * BLOCK
    qpos = blk + jnp.arange(BLOCK)[None, :]
    kpos = blk - BLOCK + jnp.arange(2 * BLOCK)[None, :]
    allowed = window_mask(qpos, kpos)
    o = sink_attention(qb, kk, vv, allowed[None, :, None, None], sink)
    return o.reshape(b, s, B_HEADS * B_DH)


def window_attn_sample(q, k, v, kbuf, vbuf, sink):
    db, t = q.shape[:2]
    wb = kbuf.shape[1]
    kk = jnp.concatenate([kbuf.astype(k.dtype), k], axis=1)
    vv = jnp.concatenate([vbuf.astype(v.dtype), v], axis=1)
    qpos = PAST_LEN + jnp.arange(t)
    kpos = jnp.concatenate([PAST_LEN - wb + jnp.arange(wb), PAST_LEN + jnp.arange(t)])
    allowed = window_mask(qpos, kpos)
    qh = q.reshape(db, t, B_KV_HEADS, B_GQA, B_DH)
    o = sink_attention(qh, kk, vv, allowed[None, None, None], sink)
    return o.reshape(db, t, B_HEADS * B_DH), kk[:, -wb:], vv[:, -wb:]


def even_mixer(hn, w_in, w_out, ws, bs, sink, kbuf, vbuf):
    b, t, _ = hn.shape
    z = hn @ w_in
    au, av, q, k, v = jnp.split(z, EVEN_SPLITS, axis=-1)
    au = gelu(au).reshape(b, t, A_GROUPS, A_DH)
    av = gelu(av).reshape(b, t, A_GROUPS, A_DH)
    k = k.reshape(b, t, B_KV_HEADS, B_DH)
    v = v.reshape(b, t, B_KV_HEADS, B_DH)
    if kbuf is None:
        ya = gmlp_prompt(au, av, ws, bs)
        yb = window_attn_prompt(q, k, v, sink)
        wb = min(WINDOW, t)
        kb, vb = k[:, -wb:], v[:, -wb:]
    else:
        ya = gmlp_sample(au, av, ws, bs)
        yb, kb, vb = window_attn_sample(q, k, v, kbuf, vbuf, sink)
    y = jnp.concatenate([ya.reshape(b, t, A_WIDTH), yb.astype(ya.dtype)], axis=-1) @ w_out
    return y, av, kb, vb


def rope(x, pos):
    half = x.shape[-1] // 2
    freqs = ROPE_BASE ** (-jnp.arange(half, dtype=jnp.float32) / half)
    ang = pos.astype(jnp.float32)[:, None] * freqs[None, :]
    cos = jnp.cos(ang)[:, None, :]
    sin = jnp.sin(ang)[:, None, :]
    x1, x2 = x[..., :half], x[..., half:]
    return jnp.concatenate([x1 * cos - x2 * sin, x1 * sin + x2 * cos], axis=-1)


def retention_log_decay():
    return jnp.log1p(-jnp.exp2(-RET_DECAY_EXP0 - jnp.arange(C_HEADS, dtype=jnp.float32)))


def retention_chunk(q, k, v, state, log_gamma):
    L = q.shape[1]
    i = jnp.arange(L, dtype=jnp.float32)
    rel = i[:, None] - i[None, :]
    decay = jnp.where(rel >= 0, jnp.exp(jnp.maximum(rel, 0.0)[None] * log_gamma[:, None, None]), 0.0)
    scores = jnp.einsum('bihd,bjhd->bhij', q, k) * decay
    o = jnp.einsum('bhij,bjhe->bihe', scores, v)
    q_dec = jnp.exp((i[:, None] + 1.0) * log_gamma[None, :])
    o = o + jnp.einsum('bihd,bhde->bihe', q, state) * q_dec[..., None]
    k_dec = jnp.exp((L - 1.0 - i)[:, None] * log_gamma[None, :])
    new_state = (jnp.exp(L * log_gamma)[:, None, None] * state
                 + jnp.einsum('bjhd,bjhe->bhde', k * k_dec[..., None], v))
    return o, new_state


def retention_prompt(q, k, v, log_gamma):
    b, s = q.shape[:2]
    nc = s // C_CHUNK

    def to_chunks(a):
        return jnp.moveaxis(a.reshape(b, nc, C_CHUNK, *a.shape[2:]), 1, 0)

    def step(st, xs):
        o, st = retention_chunk(xs[0], xs[1], xs[2], st, log_gamma)
        return st, o

    st0 = jnp.zeros((b, C_HEADS, C_DK, C_DV), jnp.float32)
    st, o = lax.scan(step, st0, (to_chunks(q), to_chunks(k), to_chunks(v)))
    return jnp.moveaxis(o, 0, 1).reshape(b, s, C_HEADS, C_DV), st


def head_groupnorm(o, g):
    mu = jnp.mean(o, axis=-1, keepdims=True)
    var = jnp.mean(jnp.square(o - mu), axis=-1, keepdims=True)
    y = (o - mu) * lax.rsqrt(var + EPS)
    return y.reshape(*o.shape[:2], C_WIDTH) * g.astype(jnp.float32)


def s5_discretize(a_re, a_im, log_dt, b_re, b_im):
    f32 = jnp.float32
    lam = lax.complex(a_re.astype(f32), a_im.astype(f32))
    dt = jnp.exp(log_dt.astype(f32))[:, None]
    lam_dt = lam * dt
    a_bar = jnp.exp(lam_dt)
    b = lax.complex(b_re.astype(f32), b_im.astype(f32))
    b_bar = ((a_bar - 1.0) / lam)[..., None] * b
    return lam_dt, a_bar, b_bar


def s5_scan(u, lam_dt, a_bar, b_bar, c, dd, x0):
    uf = u.astype(jnp.float32)
    bu = jnp.einsum('gpc,btgc->btgp', b_bar, uf.astype(jnp.complex64))
    a = jnp.broadcast_to(a_bar, bu.shape)

    def comb(left, right):
        return right[0] * left[0], right[0] * left[1] + right[1]

    _, h = lax.associative_scan(comb, (a, bu), axis=1)
    if x0 is not None:
        t = bu.shape[1]
        pw = jnp.exp(lam_dt[None] * jnp.arange(1, t + 1, dtype=jnp.float32)[:, None, None])
        h = h + pw[None] * x0[:, None]
    y = jnp.real(jnp.einsum('gcp,btgp->btgc', c, h)) + dd.astype(jnp.float32) * uf
    return y, h[:, -1]


def odd_mixer(hn, pos, w_in, w_out, norm_g, a_re, a_im, log_dt, b_re, b_im, c_re, c_im,
              dd, glu_w, glu_b, c_state, d_state):
    f32 = jnp.float32
    b, t, _ = hn.shape
    z = hn @ w_in
    q, k, v, g, u = jnp.split(z, ODD_SPLITS, axis=-1)
    q = rope(q.reshape(b, t, C_HEADS, C_DK).astype(f32), pos) * (C_DK ** -0.5)
    k = rope(k.reshape(b, t, C_HEADS, C_DK).astype(f32), pos)
    v = v.reshape(b, t, C_HEADS, C_DV).astype(f32)
    log_gamma = retention_log_decay()
    if c_state is None:
        o, s_c = retention_prompt(q, k, v, log_gamma)
    else:
        o, s_c = retention_chunk(q, k, v, c_state.astype(f32), log_gamma)
    yc = head_groupnorm(o, norm_g) * jax.nn.silu(g.astype(f32))
    lam_dt, a_bar, b_bar = s5_discretize(a_re, a_im, log_dt, b_re, b_im)
    c = lax.complex(c_re.astype(f32), c_im.astype(f32))
    yd, s_d = s5_scan(u.reshape(b, t, D_GROUPS, D_GROUP_CH), lam_dt, a_bar, b_bar, c, dd, d_state)
    yd = gelu(yd.reshape(b, t, D_WIDTH))
    yd = yd * jax.nn.sigmoid(yd @ glu_w.astype(f32) + glu_b.astype(f32))
    y = jnp.concatenate([yc, yd], axis=-1).astype(hn.dtype) @ w_out
    return y, s_c, s_d


def peer(hn, wq, k1, k2, u_tab, v_tab):
    b, t, d = hn.shape
    n = b * t
    pad = (-n) % PEER_BLOCK
    blocks = jnp.pad(hn.reshape(n, d), ((0, pad), (0, 0))).reshape(-1, PEER_BLOCK, d)
    f32 = jnp.float32

    def one_block(xb):
        qh = (xb @ wq).reshape(PEER_BLOCK, PEER_HEADS, 2, PEER_DKEY // 2).astype(f32)
        s1 = jnp.einsum('thd,hnd->thn', qh[:, :, 0], k1.astype(f32))
        s2 = jnp.einsum('thd,hnd->thn', qh[:, :, 1], k2.astype(f32))
        v1, i1 = lax.top_k(s1, PEER_TOPK)
        v2, i2 = lax.top_k(s2, PEER_TOPK)
        cand = (v1[..., :, None] + v2[..., None, :]).reshape(PEER_BLOCK, PEER_HEADS, PEER_TOPK * PEER_TOPK)
        cidx = (i1[..., :, None] * PEER_NKEYS + i2[..., None, :]).reshape(PEER_BLOCK, PEER_HEADS, PEER_TOPK * PEER_TOPK)
        sc, j = lax.top_k(cand, PEER_TOPK)
        idx = jnp.take_along_axis(cidx, j, axis=-1)
        gate = jax.nn.softmax(sc, axis=-1)
        ue = u_tab[idx]
        ve = v_tab[idx]
        act = gelu(jnp.einsum('thkd,td->thk', ue, xb).astype(f32))
        coef = (gate * act).astype(xb.dtype)
        return jnp.einsum('thk,thkd->td', coef, ve)

    out = lax.map(one_block, blocks)
    return out.reshape(-1, d)[:n].reshape(b, t, d)


def setup_inputs(seed: int = 0) -> dict:
    key = jax.random.key(seed)
    ks = iter(jax.random.split(key, 48))
    f32 = jnp.float32

    def nrm(shape, scale):
        return jax.random.normal(next(ks), shape, f32) * scale

    win_buf = min(WINDOW, PAST_LEN)
    n_idx = jnp.arange(D_STATE, dtype=f32)
    return {
        'x_prompt': nrm((BATCH, SEQ, D_MODEL), 1.0),
        'x_sample': nrm((DEC_BATCH, DEC_SEQ, D_MODEL), 1.0),
        'state_b_k': nrm((N_EVEN, DEC_BATCH, win_buf, B_KV_HEADS, B_DH), 1.0),
        'state_b_v': nrm((N_EVEN, DEC_BATCH, win_buf, B_KV_HEADS, B_DH), 1.0),
        'state_c_s': nrm((N_ODD, DEC_BATCH, C_HEADS, C_DK, C_DV), 1.0),
        'state_d_re': nrm((N_ODD, DEC_BATCH, D_GROUPS, D_STATE), 0.5),
        'state_d_im': nrm((N_ODD, DEC_BATCH, D_GROUPS, D_STATE), 0.5),
        'norm1_g': 1.0 + nrm((DEPTH, D_MODEL), 0.01),
        'norm2_g': 1.0 + nrm((DEPTH, D_MODEL), 0.01),
        'final_g': 1.0 + nrm((D_MODEL,), 0.01),
        'w_in_even': nrm((N_EVEN, D_MODEL, EVEN_IN), D_MODEL ** -0.5),
        'w_out_even': nrm((N_EVEN, MIX_WIDTH, D_MODEL), MIX_WIDTH ** -0.5),
        'a_ws': nrm((N_EVEN, A_GROUPS, CHUNK, CHUNK), CHUNK ** -0.5),
        'a_bs': 1.0 + nrm((N_EVEN, A_GROUPS, CHUNK), 0.1),
        'b_sink': nrm((N_EVEN, B_HEADS), 0.5),
        'w_in_odd': nrm((N_ODD, D_MODEL, ODD_IN), D_MODEL ** -0.5),
        'w_out_odd': nrm((N_ODD, MIX_WIDTH, D_MODEL), MIX_WIDTH ** -0.5),
        'c_norm_g': 1.0 + nrm((N_ODD, C_WIDTH), 0.01),
        'd_a_re': -0.5 + nrm((N_ODD, D_GROUPS, D_STATE), 0.01),
        'd_a_im': jnp.pi * n_idx + nrm((N_ODD, D_GROUPS, D_STATE), 0.01),
        'd_log_dt': jax.random.uniform(next(ks), (N_ODD, D_GROUPS), f32, math.log(1e-3), math.log(1e-1)),
        'd_b_re': nrm((N_ODD, D_GROUPS, D_STATE, D_GROUP_CH), (2.0 * D_GROUP_CH) ** -0.5),
        'd_b_im': nrm((N_ODD, D_GROUPS, D_STATE, D_GROUP_CH), (2.0 * D_GROUP_CH) ** -0.5),
        'd_c_re': nrm((N_ODD, D_GROUPS, D_GROUP_CH, D_STATE), (2.0 * D_STATE) ** -0.5),
        'd_c_im': nrm((N_ODD, D_GROUPS, D_GROUP_CH, D_STATE), (2.0 * D_STATE) ** -0.5),
        'd_d': nrm((N_ODD, D_GROUPS, D_GROUP_CH), 1.0),
        'd_glu_w': nrm((N_ODD, D_WIDTH, D_WIDTH), D_WIDTH ** -0.5),
        'd_glu_b': nrm((N_ODD, D_WIDTH), 0.01),
        'peer_wq': nrm((DEPTH, D_MODEL, PEER_HEADS * PEER_DKEY), D_MODEL ** -0.5),
        'peer_k1': nrm((DEPTH, PEER_HEADS, PEER_NKEYS, PEER_DKEY // 2), (PEER_DKEY // 2) ** -0.5),
        'peer_k2': nrm((DEPTH, PEER_HEADS, PEER_NKEYS, PEER_DKEY // 2), (PEER_DKEY // 2) ** -0.5),
        'peer_u': nrm((DEPTH, PEER_EXPERTS, D_MODEL), D_MODEL ** -0.5),
        'peer_v': nrm((DEPTH, PEER_EXPERTS, D_MODEL), PEER_TOPK ** -0.5),
    }


def reference(x_prompt, x_sample, state_b_k, state_b_v, state_c_s, state_d_re, state_d_im,
              norm1_g, norm2_g, final_g,
              w_in_even, w_out_even, a_ws, a_bs, b_sink,
              w_in_odd, w_out_odd, c_norm_g, d_a_re, d_a_im, d_log_dt,
              d_b_re, d_b_im, d_c_re, d_c_im, d_d, d_glu_w, d_glu_b,
              peer_wq, peer_k1, peer_k2, peer_u, peer_v):
    f32 = jnp.float32

    def trunk(x, pos, bk, bv, cs, dre, dim):
        sample = bk is not None
        h = x
        a_rows, k_bufs, v_bufs, c_states, d_res, d_ims = [], [], [], [], [], []
        for l in range(DEPTH):
            hn = rmsnorm(h, norm1_g[l])
            if l % 2 == 0:
                e = l // 2
                mix, av, kb, vb = even_mixer(hn, w_in_even[e], w_out_even[e], a_ws[e], a_bs[e], b_sink[e],
                                             bk[e] if sample else None, bv[e] if sample else None)
                if sample:
                    a_rows.append(av)
                k_bufs.append(kb)
                v_bufs.append(vb)
            else:
                o = l // 2
                d0 = lax.complex(dre[o].astype(f32), dim[o].astype(f32)) if sample else None
                mix, s_c, s_d = odd_mixer(hn, pos, w_in_odd[o], w_out_odd[o], c_norm_g[o],
                                          d_a_re[o], d_a_im[o], d_log_dt[o], d_b_re[o], d_b_im[o],
                                          d_c_re[o], d_c_im[o], d_d[o], d_glu_w[o], d_glu_b[o],
                                          cs[o] if sample else None, d0)
                c_states.append(s_c)
                d_res.append(jnp.real(s_d))
                d_ims.append(jnp.imag(s_d))
            h = h + mix
            h = h + peer(rmsnorm(h, norm2_g[l]), peer_wq[l], peer_k1[l], peer_k2[l], peer_u[l], peer_v[l])
        a_state = jnp.stack(a_rows) if sample else None
        return (rmsnorm(h, final_g), a_state, jnp.stack(k_bufs), jnp.stack(v_bufs),
                jnp.stack(c_states), jnp.stack(d_res), jnp.stack(d_ims))

    pos_p = jnp.arange(x_prompt.shape[1])
    y_prompt, _, bk_p, bv_p, cs_p, dre_p, dim_p = trunk(x_prompt, pos_p, None, None, None, None, None)
    pos_s = PAST_LEN + jnp.arange(x_sample.shape[1])
    y_sample, av_s, bk_s, bv_s, cs_s, dre_s, dim_s = trunk(x_sample, pos_s, state_b_k, state_b_v,
                                                           state_c_s, state_d_re, state_d_im)
    return (y_prompt, y_sample, av_s, bk_p, bv_p, bk_s, bv_s, cs_p, cs_s, dre_p, dim_p, dre_s, dim_s)
```

```python
import functools
import math

import jax
import jax.numpy as jnp
from jax import lax
from jax.experimental import pallas as pl
from jax.experimental.pallas import tpu as pltpu

F32 = jnp.float32
BF16 = jnp.bfloat16
HIGHEST = lax.Precision.HIGHEST

EPS = 1e-6
NEG_BIG = -1e30
NEG_INF = float("-inf")

D_MODEL = 2048
PAST_LEN = 8192
LANES = 128
SUBLANES = 8
VMEM_LIMIT = 56 * 1024 * 1024

CHUNK = 128
A_GROUPS = 8
A_WIDTH = 1024
B_HEADS = 16
B_KV_HEADS = 4
B_GQA = 4
B_DH = 64
C_HEADS = 8
C_DK = 64
C_DV = 128
RET_DECAY_EXP0 = 5.0
ROPE_BASE = 10000.0
D_GROUPS = 64
D_STATE = 64
D_GROUP_CH = 16
D_ROWS = 8
D_ROW_STATE = 512
D_ROW_CH = 128
PEER_HEADS = 8
PEER_NKEYS = 128
PEER_TOPK = 16
PEER_HALF = 128


def _params(semantics):
    return pltpu.CompilerParams(dimension_semantics=semantics, vmem_limit_bytes=VMEM_LIMIT)


def _gelu(x):
    return 0.5 * x * (1.0 + lax.erf(x * (1.0 / math.sqrt(2.0))))


def _rmsnorm(x, g):
    return x * lax.rsqrt(jnp.mean(x * x, axis=-1, keepdims=True) + EPS) * g


def _dot_nt(a, b, precision=None):
    return lax.dot_general(a, b, (((1,), (1,)), ((), ())), precision=precision,
                           preferred_element_type=F32)


def _dot_tn(a, b, precision=None):
    return lax.dot_general(a, b, (((0,), (0,)), ((), ())), precision=precision,
                           preferred_element_type=F32)


def _mm_body(*refs, has_norm, has_pair, has_resid):
    it = iter(refs)
    x_ref, w_ref = next(it), next(it)
    g_ref = next(it) if has_norm else None
    x2_ref, w2_ref = (next(it), next(it)) if has_pair else (None, None)
    r_ref = next(it) if has_resid else None
    o_ref = next(it)
    x = x_ref[...]
    if has_norm:
        x = _rmsnorm(x, g_ref[...])
    acc = jnp.dot(x.astype(BF16), w_ref[...], preferred_element_type=F32)
    if has_pair:
        acc = acc + jnp.dot(x2_ref[...].astype(BF16), w2_ref[...], preferred_element_type=F32)
    if has_resid:
        acc = acc + r_ref[...]
    o_ref[...] = acc


def _matmul(x, w, *, norm_g=None, x2=None, w2=None, resid=None, tm=512, tn=512):
    m, k = x.shape
    n = w.shape[1]
    assert m % tm == 0 and n % tn == 0
    args = [x, w]
    specs = [pl.BlockSpec((tm, k), lambda i, j: (i, 0)), pl.BlockSpec((k, tn), lambda i, j: (0, j))]
    if norm_g is not None:
        args.append(norm_g.reshape(1, k))
        specs.append(pl.BlockSpec((1, k), lambda i, j: (0, 0)))
    if x2 is not None:
        k2 = x2.shape[1]
        args += [x2, w2]
        specs += [pl.BlockSpec((tm, k2), lambda i, j: (i, 0)), pl.BlockSpec((k2, tn), lambda i, j: (0, j))]
    if resid is not None:
        args.append(resid)
        specs.append(pl.BlockSpec((tm, tn), lambda i, j: (i, j)))
    body = functools.partial(_mm_body, has_norm=norm_g is not None, has_pair=x2 is not None,
                             has_resid=resid is not None)
    return pl.pallas_call(
        body, grid=(m // tm, n // tn), in_specs=specs,
        out_specs=pl.BlockSpec((tm, tn), lambda i, j: (i, j)),
        out_shape=jax.ShapeDtypeStruct((m, n), F32),
        compiler_params=_params(("parallel", "arbitrary")), name="matmul",
    )(*args)


def _norm_body(x_ref, g_ref, o_ref):
    o_ref[...] = _rmsnorm(x_ref[...], g_ref[...])


def _final_norm(x, g, tm=512):
    m, k = x.shape
    return pl.pallas_call(
        _norm_body, grid=(m // tm,),
        in_specs=[pl.BlockSpec((tm, k), lambda i: (i, 0)), pl.BlockSpec((1, k), lambda i: (0, 0))],
        out_specs=pl.BlockSpec((tm, k), lambda i: (i, 0)),
        out_shape=jax.ShapeDtypeStruct((m, k), F32),
        compiler_params=_params(("parallel",)), name="final_norm",
    )(x, g.reshape(1, k))


def _sink_column(sink_ref, kvh, rows_per_head, n_rows):
    grp = lax.broadcasted_iota(jnp.int32, (n_rows, 1), 0) // rows_per_head
    sk = jnp.full((n_rows, 1), sink_ref[kvh * B_GQA + B_GQA - 1], F32)
    for g in range(B_GQA - 2, -1, -1):
        sk = jnp.where(grp == g, sink_ref[kvh * B_GQA + g], sk)
    return sk


def _even_prompt_body(sink_ref, au_ref, av_ref, q_ref, kvc_ref, kvp_ref, ws_ref, bs_ref, o_ref):
    blk = pl.program_id(0)
    au = _gelu(au_ref[...])
    av = _gelu(av_ref[...])
    row = lax.broadcasted_iota(jnp.int32, (CHUNK, CHUNK), 0)
    col = lax.broadcasted_iota(jnp.int32, (CHUNK, CHUNK), 1)
    causal = row >= col
    for g in range(A_GROUPS):
        lanes = slice(g * LANES, (g + 1) * LANES)
        w = jnp.where(causal, ws_ref[g], 0.0).astype(BF16)
        mixed = jnp.dot(w, av[:, lanes].astype(BF16), preferred_element_type=F32) + bs_ref[g]
        o_ref[:, lanes] = au[:, lanes] * mixed

    q = q_ref[...]
    kvc = kvc_ref[...]
    kvp = kvp_ref[...]
    n_rows = B_GQA * CHUNK
    qi = lax.broadcasted_iota(jnp.int32, (n_rows, 2 * CHUNK), 0) % CHUNK
    kc = lax.broadcasted_iota(jnp.int32, (n_rows, 2 * CHUNK), 1)
    dist = qi + CHUNK - kc
    allowed = (dist >= 0) & (dist < CHUNK) & ((kc >= CHUNK) | (blk > 0))
    outs = []
    for kvh in range(B_KV_HEADS):
        ks = slice(kvh * B_DH, (kvh + 1) * B_DH)
        vs = slice(B_KV_HEADS * B_DH + kvh * B_DH, B_KV_HEADS * B_DH + (kvh + 1) * B_DH)
        kk = jnp.concatenate([kvp[:, ks], kvc[:, ks]], axis=0).astype(BF16)
        vv = jnp.concatenate([kvp[:, vs], kvc[:, vs]], axis=0).astype(BF16)
        q4 = jnp.concatenate(
            [q[:, (kvh * B_GQA + g) * B_DH:(kvh * B_GQA + g + 1) * B_DH] for g in range(B_GQA)], axis=0)
        s = _dot_nt(q4.astype(BF16), kk) * (B_DH ** -0.5)
        s = jnp.where(allowed, s, NEG_BIG)
        sk = _sink_column(sink_ref, kvh, CHUNK, n_rows)
        mx = jnp.maximum(jnp.max(s, axis=-1, keepdims=True), sk)
        p = jnp.exp(s - mx)
        p = p / (jnp.sum(p, axis=-1, keepdims=True) + jnp.exp(sk - mx))
        o = jnp.dot(p.astype(BF16), vv, preferred_element_type=F32)
        outs += [o[g * CHUNK:(g + 1) * CHUNK] for g in range(B_GQA)]
    o_ref[:, A_WIDTH:] = jnp.concatenate(outs, axis=1)


def _even_prompt(z, ws, bs_full, sink):
    t = z.shape[0]
    nb = t // CHUNK
    wide = A_WIDTH
    kvw = 2 * B_KV_HEADS * B_DH
    kv_blk = (2 * A_WIDTH + B_HEADS * B_DH) // kvw
    return pl.pallas_call(
        _even_prompt_body, grid=(nb,),
        in_specs=[
            pl.BlockSpec(memory_space=pltpu.SMEM),
            pl.BlockSpec((CHUNK, wide), lambda i: (i, 0)),
            pl.BlockSpec((CHUNK, wide), lambda i: (i, 1)),
            pl.BlockSpec((CHUNK, wide), lambda i: (i, 2)),
            pl.BlockSpec((CHUNK, kvw), lambda i: (i, kv_blk)),
            pl.BlockSpec((CHUNK, kvw), lambda i: (jnp.maximum(i - 1, 0), kv_blk)),
            pl.BlockSpec((A_GROUPS, CHUNK, CHUNK), lambda i: (0, 0, 0)),
            pl.BlockSpec((A_GROUPS, CHUNK, LANES), lambda i: (0, 0, 0)),
        ],
        out_specs=pl.BlockSpec((CHUNK, D_MODEL), lambda i: (i, 0)),
        out_shape=jax.ShapeDtypeStruct((t, D_MODEL), F32),
        compiler_params=_params(("parallel",)), name="even_prompt",
    )(sink, z, z, z, z, z, ws, bs_full)


EVEN_SAMPLE_BATCH = 8
T_PAD = 8


def _even_sample_body(sink_ref, au_ref, av_ref, q_ref, kv_ref, kb_ref, vb_ref, wa_ref, wb_ref,
                      y_ref, avo_ref):
    n_rows = B_GQA * T_PAD
    tq = lax.broadcasted_iota(jnp.int32, (n_rows, CHUNK), 0) % T_PAD
    kc = lax.broadcasted_iota(jnp.int32, (n_rows, CHUNK), 1)
    buf_allowed = kc > tq
    tq1 = lax.broadcasted_iota(jnp.int32, (n_rows, 1), 0) % T_PAD
    n_new = wa_ref.shape[0]
    for b in range(EVEN_SAMPLE_BATCH):
        au = _gelu(au_ref[b])
        av = _gelu(av_ref[b])
        avo_ref[b] = av
        mixed = wb_ref[...]
        for j in range(n_new):
            mixed = mixed + wa_ref[j] * av[j:j + 1, :]
        y_ref[b, :, :A_WIDTH] = au * mixed

        q = q_ref[b]
        kv = kv_ref[b]
        kb = kb_ref[b]
        vb = vb_ref[b]
        outs = []
        for kvh in range(B_KV_HEADS):
            ks = slice(kvh * B_DH, (kvh + 1) * B_DH)
            vs = slice(B_KV_HEADS * B_DH + kvh * B_DH, B_KV_HEADS * B_DH + (kvh + 1) * B_DH)
            q4 = jnp.concatenate(
                [q[:, (kvh * B_GQA + g) * B_DH:(kvh * B_GQA + g + 1) * B_DH] for g in range(B_GQA)], axis=0)
            scale = B_DH ** -0.5
            s_buf = _dot_nt(q4.astype(BF16), kb[:, ks].astype(BF16)) * scale
            s_buf = jnp.where(buf_allowed, s_buf, NEG_BIG)
            q4r = q4.astype(BF16).astype(F32)
            s_new = []
            for j in range(n_new):
                kj = kv[j:j + 1, ks].astype(BF16).astype(F32)
                sj = jnp.sum(q4r * kj, axis=-1, keepdims=True) * scale
                s_new.append(jnp.where(tq1 >= j, sj, NEG_BIG))
            sk = _sink_column(sink_ref, kvh, T_PAD, n_rows)
            mx = jnp.maximum(jnp.max(s_buf, axis=-1, keepdims=True), sk)
            for sj in s_new:
                mx = jnp.maximum(mx, sj)
            p_buf = jnp.exp(s_buf - mx)
            p_new = [jnp.exp(sj - mx) for sj in s_new]
            den = jnp.sum(p_buf, axis=-1, keepdims=True) + jnp.exp(sk - mx)
            for pj in p_new:
                den = den + pj
            inv = 1.0 / den
            o = jnp.dot((p_buf * inv).astype(BF16), vb[:, ks].astype(BF16), preferred_element_type=F32)
            for j in range(n_new):
                vj = kv[j:j + 1, vs].astype(BF16).astype(F32)
                o = o + (p_new[j] * inv).astype(BF16).astype(F32) * vj
            outs += [o[g * T_PAD:(g + 1) * T_PAD] for g in range(B_GQA)]
        y_ref[b, :, A_WIDTH:] = jnp.concatenate(outs, axis=1)


def _even_sample(z3, kbuf, vbuf, wa, wb, sink):
    nb = z3.shape[0]
    bb = EVEN_SAMPLE_BATCH
    kvw = 2 * B_KV_HEADS * B_DH
    kv_blk = (2 * A_WIDTH + B_HEADS * B_DH) // kvw
    win = kbuf.shape[1]
    return pl.pallas_call(
        _even_sample_body, grid=(nb // bb,),
        in_specs=[
            pl.BlockSpec(memory_space=pltpu.SMEM),
            pl.BlockSpec((bb, T_PAD, A_WIDTH), lambda i: (i, 0, 0)),
            pl.BlockSpec((bb, T_PAD, A_WIDTH), lambda i: (i, 0, 1)),
            pl.BlockSpec((bb, T_PAD, A_WIDTH), lambda i: (i, 0, 2)),
            pl.BlockSpec((bb, T_PAD, kvw), lambda i: (i, 0, kv_blk)),
            pl.BlockSpec((bb, win, kvw // 2), lambda i: (i, 0, 0)),
            pl.BlockSpec((bb, win, kvw // 2), lambda i: (i, 0, 0)),
            pl.BlockSpec(wa.shape, lambda i: (0, 0, 0)),
            pl.BlockSpec(wb.shape, lambda i: (0, 0)),
        ],
        out_specs=[
            pl.BlockSpec((bb, T_PAD, D_MODEL), lambda i: (i, 0, 0)),
            pl.BlockSpec((bb, T_PAD, A_WIDTH), lambda i: (i, 0, 0)),
        ],
        out_shape=[
            jax.ShapeDtypeStruct((nb, T_PAD, D_MODEL), F32),
            jax.ShapeDtypeStruct((nb, T_PAD, A_WIDTH), F32),
        ],
        compiler_params=_params(("parallel",)), name="even_sample",
    )(sink, z3, z3, z3, z3, kbuf, vbuf, wa, wb)


def _rope(x, cos_f, sin_s):
    width = x.shape[-1]
    half = C_DK // 2
    lane = lax.broadcasted_iota(jnp.int32, x.shape, 1) % C_DK
    swapped = jnp.where(lane < half, pltpu.roll(x, width - half, 1), pltpu.roll(x, half, 1))
    return x * cos_f + swapped * sin_s


def _groupnorm_gate(o, gain, gate):
    mu = jnp.mean(o, axis=-1, keepdims=True)
    var = jnp.mean(jnp.square(o - mu), axis=-1, keepdims=True)
    return (o - mu) * lax.rsqrt(var + EPS) * gain * (gate * jax.nn.sigmoid(gate))


def _retention_body(qk_ref, v_ref, g_ref, cos_ref, sin_ref, decay_ref, qdec_ref, kdec_ref, sdec_ref,
                    gain_ref, o_ref, st_ref, state):
    @pl.when(pl.program_id(0) == 0)
    def _():
        state[...] = jnp.zeros_like(state)

    qk = qk_ref[...]
    width = C_HEADS * C_DK
    q = _rope(qk[:, :width], cos_ref[...], sin_ref[...]) * (C_DK ** -0.5)
    k = _rope(qk[:, width:], cos_ref[...], sin_ref[...])
    v = v_ref[...]
    g = g_ref[...]
    for h in range(C_HEADS):
        qh = q[:, h * C_DK:(h + 1) * C_DK]
        kh = k[:, h * C_DK:(h + 1) * C_DK]
        vh = v[:, h * C_DV:(h + 1) * C_DV]
        st = state[h]
        scores = _dot_nt(qh, kh, HIGHEST) * decay_ref[h]
        o = jnp.dot(scores, vh, precision=HIGHEST, preferred_element_type=F32)
        o = o + jnp.dot(qh, st, precision=HIGHEST, preferred_element_type=F32) * qdec_ref[h]
        new_st = sdec_ref[h] * st + _dot_tn(kh * kdec_ref[h], vh, HIGHEST)
        state[h] = new_st
        st_ref[h] = new_st
        lanes = slice(h * C_DV, (h + 1) * C_DV)
        o_ref[:, lanes] = _groupnorm_gate(o, gain_ref[:, lanes], g[:, lanes])


def _retention_prompt(z, cos_f, sin_s, tabs, gain):
    t = z.shape[0]
    nc = t // CHUNK
    decay, qdec, kdec, sdec = tabs
    width = C_HEADS * C_DV
    const3 = lambda i: (0, 0, 0)
    return pl.pallas_call(
        _retention_body, grid=(nc,),
        in_specs=[
            pl.BlockSpec((CHUNK, width), lambda i: (i, 0)),
            pl.BlockSpec((CHUNK, width), lambda i: (i, 1)),
            pl.BlockSpec((CHUNK, width), lambda i: (i, 2)),
            pl.BlockSpec((CHUNK, C_HEADS * C_DK), lambda i: (i, 0)),
            pl.BlockSpec((CHUNK, C_HEADS * C_DK), lambda i: (i, 0)),
            pl.BlockSpec(decay.shape, const3),
            pl.BlockSpec(qdec.shape, const3),
            pl.BlockSpec(kdec.shape, const3),
            pl.BlockSpec(sdec.shape, const3),
            pl.BlockSpec((1, width), lambda i: (0, 0)),
        ],
        out_specs=[
            pl.BlockSpec((CHUNK, width), lambda i: (i, 0)),
            pl.BlockSpec((C_HEADS, C_DK, C_DV), const3),
        ],
        out_shape=[
            jax.ShapeDtypeStruct((t, width), F32),
            jax.ShapeDtypeStruct((C_HEADS, C_DK, C_DV), F32),
        ],
        scratch_shapes=[pltpu.VMEM((C_HEADS, C_DK, C_DV), F32)],
        compiler_params=_params(("arbitrary",)), name="retention_prompt",
    )(z, z, z, cos_f, sin_s, decay, qdec, kdec, sdec, gain.reshape(1, width))


RET_SAMPLE_BATCH = 8


def _retention_sample_body(qk_ref, v_ref, g_ref, cos_ref, sin_ref, decay_ref, qdec_ref, kdec_ref,
                           sdec_ref, gain_ref, st_in_ref, o_ref, st_ref):
    width = C_HEADS * C_DK
    for b in range(RET_SAMPLE_BATCH):
        qk = qk_ref[b]
        q = _rope(qk[:, :width], cos_ref[...], sin_ref[...]) * (C_DK ** -0.5)
        k = _rope(qk[:, width:], cos_ref[...], sin_ref[...])
        v = v_ref[b]
        g = g_ref[b]
        for h in range(C_HEADS):
            qh = q[:, h * C_DK:(h + 1) * C_DK]
            kh = k[:, h * C_DK:(h + 1) * C_DK]
            vh = v[:, h * C_DV:(h + 1) * C_DV]
            st = st_in_ref[b, h]
            scores = _dot_nt(qh, kh, HIGHEST) * decay_ref[h]
            o = jnp.dot(scores, vh, precision=HIGHEST, preferred_element_type=F32)
            o = o + jnp.dot(qh, st, precision=HIGHEST, preferred_element_type=F32) * qdec_ref[h]
            st_ref[b, h] = sdec_ref[h] * st + _dot_tn(kh * kdec_ref[h], vh, HIGHEST)
            lanes = slice(h * C_DV, (h + 1) * C_DV)
            o_ref[b, :, lanes] = _groupnorm_gate(o, gain_ref[:, lanes], g[:, lanes])


def _retention_sample(z3, cos_f, sin_s, tabs, gain, st_in):
    nb = z3.shape[0]
    bb = RET_SAMPLE_BATCH
    decay, qdec, kdec, sdec = tabs
    width = C_HEADS * C_DV
    const3 = lambda i: (0, 0, 0)
    return pl.pallas_call(
        _retention_sample_body, grid=(nb // bb,),
        in_specs=[
            pl.BlockSpec((bb, T_PAD, width), lambda i: (i, 0, 0)),
            pl.BlockSpec((bb, T_PAD, width), lambda i: (i, 0, 1)),
            pl.BlockSpec((bb, T_PAD, width), lambda i: (i, 0, 2)),
            pl.BlockSpec((T_PAD, C_HEADS * C_DK), lambda i: (0, 0)),
            pl.BlockSpec((T_PAD, C_HEADS * C_DK), lambda i: (0, 0)),
            pl.BlockSpec(decay.shape, const3),
            pl.BlockSpec(qdec.shape, const3),
            pl.BlockSpec(kdec.shape, const3),
            pl.BlockSpec(sdec.shape, const3),
            pl.BlockSpec((1, width), lambda i: (0, 0)),
            pl.BlockSpec((bb, C_HEADS, C_DK, C_DV), lambda i: (i, 0, 0, 0)),
        ],
        out_specs=[
            pl.BlockSpec((bb, T_PAD, width), lambda i: (i, 0, 0)),
            pl.BlockSpec((bb, C_HEADS, C_DK, C_DV), lambda i: (i, 0, 0, 0)),
        ],
        out_shape=[
            jax.ShapeDtypeStruct((nb, T_PAD, width), F32),
            jax.ShapeDtypeStruct((nb, C_HEADS, C_DK, C_DV), F32),
        ],
        compiler_params=_params(("parallel",)), name="retention_sample",
    )(z3, z3, z3, cos_f, sin_s, decay, qdec, kdec, sdec, gain.reshape(1, width), st_in)


def _retention_tables(length, n_valid):
    log_gamma = jnp.log1p(-jnp.exp2(-RET_DECAY_EXP0 - jnp.arange(C_HEADS, dtype=F32)))
    i = jnp.arange(length, dtype=F32)
    valid = (jnp.arange(length) < n_valid)
    rel = i[:, None] - i[None, :]
    decay = jnp.where(rel >= 0, jnp.exp(jnp.maximum(rel, 0.0)[None] * log_gamma[:, None, None]), 0.0)
    decay = jnp.where(valid[None, None, :], decay, 0.0)
    q_dec = jnp.exp((i[None, :] + 1.0) * log_gamma[:, None])
    k_dec = jnp.where(valid[None, :], jnp.exp((n_valid - 1.0 - i)[None, :] * log_gamma[:, None]), 0.0)
    s_dec = jnp.exp(n_valid * log_gamma)
    qdec = jnp.broadcast_to(q_dec[:, :, None], (C_HEADS, length, C_DV))
    kdec = jnp.broadcast_to(k_dec[:, :, None], (C_HEADS, length, C_DK))
    sdec = jnp.broadcast_to(s_dec[:, None, None], (C_HEADS, C_DK, C_DV))
    return decay, qdec, kdec, sdec


def _rope_tables(pos):
    half = C_DK // 2
    freqs = ROPE_BASE ** (-jnp.arange(half, dtype=F32) / half)
    ang = pos.astype(F32)[:, None] * freqs[None, :]
    cos, sin = jnp.cos(ang), jnp.sin(ang)
    cos_f = jnp.tile(jnp.concatenate([cos, cos], axis=1), (1, C_HEADS))
    sin_s = jnp.tile(jnp.concatenate([-sin, sin], axis=1), (1, C_HEADS))
    return cos_f, sin_s


def _s5_discretize(are_ref, aim_ref, ldt_ref):
    a_re, a_im = are_ref[...], aim_ref[...]
    dt = jnp.exp(ldt_ref[...])
    mag = jnp.exp(a_re * dt)
    ab_re = mag * jnp.cos(a_im * dt)
    ab_im = mag * jnp.sin(a_im * dt)
    num_re, num_im = ab_re - 1.0, ab_im
    den = a_re * a_re + a_im * a_im
    co_re = (num_re * a_re + num_im * a_im) / den
    co_im = (num_im * a_re - num_re * a_im) / den
    return ab_re, ab_im, co_re, co_im


S5_CHUNK = 256


def _s5_prompt_body(u_ref, are_ref, aim_ref, ldt_ref, bre_ref, bim_ref, cre_ref, cim_ref, dd_ref,
                    y_ref, st_ref, hre, him, state, disc):
    n = S5_CHUNK

    @pl.when(pl.program_id(0) == 0)
    def _():
        ab_re, ab_im, co_re, co_im = _s5_discretize(are_ref, aim_ref, ldt_ref)
        disc[0] = ab_re
        disc[1] = ab_im
        disc[2] = co_re
        disc[3] = co_im
        state[...] = jnp.zeros_like(state)

    n_chunks = D_ROW_STATE // LANES
    for s in range(D_ROWS):
        rows = pl.ds(s, n, stride=D_ROWS)
        u_s = u_ref[rows, :]
        raw_re = jnp.dot(u_s, bre_ref[s], precision=HIGHEST, preferred_element_type=F32)
        raw_im = jnp.dot(u_s, bim_ref[s], precision=HIGHEST, preferred_element_type=F32)
        co_re = disc[2, s:s + 1, :]
        co_im = disc[3, s:s + 1, :]
        bu_re = co_re * raw_re - co_im * raw_im
        bu_im = co_re * raw_im + co_im * raw_re
        for c in range(n_chunks):
            hre[c, rows, :] = bu_re[:, c * LANES:(c + 1) * LANES]
            him[c, rows, :] = bu_im[:, c * LANES:(c + 1) * LANES]

    ab_re = disc[0]
    ab_im = disc[1]

    def step(t, carry):
        h_re, h_im = carry
        rows = pl.ds(pl.multiple_of(t * D_ROWS, D_ROWS), D_ROWS)
        bu_re = jnp.concatenate([hre[c, rows, :] for c in range(n_chunks)], axis=1)
        bu_im = jnp.concatenate([him[c, rows, :] for c in range(n_chunks)], axis=1)
        n_re = ab_re * h_re - ab_im * h_im + bu_re
        n_im = ab_re * h_im + ab_im * h_re + bu_im
        for c in range(n_chunks):
            hre[c, rows, :] = n_re[:, c * LANES:(c + 1) * LANES]
            him[c, rows, :] = n_im[:, c * LANES:(c + 1) * LANES]
        return n_re, n_im

    h_re, h_im = lax.fori_loop(0, n, step, (state[0], state[1]), unroll=4)
    state[0] = h_re
    state[1] = h_im
    st_ref[0] = h_re
    st_ref[1] = h_im

    for s in range(D_ROWS):
        rows = pl.ds(s, n, stride=D_ROWS)
        hist_re = jnp.concatenate([hre[c, rows, :] for c in range(n_chunks)], axis=1)
        hist_im = jnp.concatenate([him[c, rows, :] for c in range(n_chunks)], axis=1)
        y = (jnp.dot(hist_re, cre_ref[s], precision=HIGHEST, preferred_element_type=F32)
             - jnp.dot(hist_im, cim_ref[s], precision=HIGHEST, preferred_element_type=F32)
             + dd_ref[s:s + 1, :] * u_ref[rows, :])
        y_ref[rows, :] = _gelu(y)


def _s5_prompt(u8, par):
    a_re, a_im, ldt, b_re, b_im, c_re, c_im, dd = par
    rows = u8.shape[0]
    blk = S5_CHUNK * D_ROWS
    c2 = lambda i: (0, 0)
    c3 = lambda i: (0, 0, 0)
    return pl.pallas_call(
        _s5_prompt_body, grid=(rows // blk,),
        in_specs=[
            pl.BlockSpec((blk, D_ROW_CH), lambda i: (i, 0)),
            pl.BlockSpec(a_re.shape, c2), pl.BlockSpec(a_im.shape, c2), pl.BlockSpec(ldt.shape, c2),
            pl.BlockSpec(b_re.shape, c3), pl.BlockSpec(b_im.shape, c3),
            pl.BlockSpec(c_re.shape, c3), pl.BlockSpec(c_im.shape, c3),
            pl.BlockSpec(dd.shape, c2),
        ],
        out_specs=[
            pl.BlockSpec((blk, D_ROW_CH), lambda i: (i, 0)),
            pl.BlockSpec((2, D_ROWS, D_ROW_STATE), c3),
        ],
        out_shape=[
            jax.ShapeDtypeStruct((rows, D_ROW_CH), F32),
            jax.ShapeDtypeStruct((2, D_ROWS, D_ROW_STATE), F32),
        ],
        scratch_shapes=[
            pltpu.VMEM((D_ROW_STATE // LANES, blk, LANES), F32),
            pltpu.VMEM((D_ROW_STATE // LANES, blk, LANES), F32),
            pltpu.VMEM((2, D_ROWS, D_ROW_STATE), F32), pltpu.VMEM((4, D_ROWS, D_ROW_STATE), F32),
        ],
        compiler_params=_params(("arbitrary",)), name="s5_prompt",
    )(u8, a_re, a_im, ldt, b_re, b_im, c_re, c_im, dd)


def _s5_sample_body(u_ref, xre_ref, xim_ref, are_ref, aim_ref, ldt_ref, bre_ref, bim_ref, cre_ref,
                    cim_ref, dd_ref, y_ref, sre_ref, sim_ref):
    ab_re, ab_im, co_re, co_im = _s5_discretize(are_ref, aim_ref, ldt_ref)
    n_t = u_ref.shape[0]
    for s in range(D_ROWS):
        a_r, a_i = ab_re[s:s + 1, :], ab_im[s:s + 1, :]
        c_r, c_i = co_re[s:s + 1, :], co_im[s:s + 1, :]
        h_re, h_im = xre_ref[s], xim_ref[s]
        for t in range(n_t):
            u = u_ref[t, s]
            raw_re = jnp.dot(u, bre_ref[s], precision=HIGHEST, preferred_element_type=F32)
            raw_im = jnp.dot(u, bim_ref[s], precision=HIGHEST, preferred_element_type=F32)
            bu_re = c_r * raw_re - c_i * raw_im
            bu_im = c_r * raw_im + c_i * raw_re
            h_re, h_im = a_r * h_re - a_i * h_im + bu_re, a_r * h_im + a_i * h_re + bu_im
            y = (jnp.dot(h_re, cre_ref[s], precision=HIGHEST, preferred_element_type=F32)
                 - jnp.dot(h_im, cim_ref[s], precision=HIGHEST, preferred_element_type=F32)
                 + dd_ref[s:s + 1, :] * u)
            y_ref[t, s] = _gelu(y)
        sre_ref[s] = h_re
        sim_ref[s] = h_im


def _s5_sample(u_ts, x_re, x_im, par):
    a_re, a_im, ldt, b_re, b_im, c_re, c_im, dd = par
    return pl.pallas_call(
        _s5_sample_body,
        out_shape=[
            jax.ShapeDtypeStruct(u_ts.shape, F32),
            jax.ShapeDtypeStruct(x_re.shape, F32),
            jax.ShapeDtypeStruct(x_im.shape, F32),
        ],
        compiler_params=pltpu.CompilerParams(vmem_limit_bytes=VMEM_LIMIT), name="s5_sample",
    )(u_ts, x_re, x_im, a_re, a_im, ldt, b_re, b_im, c_re, c_im, dd)


def _s5_params(a_re, a_im, log_dt, b_re, b_im, c_re, c_im, dd):
    eye = jnp.eye(D_ROWS, dtype=F32)

    def rows(a):
        return a.reshape(D_ROWS, D_ROW_STATE)

    def b_blocks(b):
        b4 = b.reshape(D_ROWS, D_ROWS, D_STATE, D_GROUP_CH).transpose(0, 1, 3, 2)
        return jnp.einsum('sgcp,gh->sgchp', b4, eye).reshape(D_ROWS, D_ROW_CH, D_ROW_STATE)

    def c_blocks(c):
        c4 = c.reshape(D_ROWS, D_ROWS, D_GROUP_CH, D_STATE)
        return jnp.einsum('sgcp,gh->sgphc', c4, eye).reshape(D_ROWS, D_ROW_STATE, D_ROW_CH)

    ldt = jnp.broadcast_to(log_dt[:, None], (D_GROUPS, D_STATE))
    return (rows(a_re), rows(a_im), rows(ldt), b_blocks(b_re), b_blocks(b_im),
            c_blocks(c_re), c_blocks(c_im), dd.reshape(D_ROWS, D_ROW_CH))


def _glu_body(x_ref, w_ref, b_ref, o_ref):
    x = x_ref[...]
    gate = jnp.dot(x.astype(BF16), w_ref[...], preferred_element_type=F32) + b_ref[...]
    o_ref[...] = x * jax.nn.sigmoid(gate)


def _glu(x, w, b, tm=512):
    m, k = x.shape
    return pl.pallas_call(
        _glu_body, grid=(m // tm,),
        in_specs=[pl.BlockSpec((tm, k), lambda i: (i, 0)), pl.BlockSpec((k, k), lambda i: (0, 0)),
                  pl.BlockSpec((1, k), lambda i: (0, 0))],
        out_specs=pl.BlockSpec((tm, k), lambda i: (i, 0)),
        out_shape=jax.ShapeDtypeStruct((m, k), F32),
        compiler_params=_params(("parallel",)), name="glu",
    )(x, w, b.reshape(1, k))


PEER_SEL_TOKENS = 256


def _extract_top(s, n_out):
    n_rows = s.shape[0]
    rid = lax.broadcasted_iota(jnp.int32, s.shape, 0).astype(F32)
    vals = []
    for _ in range(n_out):
        m = jnp.max(s, axis=0, keepdims=True)
        first = jnp.min(jnp.where(s == m, rid, float(n_rows)), axis=0, keepdims=True)
        s = jnp.where(rid == first, NEG_INF, s)
        vals.append(m)
    return vals, s


def _stack_rows(rows):
    n = len(rows)
    rid = lax.broadcasted_iota(jnp.int32, (n, rows[0].shape[1]), 0)
    out = jnp.broadcast_to(rows[n - 1], rid.shape)
    for i in range(n - 2, -1, -1):
        out = jnp.where(rid == i, rows[i], out)
    return out


def _peer_select_body(q_ref, k1_ref, k2_ref, s1_ref, g1_ref, s2_ref, e2_ref, tau_ref):
    for h in range(PEER_HEADS):
        q1 = q_ref[:, (2 * h) * PEER_HALF:(2 * h + 1) * PEER_HALF]
        q2 = q_ref[:, (2 * h + 1) * PEER_HALF:(2 * h + 2) * PEER_HALF]
        s1 = _dot_nt(k1_ref[h], q1, HIGHEST)
        s2 = _dot_nt(k2_ref[h], q2, HIGHEST)
        v1, r1 = _extract_top(s1, PEER_TOPK)
        v2, r2 = _extract_top(s2, PEER_TOPK)
        v2_all = _stack_rows(v2)
        cand = jnp.concatenate([v1[a] + v2_all for a in range(PEER_TOPK)], axis=0)
        top, _ = _extract_top(cand, PEER_TOPK)
        mx = top[0]
        z = jnp.exp(top[0] - mx)
        for kk in range(1, PEER_TOPK):
            z = z + jnp.exp(top[kk] - mx)
        e1 = jnp.where(r1 == NEG_INF, jnp.exp(s1 - v1[0]), 0.0)
        e2 = jnp.where(r2 == NEG_INF, jnp.exp(s2 - v2[0]), 0.0)
        g1 = e1 / z
        tau = top[PEER_TOPK - 1]
        for j in range(PEER_SEL_TOKENS // LANES):
            lanes = slice(j * LANES, (j + 1) * LANES)
            s1_ref[j, h] = s1[:, lanes]
            g1_ref[j, h] = g1[:, lanes]
            s2_ref[j, h] = s2[:, lanes]
            e2_ref[j, h] = e2[:, lanes]
            tau_ref[j, h:h + 1, :] = tau[:, lanes]


def _peer_select(q, k1, k2):
    t = q.shape[0]
    tb = PEER_SEL_TOKENS
    nj = tb // LANES
    big = pl.BlockSpec((nj, PEER_HEADS, PEER_NKEYS, LANES), lambda i: (i, 0, 0, 0))
    big_shape = jax.ShapeDtypeStruct((t // LANES, PEER_HEADS, PEER_NKEYS, LANES), F32)
    return pl.pallas_call(
        _peer_select_body, grid=(t // tb,),
        in_specs=[
            pl.BlockSpec((tb, q.shape[1]), lambda i: (i, 0)),
            pl.BlockSpec(k1.shape, lambda i: (0, 0, 0)),
            pl.BlockSpec(k2.shape, lambda i: (0, 0, 0)),
        ],
        out_specs=[big, big, big, big, pl.BlockSpec((nj, PEER_HEADS, LANES), lambda i: (i, 0, 0))],
        out_shape=[big_shape, big_shape, big_shape, big_shape,
                   jax.ShapeDtypeStruct((t // LANES, PEER_HEADS, LANES), F32)],
        compiler_params=_params(("parallel",)), name="peer_select",
    )(q, k1, k2)


PEER_TOKENS = 512
PEER_EXPERT_TILE = 512


PEER_SLABS = PEER_EXPERT_TILE // PEER_NKEYS


def _peer_dense_body(h_ref, g_ref, u_ref, v_ref, s1_ref, g1_ref, s2_ref, e2_ref, tau_ref, o_ref,
                     xn, act_s, coef_s, acc):
    e = pl.program_id(1)
    n_chunks = PEER_TOKENS // LANES

    @pl.when(e == 0)
    def _():
        xn[...] = _rmsnorm(h_ref[...], g_ref[...]).astype(BF16)
        acc[...] = jnp.zeros_like(acc)

    act = _dot_nt(u_ref[...], xn[...])
    for j in range(n_chunks):
        act_s[j] = act[:, j * LANES:(j + 1) * LANES]

    def chunk(j, carry):
        for c in range(PEER_SLABS):
            gate = jnp.zeros((PEER_NKEYS, LANES), F32)
            for h in range(PEER_HEADS):
                s1 = s1_ref[j, h, c:c + 1, :]
                g1 = g1_ref[j, h, c:c + 1, :]
                tau = tau_ref[j, h:h + 1, :]
                sel = (s2_ref[j, h] + s1) >= tau
                gate = gate + jnp.where(sel, e2_ref[j, h] * g1, 0.0)
            rows = slice(c * PEER_NKEYS, (c + 1) * PEER_NKEYS)
            coef_s[j, rows, :] = (gate * _gelu(act_s[j, rows, :])).astype(BF16)
        return carry

    lax.fori_loop(0, n_chunks, chunk, 0)
    coef = jnp.concatenate([coef_s[j] for j in range(n_chunks)], axis=1)
    acc[...] += _dot_tn(coef, v_ref[...])

    @pl.when(e == pl.num_programs(1) - 1)
    def _():
        o_ref[...] = h_ref[...] + acc[...]


def _peer_dense(h, g, u_tab, v_tab, sel):
    s1, g1, s2, e2, tau = sel
    t, d = h.shape
    n_exp = u_tab.shape[0]
    tb, te = PEER_TOKENS, PEER_EXPERT_TILE
    nj = tb // LANES
    n_tiles = n_exp // te

    def by_tile(a):
        a5 = a.reshape(a.shape[0], PEER_HEADS, n_tiles, PEER_SLABS, LANES)
        return a5.transpose(2, 0, 1, 3, 4)

    tile_spec = pl.BlockSpec((None, nj, PEER_HEADS, PEER_SLABS, LANES), lambda i, e: (e, i, 0, 0, 0))
    tok_spec = pl.BlockSpec((nj, PEER_HEADS, PEER_NKEYS, LANES), lambda i, e: (i, 0, 0, 0))
    return pl.pallas_call(
        _peer_dense_body, grid=(t // tb, n_tiles),
        in_specs=[
            pl.BlockSpec((tb, d), lambda i, e: (i, 0)),
            pl.BlockSpec((1, d), lambda i, e: (0, 0)),
            pl.BlockSpec((te, d), lambda i, e: (e, 0)),
            pl.BlockSpec((te, d), lambda i, e: (e, 0)),
            tile_spec, tile_spec, tok_spec, tok_spec,
            pl.BlockSpec((nj, PEER_HEADS, LANES), lambda i, e: (i, 0, 0)),
        ],
        out_specs=pl.BlockSpec((tb, d), lambda i, e: (i, 0)),
        out_shape=jax.ShapeDtypeStruct((t, d), F32),
        scratch_shapes=[pltpu.VMEM((tb, d), BF16), pltpu.VMEM((nj, te, LANES), F32),
                        pltpu.VMEM((nj, te, LANES), BF16), pltpu.VMEM((tb, d), F32)],
        compiler_params=_params(("parallel", "arbitrary")), name="peer_dense",
    )(h, g.reshape(1, d), u_tab, v_tab, by_tile(s1), by_tile(g1), s2, e2, tau)


def _peer(h, g, wq, k1, k2, u_tab, v_tab):
    q = _matmul(h, wq, norm_g=g)
    return _peer_dense(h, g, u_tab, v_tab, _peer_select(q, k1, k2))


def kernel(x_prompt, x_sample, state_b_k, state_b_v, state_c_s, state_d_re, state_d_im, norm1_g, norm2_g, final_g, w_in_even, w_out_even, a_ws, a_bs, b_sink, w_in_odd, w_out_odd, c_norm_g, d_a_re, d_a_im, d_log_dt, d_b_re, d_b_im, d_c_re, d_c_im, d_d, d_glu_w, d_glu_b, peer_wq, peer_k1, peer_k2, peer_u, peer_v):
    seq = x_prompt.shape[1]
    n_batch, n_new = x_sample.shape[:2]
    past = PAST_LEN
    hp = x_prompt.reshape(seq, D_MODEL)
    hs = x_sample.reshape(n_batch * n_new, D_MODEL)
    bf = lambda a: a.astype(BF16)

    def pad_tokens(a):
        a3 = a.reshape(n_batch, n_new, a.shape[-1])
        return jnp.pad(a3, ((0, 0), (0, T_PAD - n_new), (0, 0)))

    def unpad_tokens(a3):
        return a3[:, :n_new].reshape(n_batch * n_new, a3.shape[-1])

    w_in = bf(w_in_even[0])
    w_out = bf(w_out_even[0])
    zp = _matmul(hp, w_in, norm_g=norm1_g[0])
    zs = _matmul(hs, w_in, norm_g=norm1_g[0])
    bs_full = jnp.broadcast_to(a_bs[0][:, :, None], (A_GROUPS, CHUNK, LANES))
    yp = _even_prompt(zp, a_ws[0], bs_full, b_sink[0])
    ws_small = jnp.tril(a_ws[0][:, :n_new, :n_new]).transpose(2, 1, 0)
    wa = jnp.repeat(jnp.pad(ws_small, ((0, 0), (0, T_PAD - n_new), (0, 0))), LANES, axis=-1)
    wb = jnp.repeat(jnp.pad(a_bs[0][:, :n_new].T, ((0, T_PAD - n_new), (0, 0))), LANES, axis=-1)
    win = state_b_k.shape[2]
    kbuf = state_b_k[0].reshape(n_batch, win, B_KV_HEADS * B_DH)
    vbuf = state_b_v[0].reshape(n_batch, win, B_KV_HEADS * B_DH)
    ys3, av3 = _even_sample(pad_tokens(zs), kbuf, vbuf, wa, wb, b_sink[0])
    hp = _matmul(yp, w_out, resid=hp)
    hs = _matmul(unpad_tokens(ys3), w_out, resid=hs)

    k_off = 2 * A_WIDTH + B_HEADS * B_DH
    v_off = k_off + B_KV_HEADS * B_DH
    kv_shape = (1, -1, win, B_KV_HEADS, B_DH)
    a_v_sample = av3[:, :n_new].reshape(1, n_batch, n_new, A_GROUPS, A_WIDTH // A_GROUPS)
    b_k_prompt = zp[seq - win:, k_off:v_off].reshape(kv_shape)
    b_v_prompt = zp[seq - win:, v_off:].reshape(kv_shape)
    k_new = zs[:, k_off:v_off].reshape(n_batch, n_new, B_KV_HEADS * B_DH)
    v_new = zs[:, v_off:].reshape(n_batch, n_new, B_KV_HEADS * B_DH)
    b_k_sample = jnp.concatenate([kbuf, k_new], axis=1)[:, -win:].reshape(kv_shape)
    b_v_sample = jnp.concatenate([vbuf, v_new], axis=1)[:, -win:].reshape(kv_shape)

    u_tab, v_tab = bf(peer_u[0]), bf(peer_v[0])
    wq = bf(peer_wq[0])
    hp = _peer(hp, norm2_g[0], wq, peer_k1[0], peer_k2[0], u_tab, v_tab)
    hs = _peer(hs, norm2_g[0], wq, peer_k1[0], peer_k2[0], u_tab, v_tab)

    w_in = bf(w_in_odd[0])
    w_out = bf(w_out_odd[0])
    c_width = C_HEADS * C_DV
    zp = _matmul(hp, w_in, norm_g=norm1_g[1])
    zs = _matmul(hs, w_in, norm_g=norm1_g[1])
    cos_p, sin_p = _rope_tables(jnp.arange(seq))
    cos_s, sin_s = _rope_tables(past + jnp.arange(T_PAD))
    ycp, c_s_prompt = _retention_prompt(zp, cos_p, sin_p, _retention_tables(CHUNK, CHUNK), c_norm_g[0])
    zs3 = pad_tokens(zs)
    ycs3, c_s_sample = _retention_sample(zs3, cos_s, sin_s, _retention_tables(T_PAD, n_new), c_norm_g[0],
                                         state_c_s[0])
    par = _s5_params(d_a_re[0], d_a_im[0], d_log_dt[0], d_b_re[0], d_b_im[0], d_c_re[0], d_c_im[0], d_d[0])
    u_off = 3 * c_width
    up8 = zp[:, u_off:].reshape(seq * D_ROWS, D_ROW_CH)
    ydp8, d_prompt = _s5_prompt(up8, par)
    us_ts = zs[:, u_off:].reshape(n_batch, n_new, D_ROWS, D_ROW_CH).transpose(1, 2, 0, 3)
    x_re = state_d_re[0].reshape(n_batch, D_ROWS, D_ROW_STATE).transpose(1, 0, 2)
    x_im = state_d_im[0].reshape(n_batch, D_ROWS, D_ROW_STATE).transpose(1, 0, 2)
    yds_ts, s_re, s_im = _s5_sample(us_ts, x_re, x_im, par)
    glu_w = bf(d_glu_w[0])
    ydp = _glu(ydp8.reshape(seq, c_width), glu_w, d_glu_b[0])
    yds = _glu(yds_ts.transpose(2, 0, 1, 3).reshape(n_batch * n_new, c_width), glu_w, d_glu_b[0])
    hp = _matmul(ycp, w_out[:c_width], x2=ydp, w2=w_out[c_width:], resid=hp)
    hs = _matmul(unpad_tokens(ycs3), w_out[:c_width], x2=yds, w2=w_out[c_width:], resid=hs)

    u_tab, v_tab = bf(peer_u[1]), bf(peer_v[1])
    wq = bf(peer_wq[1])
    hp = _peer(hp, norm2_g[1], wq, peer_k1[1], peer_k2[1], u_tab, v_tab)
    hs = _peer(hs, norm2_g[1], wq, peer_k1[1], peer_k2[1], u_tab, v_tab)

    y_prompt = _final_norm(hp, final_g).reshape(x_prompt.shape)
    y_sample = _final_norm(hs, final_g).reshape(x_sample.shape)

    d_shape = (1, -1, D_GROUPS, D_STATE)
    return (y_prompt, y_sample, a_v_sample, b_k_prompt, b_v_prompt, b_k_sample, b_v_sample,
            c_s_prompt.reshape(1, 1, C_HEADS, C_DK, C_DV), c_s_sample[None],
            d_prompt[0].reshape(d_shape), d_prompt[1].reshape(d_shape),
            s_re.transpose(1, 0, 2).reshape(d_shape), s_im.transpose(1, 0, 2).reshape(d_shape))
```

```python
import functools
import math

import jax
import jax.numpy as jnp
from jax import lax
from jax.experimental import pallas as pl
from jax.experimental.pallas import tpu as pltpu

F32 = jnp.float32
BF16 = jnp.bfloat16
HIGHEST = lax.Precision.HIGHEST

EPS = 1e-6
NEG_BIG = -1e30
NEG_INF = float("-inf")

D_MODEL = 2048
PAST_LEN = 8192
LANES = 128
SUBLANES = 8
VMEM_LIMIT = 56 * 1024 * 1024

CHUNK = 128
A_GROUPS = 8
A_WIDTH = 1024
B_HEADS = 16
B_KV_HEADS = 4
B_GQA = 4
B_DH = 64
C_HEADS = 8
C_DK = 64
C_DV = 128
RET_DECAY_EXP0 = 5.0
ROPE_BASE = 10000.0
D_GROUPS = 64
D_STATE = 64
D_GROUP_CH = 16
D_ROWS = 8
D_ROW_STATE = 512
D_ROW_CH = 128
PEER_HEADS = 8
PEER_NKEYS = 128
PEER_TOPK = 16
PEER_HALF = 128


def _params(semantics):
    return pltpu.CompilerParams(dimension_semantics=semantics, vmem_limit_bytes=VMEM_LIMIT)


def _gelu(x):
    return 0.5 * x * (1.0 + lax.erf(x * (1.0 / math.sqrt(2.0))))


def _rmsnorm(x, g):
    return x * lax.rsqrt(jnp.mean(x * x, axis=-1, keepdims=True) + EPS) * g


def _dot_nt(a, b, precision=None):
    return lax.dot_general(a, b, (((1,), (1,)), ((), ())), precision=precision,
                           preferred_element_type=F32)


def _dot_tn(a, b, precision=None):
    return lax.dot_general(a, b, (((0,), (0,)), ((), ())), precision=precision,
                           preferred_element_type=F32)


def _mm_body(*refs, has_norm, has_pair, has_resid):
    it = iter(refs)
    x_ref, w_ref = next(it), next(it)
    g_ref = next(it) if has_norm else None
    x2_ref, w2_ref = (next(it), next(it)) if has_pair else (None, None)
    r_ref = next(it) if has_resid else None
    o_ref = next(it)
    x = x_ref[...]
    if has_norm:
        x = _rmsnorm(x, g_ref[...])
    xb = x.astype(BF16)
    x2b = x2_ref[...].astype(BF16) if has_pair else None
    n = o_ref.shape[1]
    for j in range(n // MM_COLS):
        cols = slice(j * MM_COLS, (j + 1) * MM_COLS)
        acc = jnp.dot(xb, w_ref[:, cols], preferred_element_type=F32)
        if has_pair:
            acc = acc + jnp.dot(x2b, w2_ref[:, cols], preferred_element_type=F32)
        if has_resid:
            acc = acc + r_ref[:, cols]
        o_ref[:, cols] = acc


MM_COLS = 512


def _matmul(x, w, *, norm_g=None, x2=None, w2=None, resid=None, tm=512):
    m, k = x.shape
    n = w.shape[1]
    assert m % tm == 0 and n % MM_COLS == 0
    resident = pl.Buffered(1)
    args = [x, w]
    specs = [pl.BlockSpec((tm, k), lambda i: (i, 0)),
             pl.BlockSpec((k, n), lambda i: (0, 0), pipeline_mode=resident)]
    if norm_g is not None:
        args.append(norm_g.reshape(1, k))
        specs.append(pl.BlockSpec((1, k), lambda i: (0, 0)))
    if x2 is not None:
        k2 = x2.shape[1]
        args += [x2, w2]
        specs += [pl.BlockSpec((tm, k2), lambda i: (i, 0)),
                  pl.BlockSpec((k2, n), lambda i: (0, 0), pipeline_mode=resident)]
    if resid is not None:
        args.append(resid)
        specs.append(pl.BlockSpec((tm, n), lambda i: (i, 0)))
    body = functools.partial(_mm_body, has_norm=norm_g is not None, has_pair=x2 is not None,
                             has_resid=resid is not None)
    return pl.pallas_call(
        body, grid=(m // tm,), in_specs=specs,
        out_specs=pl.BlockSpec((tm, n), lambda i: (i, 0)),
        out_shape=jax.ShapeDtypeStruct((m, n), F32),
        compiler_params=_params(("parallel",)), name="matmul",
    )(*args)


def _norm_body(x_ref, g_ref, o_ref):
    o_ref[...] = _rmsnorm(x_ref[...], g_ref[...])


def _final_norm(x, g, tm=512):
    m, k = x.shape
    return pl.pallas_call(
        _norm_body, grid=(m // tm,),
        in_specs=[pl.BlockSpec((tm, k), lambda i: (i, 0)), pl.BlockSpec((1, k), lambda i: (0, 0))],
        out_specs=pl.BlockSpec((tm, k), lambda i: (i, 0)),
        out_shape=jax.ShapeDtypeStruct((m, k), F32),
        compiler_params=_params(("parallel",)), name="final_norm",
    )(x, g.reshape(1, k))


def _sink_column(sink_ref, kvh, rows_per_head, n_rows):
    grp = lax.broadcasted_iota(jnp.int32, (n_rows, 1), 0) // rows_per_head
    sk = jnp.full((n_rows, 1), sink_ref[kvh * B_GQA + B_GQA - 1], F32)
    for g in range(B_GQA - 2, -1, -1):
        sk = jnp.where(grp == g, sink_ref[kvh * B_GQA + g], sk)
    return sk


def _even_prompt_body(sink_ref, au_ref, av_ref, q_ref, kvc_ref, kvp_ref, ws_ref, bs_ref, o_ref):
    blk = pl.program_id(0)
    au = _gelu(au_ref[...])
    av = _gelu(av_ref[...])
    row = lax.broadcasted_iota(jnp.int32, (CHUNK, CHUNK), 0)
    col = lax.broadcasted_iota(jnp.int32, (CHUNK, CHUNK), 1)
    causal = row >= col
    for g in range(A_GROUPS):
        lanes = slice(g * LANES, (g + 1) * LANES)
        w = jnp.where(causal, ws_ref[g], 0.0).astype(BF16)
        mixed = jnp.dot(w, av[:, lanes].astype(BF16), preferred_element_type=F32) + bs_ref[g]
        o_ref[:, lanes] = au[:, lanes] * mixed

    q = q_ref[...]
    kvc = kvc_ref[...]
    kvp = kvp_ref[...]
    n_rows = B_GQA * CHUNK
    qi = lax.broadcasted_iota(jnp.int32, (n_rows, 2 * CHUNK), 0) % CHUNK
    kc = lax.broadcasted_iota(jnp.int32, (n_rows, 2 * CHUNK), 1)
    dist = qi + CHUNK - kc
    allowed = (dist >= 0) & (dist < CHUNK) & ((kc >= CHUNK) | (blk > 0))
    outs = []
    for kvh in range(B_KV_HEADS):
        ks = slice(kvh * B_DH, (kvh + 1) * B_DH)
        vs = slice(B_KV_HEADS * B_DH + kvh * B_DH, B_KV_HEADS * B_DH + (kvh + 1) * B_DH)
        kk = jnp.concatenate([kvp[:, ks], kvc[:, ks]], axis=0).astype(BF16)
        vv = jnp.concatenate([kvp[:, vs], kvc[:, vs]], axis=0).astype(BF16)
        q4 = jnp.concatenate(
            [q[:, (kvh * B_GQA + g) * B_DH:(kvh * B_GQA + g + 1) * B_DH] for g in range(B_GQA)], axis=0)
        s = _dot_nt(q4.astype(BF16), kk) * (B_DH ** -0.5)
        s = jnp.where(allowed, s, NEG_BIG)
        sk = _sink_column(sink_ref, kvh, CHUNK, n_rows)
        mx = jnp.maximum(jnp.max(s, axis=-1, keepdims=True), sk)
        p = jnp.exp(s - mx)
        p = p / (jnp.sum(p, axis=-1, keepdims=True) + jnp.exp(sk - mx))
        o = jnp.dot(p.astype(BF16), vv, preferred_element_type=F32)
        outs += [o[g * CHUNK:(g + 1) * CHUNK] for g in range(B_GQA)]
    o_ref[:, A_WIDTH:] = jnp.concatenate(outs, axis=1)


def _even_prompt(z, ws, bs_full, sink):
    t = z.shape[0]
    nb = t // CHUNK
    wide = A_WIDTH
    kvw = 2 * B_KV_HEADS * B_DH
    kv_blk = (2 * A_WIDTH + B_HEADS * B_DH) // kvw
    return pl.pallas_call(
        _even_prompt_body, grid=(nb,),
        in_specs=[
            pl.BlockSpec(memory_space=pltpu.SMEM),
            pl.BlockSpec((CHUNK, wide), lambda i: (i, 0)),
            pl.BlockSpec((CHUNK, wide), lambda i: (i, 1)),
            pl.BlockSpec((CHUNK, wide), lambda i: (i, 2)),
            pl.BlockSpec((CHUNK, kvw), lambda i: (i, kv_blk)),
            pl.BlockSpec((CHUNK, kvw), lambda i: (jnp.maximum(i - 1, 0), kv_blk)),
            pl.BlockSpec((A_GROUPS, CHUNK, CHUNK), lambda i: (0, 0, 0)),
            pl.BlockSpec((A_GROUPS, CHUNK, LANES), lambda i: (0, 0, 0)),
        ],
        out_specs=pl.BlockSpec((CHUNK, D_MODEL), lambda i: (i, 0)),
        out_shape=jax.ShapeDtypeStruct((t, D_MODEL), F32),
        compiler_params=_params(("parallel",)), name="even_prompt",
    )(sink, z, z, z, z, z, ws, bs_full)


EVEN_SAMPLE_BATCH = 8
T_PAD = 8


def _even_sample_body(sink_ref, au_ref, av_ref, q_ref, kv_ref, kb_ref, vb_ref, wa_ref, wb_ref,
                      y_ref, avo_ref):
    n_rows = B_GQA * T_PAD
    tq = lax.broadcasted_iota(jnp.int32, (n_rows, CHUNK), 0) % T_PAD
    kc = lax.broadcasted_iota(jnp.int32, (n_rows, CHUNK), 1)
    buf_allowed = kc > tq
    tq1 = lax.broadcasted_iota(jnp.int32, (n_rows, 1), 0) % T_PAD
    n_new = wa_ref.shape[0]
    for b in range(EVEN_SAMPLE_BATCH):
        au = _gelu(au_ref[b])
        av = _gelu(av_ref[b])
        avo_ref[b] = av
        mixed = wb_ref[...]
        for j in range(n_new):
            mixed = mixed + wa_ref[j] * av[j:j + 1, :]
        y_ref[b, :, :A_WIDTH] = au * mixed

        q = q_ref[b]
        kv = kv_ref[b]
        kb = kb_ref[b]
        vb = vb_ref[b]
        outs = []
        for kvh in range(B_KV_HEADS):
            ks = slice(kvh * B_DH, (kvh + 1) * B_DH)
            vs = slice(B_KV_HEADS * B_DH + kvh * B_DH, B_KV_HEADS * B_DH + (kvh + 1) * B_DH)
            q4 = jnp.concatenate(
                [q[:, (kvh * B_GQA + g) * B_DH:(kvh * B_GQA + g + 1) * B_DH] for g in range(B_GQA)], axis=0)
            scale = B_DH ** -0.5
            s_buf = _dot_nt(q4.astype(BF16), kb[:, ks].astype(BF16)) * scale
            s_buf = jnp.where(buf_allowed, s_buf, NEG_BIG)
            q4r = q4.astype(BF16).astype(F32)
            s_new = []
            for j in range(n_new):
                kj = kv[j:j + 1, ks].astype(BF16).astype(F32)
                sj = jnp.sum(q4r * kj, axis=-1, keepdims=True) * scale
                s_new.append(jnp.where(tq1 >= j, sj, NEG_BIG))
            sk = _sink_column(sink_ref, kvh, T_PAD, n_rows)
            mx = jnp.maximum(jnp.max(s_buf, axis=-1, keepdims=True), sk)
            for sj in s_new:
                mx = jnp.maximum(mx, sj)
            p_buf = jnp.exp(s_buf - mx)
            p_new = [jnp.exp(sj - mx) for sj in s_new]
            den = jnp.sum(p_buf, axis=-1, keepdims=True) + jnp.exp(sk - mx)
            for pj in p_new:
                den = den + pj
            inv = 1.0 / den
            o = jnp.dot((p_buf * inv).astype(BF16), vb[:, ks].astype(BF16), preferred_element_type=F32)
            for j in range(n_new):
                vj = kv[j:j + 1, vs].astype(BF16).astype(F32)
                o = o + (p_new[j] * inv).astype(BF16).astype(F32) * vj
            outs += [o[g * T_PAD:(g + 1) * T_PAD] for g in range(B_GQA)]
        y_ref[b, :, A_WIDTH:] = jnp.concatenate(outs, axis=1)


def _even_sample(z3, kbuf, vbuf, wa, wb, sink):
    nb = z3.shape[0]
    bb = EVEN_SAMPLE_BATCH
    kvw = 2 * B_KV_HEADS * B_DH
    kv_blk = (2 * A_WIDTH + B_HEADS * B_DH) // kvw
    win = kbuf.shape[1]
    return pl.pallas_call(
        _even_sample_body, grid=(nb // bb,),
        in_specs=[
            pl.BlockSpec(memory_space=pltpu.SMEM),
            pl.BlockSpec((bb, T_PAD, A_WIDTH), lambda i: (i, 0, 0)),
            pl.BlockSpec((bb, T_PAD, A_WIDTH), lambda i: (i, 0, 1)),
            pl.BlockSpec((bb, T_PAD, A_WIDTH), lambda i: (i, 0, 2)),
            pl.BlockSpec((bb, T_PAD, kvw), lambda i: (i, 0, kv_blk)),
            pl.BlockSpec((bb, win, kvw // 2), lambda i: (i, 0, 0)),
            pl.BlockSpec((bb, win, kvw // 2), lambda i: (i, 0, 0)),
            pl.BlockSpec(wa.shape, lambda i: (0, 0, 0)),
            pl.BlockSpec(wb.shape, lambda i: (0, 0)),
        ],
        out_specs=[
            pl.BlockSpec((bb, T_PAD, D_MODEL), lambda i: (i, 0, 0)),
            pl.BlockSpec((bb, T_PAD, A_WIDTH), lambda i: (i, 0, 0)),
        ],
        out_shape=[
            jax.ShapeDtypeStruct((nb, T_PAD, D_MODEL), F32),
            jax.ShapeDtypeStruct((nb, T_PAD, A_WIDTH), F32),
        ],
        compiler_params=_params(("parallel",)), name="even_sample",
    )(sink, z3, z3, z3, z3, kbuf, vbuf, wa, wb)


def _rope(x, cos_f, sin_s):
    width = x.shape[-1]
    half = C_DK // 2
    lane = lax.broadcasted_iota(jnp.int32, x.shape, 1) % C_DK
    swapped = jnp.where(lane < half, pltpu.roll(x, width - half, 1), pltpu.roll(x, half, 1))
    return x * cos_f + swapped * sin_s


def _groupnorm_gate(o, gain, gate):
    mu = jnp.mean(o, axis=-1, keepdims=True)
    var = jnp.mean(jnp.square(o - mu), axis=-1, keepdims=True)
    return (o - mu) * lax.rsqrt(var + EPS) * gain * (gate * jax.nn.sigmoid(gate))


def _retention_body(qk_ref, v_ref, g_ref, cos_ref, sin_ref, decay_ref, qdec_ref, kdec_ref, sdec_ref,
                    gain_ref, o_ref, st_ref, state):
    @pl.when(pl.program_id(0) == 0)
    def _():
        state[...] = jnp.zeros_like(state)

    qk = qk_ref[...]
    width = C_HEADS * C_DK
    q = _rope(qk[:, :width], cos_ref[...], sin_ref[...]) * (C_DK ** -0.5)
    k = _rope(qk[:, width:], cos_ref[...], sin_ref[...])
    v = v_ref[...]
    g = g_ref[...]
    for h in range(C_HEADS):
        qh = q[:, h * C_DK:(h + 1) * C_DK]
        kh = k[:, h * C_DK:(h + 1) * C_DK]
        vh = v[:, h * C_DV:(h + 1) * C_DV]
        st = state[h]
        scores = _dot_nt(qh, kh, HIGHEST) * decay_ref[h]
        o = jnp.dot(scores, vh, precision=HIGHEST, preferred_element_type=F32)
        o = o + jnp.dot(qh, st, precision=HIGHEST, preferred_element_type=F32) * qdec_ref[h]
        new_st = sdec_ref[h] * st + _dot_tn(kh * kdec_ref[h], vh, HIGHEST)
        state[h] = new_st
        st_ref[h] = new_st
        lanes = slice(h * C_DV, (h + 1) * C_DV)
        o_ref[:, lanes] = _groupnorm_gate(o, gain_ref[:, lanes], g[:, lanes])


def _retention_prompt(z, cos_f, sin_s, tabs, gain):
    t = z.shape[0]
    nc = t // CHUNK
    decay, qdec, kdec, sdec = tabs
    width = C_HEADS * C_DV
    const3 = lambda i: (0, 0, 0)
    return pl.pallas_call(
        _retention_body, grid=(nc,),
        in_specs=[
            pl.BlockSpec((CHUNK, width), lambda i: (i, 0)),
            pl.BlockSpec((CHUNK, width), lambda i: (i, 1)),
            pl.BlockSpec((CHUNK, width), lambda i: (i, 2)),
            pl.BlockSpec((CHUNK, C_HEADS * C_DK), lambda i: (i, 0)),
            pl.BlockSpec((CHUNK, C_HEADS * C_DK), lambda i: (i, 0)),
            pl.BlockSpec(decay.shape, const3),
            pl.BlockSpec(qdec.shape, const3),
            pl.BlockSpec(kdec.shape, const3),
            pl.BlockSpec(sdec.shape, const3),
            pl.BlockSpec((1, width), lambda i: (0, 0)),
        ],
        out_specs=[
            pl.BlockSpec((CHUNK, width), lambda i: (i, 0)),
            pl.BlockSpec((C_HEADS, C_DK, C_DV), const3),
        ],
        out_shape=[
            jax.ShapeDtypeStruct((t, width), F32),
            jax.ShapeDtypeStruct((C_HEADS, C_DK, C_DV), F32),
        ],
        scratch_shapes=[pltpu.VMEM((C_HEADS, C_DK, C_DV), F32)],
        compiler_params=_params(("arbitrary",)), name="retention_prompt",
    )(z, z, z, cos_f, sin_s, decay, qdec, kdec, sdec, gain.reshape(1, width))


RET_SAMPLE_BATCH = 8


def _retention_sample_body(qk_ref, v_ref, g_ref, cos_ref, sin_ref, decay_ref, qdec_ref, kdec_ref,
                           sdec_ref, gain_ref, st_in_ref, o_ref, st_ref):
    width = C_HEADS * C_DK
    for b in range(RET_SAMPLE_BATCH):
        qk = qk_ref[b]
        q = _rope(qk[:, :width], cos_ref[...], sin_ref[...]) * (C_DK ** -0.5)
        k = _rope(qk[:, width:], cos_ref[...], sin_ref[...])
        v = v_ref[b]
        g = g_ref[b]
        for h in range(C_HEADS):
            qh = q[:, h * C_DK:(h + 1) * C_DK]
            kh = k[:, h * C_DK:(h + 1) * C_DK]
            vh = v[:, h * C_DV:(h + 1) * C_DV]
            st = st_in_ref[b, h]
            scores = _dot_nt(qh, kh, HIGHEST) * decay_ref[h]
            o = jnp.dot(scores, vh, precision=HIGHEST, preferred_element_type=F32)
            o = o + jnp.dot(qh, st, precision=HIGHEST, preferred_element_type=F32) * qdec_ref[h]
            st_ref[b, h] = sdec_ref[h] * st + _dot_tn(kh * kdec_ref[h], vh, HIGHEST)
            lanes = slice(h * C_DV, (h + 1) * C_DV)
            o_ref[b, :, lanes] = _groupnorm_gate(o, gain_ref[:, lanes], g[:, lanes])


def _retention_sample(z3, cos_f, sin_s, tabs, gain, st_in):
    nb = z3.shape[0]
    bb = RET_SAMPLE_BATCH
    decay, qdec, kdec, sdec = tabs
    width = C_HEADS * C_DV
    const3 = lambda i: (0, 0, 0)
    return pl.pallas_call(
        _retention_sample_body, grid=(nb // bb,),
        in_specs=[
            pl.BlockSpec((bb, T_PAD, width), lambda i: (i, 0, 0)),
            pl.BlockSpec((bb, T_PAD, width), lambda i: (i, 0, 1)),
            pl.BlockSpec((bb, T_PAD, width), lambda i: (i, 0, 2)),
            pl.BlockSpec((T_PAD, C_HEADS * C_DK), lambda i: (0, 0)),
            pl.BlockSpec((T_PAD, C_HEADS * C_DK), lambda i: (0, 0)),
            pl.BlockSpec(decay.shape, const3),
            pl.BlockSpec(qdec.shape, const3),
            pl.BlockSpec(kdec.shape, const3),
            pl.BlockSpec(sdec.shape, const3),
            pl.BlockSpec((1, width), lambda i: (0, 0)),
            pl.BlockSpec((bb, C_HEADS, C_DK, C_DV), lambda i: (i, 0, 0, 0)),
        ],
        out_specs=[
            pl.BlockSpec((bb, T_PAD, width), lambda i: (i, 0, 0)),
            pl.BlockSpec((bb, C_HEADS, C_DK, C_DV), lambda i: (i, 0, 0, 0)),
        ],
        out_shape=[
            jax.ShapeDtypeStruct((nb, T_PAD, width), F32),
            jax.ShapeDtypeStruct((nb, C_HEADS, C_DK, C_DV), F32),
        ],
        compiler_params=_params(("parallel",)), name="retention_sample",
    )(z3, z3, z3, cos_f, sin_s, decay, qdec, kdec, sdec, gain.reshape(1, width), st_in)


def _retention_tables(length, n_valid):
    log_gamma = jnp.log1p(-jnp.exp2(-RET_DECAY_EXP0 - jnp.arange(C_HEADS, dtype=F32)))
    i = jnp.arange(length, dtype=F32)
    valid = (jnp.arange(length) < n_valid)
    rel = i[:, None] - i[None, :]
    decay = jnp.where(rel >= 0, jnp.exp(jnp.maximum(rel, 0.0)[None] * log_gamma[:, None, None]), 0.0)
    decay = jnp.where(valid[None, None, :], decay, 0.0)
    q_dec = jnp.exp((i[None, :] + 1.0) * log_gamma[:, None])
    k_dec = jnp.where(valid[None, :], jnp.exp((n_valid - 1.0 - i)[None, :] * log_gamma[:, None]), 0.0)
    s_dec = jnp.exp(n_valid * log_gamma)
    qdec = jnp.broadcast_to(q_dec[:, :, None], (C_HEADS, length, C_DV))
    kdec = jnp.broadcast_to(k_dec[:, :, None], (C_HEADS, length, C_DK))
    sdec = jnp.broadcast_to(s_dec[:, None, None], (C_HEADS, C_DK, C_DV))
    return decay, qdec, kdec, sdec


def _rope_tables(pos):
    half = C_DK // 2
    freqs = ROPE_BASE ** (-jnp.arange(half, dtype=F32) / half)
    ang = pos.astype(F32)[:, None] * freqs[None, :]
    cos, sin = jnp.cos(ang), jnp.sin(ang)
    cos_f = jnp.tile(jnp.concatenate([cos, cos], axis=1), (1, C_HEADS))
    sin_s = jnp.tile(jnp.concatenate([-sin, sin], axis=1), (1, C_HEADS))
    return cos_f, sin_s


def _s5_discretize(are_ref, aim_ref, ldt_ref):
    a_re, a_im = are_ref[...], aim_ref[...]
    dt = jnp.exp(ldt_ref[...])
    mag = jnp.exp(a_re * dt)
    ab_re = mag * jnp.cos(a_im * dt)
    ab_im = mag * jnp.sin(a_im * dt)
    num_re, num_im = ab_re - 1.0, ab_im
    den = a_re * a_re + a_im * a_im
    co_re = (num_re * a_re + num_im * a_im) / den
    co_im = (num_im * a_re - num_re * a_im) / den
    return ab_re, ab_im, co_re, co_im


S5_CHUNK = 256


def _split_bf16(x):
    hi = x.astype(BF16)
    return hi, (x - hi.astype(F32)).astype(BF16)


def _dot_split(x, w_hi, w_lo):
    x_hi, x_lo = _split_bf16(x)
    return (jnp.dot(x_hi, w_hi, preferred_element_type=F32)
            + jnp.dot(x_lo, w_hi, preferred_element_type=F32)
            + jnp.dot(x_hi, w_lo, preferred_element_type=F32))


def _s5_prompt_body(u_ref, are_ref, aim_ref, ldt_ref, bre_hi_ref, bre_lo_ref, bim_hi_ref, bim_lo_ref,
                    cre_ref, cim_ref, dd_ref, y_ref, st_ref, hre, him, state, disc):
    n = S5_CHUNK

    @pl.when(pl.program_id(0) == 0)
    def _():
        ab_re, ab_im, co_re, co_im = _s5_discretize(are_ref, aim_ref, ldt_ref)
        disc[0] = ab_re
        disc[1] = ab_im
        disc[2] = co_re
        disc[3] = co_im
        state[...] = jnp.zeros_like(state)

    n_chunks = D_ROW_STATE // LANES
    for s in range(D_ROWS):
        rows = pl.ds(s, n, stride=D_ROWS)
        u_s = u_ref[:, s * D_ROW_CH:(s + 1) * D_ROW_CH]
        raw_re = _dot_split(u_s, bre_hi_ref[s], bre_lo_ref[s])
        raw_im = _dot_split(u_s, bim_hi_ref[s], bim_lo_ref[s])
        co_re = disc[2, s:s + 1, :]
        co_im = disc[3, s:s + 1, :]
        bu_re = co_re * raw_re - co_im * raw_im
        bu_im = co_re * raw_im + co_im * raw_re
        for c in range(n_chunks):
            hre[c, rows, :] = bu_re[:, c * LANES:(c + 1) * LANES]
            him[c, rows, :] = bu_im[:, c * LANES:(c + 1) * LANES]

    ab_re = disc[0]
    ab_im = disc[1]

    def step(t, carry):
        h_re, h_im = carry
        rows = pl.ds(pl.multiple_of(t * D_ROWS, D_ROWS), D_ROWS)
        bu_re = jnp.concatenate([hre[c, rows, :] for c in range(n_chunks)], axis=1)
        bu_im = jnp.concatenate([him[c, rows, :] for c in range(n_chunks)], axis=1)
        n_re = ab_re * h_re - ab_im * h_im + bu_re
        n_im = ab_re * h_im + ab_im * h_re + bu_im
        for c in range(n_chunks):
            hre[c, rows, :] = n_re[:, c * LANES:(c + 1) * LANES]
            him[c, rows, :] = n_im[:, c * LANES:(c + 1) * LANES]
        return n_re, n_im

    h_re, h_im = lax.fori_loop(0, n, step, (state[0], state[1]), unroll=4)
    state[0] = h_re
    state[1] = h_im
    st_ref[0] = h_re
    st_ref[1] = h_im

    for s in range(D_ROWS):
        rows = pl.ds(s, n, stride=D_ROWS)
        hist_re = jnp.concatenate([hre[c, rows, :] for c in range(n_chunks)], axis=1)
        hist_im = jnp.concatenate([him[c, rows, :] for c in range(n_chunks)], axis=1)
        cols = slice(s * D_ROW_CH, (s + 1) * D_ROW_CH)
        y = (jnp.dot(hist_re.astype(BF16), cre_ref[s], preferred_element_type=F32)
             - jnp.dot(hist_im.astype(BF16), cim_ref[s], preferred_element_type=F32)
             + dd_ref[s:s + 1, :] * u_ref[:, cols])
        y_ref[:, cols] = _gelu(y)


def _s5_prompt(z, u_col_block, par):
    a_re, a_im, ldt, b_re, b_im, c_re, c_im, dd = par
    b_re_hi, b_re_lo = _split_bf16(b_re)
    b_im_hi, b_im_lo = _split_bf16(b_im)
    t = z.shape[0]
    blk = S5_CHUNK * D_ROWS
    width = D_ROWS * D_ROW_CH
    c2 = lambda i: (0, 0)
    c3 = lambda i: (0, 0, 0)
    return pl.pallas_call(
        _s5_prompt_body, grid=(t // S5_CHUNK,),
        in_specs=[
            pl.BlockSpec((S5_CHUNK, width), lambda i: (i, u_col_block)),
            pl.BlockSpec(a_re.shape, c2), pl.BlockSpec(a_im.shape, c2), pl.BlockSpec(ldt.shape, c2),
            pl.BlockSpec(b_re.shape, c3), pl.BlockSpec(b_re.shape, c3),
            pl.BlockSpec(b_im.shape, c3), pl.BlockSpec(b_im.shape, c3),
            pl.BlockSpec(c_re.shape, c3), pl.BlockSpec(c_im.shape, c3),
            pl.BlockSpec(dd.shape, c2),
        ],
        out_specs=[
            pl.BlockSpec((S5_CHUNK, width), lambda i: (i, 0)),
            pl.BlockSpec((2, D_ROWS, D_ROW_STATE), c3),
        ],
        out_shape=[
            jax.ShapeDtypeStruct((t, width), F32),
            jax.ShapeDtypeStruct((2, D_ROWS, D_ROW_STATE), F32),
        ],
        scratch_shapes=[
            pltpu.VMEM((D_ROW_STATE // LANES, blk, LANES), F32),
            pltpu.VMEM((D_ROW_STATE // LANES, blk, LANES), F32),
            pltpu.VMEM((2, D_ROWS, D_ROW_STATE), F32), pltpu.VMEM((4, D_ROWS, D_ROW_STATE), F32),
        ],
        compiler_params=_params(("arbitrary",)), name="s5_prompt",
    )(z, a_re, a_im, ldt, b_re_hi, b_re_lo, b_im_hi, b_im_lo, c_re.astype(BF16), c_im.astype(BF16), dd)


def _s5_sample_body(u_ref, xre_ref, xim_ref, are_ref, aim_ref, ldt_ref, bre_ref, bim_ref, cre_ref,
                    cim_ref, dd_ref, y_ref, sre_ref, sim_ref):
    ab_re, ab_im, co_re, co_im = _s5_discretize(are_ref, aim_ref, ldt_ref)
    n_t = u_ref.shape[0]
    for s in range(D_ROWS):
        a_r, a_i = ab_re[s:s + 1, :], ab_im[s:s + 1, :]
        c_r, c_i = co_re[s:s + 1, :], co_im[s:s + 1, :]
        h_re, h_im = xre_ref[s], xim_ref[s]
        for t in range(n_t):
            u = u_ref[t, s]
            raw_re = jnp.dot(u, bre_ref[s], precision=HIGHEST, preferred_element_type=F32)
            raw_im = jnp.dot(u, bim_ref[s], precision=HIGHEST, preferred_element_type=F32)
            bu_re = c_r * raw_re - c_i * raw_im
            bu_im = c_r * raw_im + c_i * raw_re
            h_re, h_im = a_r * h_re - a_i * h_im + bu_re, a_r * h_im + a_i * h_re + bu_im
            y = (jnp.dot(h_re, cre_ref[s], precision=HIGHEST, preferred_element_type=F32)
                 - jnp.dot(h_im, cim_ref[s], precision=HIGHEST, preferred_element_type=F32)
                 + dd_ref[s:s + 1, :] * u)
            y_ref[t, s] = _gelu(y)
        sre_ref[s] = h_re
        sim_ref[s] = h_im


def _s5_sample(u_ts, x_re, x_im, par):
    a_re, a_im, ldt, b_re, b_im, c_re, c_im, dd = par
    return pl.pallas_call(
        _s5_sample_body,
        out_shape=[
            jax.ShapeDtypeStruct(u_ts.shape, F32),
            jax.ShapeDtypeStruct(x_re.shape, F32),
            jax.ShapeDtypeStruct(x_im.shape, F32),
        ],
        compiler_params=pltpu.CompilerParams(vmem_limit_bytes=VMEM_LIMIT), name="s5_sample",
    )(u_ts, x_re, x_im, a_re, a_im, ldt, b_re, b_im, c_re, c_im, dd)


def _s5_params(a_re, a_im, log_dt, b_re, b_im, c_re, c_im, dd):
    eye = jnp.eye(D_ROWS, dtype=F32)

    def rows(a):
        return a.reshape(D_ROWS, D_ROW_STATE)

    def b_blocks(b):
        b4 = b.reshape(D_ROWS, D_ROWS, D_STATE, D_GROUP_CH).transpose(0, 1, 3, 2)
        return jnp.einsum('sgcp,gh->sgchp', b4, eye).reshape(D_ROWS, D_ROW_CH, D_ROW_STATE)

    def c_blocks(c):
        c4 = c.reshape(D_ROWS, D_ROWS, D_GROUP_CH, D_STATE)
        return jnp.einsum('sgcp,gh->sgphc', c4, eye).reshape(D_ROWS, D_ROW_STATE, D_ROW_CH)

    ldt = jnp.broadcast_to(log_dt[:, None], (D_GROUPS, D_STATE))
    return (rows(a_re), rows(a_im), rows(ldt), b_blocks(b_re), b_blocks(b_im),
            c_blocks(c_re), c_blocks(c_im), dd.reshape(D_ROWS, D_ROW_CH))


def _glu_body(x_ref, w_ref, b_ref, o_ref):
    x = x_ref[...]
    gate = jnp.dot(x.astype(BF16), w_ref[...], preferred_element_type=F32) + b_ref[...]
    o_ref[...] = x * jax.nn.sigmoid(gate)


def _glu(x, w, b, tm=512):
    m, k = x.shape
    return pl.pallas_call(
        _glu_body, grid=(m // tm,),
        in_specs=[pl.BlockSpec((tm, k), lambda i: (i, 0)), pl.BlockSpec((k, k), lambda i: (0, 0)),
                  pl.BlockSpec((1, k), lambda i: (0, 0))],
        out_specs=pl.BlockSpec((tm, k), lambda i: (i, 0)),
        out_shape=jax.ShapeDtypeStruct((m, k), F32),
        compiler_params=_params(("parallel",)), name="glu",
    )(x, w, b.reshape(1, k))


PEER_SEL_TOKENS = 256


def _extract_top(s, n_out, exact):
    n_rows = s.shape[0]
    rid = lax.broadcasted_iota(jnp.int32, s.shape, 0).astype(F32) if exact else None
    vals = []
    for _ in range(n_out):
        m = jnp.max(s, axis=0, keepdims=True)
        if exact:
            first = jnp.min(jnp.where(s == m, rid, float(n_rows)), axis=0, keepdims=True)
            s = jnp.where(rid == first, NEG_INF, s)
        else:
            s = jnp.where(s == m, NEG_INF, s)
        vals.append(m)
    return vals, s


def _removed_count(s):
    return jnp.sum(jnp.where(s == NEG_INF, 1.0, 0.0), axis=0, keepdims=True)


_CAND_COUNTS = [PEER_TOPK // (a + 1) for a in range(PEER_TOPK)]


def _candidates(v1, v2):
    v1_all = _stack_rows(v1)
    v2_all = _stack_rows(v2)
    rid = lax.broadcasted_iota(jnp.int32, (SUBLANES, v1[0].shape[1]), 0)
    tiles = [v1[0] + v2_all]
    n_pad = 0
    a = 1
    while _CAND_COUNTS[a] > 1:
        tiles.append(jnp.where(rid < _CAND_COUNTS[a], v1[a] + v2_all[:SUBLANES], NEG_INF))
        n_pad += SUBLANES - _CAND_COUNTS[a]
        a += 1
    assert PEER_TOPK - a == SUBLANES
    tiles.append(v1_all[a:] + v2[0])
    return jnp.concatenate(tiles, axis=0), n_pad


def _stack_rows(rows):
    n = len(rows)
    rid = lax.broadcasted_iota(jnp.int32, (n, rows[0].shape[1]), 0)
    out = jnp.broadcast_to(rows[n - 1], rid.shape)
    for i in range(n - 2, -1, -1):
        out = jnp.where(rid == i, rows[i], out)
    return out


def _peer_select_body(q_ref, k1_ref, k2_ref, th_ref, g1_ref, s2_ref, e2_ref):
    def head(h, exact):
        q1 = q_ref[:, (2 * h) * PEER_HALF:(2 * h + 1) * PEER_HALF]
        q2 = q_ref[:, (2 * h + 1) * PEER_HALF:(2 * h + 2) * PEER_HALF]
        s1 = _dot_nt(k1_ref[h], q1, HIGHEST)
        s2 = _dot_nt(k2_ref[h], q2, HIGHEST)
        v1, r1 = _extract_top(s1, PEER_TOPK, exact)
        v2, r2 = _extract_top(s2, PEER_TOPK, exact)
        cand, n_pad = _candidates(v1, v2)
        top, rc = _extract_top(cand, PEER_TOPK, exact)
        mx = top[0]
        z = jnp.exp(top[0] - mx)
        for kk in range(1, PEER_TOPK):
            z = z + jnp.exp(top[kk] - mx)
        e1 = jnp.where(r1 == NEG_INF, jnp.exp(s1 - v1[0]), 0.0)
        e2 = jnp.where(r2 == NEG_INF, jnp.exp(s2 - v2[0]), 0.0)
        g1 = e1 / z
        tau = top[PEER_TOPK - 1]
        th = jnp.full(s1.shape, jnp.inf, F32)
        for b in range(PEER_TOPK):
            th = jnp.where(s1 + v2[b] >= tau, v2[b], th)
        for j in range(PEER_SEL_TOKENS // LANES):
            lanes = slice(j * LANES, (j + 1) * LANES)
            th_ref[j, h] = th[:, lanes]
            g1_ref[j, h] = g1[:, lanes]
            s2_ref[j, h] = s2[:, lanes]
            e2_ref[j, h] = e2[:, lanes]
        if exact:
            return None
        ok = ((_removed_count(r1) == PEER_TOPK) & (_removed_count(r2) == PEER_TOPK)
              & (_removed_count(rc) == PEER_TOPK + n_pad))
        return jnp.where(ok, 0.0, 1.0)

    repeated = head(0, False)
    for h in range(1, PEER_HEADS):
        repeated = jnp.maximum(repeated, head(h, False))

    @pl.when(jnp.max(repeated) > 0.0)
    def _():
        for h in range(PEER_HEADS):
            head(h, True)


def _peer_select(q, k1, k2):
    t = q.shape[0]
    tb = PEER_SEL_TOKENS
    nj = tb // LANES
    big = pl.BlockSpec((nj, PEER_HEADS, PEER_NKEYS, LANES), lambda i: (i, 0, 0, 0))
    big_shape = jax.ShapeDtypeStruct((t // LANES, PEER_HEADS, PEER_NKEYS, LANES), F32)
    return pl.pallas_call(
        _peer_select_body, grid=(t // tb,),
        in_specs=[
            pl.BlockSpec((tb, q.shape[1]), lambda i: (i, 0)),
            pl.BlockSpec(k1.shape, lambda i: (0, 0, 0)),
            pl.BlockSpec(k2.shape, lambda i: (0, 0, 0)),
        ],
        out_specs=[big, big, big, big],
        out_shape=[big_shape, big_shape, big_shape, big_shape],
        compiler_params=_params(("parallel",)), name="peer_select",
    )(q, k1, k2)


PEER_TOKENS = 512
PEER_EXPERT_TILE = 512
PEER_SLABS = PEER_EXPERT_TILE // PEER_NKEYS

def _peer_dense_body(h_ref, g_ref, u_ref, v_ref, th_ref, g1_ref, s2_ref, e2_ref, o_ref,
                     xn, act_s, coef_s, acc):
    e = pl.program_id(1)
    n_chunks = PEER_TOKENS // LANES

    @pl.when(e == 0)
    def _():
        xn[...] = _rmsnorm(h_ref[...], g_ref[...]).astype(BF16)
        acc[...] = jnp.zeros_like(acc)

    act = _dot_nt(u_ref[...], xn[...])
    for j in range(n_chunks):
        act_s[j] = act[:, j * LANES:(j + 1) * LANES]

    def chunk(j, carry):
        for c in range(PEER_SLABS):
            gate = jnp.zeros((PEER_NKEYS, LANES), F32)
            for h in range(PEER_HEADS):
                sel = s2_ref[j, h] >= th_ref[j, h, c:c + 1, :]
                gate = gate + jnp.where(sel, e2_ref[j, h] * g1_ref[j, h, c:c + 1, :], 0.0)
            rows = slice(c * PEER_NKEYS, (c + 1) * PEER_NKEYS)
            coef_s[j, rows, :] = (gate * _gelu(act_s[j, rows, :])).astype(BF16)
        return carry

    lax.fori_loop(0, n_chunks, chunk, 0)
    coef = jnp.concatenate([coef_s[j] for j in range(n_chunks)], axis=1)
    acc[...] += _dot_tn(coef, v_ref[...])

    @pl.when(e == pl.num_programs(1) - 1)
    def _():
        o_ref[...] = h_ref[...] + acc[...]


def _peer_dense(h, g, u_tab, v_tab, sel):
    th, g1, s2, e2 = sel
    t, d = h.shape
    n_exp = u_tab.shape[0]
    tb, te = PEER_TOKENS, PEER_EXPERT_TILE
    nj = tb // LANES
    n_tiles = n_exp // te

    def by_tile(a):
        a5 = a.reshape(a.shape[0], PEER_HEADS, n_tiles, PEER_SLABS, LANES)
        return a5.transpose(2, 0, 1, 3, 4)

    tile_spec = pl.BlockSpec((None, nj, PEER_HEADS, PEER_SLABS, LANES), lambda i, e: (e, i, 0, 0, 0))
    tok_spec = pl.BlockSpec((nj, PEER_HEADS, PEER_NKEYS, LANES), lambda i, e: (i, 0, 0, 0))
    return pl.pallas_call(
        _peer_dense_body, grid=(t // tb, n_tiles),
        in_specs=[
            pl.BlockSpec((tb, d), lambda i, e: (i, 0)),
            pl.BlockSpec((1, d), lambda i, e: (0, 0)),
            pl.BlockSpec((te, d), lambda i, e: (e, 0)),
            pl.BlockSpec((te, d), lambda i, e: (e, 0)),
            tile_spec, tile_spec, tok_spec, tok_spec,
        ],
        out_specs=pl.BlockSpec((tb, d), lambda i, e: (i, 0)),
        out_shape=jax.ShapeDtypeStruct((t, d), F32),
        scratch_shapes=[pltpu.VMEM((tb, d), BF16), pltpu.VMEM((nj, te, LANES), F32),
                        pltpu.VMEM((nj, te, LANES), BF16), pltpu.VMEM((tb, d), F32)],
        compiler_params=_params(("parallel", "arbitrary")), name="peer_dense",
    )(h, g.reshape(1, d), u_tab, v_tab, by_tile(th), by_tile(g1), s2, e2)


def _peer(h, g, wq, k1, k2, u_tab, v_tab):
    q = _matmul(h, wq, norm_g=g)
    return _peer_dense(h, g, u_tab, v_tab, _peer_select(q, k1, k2))


def kernel(x_prompt, x_sample, state_b_k, state_b_v, state_c_s, state_d_re, state_d_im, norm1_g, norm2_g, final_g, w_in_even, w_out_even, a_ws, a_bs, b_sink, w_in_odd, w_out_odd, c_norm_g, d_a_re, d_a_im, d_log_dt, d_b_re, d_b_im, d_c_re, d_c_im, d_d, d_glu_w, d_glu_b, peer_wq, peer_k1, peer_k2, peer_u, peer_v):
    seq = x_prompt.shape[1]
    n_batch, n_new = x_sample.shape[:2]
    past = PAST_LEN
    hp = x_prompt.reshape(seq, D_MODEL)
    hs = x_sample.reshape(n_batch * n_new, D_MODEL)
    bf = lambda a: a.astype(BF16)

    def pad_tokens(a):
        a3 = a.reshape(n_batch, n_new, a.shape[-1])
        return jnp.pad(a3, ((0, 0), (0, T_PAD - n_new), (0, 0)))

    def unpad_tokens(a3):
        return a3[:, :n_new].reshape(n_batch * n_new, a3.shape[-1])

    w_in = bf(w_in_even[0])
    w_out = bf(w_out_even[0])
    zp = _matmul(hp, w_in, norm_g=norm1_g[0])
    zs = _matmul(hs, w_in, norm_g=norm1_g[0])
    bs_full = jnp.broadcast_to(a_bs[0][:, :, None], (A_GROUPS, CHUNK, LANES))
    yp = _even_prompt(zp, a_ws[0], bs_full, b_sink[0])
    ws_small = jnp.tril(a_ws[0][:, :n_new, :n_new]).transpose(2, 1, 0)
    wa = jnp.repeat(jnp.pad(ws_small, ((0, 0), (0, T_PAD - n_new), (0, 0))), LANES, axis=-1)
    wb = jnp.repeat(jnp.pad(a_bs[0][:, :n_new].T, ((0, T_PAD - n_new), (0, 0))), LANES, axis=-1)
    win = state_b_k.shape[2]
    kbuf = state_b_k[0].reshape(n_batch, win, B_KV_HEADS * B_DH)
    vbuf = state_b_v[0].reshape(n_batch, win, B_KV_HEADS * B_DH)
    ys3, av3 = _even_sample(pad_tokens(zs), kbuf, vbuf, wa, wb, b_sink[0])
    hp = _matmul(yp, w_out, resid=hp)
    hs = _matmul(unpad_tokens(ys3), w_out, resid=hs)

    k_off = 2 * A_WIDTH + B_HEADS * B_DH
    v_off = k_off + B_KV_HEADS * B_DH
    kv_shape = (1, -1, win, B_KV_HEADS, B_DH)
    a_v_sample = av3[:, :n_new].reshape(1, n_batch, n_new, A_GROUPS, A_WIDTH // A_GROUPS)
    b_k_prompt = zp[seq - win:, k_off:v_off].reshape(kv_shape)
    b_v_prompt = zp[seq - win:, v_off:].reshape(kv_shape)
    k_new = zs[:, k_off:v_off].reshape(n_batch, n_new, B_KV_HEADS * B_DH)
    v_new = zs[:, v_off:].reshape(n_batch, n_new, B_KV_HEADS * B_DH)
    b_k_sample = jnp.concatenate([kbuf, k_new], axis=1)[:, -win:].reshape(kv_shape)
    b_v_sample = jnp.concatenate([vbuf, v_new], axis=1)[:, -win:].reshape(kv_shape)

    u_tab, v_tab = bf(peer_u[0]), bf(peer_v[0])
    wq = bf(peer_wq[0])
    hp = _peer(hp, norm2_g[0], wq, peer_k1[0], peer_k2[0], u_tab, v_tab)
    hs = _peer(hs, norm2_g[0], wq, peer_k1[0], peer_k2[0], u_tab, v_tab)

    w_in = bf(w_in_odd[0])
    w_out = bf(w_out_odd[0])
    c_width = C_HEADS * C_DV
    zp = _matmul(hp, w_in, norm_g=norm1_g[1])
    zs = _matmul(hs, w_in, norm_g=norm1_g[1])
    cos_p, sin_p = _rope_tables(jnp.arange(seq))
    cos_s, sin_s = _rope_tables(past + jnp.arange(T_PAD))
    ycp, c_s_prompt = _retention_prompt(zp, cos_p, sin_p, _retention_tables(CHUNK, CHUNK), c_norm_g[0])
    zs3 = pad_tokens(zs)
    ycs3, c_s_sample = _retention_sample(zs3, cos_s, sin_s, _retention_tables(T_PAD, n_new), c_norm_g[0],
                                         state_c_s[0])
    par = _s5_params(d_a_re[0], d_a_im[0], d_log_dt[0], d_b_re[0], d_b_im[0], d_c_re[0], d_c_im[0], d_d[0])
    u_off = 3 * c_width
    ydp_pre, d_prompt = _s5_prompt(zp, u_off // c_width, par)
    us_ts = zs[:, u_off:].reshape(n_batch, n_new, D_ROWS, D_ROW_CH).transpose(1, 2, 0, 3)
    x_re = state_d_re[0].reshape(n_batch, D_ROWS, D_ROW_STATE).transpose(1, 0, 2)
    x_im = state_d_im[0].reshape(n_batch, D_ROWS, D_ROW_STATE).transpose(1, 0, 2)
    yds_ts, s_re, s_im = _s5_sample(us_ts, x_re, x_im, par)
    glu_w = bf(d_glu_w[0])
    ydp = _glu(ydp_pre, glu_w, d_glu_b[0])
    yds = _glu(yds_ts.transpose(2, 0, 1, 3).reshape(n_batch * n_new, c_width), glu_w, d_glu_b[0])
    hp = _matmul(ycp, w_out[:c_width], x2=ydp, w2=w_out[c_width:], resid=hp)
    hs = _matmul(unpad_tokens(ycs3), w_out[:c_width], x2=yds, w2=w_out[c_width:], resid=hs)

    u_tab, v_tab = bf(peer_u[1]), bf(peer_v[1])
    wq = bf(peer_wq[1])
    hp = _peer(hp, norm2_g[1], wq, peer_k1[1], peer_k2[1], u_tab, v_tab)
    hs = _peer(hs, norm2_g[1], wq, peer_k1[1], peer_k2[1], u_tab, v_tab)

    y_prompt = _final_norm(hp, final_g).reshape(x_prompt.shape)
    y_sample = _final_norm(hs, final_g).reshape(x_sample.shape)

    d_shape = (1, -1, D_GROUPS, D_STATE)
    return (y_prompt, y_sample, a_v_sample, b_k_prompt, b_v_prompt, b_k_sample, b_v_sample,
            c_s_prompt.reshape(1, 1, C_HEADS, C_DK, C_DV), c_s_sample[None],
            d_prompt[0].reshape(d_shape), d_prompt[1].reshape(d_shape),
            s_re.transpose(1, 0, 2).reshape(d_shape), s_im.transpose(1, 0, 2).reshape(d_shape))
```

```python
import functools
import math

import jax
import jax.numpy as jnp
from jax import lax
from jax.experimental import pallas as pl
from jax.experimental.pallas import tpu as pltpu

F32 = jnp.float32
BF16 = jnp.bfloat16
HIGHEST = lax.Precision.HIGHEST

EPS = 1e-6
NEG_BIG = -1e30
NEG_INF = float("-inf")

D_MODEL = 2048
PAST_LEN = 8192
LANES = 128
SUBLANES = 8
VMEM_LIMIT = 56 * 1024 * 1024

CHUNK = 128
A_GROUPS = 8
A_WIDTH = 1024
B_HEADS = 16
B_KV_HEADS = 4
B_GQA = 4
B_DH = 64
C_HEADS = 8
C_DK = 64
C_DV = 128
RET_DECAY_EXP0 = 5.0
ROPE_BASE = 10000.0
D_GROUPS = 64
D_STATE = 64
D_GROUP_CH = 16
D_ROWS = 8
D_ROW_STATE = 512
D_ROW_CH = 128
PEER_HEADS = 8
PEER_NKEYS = 128
PEER_TOPK = 16
PEER_HALF = 128


def _params(semantics):
    return pltpu.CompilerParams(dimension_semantics=semantics, vmem_limit_bytes=VMEM_LIMIT)


def _gelu(x):
    return 0.5 * x * (1.0 + lax.erf(x * (1.0 / math.sqrt(2.0))))


def _rmsnorm(x, g):
    return x * lax.rsqrt(jnp.mean(x * x, axis=-1, keepdims=True) + EPS) * g


def _dot_nt(a, b, precision=None):
    return lax.dot_general(a, b, (((1,), (1,)), ((), ())), precision=precision,
                           preferred_element_type=F32)


def _dot_tn(a, b, precision=None):
    return lax.dot_general(a, b, (((0,), (0,)), ((), ())), precision=precision,
                           preferred_element_type=F32)


def _mm_body(*refs, has_norm, has_pair, has_resid):
    it = iter(refs)
    x_ref, w_ref = next(it), next(it)
    g_ref = next(it) if has_norm else None
    x2_ref, w2_ref = (next(it), next(it)) if has_pair else (None, None)
    r_ref = next(it) if has_resid else None
    o_ref = next(it)
    x = x_ref[...]
    if has_norm:
        x = _rmsnorm(x, g_ref[...])
    xb = x.astype(BF16)
    x2b = x2_ref[...].astype(BF16) if has_pair else None
    n = o_ref.shape[1]
    for j in range(n // MM_COLS):
        cols = slice(j * MM_COLS, (j + 1) * MM_COLS)
        acc = jnp.dot(xb, w_ref[:, cols], preferred_element_type=F32)
        if has_pair:
            acc = acc + jnp.dot(x2b, w2_ref[:, cols], preferred_element_type=F32)
        if has_resid:
            acc = acc + r_ref[:, cols]
        o_ref[:, cols] = acc


MM_COLS = 512


def _matmul(x, w, *, norm_g=None, x2=None, w2=None, resid=None, tm=512):
    m, k = x.shape
    n = w.shape[1]
    assert m % tm == 0 and n % MM_COLS == 0
    resident = pl.Buffered(1)
    args = [x, w]
    specs = [pl.BlockSpec((tm, k), lambda i: (i, 0)),
             pl.BlockSpec((k, n), lambda i: (0, 0), pipeline_mode=resident)]
    if norm_g is not None:
        args.append(norm_g.reshape(1, k))
        specs.append(pl.BlockSpec((1, k), lambda i: (0, 0)))
    if x2 is not None:
        k2 = x2.shape[1]
        args += [x2, w2]
        specs += [pl.BlockSpec((tm, k2), lambda i: (i, 0)),
                  pl.BlockSpec((k2, n), lambda i: (0, 0), pipeline_mode=resident)]
    if resid is not None:
        args.append(resid)
        specs.append(pl.BlockSpec((tm, n), lambda i: (i, 0)))
    body = functools.partial(_mm_body, has_norm=norm_g is not None, has_pair=x2 is not None,
                             has_resid=resid is not None)
    return pl.pallas_call(
        body, grid=(m // tm,), in_specs=specs,
        out_specs=pl.BlockSpec((tm, n), lambda i: (i, 0)),
        out_shape=jax.ShapeDtypeStruct((m, n), F32),
        compiler_params=_params(("parallel",)), name="matmul",
    )(*args)


def _sink_column(sink_ref, kvh, rows_per_head, n_rows):
    grp = lax.broadcasted_iota(jnp.int32, (n_rows, 1), 0) // rows_per_head
    sk = jnp.full((n_rows, 1), sink_ref[kvh * B_GQA + B_GQA - 1], F32)
    for g in range(B_GQA - 2, -1, -1):
        sk = jnp.where(grp == g, sink_ref[kvh * B_GQA + g], sk)
    return sk


def _even_prompt_body(sink_ref, au_ref, av_ref, q_ref, kvc_ref, kvp_ref, ws_ref, bs_ref, o_ref):
    blk = pl.program_id(0)
    au = _gelu(au_ref[...])
    av = _gelu(av_ref[...])
    row = lax.broadcasted_iota(jnp.int32, (CHUNK, CHUNK), 0)
    col = lax.broadcasted_iota(jnp.int32, (CHUNK, CHUNK), 1)
    causal = row >= col
    for g in range(A_GROUPS):
        lanes = slice(g * LANES, (g + 1) * LANES)
        w = jnp.where(causal, ws_ref[g], 0.0).astype(BF16)
        mixed = jnp.dot(w, av[:, lanes].astype(BF16), preferred_element_type=F32) + bs_ref[g]
        o_ref[:, lanes] = au[:, lanes] * mixed

    q = q_ref[...]
    kvc = kvc_ref[...]
    kvp = kvp_ref[...]
    n_rows = B_GQA * CHUNK
    qi = lax.broadcasted_iota(jnp.int32, (n_rows, 2 * CHUNK), 0) % CHUNK
    kc = lax.broadcasted_iota(jnp.int32, (n_rows, 2 * CHUNK), 1)
    dist = qi + CHUNK - kc
    allowed = (dist >= 0) & (dist < CHUNK) & ((kc >= CHUNK) | (blk > 0))
    outs = []
    for kvh in range(B_KV_HEADS):
        ks = slice(kvh * B_DH, (kvh + 1) * B_DH)
        vs = slice(B_KV_HEADS * B_DH + kvh * B_DH, B_KV_HEADS * B_DH + (kvh + 1) * B_DH)
        kk = jnp.concatenate([kvp[:, ks], kvc[:, ks]], axis=0).astype(BF16)
        vv = jnp.concatenate([kvp[:, vs], kvc[:, vs]], axis=0).astype(BF16)
        q4 = jnp.concatenate(
            [q[:, (kvh * B_GQA + g) * B_DH:(kvh * B_GQA + g + 1) * B_DH] for g in range(B_GQA)], axis=0)
        s = _dot_nt(q4.astype(BF16), kk) * (B_DH ** -0.5)
        s = jnp.where(allowed, s, NEG_BIG)
        sk = _sink_column(sink_ref, kvh, CHUNK, n_rows)
        mx = jnp.maximum(jnp.max(s, axis=-1, keepdims=True), sk)
        p = jnp.exp(s - mx)
        p = p / (jnp.sum(p, axis=-1, keepdims=True) + jnp.exp(sk - mx))
        o = jnp.dot(p.astype(BF16), vv, preferred_element_type=F32)
        outs += [o[g * CHUNK:(g + 1) * CHUNK] for g in range(B_GQA)]
    o_ref[:, A_WIDTH:] = jnp.concatenate(outs, axis=1)


def _even_prompt(z, ws, bs_full, sink):
    t = z.shape[0]
    nb = t // CHUNK
    wide = A_WIDTH
    kvw = 2 * B_KV_HEADS * B_DH
    kv_blk = (2 * A_WIDTH + B_HEADS * B_DH) // kvw
    return pl.pallas_call(
        _even_prompt_body, grid=(nb,),
        in_specs=[
            pl.BlockSpec(memory_space=pltpu.SMEM),
            pl.BlockSpec((CHUNK, wide), lambda i: (i, 0)),
            pl.BlockSpec((CHUNK, wide), lambda i: (i, 1)),
            pl.BlockSpec((CHUNK, wide), lambda i: (i, 2)),
            pl.BlockSpec((CHUNK, kvw), lambda i: (i, kv_blk)),
            pl.BlockSpec((CHUNK, kvw), lambda i: (jnp.maximum(i - 1, 0), kv_blk)),
            pl.BlockSpec((A_GROUPS, CHUNK, CHUNK), lambda i: (0, 0, 0)),
            pl.BlockSpec((A_GROUPS, CHUNK, LANES), lambda i: (0, 0, 0)),
        ],
        out_specs=pl.BlockSpec((CHUNK, D_MODEL), lambda i: (i, 0)),
        out_shape=jax.ShapeDtypeStruct((t, D_MODEL), F32),
        compiler_params=_params(("parallel",)), name="even_prompt",
    )(sink, z, z, z, z, z, ws, bs_full)


EVEN_SAMPLE_BATCH = 8
T_PAD = 8


def _even_sample_body(sink_ref, au_ref, av_ref, q_ref, kv_ref, kb_ref, vb_ref, wa_ref, wb_ref,
                      y_ref, avo_ref):
    n_rows = B_GQA * T_PAD
    tq = lax.broadcasted_iota(jnp.int32, (n_rows, CHUNK), 0) % T_PAD
    kc = lax.broadcasted_iota(jnp.int32, (n_rows, CHUNK), 1)
    buf_allowed = kc > tq
    tq1 = lax.broadcasted_iota(jnp.int32, (n_rows, 1), 0) % T_PAD
    n_new = wa_ref.shape[0]
    for b in range(EVEN_SAMPLE_BATCH):
        au = _gelu(au_ref[b])
        av = _gelu(av_ref[b])
        avo_ref[b] = av
        mixed = wb_ref[...]
        for j in range(n_new):
            mixed = mixed + wa_ref[j] * av[j:j + 1, :]
        y_ref[b, :, :A_WIDTH] = au * mixed

        q = q_ref[b]
        kv = kv_ref[b]
        kb = kb_ref[b]
        vb = vb_ref[b]
        outs = []
        for kvh in range(B_KV_HEADS):
            ks = slice(kvh * B_DH, (kvh + 1) * B_DH)
            vs = slice(B_KV_HEADS * B_DH + kvh * B_DH, B_KV_HEADS * B_DH + (kvh + 1) * B_DH)
            q4 = jnp.concatenate(
                [q[:, (kvh * B_GQA + g) * B_DH:(kvh * B_GQA + g + 1) * B_DH] for g in range(B_GQA)], axis=0)
            scale = B_DH ** -0.5
            s_buf = _dot_nt(q4.astype(BF16), kb[:, ks].astype(BF16)) * scale
            s_buf = jnp.where(buf_allowed, s_buf, NEG_BIG)
            q4r = q4.astype(BF16).astype(F32)
            s_new = []
            for j in range(n_new):
                kj = kv[j:j + 1, ks].astype(BF16).astype(F32)
                sj = jnp.sum(q4r * kj, axis=-1, keepdims=True) * scale
                s_new.append(jnp.where(tq1 >= j, sj, NEG_BIG))
            sk = _sink_column(sink_ref, kvh, T_PAD, n_rows)
            mx = jnp.maximum(jnp.max(s_buf, axis=-1, keepdims=True), sk)
            for sj in s_new:
                mx = jnp.maximum(mx, sj)
            p_buf = jnp.exp(s_buf - mx)
            p_new = [jnp.exp(sj - mx) for sj in s_new]
            den = jnp.sum(p_buf, axis=-1, keepdims=True) + jnp.exp(sk - mx)
            for pj in p_new:
                den = den + pj
            inv = 1.0 / den
            o = jnp.dot((p_buf * inv).astype(BF16), vb[:, ks].astype(BF16), preferred_element_type=F32)
            for j in range(n_new):
                vj = kv[j:j + 1, vs].astype(BF16).astype(F32)
                o = o + (p_new[j] * inv).astype(BF16).astype(F32) * vj
            outs += [o[g * T_PAD:(g + 1) * T_PAD] for g in range(B_GQA)]
        y_ref[b, :, A_WIDTH:] = jnp.concatenate(outs, axis=1)


def _even_sample(z3, kbuf, vbuf, wa, wb, sink):
    nb = z3.shape[0]
    bb = EVEN_SAMPLE_BATCH
    kvw = 2 * B_KV_HEADS * B_DH
    kv_blk = (2 * A_WIDTH + B_HEADS * B_DH) // kvw
    win = kbuf.shape[1]
    return pl.pallas_call(
        _even_sample_body, grid=(nb // bb,),
        in_specs=[
            pl.BlockSpec(memory_space=pltpu.SMEM),
            pl.BlockSpec((bb, T_PAD, A_WIDTH), lambda i: (i, 0, 0)),
            pl.BlockSpec((bb, T_PAD, A_WIDTH), lambda i: (i, 0, 1)),
            pl.BlockSpec((bb, T_PAD, A_WIDTH), lambda i: (i, 0, 2)),
            pl.BlockSpec((bb, T_PAD, kvw), lambda i: (i, 0, kv_blk)),
            pl.BlockSpec((bb, win, kvw // 2), lambda i: (i, 0, 0)),
            pl.BlockSpec((bb, win, kvw // 2), lambda i: (i, 0, 0)),
            pl.BlockSpec(wa.shape, lambda i: (0, 0, 0)),
            pl.BlockSpec(wb.shape, lambda i: (0, 0)),
        ],
        out_specs=[
            pl.BlockSpec((bb, T_PAD, D_MODEL), lambda i: (i, 0, 0)),
            pl.BlockSpec((bb, T_PAD, A_WIDTH), lambda i: (i, 0, 0)),
        ],
        out_shape=[
            jax.ShapeDtypeStruct((nb, T_PAD, D_MODEL), F32),
            jax.ShapeDtypeStruct((nb, T_PAD, A_WIDTH), F32),
        ],
        compiler_params=_params(("parallel",)), name="even_sample",
    )(sink, z3, z3, z3, z3, kbuf, vbuf, wa, wb)


def _rope(x, cos_f, sin_s):
    width = x.shape[-1]
    half = C_DK // 2
    lane = lax.broadcasted_iota(jnp.int32, x.shape, 1) % C_DK
    swapped = jnp.where(lane < half, pltpu.roll(x, width - half, 1), pltpu.roll(x, half, 1))
    return x * cos_f + swapped * sin_s


def _groupnorm_gate(o, gain, gate):
    mu = jnp.mean(o, axis=-1, keepdims=True)
    var = jnp.mean(jnp.square(o - mu), axis=-1, keepdims=True)
    return (o - mu) * lax.rsqrt(var + EPS) * gain * (gate * jax.nn.sigmoid(gate))


def _retention_body(qk_ref, v_ref, g_ref, cos_ref, sin_ref, decay_ref, qdec_ref, kdec_ref, sdec_ref,
                    gain_ref, o_ref, st_ref, state):
    @pl.when(pl.program_id(0) == 0)
    def _():
        state[...] = jnp.zeros_like(state)

    qk = qk_ref[...]
    width = C_HEADS * C_DK
    q = _rope(qk[:, :width], cos_ref[...], sin_ref[...]) * (C_DK ** -0.5)
    k = _rope(qk[:, width:], cos_ref[...], sin_ref[...])
    v = v_ref[...]
    g = g_ref[...]
    for h in range(C_HEADS):
        qh = q[:, h * C_DK:(h + 1) * C_DK]
        kh = k[:, h * C_DK:(h + 1) * C_DK]
        vh = v[:, h * C_DV:(h + 1) * C_DV]
        st = state[h]
        scores = _dot_nt(qh, kh, HIGHEST) * decay_ref[h]
        o = jnp.dot(scores, vh, precision=HIGHEST, preferred_element_type=F32)
        o = o + jnp.dot(qh, st, precision=HIGHEST, preferred_element_type=F32) * qdec_ref[h]
        new_st = sdec_ref[h] * st + _dot_tn(kh * kdec_ref[h], vh, HIGHEST)
        state[h] = new_st
        st_ref[h] = new_st
        lanes = slice(h * C_DV, (h + 1) * C_DV)
        o_ref[:, lanes] = _groupnorm_gate(o, gain_ref[:, lanes], g[:, lanes])


def _retention_prompt(z, cos_f, sin_s, tabs, gain):
    t = z.shape[0]
    nc = t // CHUNK
    decay, qdec, kdec, sdec = tabs
    width = C_HEADS * C_DV
    const3 = lambda i: (0, 0, 0)
    return pl.pallas_call(
        _retention_body, grid=(nc,),
        in_specs=[
            pl.BlockSpec((CHUNK, width), lambda i: (i, 0)),
            pl.BlockSpec((CHUNK, width), lambda i: (i, 1)),
            pl.BlockSpec((CHUNK, width), lambda i: (i, 2)),
            pl.BlockSpec((CHUNK, C_HEADS * C_DK), lambda i: (i, 0)),
            pl.BlockSpec((CHUNK, C_HEADS * C_DK), lambda i: (i, 0)),
            pl.BlockSpec(decay.shape, const3),
            pl.BlockSpec(qdec.shape, const3),
            pl.BlockSpec(kdec.shape, const3),
            pl.BlockSpec(sdec.shape, const3),
            pl.BlockSpec((1, width), lambda i: (0, 0)),
        ],
        out_specs=[
            pl.BlockSpec((CHUNK, width), lambda i: (i, 0)),
            pl.BlockSpec((C_HEADS, C_DK, C_DV), const3),
        ],
        out_shape=[
            jax.ShapeDtypeStruct((t, width), F32),
            jax.ShapeDtypeStruct((C_HEADS, C_DK, C_DV), F32),
        ],
        scratch_shapes=[pltpu.VMEM((C_HEADS, C_DK, C_DV), F32)],
        compiler_params=_params(("arbitrary",)), name="retention_prompt",
    )(z, z, z, cos_f, sin_s, decay, qdec, kdec, sdec, gain.reshape(1, width))


RET_SAMPLE_BATCH = 8


def _retention_sample_body(qk_ref, v_ref, g_ref, cos_ref, sin_ref, decay_ref, qdec_ref, kdec_ref,
                           sdec_ref, gain_ref, st_in_ref, o_ref, st_ref):
    width = C_HEADS * C_DK
    for b in range(RET_SAMPLE_BATCH):
        qk = qk_ref[b]
        q = _rope(qk[:, :width], cos_ref[...], sin_ref[...]) * (C_DK ** -0.5)
        k = _rope(qk[:, width:], cos_ref[...], sin_ref[...])
        v = v_ref[b]
        g = g_ref[b]
        for h in range(C_HEADS):
            qh = q[:, h * C_DK:(h + 1) * C_DK]
            kh = k[:, h * C_DK:(h + 1) * C_DK]
            vh = v[:, h * C_DV:(h + 1) * C_DV]
            st = st_in_ref[b, h]
            scores = _dot_nt(qh, kh, HIGHEST) * decay_ref[h]
            o = jnp.dot(scores, vh, precision=HIGHEST, preferred_element_type=F32)
            o = o + jnp.dot(qh, st, precision=HIGHEST, preferred_element_type=F32) * qdec_ref[h]
            st_ref[b, h] = sdec_ref[h] * st + _dot_tn(kh * kdec_ref[h], vh, HIGHEST)
            lanes = slice(h * C_DV, (h + 1) * C_DV)
            o_ref[b, :, lanes] = _groupnorm_gate(o, gain_ref[:, lanes], g[:, lanes])


def _retention_sample(z3, cos_f, sin_s, tabs, gain, st_in):
    nb = z3.shape[0]
    bb = RET_SAMPLE_BATCH
    decay, qdec, kdec, sdec = tabs
    width = C_HEADS * C_DV
    const3 = lambda i: (0, 0, 0)
    return pl.pallas_call(
        _retention_sample_body, grid=(nb // bb,),
        in_specs=[
            pl.BlockSpec((bb, T_PAD, width), lambda i: (i, 0, 0)),
            pl.BlockSpec((bb, T_PAD, width), lambda i: (i, 0, 1)),
            pl.BlockSpec((bb, T_PAD, width), lambda i: (i, 0, 2)),
            pl.BlockSpec((T_PAD, C_HEADS * C_DK), lambda i: (0, 0)),
            pl.BlockSpec((T_PAD, C_HEADS * C_DK), lambda i: (0, 0)),
            pl.BlockSpec(decay.shape, const3),
            pl.BlockSpec(qdec.shape, const3),
            pl.BlockSpec(kdec.shape, const3),
            pl.BlockSpec(sdec.shape, const3),
            pl.BlockSpec((1, width), lambda i: (0, 0)),
            pl.BlockSpec((bb, C_HEADS, C_DK, C_DV), lambda i: (i, 0, 0, 0)),
        ],
        out_specs=[
            pl.BlockSpec((bb, T_PAD, width), lambda i: (i, 0, 0)),
            pl.BlockSpec((bb, C_HEADS, C_DK, C_DV), lambda i: (i, 0, 0, 0)),
        ],
        out_shape=[
            jax.ShapeDtypeStruct((nb, T_PAD, width), F32),
            jax.ShapeDtypeStruct((nb, C_HEADS, C_DK, C_DV), F32),
        ],
        compiler_params=_params(("parallel",)), name="retention_sample",
    )(z3, z3, z3, cos_f, sin_s, decay, qdec, kdec, sdec, gain.reshape(1, width), st_in)


def _retention_tables(length, n_valid):
    log_gamma = jnp.log1p(-jnp.exp2(-RET_DECAY_EXP0 - jnp.arange(C_HEADS, dtype=F32)))
    i = jnp.arange(length, dtype=F32)
    valid = (jnp.arange(length) < n_valid)
    rel = i[:, None] - i[None, :]
    decay = jnp.where(rel >= 0, jnp.exp(jnp.maximum(rel, 0.0)[None] * log_gamma[:, None, None]), 0.0)
    decay = jnp.where(valid[None, None, :], decay, 0.0)
    q_dec = jnp.exp((i[None, :] + 1.0) * log_gamma[:, None])
    k_dec = jnp.where(valid[None, :], jnp.exp((n_valid - 1.0 - i)[None, :] * log_gamma[:, None]), 0.0)
    s_dec = jnp.exp(n_valid * log_gamma)
    qdec = jnp.broadcast_to(q_dec[:, :, None], (C_HEADS, length, C_DV))
    kdec = jnp.broadcast_to(k_dec[:, :, None], (C_HEADS, length, C_DK))
    sdec = jnp.broadcast_to(s_dec[:, None, None], (C_HEADS, C_DK, C_DV))
    return decay, qdec, kdec, sdec


def _rope_tables(pos):
    half = C_DK // 2
    freqs = ROPE_BASE ** (-jnp.arange(half, dtype=F32) / half)
    ang = pos.astype(F32)[:, None] * freqs[None, :]
    cos, sin = jnp.cos(ang), jnp.sin(ang)
    cos_f = jnp.tile(jnp.concatenate([cos, cos], axis=1), (1, C_HEADS))
    sin_s = jnp.tile(jnp.concatenate([-sin, sin], axis=1), (1, C_HEADS))
    return cos_f, sin_s


def _s5_discretize(are_ref, aim_ref, ldt_ref):
    a_re, a_im = are_ref[...], aim_ref[...]
    dt = jnp.exp(ldt_ref[...])
    mag = jnp.exp(a_re * dt)
    ab_re = mag * jnp.cos(a_im * dt)
    ab_im = mag * jnp.sin(a_im * dt)
    num_re, num_im = ab_re - 1.0, ab_im
    den = a_re * a_re + a_im * a_im
    co_re = (num_re * a_re + num_im * a_im) / den
    co_im = (num_im * a_re - num_re * a_im) / den
    return ab_re, ab_im, co_re, co_im


S5_CHUNK = 256


def _split_bf16(x):
    hi = x.astype(BF16)
    return hi, (x - hi.astype(F32)).astype(BF16)


def _dot_split(x, w_hi, w_lo):
    x_hi, x_lo = _split_bf16(x)
    return (jnp.dot(x_hi, w_hi, preferred_element_type=F32)
            + jnp.dot(x_lo, w_hi, preferred_element_type=F32)
            + jnp.dot(x_hi, w_lo, preferred_element_type=F32))


def _s5_prompt_body(u_ref, are_ref, aim_ref, ldt_ref, bre_hi_ref, bre_lo_ref, bim_hi_ref, bim_lo_ref,
                    cre_ref, cim_ref, dd_ref, y_ref, st_ref, hre, him, state, disc):
    n = S5_CHUNK

    @pl.when(pl.program_id(0) == 0)
    def _():
        ab_re, ab_im, co_re, co_im = _s5_discretize(are_ref, aim_ref, ldt_ref)
        disc[0] = ab_re
        disc[1] = ab_im
        disc[2] = co_re
        disc[3] = co_im
        state[...] = jnp.zeros_like(state)

    n_chunks = D_ROW_STATE // LANES
    for s in range(D_ROWS):
        rows = pl.ds(s, n, stride=D_ROWS)
        u_s = u_ref[:, s * D_ROW_CH:(s + 1) * D_ROW_CH]
        raw_re = _dot_split(u_s, bre_hi_ref[s], bre_lo_ref[s])
        raw_im = _dot_split(u_s, bim_hi_ref[s], bim_lo_ref[s])
        co_re = disc[2, s:s + 1, :]
        co_im = disc[3, s:s + 1, :]
        bu_re = co_re * raw_re - co_im * raw_im
        bu_im = co_re * raw_im + co_im * raw_re
        for c in range(n_chunks):
            hre[c, rows, :] = bu_re[:, c * LANES:(c + 1) * LANES]
            him[c, rows, :] = bu_im[:, c * LANES:(c + 1) * LANES]

    ab_re = disc[0]
    ab_im = disc[1]

    def step(t, carry):
        h_re, h_im = carry
        rows = pl.ds(pl.multiple_of(t * D_ROWS, D_ROWS), D_ROWS)
        bu_re = jnp.concatenate([hre[c, rows, :] for c in range(n_chunks)], axis=1)
        bu_im = jnp.concatenate([him[c, rows, :] for c in range(n_chunks)], axis=1)
        n_re = ab_re * h_re - ab_im * h_im + bu_re
        n_im = ab_re * h_im + ab_im * h_re + bu_im
        for c in range(n_chunks):
            hre[c, rows, :] = n_re[:, c * LANES:(c + 1) * LANES]
            him[c, rows, :] = n_im[:, c * LANES:(c + 1) * LANES]
        return n_re, n_im

    h_re, h_im = lax.fori_loop(0, n, step, (state[0], state[1]), unroll=4)
    state[0] = h_re
    state[1] = h_im
    st_ref[0] = h_re
    st_ref[1] = h_im

    for s in range(D_ROWS):
        rows = pl.ds(s, n, stride=D_ROWS)
        hist_re = jnp.concatenate([hre[c, rows, :] for c in range(n_chunks)], axis=1)
        hist_im = jnp.concatenate([him[c, rows, :] for c in range(n_chunks)], axis=1)
        cols = slice(s * D_ROW_CH, (s + 1) * D_ROW_CH)
        y = (jnp.dot(hist_re.astype(BF16), cre_ref[s], preferred_element_type=F32)
             - jnp.dot(hist_im.astype(BF16), cim_ref[s], preferred_element_type=F32)
             + dd_ref[s:s + 1, :] * u_ref[:, cols])
        y_ref[:, cols] = _gelu(y)


def _s5_prompt(z, u_col_block, par):
    a_re, a_im, ldt, b_re, b_im, c_re, c_im, dd = par
    b_re_hi, b_re_lo = _split_bf16(b_re)
    b_im_hi, b_im_lo = _split_bf16(b_im)
    t = z.shape[0]
    blk = S5_CHUNK * D_ROWS
    width = D_ROWS * D_ROW_CH
    c2 = lambda i: (0, 0)
    c3 = lambda i: (0, 0, 0)
    return pl.pallas_call(
        _s5_prompt_body, grid=(t // S5_CHUNK,),
        in_specs=[
            pl.BlockSpec((S5_CHUNK, width), lambda i: (i, u_col_block)),
            pl.BlockSpec(a_re.shape, c2), pl.BlockSpec(a_im.shape, c2), pl.BlockSpec(ldt.shape, c2),
            pl.BlockSpec(b_re.shape, c3), pl.BlockSpec(b_re.shape, c3),
            pl.BlockSpec(b_im.shape, c3), pl.BlockSpec(b_im.shape, c3),
            pl.BlockSpec(c_re.shape, c3), pl.BlockSpec(c_im.shape, c3),
            pl.BlockSpec(dd.shape, c2),
        ],
        out_specs=[
            pl.BlockSpec((S5_CHUNK, width), lambda i: (i, 0)),
            pl.BlockSpec((2, D_ROWS, D_ROW_STATE), c3),
        ],
        out_shape=[
            jax.ShapeDtypeStruct((t, width), F32),
            jax.ShapeDtypeStruct((2, D_ROWS, D_ROW_STATE), F32),
        ],
        scratch_shapes=[
            pltpu.VMEM((D_ROW_STATE // LANES, blk, LANES), F32),
            pltpu.VMEM((D_ROW_STATE // LANES, blk, LANES), F32),
            pltpu.VMEM((2, D_ROWS, D_ROW_STATE), F32), pltpu.VMEM((4, D_ROWS, D_ROW_STATE), F32),
        ],
        compiler_params=_params(("arbitrary",)), name="s5_prompt",
    )(z, a_re, a_im, ldt, b_re_hi, b_re_lo, b_im_hi, b_im_lo, c_re.astype(BF16), c_im.astype(BF16), dd)


def _s5_sample_body(u_ref, xre_ref, xim_ref, are_ref, aim_ref, ldt_ref, bre_ref, bim_ref, cre_ref,
                    cim_ref, dd_ref, y_ref, sre_ref, sim_ref):
    ab_re, ab_im, co_re, co_im = _s5_discretize(are_ref, aim_ref, ldt_ref)
    n_t = u_ref.shape[0]
    for s in range(D_ROWS):
        a_r, a_i = ab_re[s:s + 1, :], ab_im[s:s + 1, :]
        c_r, c_i = co_re[s:s + 1, :], co_im[s:s + 1, :]
        h_re, h_im = xre_ref[s], xim_ref[s]
        for t in range(n_t):
            u = u_ref[t, s]
            raw_re = jnp.dot(u, bre_ref[s], precision=HIGHEST, preferred_element_type=F32)
            raw_im = jnp.dot(u, bim_ref[s], precision=HIGHEST, preferred_element_type=F32)
            bu_re = c_r * raw_re - c_i * raw_im
            bu_im = c_r * raw_im + c_i * raw_re
            h_re, h_im = a_r * h_re - a_i * h_im + bu_re, a_r * h_im + a_i * h_re + bu_im
            y = (jnp.dot(h_re, cre_ref[s], precision=HIGHEST, preferred_element_type=F32)
                 - jnp.dot(h_im, cim_ref[s], precision=HIGHEST, preferred_element_type=F32)
                 + dd_ref[s:s + 1, :] * u)
            y_ref[t, s] = _gelu(y)
        sre_ref[s] = h_re
        sim_ref[s] = h_im


def _s5_sample(u_ts, x_re, x_im, par):
    a_re, a_im, ldt, b_re, b_im, c_re, c_im, dd = par
    return pl.pallas_call(
        _s5_sample_body,
        out_shape=[
            jax.ShapeDtypeStruct(u_ts.shape, F32),
            jax.ShapeDtypeStruct(x_re.shape, F32),
            jax.ShapeDtypeStruct(x_im.shape, F32),
        ],
        compiler_params=pltpu.CompilerParams(vmem_limit_bytes=VMEM_LIMIT), name="s5_sample",
    )(u_ts, x_re, x_im, a_re, a_im, ldt, b_re, b_im, c_re, c_im, dd)


def _s5_params(a_re, a_im, log_dt, b_re, b_im, c_re, c_im, dd):
    eye = jnp.eye(D_ROWS, dtype=F32)

    def rows(a):
        return a.reshape(D_ROWS, D_ROW_STATE)

    def b_blocks(b):
        b4 = b.reshape(D_ROWS, D_ROWS, D_STATE, D_GROUP_CH).transpose(0, 1, 3, 2)
        return jnp.einsum('sgcp,gh->sgchp', b4, eye).reshape(D_ROWS, D_ROW_CH, D_ROW_STATE)

    def c_blocks(c):
        c4 = c.reshape(D_ROWS, D_ROWS, D_GROUP_CH, D_STATE)
        return jnp.einsum('sgcp,gh->sgphc', c4, eye).reshape(D_ROWS, D_ROW_STATE, D_ROW_CH)

    ldt = jnp.broadcast_to(log_dt[:, None], (D_GROUPS, D_STATE))
    return (rows(a_re), rows(a_im), rows(ldt), b_blocks(b_re), b_blocks(b_im),
            c_blocks(c_re), c_blocks(c_im), dd.reshape(D_ROWS, D_ROW_CH))


def _glu_body(x_ref, w_ref, b_ref, o_ref):
    x = x_ref[...]
    gate = jnp.dot(x.astype(BF16), w_ref[...], preferred_element_type=F32) + b_ref[...]
    o_ref[...] = x * jax.nn.sigmoid(gate)


def _glu(x, w, b, tm=512):
    m, k = x.shape
    return pl.pallas_call(
        _glu_body, grid=(m // tm,),
        in_specs=[pl.BlockSpec((tm, k), lambda i: (i, 0)), pl.BlockSpec((k, k), lambda i: (0, 0)),
                  pl.BlockSpec((1, k), lambda i: (0, 0))],
        out_specs=pl.BlockSpec((tm, k), lambda i: (i, 0)),
        out_shape=jax.ShapeDtypeStruct((m, k), F32),
        compiler_params=_params(("parallel",)), name="glu",
    )(x, w, b.reshape(1, k))


PEER_SEL_TOKENS = 256


def _extract_top(s, n_out, exact, want_rank=False):
    n_rows = s.shape[0]
    rid = lax.broadcasted_iota(jnp.int32, s.shape, 0).astype(F32) if exact else None
    rank = jnp.full(s.shape, float(n_out), F32) if want_rank else None
    vals = []
    for r in range(n_out):
        m = jnp.max(s, axis=0, keepdims=True)
        if exact:
            first = jnp.min(jnp.where(s == m, rid, float(n_rows)), axis=0, keepdims=True)
            hit = rid == first
        else:
            hit = s == m
        if want_rank:
            rank = jnp.where(hit, float(r), rank)
        s = jnp.where(hit, NEG_INF, s)
        vals.append(m)
    return vals, s, rank


def _removed_count(s):
    return jnp.sum(jnp.where(s == NEG_INF, 1.0, 0.0), axis=0, keepdims=True)


_CAND_COUNTS = [PEER_TOPK // (a + 1) for a in range(PEER_TOPK)]


def _candidates(v1, v2):
    v1_all = _stack_rows(v1)
    v2_all = _stack_rows(v2)
    rid = lax.broadcasted_iota(jnp.int32, (SUBLANES, v1[0].shape[1]), 0)
    tiles = [v1[0] + v2_all]
    n_pad = 0
    a = 1
    while _CAND_COUNTS[a] > 1:
        tiles.append(jnp.where(rid < _CAND_COUNTS[a], v1[a] + v2_all[:SUBLANES], NEG_INF))
        n_pad += SUBLANES - _CAND_COUNTS[a]
        a += 1
    assert PEER_TOPK - a == SUBLANES
    tiles.append(v1_all[a:] + v2[0])
    return jnp.concatenate(tiles, axis=0), n_pad


def _stack_rows(rows):
    n = len(rows)
    rid = lax.broadcasted_iota(jnp.int32, (n, rows[0].shape[1]), 0)
    out = jnp.broadcast_to(rows[n - 1], rid.shape)
    for i in range(n - 2, -1, -1):
        out = jnp.where(rid == i, rows[i], out)
    return out


def _selected_per_first_key(removed, rank1):
    counts = [jnp.sum(removed[:PEER_TOPK], axis=0, keepdims=True)]
    row = PEER_TOPK
    a = 1
    while _CAND_COUNTS[a] > 1:
        counts.append(jnp.sum(removed[row:row + SUBLANES], axis=0, keepdims=True))
        row += SUBLANES
        a += 1
    for i in range(PEER_TOPK - a):
        counts.append(removed[row + i:row + i + 1])
    n_sel = jnp.zeros(rank1.shape, F32)
    for a, cnt in enumerate(counts):
        n_sel = jnp.where(rank1 == float(a), cnt, n_sel)
    return n_sel


def _peer_select_body(q_ref, k1_ref, k2_ref, ns_ref, g1_ref, r2_ref, e2_ref):
    def head(h, exact):
        q1 = q_ref[:, (2 * h) * PEER_HALF:(2 * h + 1) * PEER_HALF]
        q2 = q_ref[:, (2 * h + 1) * PEER_HALF:(2 * h + 2) * PEER_HALF]
        s1 = _dot_nt(k1_ref[h], q1, HIGHEST)
        s2 = _dot_nt(k2_ref[h], q2, HIGHEST)
        v1, left1, rank1 = _extract_top(s1, PEER_TOPK, exact, want_rank=True)
        v2, left2, rank2 = _extract_top(s2, PEER_TOPK, exact, want_rank=True)
        cand, n_pad = _candidates(v1, v2)
        top, left_c, _ = _extract_top(cand, PEER_TOPK, exact)
        mx = top[0]
        z = jnp.exp(top[0] - mx)
        for kk in range(1, PEER_TOPK):
            z = z + jnp.exp(top[kk] - mx)
        g1 = jnp.where(rank1 < PEER_TOPK, jnp.exp(s1 - v1[0]), 0.0) / z
        e2 = jnp.where(rank2 < PEER_TOPK, jnp.exp(s2 - v2[0]), 0.0)
        removed = jnp.where((left_c == NEG_INF) & (cand != NEG_INF), 1.0, 0.0)
        n_sel = _selected_per_first_key(removed, rank1)
        for j in range(PEER_SEL_TOKENS // LANES):
            lanes = slice(j * LANES, (j + 1) * LANES)
            ns_ref[j, h] = n_sel[:, lanes]
            g1_ref[j, h] = g1[:, lanes]
            r2_ref[j, h] = rank2[:, lanes]
            e2_ref[j, h] = e2[:, lanes]
        if exact:
            return None
        ok = ((_removed_count(left1) == PEER_TOPK) & (_removed_count(left2) == PEER_TOPK)
              & (_removed_count(left_c) == PEER_TOPK + n_pad))
        return jnp.where(ok, 0.0, 1.0)

    repeated = [jnp.max(head(h, False)) for h in range(PEER_HEADS)]
    for h in range(PEER_HEADS):
        @pl.when(repeated[h] > 0.0)
        def _():
            head(h, True)


def _peer_select(q, k1, k2):
    t = q.shape[0]
    tb = PEER_SEL_TOKENS
    nj = tb // LANES
    big = pl.BlockSpec((nj, PEER_HEADS, PEER_NKEYS, LANES), lambda i: (i, 0, 0, 0))
    shape = jax.ShapeDtypeStruct((t // LANES, PEER_HEADS, PEER_NKEYS, LANES), F32)
    return pl.pallas_call(
        _peer_select_body, grid=(t // tb,),
        in_specs=[
            pl.BlockSpec((tb, q.shape[1]), lambda i: (i, 0)),
            pl.BlockSpec(k1.shape, lambda i: (0, 0, 0)),
            pl.BlockSpec(k2.shape, lambda i: (0, 0, 0)),
        ],
        out_specs=[big, big, big, big],
        out_shape=[shape, shape, shape, shape],
        compiler_params=_params(("parallel",)), name="peer_select",
    )(q, k1, k2)


PEER_TOKENS = 512
PEER_EXPERT_TILE = 512
PEER_SLABS = PEER_EXPERT_TILE // PEER_NKEYS

def _peer_dense_body(h_ref, g_ref, u_ref, v_ref, ns_ref, g1_ref, r2_ref, e2_ref, og_ref, o_ref,
                     xn, act_s, coef_s, acc, *, norm_out):
    e = pl.program_id(1)
    n_chunks = PEER_TOKENS // LANES

    @pl.when(e == 0)
    def _():
        xn[...] = _rmsnorm(h_ref[...], g_ref[...]).astype(BF16)
        acc[...] = jnp.zeros_like(acc)

    act = _dot_nt(u_ref[...], xn[...])
    for j in range(n_chunks):
        act_s[j] = act[:, j * LANES:(j + 1) * LANES]

    def chunk(j, carry):
        for c in range(PEER_SLABS):
            gate = jnp.zeros((PEER_NKEYS, LANES), F32)
            for h in range(PEER_HEADS):
                sel = r2_ref[j, h] < ns_ref[j, h, c:c + 1, :]
                gate = gate + jnp.where(sel, e2_ref[j, h] * g1_ref[j, h, c:c + 1, :], 0.0)
            rows = slice(c * PEER_NKEYS, (c + 1) * PEER_NKEYS)
            coef_s[j, rows, :] = (gate * _gelu(act_s[j, rows, :])).astype(BF16)
        return carry

    lax.fori_loop(0, n_chunks, chunk, 0)
    coef = jnp.concatenate([coef_s[j] for j in range(n_chunks)], axis=1)
    acc[...] += _dot_tn(coef, v_ref[...])

    @pl.when(e == pl.num_programs(1) - 1)
    def _():
        out = h_ref[...] + acc[...]
        o_ref[...] = _rmsnorm(out, og_ref[...]) if norm_out else out


def _peer_dense(h, g, u_tabs, v_tabs, layer, sel, out_g, norm_out):
    ns, g1, r2, e2 = sel
    t, d = h.shape
    n_exp = u_tabs.shape[1]
    tb, te = PEER_TOKENS, PEER_EXPERT_TILE
    nj = tb // LANES
    n_tiles = n_exp // te

    def by_tile(a):
        a5 = a.reshape(a.shape[0], PEER_HEADS, n_tiles, PEER_SLABS, LANES)
        return a5.transpose(2, 0, 1, 3, 4)

    tile_spec = pl.BlockSpec((None, nj, PEER_HEADS, PEER_SLABS, LANES), lambda i, e: (e, i, 0, 0, 0))
    tok_spec = pl.BlockSpec((nj, PEER_HEADS, PEER_NKEYS, LANES), lambda i, e: (i, 0, 0, 0))
    return pl.pallas_call(
        functools.partial(_peer_dense_body, norm_out=norm_out), grid=(t // tb, n_tiles),
        in_specs=[
            pl.BlockSpec((tb, d), lambda i, e: (i, 0)),
            pl.BlockSpec((1, d), lambda i, e: (0, 0)),
            pl.BlockSpec((None, te, d), lambda i, e: (layer, e, 0)),
            pl.BlockSpec((None, te, d), lambda i, e: (layer, e, 0)),
            tile_spec, tile_spec, tok_spec, tok_spec,
            pl.BlockSpec((1, d), lambda i, e: (0, 0)),
        ],
        out_specs=pl.BlockSpec((tb, d), lambda i, e: (i, 0)),
        out_shape=jax.ShapeDtypeStruct((t, d), F32),
        scratch_shapes=[pltpu.VMEM((tb, d), BF16), pltpu.VMEM((nj, te, LANES), F32),
                        pltpu.VMEM((nj, te, LANES), BF16), pltpu.VMEM((tb, d), F32)],
        compiler_params=_params(("parallel", "arbitrary")), name="peer_dense",
    )(h, g.reshape(1, d), u_tabs, v_tabs, by_tile(ns), by_tile(g1), r2, e2, out_g.reshape(1, d))


def _peer(h, g, wq, k1, k2, u_tabs, v_tabs, layer, out_g, norm_out):
    q = _matmul(h, wq, norm_g=g)
    return _peer_dense(h, g, u_tabs, v_tabs, layer, _peer_select(q, k1, k2), out_g, norm_out)


def kernel(x_prompt, x_sample, state_b_k, state_b_v, state_c_s, state_d_re, state_d_im, norm1_g, norm2_g, final_g, w_in_even, w_out_even, a_ws, a_bs, b_sink, w_in_odd, w_out_odd, c_norm_g, d_a_re, d_a_im, d_log_dt, d_b_re, d_b_im, d_c_re, d_c_im, d_d, d_glu_w, d_glu_b, peer_wq, peer_k1, peer_k2, peer_u, peer_v):
    seq = x_prompt.shape[1]
    n_batch, n_new = x_sample.shape[:2]
    past = PAST_LEN
    hp = x_prompt.reshape(seq, D_MODEL)
    hs = x_sample.reshape(n_batch * n_new, D_MODEL)
    bf = lambda a: a.astype(BF16)

    def pad_tokens(a):
        a3 = a.reshape(n_batch, n_new, a.shape[-1])
        return jnp.pad(a3, ((0, 0), (0, T_PAD - n_new), (0, 0)))

    def unpad_tokens(a3):
        return a3[:, :n_new].reshape(n_batch * n_new, a3.shape[-1])

    w_in = bf(w_in_even[0])
    w_out = bf(w_out_even[0])
    zp = _matmul(hp, w_in, norm_g=norm1_g[0])
    zs = _matmul(hs, w_in, norm_g=norm1_g[0])
    bs_full = jnp.broadcast_to(a_bs[0][:, :, None], (A_GROUPS, CHUNK, LANES))
    yp = _even_prompt(zp, a_ws[0], bs_full, b_sink[0])
    ws_small = jnp.tril(a_ws[0][:, :n_new, :n_new]).transpose(2, 1, 0)
    wa = jnp.repeat(jnp.pad(ws_small, ((0, 0), (0, T_PAD - n_new), (0, 0))), LANES, axis=-1)
    wb = jnp.repeat(jnp.pad(a_bs[0][:, :n_new].T, ((0, T_PAD - n_new), (0, 0))), LANES, axis=-1)
    win = state_b_k.shape[2]
    kbuf = state_b_k[0].reshape(n_batch, win, B_KV_HEADS * B_DH)
    vbuf = state_b_v[0].reshape(n_batch, win, B_KV_HEADS * B_DH)
    ys3, av3 = _even_sample(pad_tokens(zs), kbuf, vbuf, wa, wb, b_sink[0])
    hp = _matmul(yp, w_out, resid=hp)
    hs = _matmul(unpad_tokens(ys3), w_out, resid=hs)

    k_off = 2 * A_WIDTH + B_HEADS * B_DH
    v_off = k_off + B_KV_HEADS * B_DH
    kv_shape = (1, -1, win, B_KV_HEADS, B_DH)
    a_v_sample = av3[:, :n_new].reshape(1, n_batch, n_new, A_GROUPS, A_WIDTH // A_GROUPS)
    b_k_prompt = zp[seq - win:, k_off:v_off].reshape(kv_shape)
    b_v_prompt = zp[seq - win:, v_off:].reshape(kv_shape)
    k_new = zs[:, k_off:v_off].reshape(n_batch, n_new, B_KV_HEADS * B_DH)
    v_new = zs[:, v_off:].reshape(n_batch, n_new, B_KV_HEADS * B_DH)
    b_k_sample = jnp.concatenate([kbuf, k_new], axis=1)[:, -win:].reshape(kv_shape)
    b_v_sample = jnp.concatenate([vbuf, v_new], axis=1)[:, -win:].reshape(kv_shape)

    u_tabs, v_tabs = bf(peer_u), bf(peer_v)
    wq = bf(peer_wq[0])
    hp = _peer(hp, norm2_g[0], wq, peer_k1[0], peer_k2[0], u_tabs, v_tabs, 0, final_g, False)
    hs = _peer(hs, norm2_g[0], wq, peer_k1[0], peer_k2[0], u_tabs, v_tabs, 0, final_g, False)

    w_in = bf(w_in_odd[0])
    w_out = bf(w_out_odd[0])
    c_width = C_HEADS * C_DV
    zp = _matmul(hp, w_in, norm_g=norm1_g[1])
    zs = _matmul(hs, w_in, norm_g=norm1_g[1])
    cos_p, sin_p = _rope_tables(jnp.arange(seq))
    cos_s, sin_s = _rope_tables(past + jnp.arange(T_PAD))
    ycp, c_s_prompt = _retention_prompt(zp, cos_p, sin_p, _retention_tables(CHUNK, CHUNK), c_norm_g[0])
    zs3 = pad_tokens(zs)
    ycs3, c_s_sample = _retention_sample(zs3, cos_s, sin_s, _retention_tables(T_PAD, n_new), c_norm_g[0],
                                         state_c_s[0])
    par = _s5_params(d_a_re[0], d_a_im[0], d_log_dt[0], d_b_re[0], d_b_im[0], d_c_re[0], d_c_im[0], d_d[0])
    u_off = 3 * c_width
    ydp_pre, d_prompt = _s5_prompt(zp, u_off // c_width, par)
    us_ts = zs[:, u_off:].reshape(n_batch, n_new, D_ROWS, D_ROW_CH).transpose(1, 2, 0, 3)
    x_re = state_d_re[0].reshape(n_batch, D_ROWS, D_ROW_STATE).transpose(1, 0, 2)
    x_im = state_d_im[0].reshape(n_batch, D_ROWS, D_ROW_STATE).transpose(1, 0, 2)
    yds_ts, s_re, s_im = _s5_sample(us_ts, x_re, x_im, par)
    glu_w = bf(d_glu_w[0])
    ydp = _glu(ydp_pre, glu_w, d_glu_b[0])
    yds = _glu(yds_ts.transpose(2, 0, 1, 3).reshape(n_batch * n_new, c_width), glu_w, d_glu_b[0])
    hp = _matmul(ycp, w_out[:c_width], x2=ydp, w2=w_out[c_width:], resid=hp)
    hs = _matmul(unpad_tokens(ycs3), w_out[:c_width], x2=yds, w2=w_out[c_width:], resid=hs)

    wq = bf(peer_wq[1])
    y_prompt = _peer(hp, norm2_g[1], wq, peer_k1[1], peer_k2[1], u_tabs, v_tabs, 1, final_g, True)
    y_sample = _peer(hs, norm2_g[1], wq, peer_k1[1], peer_k2[1], u_tabs, v_tabs, 1, final_g, True)
    y_prompt = y_prompt.reshape(x_prompt.shape)
    y_sample = y_sample.reshape(x_sample.shape)

    d_shape = (1, -1, D_GROUPS, D_STATE)
    return (y_prompt, y_sample, a_v_sample, b_k_prompt, b_v_prompt, b_k_sample, b_v_sample,
            c_s_prompt.reshape(1, 1, C_HEADS, C_DK, C_DV), c_s_sample[None],
            d_prompt[0].reshape(d_shape), d_prompt[1].reshape(d_shape),
            s_re.transpose(1, 0, 2).reshape(d_shape), s_im.transpose(1, 0, 2).reshape(d_shape))
```

```python
import functools
import math

import jax
import jax.numpy as jnp
from jax import lax
from jax.experimental import pallas as pl
from jax.experimental.pallas import tpu as pltpu

F32 = jnp.float32
BF16 = jnp.bfloat16
HIGHEST = lax.Precision.HIGHEST

EPS = 1e-6
NEG_BIG = -1e30
NEG_INF = float("-inf")

D_MODEL = 2048
PAST_LEN = 8192
LANES = 128
SUBLANES = 8
VMEM_LIMIT = 56 * 1024 * 1024

CHUNK = 128
A_GROUPS = 8
A_WIDTH = 1024
B_HEADS = 16
B_KV_HEADS = 4
B_GQA = 4
B_DH = 64
C_HEADS = 8
C_DK = 64
C_DV = 128
RET_DECAY_EXP0 = 5.0
ROPE_BASE = 10000.0
D_GROUPS = 64
D_STATE = 64
D_GROUP_CH = 16
D_ROWS = 8
D_ROW_STATE = 512
D_ROW_CH = 128
PEER_HEADS = 8
PEER_NKEYS = 128
PEER_TOPK = 16
PEER_HALF = 128


def _params(semantics):
    return pltpu.CompilerParams(dimension_semantics=semantics, vmem_limit_bytes=VMEM_LIMIT)


def _gelu(x):
    return 0.5 * x * (1.0 + lax.erf(x * (1.0 / math.sqrt(2.0))))


def _rmsnorm(x, g):
    return x * lax.rsqrt(jnp.mean(x * x, axis=-1, keepdims=True) + EPS) * g


def _dot_nt(a, b, precision=None):
    return lax.dot_general(a, b, (((1,), (1,)), ((), ())), precision=precision,
                           preferred_element_type=F32)


def _dot_tn(a, b, precision=None):
    return lax.dot_general(a, b, (((0,), (0,)), ((), ())), precision=precision,
                           preferred_element_type=F32)


def _mm_body(*refs, has_norm, has_pair, has_resid):
    it = iter(refs)
    x_ref, w_ref = next(it), next(it)
    g_ref = next(it) if has_norm else None
    x2_ref, w2_ref = (next(it), next(it)) if has_pair else (None, None)
    r_ref = next(it) if has_resid else None
    o_ref = next(it)
    x = x_ref[...]
    if has_norm:
        x = _rmsnorm(x, g_ref[...])
    xb = x.astype(BF16)
    x2b = x2_ref[...].astype(BF16) if has_pair else None
    n = o_ref.shape[1]
    for j in range(n // MM_COLS):
        cols = slice(j * MM_COLS, (j + 1) * MM_COLS)
        acc = jnp.dot(xb, w_ref[:, cols], preferred_element_type=F32)
        if has_pair:
            acc = acc + jnp.dot(x2b, w2_ref[:, cols], preferred_element_type=F32)
        if has_resid:
            acc = acc + r_ref[:, cols]
        o_ref[:, cols] = acc


MM_COLS = 512


def _matmul(x, w, *, norm_g=None, x2=None, w2=None, resid=None, tm=512):
    m, k = x.shape
    n = w.shape[1]
    assert m % tm == 0 and n % MM_COLS == 0
    resident = pl.Buffered(1)
    args = [x, w]
    specs = [pl.BlockSpec((tm, k), lambda i: (i, 0)),
             pl.BlockSpec((k, n), lambda i: (0, 0), pipeline_mode=resident)]
    if norm_g is not None:
        args.append(norm_g.reshape(1, k))
        specs.append(pl.BlockSpec((1, k), lambda i: (0, 0)))
    if x2 is not None:
        k2 = x2.shape[1]
        args += [x2, w2]
        specs += [pl.BlockSpec((tm, k2), lambda i: (i, 0)),
                  pl.BlockSpec((k2, n), lambda i: (0, 0), pipeline_mode=resident)]
    if resid is not None:
        args.append(resid)
        specs.append(pl.BlockSpec((tm, n), lambda i: (i, 0)))
    body = functools.partial(_mm_body, has_norm=norm_g is not None, has_pair=x2 is not None,
                             has_resid=resid is not None)
    return pl.pallas_call(
        body, grid=(m // tm,), in_specs=specs,
        out_specs=pl.BlockSpec((tm, n), lambda i: (i, 0)),
        out_shape=jax.ShapeDtypeStruct((m, n), F32),
        compiler_params=_params(("parallel",)), name="matmul",
    )(*args)


def _sink_column(sink_ref, kvh, rows_per_head, n_rows):
    grp = lax.broadcasted_iota(jnp.int32, (n_rows, 1), 0) // rows_per_head
    sk = jnp.full((n_rows, 1), sink_ref[kvh * B_GQA + B_GQA - 1], F32)
    for g in range(B_GQA - 2, -1, -1):
        sk = jnp.where(grp == g, sink_ref[kvh * B_GQA + g], sk)
    return sk


def _even_prompt_body(sink_ref, au_ref, av_ref, q_ref, kvc_ref, kvp_ref, ws_ref, bs_ref, o_ref):
    blk = pl.program_id(0)
    au = _gelu(au_ref[...])
    av = _gelu(av_ref[...])
    row = lax.broadcasted_iota(jnp.int32, (CHUNK, CHUNK), 0)
    col = lax.broadcasted_iota(jnp.int32, (CHUNK, CHUNK), 1)
    causal = row >= col
    for g in range(A_GROUPS):
        lanes = slice(g * LANES, (g + 1) * LANES)
        w = jnp.where(causal, ws_ref[g], 0.0).astype(BF16)
        mixed = jnp.dot(w, av[:, lanes].astype(BF16), preferred_element_type=F32) + bs_ref[g]
        o_ref[:, lanes] = au[:, lanes] * mixed

    q = q_ref[...]
    kvc = kvc_ref[...]
    kvp = kvp_ref[...]
    n_rows = B_GQA * CHUNK
    qi = lax.broadcasted_iota(jnp.int32, (n_rows, 2 * CHUNK), 0) % CHUNK
    kc = lax.broadcasted_iota(jnp.int32, (n_rows, 2 * CHUNK), 1)
    dist = qi + CHUNK - kc
    allowed = (dist >= 0) & (dist < CHUNK) & ((kc >= CHUNK) | (blk > 0))
    outs = []
    for kvh in range(B_KV_HEADS):
        ks = slice(kvh * B_DH, (kvh + 1) * B_DH)
        vs = slice(B_KV_HEADS * B_DH + kvh * B_DH, B_KV_HEADS * B_DH + (kvh + 1) * B_DH)
        kk = jnp.concatenate([kvp[:, ks], kvc[:, ks]], axis=0).astype(BF16)
        vv = jnp.concatenate([kvp[:, vs], kvc[:, vs]], axis=0).astype(BF16)
        q4 = jnp.concatenate(
            [q[:, (kvh * B_GQA + g) * B_DH:(kvh * B_GQA + g + 1) * B_DH] for g in range(B_GQA)], axis=0)
        s = _dot_nt(q4.astype(BF16), kk) * (B_DH ** -0.5)
        s = jnp.where(allowed, s, NEG_BIG)
        sk = _sink_column(sink_ref, kvh, CHUNK, n_rows)
        mx = jnp.maximum(jnp.max(s, axis=-1, keepdims=True), sk)
        p = jnp.exp(s - mx)
        p = p / (jnp.sum(p, axis=-1, keepdims=True) + jnp.exp(sk - mx))
        o = jnp.dot(p.astype(BF16), vv, preferred_element_type=F32)
        outs += [o[g * CHUNK:(g + 1) * CHUNK] for g in range(B_GQA)]
    o_ref[:, A_WIDTH:] = jnp.concatenate(outs, axis=1)


def _even_prompt(z, ws, bs_full, sink):
    t = z.shape[0]
    nb = t // CHUNK
    wide = A_WIDTH
    kvw = 2 * B_KV_HEADS * B_DH
    kv_blk = (2 * A_WIDTH + B_HEADS * B_DH) // kvw
    return pl.pallas_call(
        _even_prompt_body, grid=(nb,),
        in_specs=[
            pl.BlockSpec(memory_space=pltpu.SMEM),
            pl.BlockSpec((CHUNK, wide), lambda i: (i, 0)),
            pl.BlockSpec((CHUNK, wide), lambda i: (i, 1)),
            pl.BlockSpec((CHUNK, wide), lambda i: (i, 2)),
            pl.BlockSpec((CHUNK, kvw), lambda i: (i, kv_blk)),
            pl.BlockSpec((CHUNK, kvw), lambda i: (jnp.maximum(i - 1, 0), kv_blk)),
            pl.BlockSpec((A_GROUPS, CHUNK, CHUNK), lambda i: (0, 0, 0)),
            pl.BlockSpec((A_GROUPS, CHUNK, LANES), lambda i: (0, 0, 0)),
        ],
        out_specs=pl.BlockSpec((CHUNK, D_MODEL), lambda i: (i, 0)),
        out_shape=jax.ShapeDtypeStruct((t, D_MODEL), F32),
        compiler_params=_params(("parallel",)), name="even_prompt",
    )(sink, z, z, z, z, z, ws, bs_full)


EVEN_SAMPLE_BATCH = 8
T_PAD = 8


def _even_sample_body(sink_ref, au_ref, av_ref, q_ref, kv_ref, kb_ref, vb_ref, wa_ref, wb_ref,
                      y_ref, avo_ref):
    kv_width = B_KV_HEADS * B_DH
    n_rows = B_HEADS * T_PAD
    tq = lax.broadcasted_iota(jnp.int32, (n_rows, CHUNK), 0) % T_PAD
    kc = lax.broadcasted_iota(jnp.int32, (n_rows, CHUNK), 1)
    buf_allowed = kc > tq
    tq1 = lax.broadcasted_iota(jnp.int32, (n_rows, 1), 0) % T_PAD
    head_of_row = lax.broadcasted_iota(jnp.int32, (n_rows, 1), 0) // T_PAD
    sk = jnp.full((n_rows, 1), sink_ref[B_HEADS - 1], F32)
    for hd in range(B_HEADS - 2, -1, -1):
        sk = jnp.where(head_of_row == hd, sink_ref[hd], sk)
    lane_kvh = lax.broadcasted_iota(jnp.int32, (T_PAD, kv_width), 1) // B_DH
    scale = B_DH ** -0.5
    n_new = wa_ref.shape[0]
    for b in range(EVEN_SAMPLE_BATCH):
        au = _gelu(au_ref[b])
        av = _gelu(av_ref[b])
        avo_ref[b] = av
        mixed = wb_ref[...]
        for j in range(n_new):
            mixed = mixed + wa_ref[j] * av[j:j + 1, :]
        y_ref[b, :, :A_WIDTH] = au * mixed

        q = q_ref[b]
        kv = kv_ref[b]
        blocks = []
        for kvh in range(B_KV_HEADS):
            for g in range(B_GQA):
                col = (kvh * B_GQA + g) * B_DH
                qg = jnp.concatenate([q[:, col:col + B_DH]] * B_KV_HEADS, axis=1)
                blocks.append(jnp.where(lane_kvh == kvh, qg, 0.0))
        qbd = jnp.concatenate(blocks, axis=0).astype(BF16)
        s_buf = _dot_nt(qbd, kb_ref[b].astype(BF16)) * scale
        s_buf = jnp.where(buf_allowed, s_buf, NEG_BIG)
        qbd_r = qbd.astype(F32)
        s_new = []
        for j in range(n_new):
            kj = kv[j:j + 1, :kv_width].astype(BF16).astype(F32)
            sj = jnp.sum(qbd_r * kj, axis=-1, keepdims=True) * scale
            s_new.append(jnp.where(tq1 >= j, sj, NEG_BIG))
        mx = jnp.maximum(jnp.max(s_buf, axis=-1, keepdims=True), sk)
        for sj in s_new:
            mx = jnp.maximum(mx, sj)
        p_buf = jnp.exp(s_buf - mx)
        p_new = [jnp.exp(sj - mx) for sj in s_new]
        den = jnp.sum(p_buf, axis=-1, keepdims=True) + jnp.exp(sk - mx)
        for pj in p_new:
            den = den + pj
        inv = 1.0 / den
        o = jnp.dot((p_buf * inv).astype(BF16), vb_ref[b].astype(BF16), preferred_element_type=F32)
        for j in range(n_new):
            vj = kv[j:j + 1, kv_width:].astype(BF16).astype(F32)
            o = o + (p_new[j] * inv).astype(BF16).astype(F32) * vj
        outs = []
        for kvh in range(B_KV_HEADS):
            for g in range(B_GQA):
                row = (kvh * B_GQA + g) * T_PAD
                outs.append(o[row:row + T_PAD, kvh * B_DH:(kvh + 1) * B_DH])
        y_ref[b, :, A_WIDTH:] = jnp.concatenate(outs, axis=1)


def _even_sample(z3, kbuf, vbuf, wa, wb, sink):
    nb = z3.shape[0]
    bb = EVEN_SAMPLE_BATCH
    kvw = 2 * B_KV_HEADS * B_DH
    kv_blk = (2 * A_WIDTH + B_HEADS * B_DH) // kvw
    win = kbuf.shape[1]
    return pl.pallas_call(
        _even_sample_body, grid=(nb // bb,),
        in_specs=[
            pl.BlockSpec(memory_space=pltpu.SMEM),
            pl.BlockSpec((bb, T_PAD, A_WIDTH), lambda i: (i, 0, 0)),
            pl.BlockSpec((bb, T_PAD, A_WIDTH), lambda i: (i, 0, 1)),
            pl.BlockSpec((bb, T_PAD, A_WIDTH), lambda i: (i, 0, 2)),
            pl.BlockSpec((bb, T_PAD, kvw), lambda i: (i, 0, kv_blk)),
            pl.BlockSpec((bb, win, kvw // 2), lambda i: (i, 0, 0)),
            pl.BlockSpec((bb, win, kvw // 2), lambda i: (i, 0, 0)),
            pl.BlockSpec(wa.shape, lambda i: (0, 0, 0)),
            pl.BlockSpec(wb.shape, lambda i: (0, 0)),
        ],
        out_specs=[
            pl.BlockSpec((bb, T_PAD, D_MODEL), lambda i: (i, 0, 0)),
            pl.BlockSpec((bb, T_PAD, A_WIDTH), lambda i: (i, 0, 0)),
        ],
        out_shape=[
            jax.ShapeDtypeStruct((nb, T_PAD, D_MODEL), F32),
            jax.ShapeDtypeStruct((nb, T_PAD, A_WIDTH), F32),
        ],
        compiler_params=_params(("parallel",)), name="even_sample",
    )(sink, z3, z3, z3, z3, kbuf, vbuf, wa, wb)


def _rope(x, cos_f, sin_s):
    width = x.shape[-1]
    half = C_DK // 2
    lane = lax.broadcasted_iota(jnp.int32, x.shape, 1) % C_DK
    swapped = jnp.where(lane < half, pltpu.roll(x, width - half, 1), pltpu.roll(x, half, 1))
    return x * cos_f + swapped * sin_s


def _retention_chunk(qh, kh, vh, st, decay, qdec, kdec, sdec):
    qb, vb = qh.astype(BF16), vh.astype(BF16)
    scores = _dot_nt(qb, kh.astype(BF16)) * decay
    o = jnp.dot(scores.astype(BF16), vb, preferred_element_type=F32)
    o = o + jnp.dot(qb, st.astype(BF16), preferred_element_type=F32) * qdec
    new_st = sdec * st + _dot_tn((kh * kdec).astype(BF16), vb)
    return o, new_st


def _groupnorm_gate(o, gain, gate):
    mu = jnp.mean(o, axis=-1, keepdims=True)
    var = jnp.mean(jnp.square(o - mu), axis=-1, keepdims=True)
    return (o - mu) * lax.rsqrt(var + EPS) * gain * (gate * jax.nn.sigmoid(gate))


def _retention_body(qk_ref, v_ref, g_ref, cos_ref, sin_ref, decay_ref, qdec_ref, kdec_ref, sdec_ref,
                    gain_ref, o_ref, st_ref, state):
    @pl.when(pl.program_id(0) == 0)
    def _():
        state[...] = jnp.zeros_like(state)

    qk = qk_ref[...]
    width = C_HEADS * C_DK
    q = _rope(qk[:, :width], cos_ref[...], sin_ref[...]) * (C_DK ** -0.5)
    k = _rope(qk[:, width:], cos_ref[...], sin_ref[...])
    v = v_ref[...]
    g = g_ref[...]
    for h in range(C_HEADS):
        qh = q[:, h * C_DK:(h + 1) * C_DK]
        kh = k[:, h * C_DK:(h + 1) * C_DK]
        vh = v[:, h * C_DV:(h + 1) * C_DV]
        st = state[h]
        o, new_st = _retention_chunk(qh, kh, vh, st, decay_ref[h], qdec_ref[h], kdec_ref[h], sdec_ref[h])
        state[h] = new_st
        st_ref[h] = new_st
        lanes = slice(h * C_DV, (h + 1) * C_DV)
        o_ref[:, lanes] = _groupnorm_gate(o, gain_ref[:, lanes], g[:, lanes])


def _retention_prompt(z, cos_f, sin_s, tabs, gain):
    t = z.shape[0]
    nc = t // CHUNK
    decay, qdec, kdec, sdec = tabs
    width = C_HEADS * C_DV
    const3 = lambda i: (0, 0, 0)
    return pl.pallas_call(
        _retention_body, grid=(nc,),
        in_specs=[
            pl.BlockSpec((CHUNK, width), lambda i: (i, 0)),
            pl.BlockSpec((CHUNK, width), lambda i: (i, 1)),
            pl.BlockSpec((CHUNK, width), lambda i: (i, 2)),
            pl.BlockSpec((CHUNK, C_HEADS * C_DK), lambda i: (i, 0)),
            pl.BlockSpec((CHUNK, C_HEADS * C_DK), lambda i: (i, 0)),
            pl.BlockSpec(decay.shape, const3),
            pl.BlockSpec(qdec.shape, const3),
            pl.BlockSpec(kdec.shape, const3),
            pl.BlockSpec(sdec.shape, const3),
            pl.BlockSpec((1, width), lambda i: (0, 0)),
        ],
        out_specs=[
            pl.BlockSpec((CHUNK, width), lambda i: (i, 0)),
            pl.BlockSpec((C_HEADS, C_DK, C_DV), const3),
        ],
        out_shape=[
            jax.ShapeDtypeStruct((t, width), F32),
            jax.ShapeDtypeStruct((C_HEADS, C_DK, C_DV), F32),
        ],
        scratch_shapes=[pltpu.VMEM((C_HEADS, C_DK, C_DV), F32)],
        compiler_params=_params(("arbitrary",)), name="retention_prompt",
    )(z, z, z, cos_f, sin_s, decay, qdec, kdec, sdec, gain.reshape(1, width))


RET_SAMPLE_BATCH = 8


def _retention_sample_body(qk_ref, v_ref, g_ref, cos_ref, sin_ref, decay_ref, qdec_ref, kdec_ref,
                           sdec_ref, gain_ref, st_in_ref, o_ref, st_ref):
    width = C_HEADS * C_DK
    lane_head = lax.broadcasted_iota(jnp.int32, (T_PAD, width), 1) // C_DK
    for b in range(RET_SAMPLE_BATCH):
        qk = qk_ref[b]
        q = _rope(qk[:, :width], cos_ref[...], sin_ref[...]) * (C_DK ** -0.5)
        k = _rope(qk[:, width:], cos_ref[...], sin_ref[...])
        v = v_ref[b]
        g = g_ref[b]
        vb = v.astype(BF16)
        qbd = jnp.concatenate([jnp.where(lane_head == h, q, 0.0) for h in range(C_HEADS)],
                              axis=0).astype(BF16)
        st = st_in_ref[b]
        scores = _dot_nt(qbd, k.astype(BF16)) * decay_ref[...]
        o_intra = jnp.dot(scores.astype(BF16), vb, preferred_element_type=F32)
        o_cross = jnp.dot(qbd, st.astype(BF16), preferred_element_type=F32) * qdec_ref[...]
        kd = (k * kdec_ref[...]).astype(BF16)
        for h in range(C_HEADS):
            rows = slice(h * T_PAD, (h + 1) * T_PAD)
            lanes = slice(h * C_DV, (h + 1) * C_DV)
            keys = slice(h * C_DK, (h + 1) * C_DK)
            o = o_intra[rows, lanes] + o_cross[rows, :]
            o_ref[b, :, lanes] = _groupnorm_gate(o, gain_ref[:, lanes], g[:, lanes])
            st_ref[b, keys, :] = sdec_ref[keys, :] * st[keys, :] + _dot_tn(kd[:, keys], vb[:, lanes])


def _retention_sample(z3, cos_f, sin_s, tabs, gain, st_in):
    nb = z3.shape[0]
    bb = RET_SAMPLE_BATCH
    decay, qdec, kdec, sdec = tabs
    width = C_HEADS * C_DV
    decay_s = decay.reshape(C_HEADS * T_PAD, T_PAD)
    qdec_s = qdec.reshape(C_HEADS * T_PAD, C_DV)
    kdec_s = kdec.transpose(1, 0, 2).reshape(T_PAD, C_HEADS * C_DK)
    sdec_s = sdec.reshape(C_HEADS * C_DK, C_DV)
    st2 = st_in.reshape(nb, C_HEADS * C_DK, C_DV)
    c2 = lambda i: (0, 0)
    out, st_out = pl.pallas_call(
        _retention_sample_body, grid=(nb // bb,),
        in_specs=[
            pl.BlockSpec((bb, T_PAD, width), lambda i: (i, 0, 0)),
            pl.BlockSpec((bb, T_PAD, width), lambda i: (i, 0, 1)),
            pl.BlockSpec((bb, T_PAD, width), lambda i: (i, 0, 2)),
            pl.BlockSpec((T_PAD, C_HEADS * C_DK), c2),
            pl.BlockSpec((T_PAD, C_HEADS * C_DK), c2),
            pl.BlockSpec(decay_s.shape, c2),
            pl.BlockSpec(qdec_s.shape, c2),
            pl.BlockSpec(kdec_s.shape, c2),
            pl.BlockSpec(sdec_s.shape, c2),
            pl.BlockSpec((1, width), c2),
            pl.BlockSpec((bb, C_HEADS * C_DK, C_DV), lambda i: (i, 0, 0)),
        ],
        out_specs=[
            pl.BlockSpec((bb, T_PAD, width), lambda i: (i, 0, 0)),
            pl.BlockSpec((bb, C_HEADS * C_DK, C_DV), lambda i: (i, 0, 0)),
        ],
        out_shape=[
            jax.ShapeDtypeStruct((nb, T_PAD, width), F32),
            jax.ShapeDtypeStruct((nb, C_HEADS * C_DK, C_DV), F32),
        ],
        compiler_params=_params(("parallel",)), name="retention_sample",
    )(z3, z3, z3, cos_f, sin_s, decay_s, qdec_s, kdec_s, sdec_s, gain.reshape(1, width), st2)
    return out, st_out.reshape(st_in.shape)


def _retention_tables(length, n_valid):
    log_gamma = jnp.log1p(-jnp.exp2(-RET_DECAY_EXP0 - jnp.arange(C_HEADS, dtype=F32)))
    i = jnp.arange(length, dtype=F32)
    valid = (jnp.arange(length) < n_valid)
    rel = i[:, None] - i[None, :]
    decay = jnp.where(rel >= 0, jnp.exp(jnp.maximum(rel, 0.0)[None] * log_gamma[:, None, None]), 0.0)
    decay = jnp.where(valid[None, None, :], decay, 0.0)
    q_dec = jnp.exp((i[None, :] + 1.0) * log_gamma[:, None])
    k_dec = jnp.where(valid[None, :], jnp.exp((n_valid - 1.0 - i)[None, :] * log_gamma[:, None]), 0.0)
    s_dec = jnp.exp(n_valid * log_gamma)
    qdec = jnp.broadcast_to(q_dec[:, :, None], (C_HEADS, length, C_DV))
    kdec = jnp.broadcast_to(k_dec[:, :, None], (C_HEADS, length, C_DK))
    sdec = jnp.broadcast_to(s_dec[:, None, None], (C_HEADS, C_DK, C_DV))
    return decay, qdec, kdec, sdec


def _rope_tables(pos):
    half = C_DK // 2
    freqs = ROPE_BASE ** (-jnp.arange(half, dtype=F32) / half)
    ang = pos.astype(F32)[:, None] * freqs[None, :]
    cos, sin = jnp.cos(ang), jnp.sin(ang)
    cos_f = jnp.tile(jnp.concatenate([cos, cos], axis=1), (1, C_HEADS))
    sin_s = jnp.tile(jnp.concatenate([-sin, sin], axis=1), (1, C_HEADS))
    return cos_f, sin_s


def _s5_discretize(are_ref, aim_ref, ldt_ref):
    a_re, a_im = are_ref[...], aim_ref[...]
    dt = jnp.exp(ldt_ref[...])
    mag = jnp.exp(a_re * dt)
    ab_re = mag * jnp.cos(a_im * dt)
    ab_im = mag * jnp.sin(a_im * dt)
    num_re, num_im = ab_re - 1.0, ab_im
    den = a_re * a_re + a_im * a_im
    co_re = (num_re * a_re + num_im * a_im) / den
    co_im = (num_im * a_re - num_re * a_im) / den
    return ab_re, ab_im, co_re, co_im


S5_CHUNK = 256


def _s5_prompt_body(u_ref, are_ref, aim_ref, ldt_ref, bre_ref, bim_ref, cre_ref, cim_ref, dd_ref,
                    y_ref, st_ref, hre, him, state, disc):
    n = S5_CHUNK

    @pl.when(pl.program_id(0) == 0)
    def _():
        ab_re, ab_im, co_re, co_im = _s5_discretize(are_ref, aim_ref, ldt_ref)
        disc[0] = ab_re
        disc[1] = ab_im
        disc[2] = co_re
        disc[3] = co_im
        state[...] = jnp.zeros_like(state)

    n_chunks = D_ROW_STATE // LANES
    for s in range(D_ROWS):
        rows = pl.ds(s, n, stride=D_ROWS)
        u_s = u_ref[:, s * D_ROW_CH:(s + 1) * D_ROW_CH]
        u_b = u_s.astype(BF16)
        raw_re = jnp.dot(u_b, bre_ref[s], preferred_element_type=F32)
        raw_im = jnp.dot(u_b, bim_ref[s], preferred_element_type=F32)
        co_re = disc[2, s:s + 1, :]
        co_im = disc[3, s:s + 1, :]
        bu_re = co_re * raw_re - co_im * raw_im
        bu_im = co_re * raw_im + co_im * raw_re
        for c in range(n_chunks):
            hre[c, rows, :] = bu_re[:, c * LANES:(c + 1) * LANES]
            him[c, rows, :] = bu_im[:, c * LANES:(c + 1) * LANES]

    ab_re = disc[0]
    ab_im = disc[1]

    def step(t, carry):
        h_re, h_im = carry
        rows = pl.ds(pl.multiple_of(t * D_ROWS, D_ROWS), D_ROWS)
        bu_re = jnp.concatenate([hre[c, rows, :] for c in range(n_chunks)], axis=1)
        bu_im = jnp.concatenate([him[c, rows, :] for c in range(n_chunks)], axis=1)
        n_re = ab_re * h_re - ab_im * h_im + bu_re
        n_im = ab_re * h_im + ab_im * h_re + bu_im
        for c in range(n_chunks):
            hre[c, rows, :] = n_re[:, c * LANES:(c + 1) * LANES]
            him[c, rows, :] = n_im[:, c * LANES:(c + 1) * LANES]
        return n_re, n_im

    h_re, h_im = lax.fori_loop(0, n, step, (state[0], state[1]), unroll=4)
    state[0] = h_re
    state[1] = h_im
    st_ref[0] = h_re
    st_ref[1] = h_im

    for s in range(D_ROWS):
        rows = pl.ds(s, n, stride=D_ROWS)
        hist_re = jnp.concatenate([hre[c, rows, :] for c in range(n_chunks)], axis=1)
        hist_im = jnp.concatenate([him[c, rows, :] for c in range(n_chunks)], axis=1)
        cols = slice(s * D_ROW_CH, (s + 1) * D_ROW_CH)
        y = (jnp.dot(hist_re.astype(BF16), cre_ref[s], preferred_element_type=F32)
             - jnp.dot(hist_im.astype(BF16), cim_ref[s], preferred_element_type=F32)
             + dd_ref[s:s + 1, :] * u_ref[:, cols])
        y_ref[:, cols] = _gelu(y)


def _s5_prompt(z, u_col_block, par):
    a_re, a_im, ldt, b_re, b_im, c_re, c_im, dd = par
    t = z.shape[0]
    blk = S5_CHUNK * D_ROWS
    width = D_ROWS * D_ROW_CH
    c2 = lambda i: (0, 0)
    c3 = lambda i: (0, 0, 0)
    return pl.pallas_call(
        _s5_prompt_body, grid=(t // S5_CHUNK,),
        in_specs=[
            pl.BlockSpec((S5_CHUNK, width), lambda i: (i, u_col_block)),
            pl.BlockSpec(a_re.shape, c2), pl.BlockSpec(a_im.shape, c2), pl.BlockSpec(ldt.shape, c2),
            pl.BlockSpec(b_re.shape, c3), pl.BlockSpec(b_im.shape, c3),
            pl.BlockSpec(c_re.shape, c3), pl.BlockSpec(c_im.shape, c3),
            pl.BlockSpec(dd.shape, c2),
        ],
        out_specs=[
            pl.BlockSpec((S5_CHUNK, width), lambda i: (i, 0)),
            pl.BlockSpec((2, D_ROWS, D_ROW_STATE), c3),
        ],
        out_shape=[
            jax.ShapeDtypeStruct((t, width), F32),
            jax.ShapeDtypeStruct((2, D_ROWS, D_ROW_STATE), F32),
        ],
        scratch_shapes=[
            pltpu.VMEM((D_ROW_STATE // LANES, blk, LANES), F32),
            pltpu.VMEM((D_ROW_STATE // LANES, blk, LANES), F32),
            pltpu.VMEM((2, D_ROWS, D_ROW_STATE), F32), pltpu.VMEM((4, D_ROWS, D_ROW_STATE), F32),
        ],
        compiler_params=_params(("arbitrary",)), name="s5_prompt",
    )(z, a_re, a_im, ldt, b_re.astype(BF16), b_im.astype(BF16), c_re.astype(BF16), c_im.astype(BF16), dd)


def _s5_sample_body(u_ref, xre_ref, xim_ref, are_ref, aim_ref, ldt_ref, bre_ref, bim_ref, cre_ref,
                    cim_ref, dd_ref, y_ref, sre_ref, sim_ref):
    ab_re, ab_im, co_re, co_im = _s5_discretize(are_ref, aim_ref, ldt_ref)
    n_t = u_ref.shape[0]
    for s in range(D_ROWS):
        a_r, a_i = ab_re[s:s + 1, :], ab_im[s:s + 1, :]
        c_r, c_i = co_re[s:s + 1, :], co_im[s:s + 1, :]
        h_re, h_im = xre_ref[s], xim_ref[s]
        for t in range(n_t):
            u = u_ref[t, s]
            raw_re = jnp.dot(u, bre_ref[s], precision=HIGHEST, preferred_element_type=F32)
            raw_im = jnp.dot(u, bim_ref[s], precision=HIGHEST, preferred_element_type=F32)
            bu_re = c_r * raw_re - c_i * raw_im
            bu_im = c_r * raw_im + c_i * raw_re
            h_re, h_im = a_r * h_re - a_i * h_im + bu_re, a_r * h_im + a_i * h_re + bu_im
            y = (jnp.dot(h_re, cre_ref[s], precision=HIGHEST, preferred_element_type=F32)
                 - jnp.dot(h_im, cim_ref[s], precision=HIGHEST, preferred_element_type=F32)
                 + dd_ref[s:s + 1, :] * u)
            y_ref[t, s] = _gelu(y)
        sre_ref[s] = h_re
        sim_ref[s] = h_im


def _s5_sample(u_ts, x_re, x_im, par):
    a_re, a_im, ldt, b_re, b_im, c_re, c_im, dd = par
    return pl.pallas_call(
        _s5_sample_body,
        out_shape=[
            jax.ShapeDtypeStruct(u_ts.shape, F32),
            jax.ShapeDtypeStruct(x_re.shape, F32),
            jax.ShapeDtypeStruct(x_im.shape, F32),
        ],
        compiler_params=pltpu.CompilerParams(vmem_limit_bytes=VMEM_LIMIT), name="s5_sample",
    )(u_ts, x_re, x_im, a_re, a_im, ldt, b_re, b_im, c_re, c_im, dd)


def _s5_params(a_re, a_im, log_dt, b_re, b_im, c_re, c_im, dd):
    eye = jnp.eye(D_ROWS, dtype=F32)

    def rows(a):
        return a.reshape(D_ROWS, D_ROW_STATE)

    def b_blocks(b):
        b4 = b.reshape(D_ROWS, D_ROWS, D_STATE, D_GROUP_CH).transpose(0, 1, 3, 2)
        return jnp.einsum('sgcp,gh->sgchp', b4, eye).reshape(D_ROWS, D_ROW_CH, D_ROW_STATE)

    def c_blocks(c):
        c4 = c.reshape(D_ROWS, D_ROWS, D_GROUP_CH, D_STATE)
        return jnp.einsum('sgcp,gh->sgphc', c4, eye).reshape(D_ROWS, D_ROW_STATE, D_ROW_CH)

    ldt = jnp.broadcast_to(log_dt[:, None], (D_GROUPS, D_STATE))
    return (rows(a_re), rows(a_im), rows(ldt), b_blocks(b_re), b_blocks(b_im),
            c_blocks(c_re), c_blocks(c_im), dd.reshape(D_ROWS, D_ROW_CH))


def _glu_body(x_ref, w_ref, b_ref, o_ref):
    x = x_ref[...]
    gate = jnp.dot(x.astype(BF16), w_ref[...], preferred_element_type=F32) + b_ref[...]
    o_ref[...] = x * jax.nn.sigmoid(gate)


def _glu(x, w, b, tm=512):
    m, k = x.shape
    return pl.pallas_call(
        _glu_body, grid=(m // tm,),
        in_specs=[pl.BlockSpec((tm, k), lambda i: (i, 0)), pl.BlockSpec((k, k), lambda i: (0, 0)),
                  pl.BlockSpec((1, k), lambda i: (0, 0))],
        out_specs=pl.BlockSpec((tm, k), lambda i: (i, 0)),
        out_shape=jax.ShapeDtypeStruct((m, k), F32),
        compiler_params=_params(("parallel",)), name="glu",
    )(x, w, b.reshape(1, k))


PEER_SEL_TOKENS = 256


def _extract_top(s, n_out, exact, want_rank=False):
    n_rows = s.shape[0]
    rid = lax.broadcasted_iota(jnp.int32, s.shape, 0).astype(F32) if exact else None
    rank = jnp.full(s.shape, float(n_out), F32) if want_rank else None
    vals = []
    for r in range(n_out):
        m = jnp.max(s, axis=0, keepdims=True)
        if exact:
            first = jnp.min(jnp.where(s == m, rid, float(n_rows)), axis=0, keepdims=True)
            hit = rid == first
        else:
            hit = s == m
        if want_rank:
            rank = jnp.where(hit, float(r), rank)
        s = jnp.where(hit, NEG_INF, s)
        vals.append(m)
    return vals, s, rank


def _removed_count(s):
    return jnp.sum(jnp.where(s == NEG_INF, 1.0, 0.0), axis=0, keepdims=True)


_CAND_COUNTS = [PEER_TOPK // (a + 1) for a in range(PEER_TOPK)]


def _candidates(v1, v2):
    v1_all = _stack_rows(v1)
    v2_all = _stack_rows(v2)
    rid = lax.broadcasted_iota(jnp.int32, (SUBLANES, v1[0].shape[1]), 0)
    tiles = [v1[0] + v2_all]
    n_pad = 0
    a = 1
    while _CAND_COUNTS[a] > 1:
        tiles.append(jnp.where(rid < _CAND_COUNTS[a], v1[a] + v2_all[:SUBLANES], NEG_INF))
        n_pad += SUBLANES - _CAND_COUNTS[a]
        a += 1
    assert PEER_TOPK - a == SUBLANES
    tiles.append(v1_all[a:] + v2[0])
    return jnp.concatenate(tiles, axis=0), n_pad


def _stack_rows(rows):
    n = len(rows)
    rid = lax.broadcasted_iota(jnp.int32, (n, rows[0].shape[1]), 0)
    out = jnp.broadcast_to(rows[n - 1], rid.shape)
    for i in range(n - 2, -1, -1):
        out = jnp.where(rid == i, rows[i], out)
    return out


def _selected_per_first_key(removed, rank1):
    counts = [jnp.sum(removed[:PEER_TOPK], axis=0, keepdims=True)]
    row = PEER_TOPK
    a = 1
    while _CAND_COUNTS[a] > 1:
        counts.append(jnp.sum(removed[row:row + SUBLANES], axis=0, keepdims=True))
        row += SUBLANES
        a += 1
    for i in range(PEER_TOPK - a):
        counts.append(removed[row + i:row + i + 1])
    n_sel = jnp.zeros(rank1.shape, F32)
    for a, cnt in enumerate(counts):
        n_sel = jnp.where(rank1 == float(a), cnt, n_sel)
    return n_sel


def _peer_select_body(q_ref, k1_ref, k2_ref, ns_ref, g1_ref, r2_ref, e2_ref):
    def head(h, exact):
        q1 = q_ref[:, (2 * h) * PEER_HALF:(2 * h + 1) * PEER_HALF]
        q2 = q_ref[:, (2 * h + 1) * PEER_HALF:(2 * h + 2) * PEER_HALF]
        s1 = _dot_nt(k1_ref[h], q1, HIGHEST)
        s2 = _dot_nt(k2_ref[h], q2, HIGHEST)
        v1, left1, rank1 = _extract_top(s1, PEER_TOPK, exact, want_rank=True)
        v2, left2, rank2 = _extract_top(s2, PEER_TOPK, exact, want_rank=True)
        cand, n_pad = _candidates(v1, v2)
        top, left_c, _ = _extract_top(cand, PEER_TOPK, exact)
        mx = top[0]
        z = jnp.exp(top[0] - mx)
        for kk in range(1, PEER_TOPK):
            z = z + jnp.exp(top[kk] - mx)
        g1 = jnp.where(rank1 < PEER_TOPK, jnp.exp(s1 - v1[0]), 0.0) / z
        e2 = jnp.where(rank2 < PEER_TOPK, jnp.exp(s2 - v2[0]), 0.0)
        removed = jnp.where((left_c == NEG_INF) & (cand != NEG_INF), 1.0, 0.0)
        n_sel = _selected_per_first_key(removed, rank1)
        for j in range(PEER_SEL_TOKENS // LANES):
            lanes = slice(j * LANES, (j + 1) * LANES)
            ns_ref[j, h] = n_sel[:, lanes]
            g1_ref[j, h] = g1[:, lanes]
            r2_ref[j, h] = rank2[:, lanes]
            e2_ref[j, h] = e2[:, lanes]
        if exact:
            return None
        ok = ((_removed_count(left1) == PEER_TOPK) & (_removed_count(left2) == PEER_TOPK)
              & (_removed_count(left_c) == PEER_TOPK + n_pad))
        return jnp.where(ok, 0.0, 1.0)

    repeated = [jnp.max(head(h, False)) for h in range(PEER_HEADS)]
    for h in range(PEER_HEADS):
        @pl.when(repeated[h] > 0.0)
        def _():
            head(h, True)


def _peer_select(q, k1, k2):
    t = q.shape[0]
    tb = PEER_SEL_TOKENS
    nj = tb // LANES
    big = pl.BlockSpec((nj, PEER_HEADS, PEER_NKEYS, LANES), lambda i: (i, 0, 0, 0))
    shape = jax.ShapeDtypeStruct((t // LANES, PEER_HEADS, PEER_NKEYS, LANES), F32)
    return pl.pallas_call(
        _peer_select_body, grid=(t // tb,),
        in_specs=[
            pl.BlockSpec((tb, q.shape[1]), lambda i: (i, 0)),
            pl.BlockSpec(k1.shape, lambda i: (0, 0, 0)),
            pl.BlockSpec(k2.shape, lambda i: (0, 0, 0)),
        ],
        out_specs=[big, big, big, big],
        out_shape=[shape, shape, shape, shape],
        compiler_params=_params(("parallel",)), name="peer_select",
    )(q, k1, k2)


PEER_TOKENS = 512
PEER_EXPERT_TILE = 512
PEER_SLABS = PEER_EXPERT_TILE // PEER_NKEYS

def _peer_dense_body(h_ref, g_ref, u_ref, v_ref, ns_ref, g1_ref, r2_ref, e2_ref, og_ref, o_ref, *rest,
                     norm_out, emit_bf16):
    if emit_bf16:
        ub_ref, vb_ref, xn, act_s, coef_s, acc = rest
    else:
        xn, act_s, coef_s, acc = rest
    e = pl.program_id(1)
    n_chunks = PEER_TOKENS // LANES

    @pl.when(e == 0)
    def _():
        xn[...] = _rmsnorm(h_ref[...], g_ref[...]).astype(BF16)
        acc[...] = jnp.zeros_like(acc)

    u_tile = u_ref[...].astype(BF16)
    v_tile = v_ref[...].astype(BF16)
    if emit_bf16:
        ub_ref[...] = u_tile
        vb_ref[...] = v_tile
    act = _dot_nt(u_tile, xn[...])
    for j in range(n_chunks):
        act_s[j] = act[:, j * LANES:(j + 1) * LANES]

    def chunk(j, carry):
        for c in range(PEER_SLABS):
            gate = jnp.zeros((PEER_NKEYS, LANES), F32)
            for h in range(PEER_HEADS):
                sel = r2_ref[j, h] < ns_ref[j, h, c:c + 1, :]
                gate = gate + jnp.where(sel, e2_ref[j, h] * g1_ref[j, h, c:c + 1, :], 0.0)
            rows = slice(c * PEER_NKEYS, (c + 1) * PEER_NKEYS)
            coef_s[j, rows, :] = (gate * _gelu(act_s[j, rows, :])).astype(BF16)
        return carry

    lax.fori_loop(0, n_chunks, chunk, 0)
    coef = jnp.concatenate([coef_s[j] for j in range(n_chunks)], axis=1)
    acc[...] += _dot_tn(coef, v_tile)

    @pl.when(e == pl.num_programs(1) - 1)
    def _():
        out = h_ref[...] + acc[...]
        o_ref[...] = _rmsnorm(out, og_ref[...]) if norm_out else out


def _peer_dense(h, g, u_tab, v_tab, layer, sel, out_g, norm_out):
    ns, g1, r2, e2 = sel
    t, d = h.shape
    emit_bf16 = u_tab.ndim == 3
    n_exp = u_tab.shape[-2]
    tb, te = PEER_TOKENS, PEER_EXPERT_TILE
    nj = tb // LANES
    n_tiles = n_exp // te
    if emit_bf16:
        assert t == tb, "every table tile must be visited exactly once when the casts are emitted"
        tab_spec = pl.BlockSpec((None, te, d), lambda i, e: (layer, e, 0))
    else:
        tab_spec = pl.BlockSpec((te, d), lambda i, e: (e, 0))
    o_spec = pl.BlockSpec((tb, d), lambda i, e: (i, 0))
    o_shape = jax.ShapeDtypeStruct((t, d), F32)
    cast_spec = pl.BlockSpec((te, d), lambda i, e: (e, 0))
    cast_shape = jax.ShapeDtypeStruct((n_exp, d), BF16)

    def by_tile(a):
        a5 = a.reshape(a.shape[0], PEER_HEADS, n_tiles, PEER_SLABS, LANES)
        return a5.transpose(2, 0, 1, 3, 4)

    tile_spec = pl.BlockSpec((None, nj, PEER_HEADS, PEER_SLABS, LANES), lambda i, e: (e, i, 0, 0, 0))
    tok_spec = pl.BlockSpec((nj, PEER_HEADS, PEER_NKEYS, LANES), lambda i, e: (i, 0, 0, 0))
    return pl.pallas_call(
        functools.partial(_peer_dense_body, norm_out=norm_out, emit_bf16=emit_bf16),
        grid=(t // tb, n_tiles),
        in_specs=[
            pl.BlockSpec((tb, d), lambda i, e: (i, 0)),
            pl.BlockSpec((1, d), lambda i, e: (0, 0)),
            tab_spec, tab_spec,
            tile_spec, tile_spec, tok_spec, tok_spec,
            pl.BlockSpec((1, d), lambda i, e: (0, 0)),
        ],
        out_specs=[o_spec, cast_spec, cast_spec] if emit_bf16 else o_spec,
        out_shape=[o_shape, cast_shape, cast_shape] if emit_bf16 else o_shape,
        scratch_shapes=[pltpu.VMEM((tb, d), BF16), pltpu.VMEM((nj, te, LANES), F32),
                        pltpu.VMEM((nj, te, LANES), BF16), pltpu.VMEM((tb, d), F32)],
        compiler_params=_params(("parallel", "arbitrary")), name="peer_dense",
    )(h, g.reshape(1, d), u_tab, v_tab, by_tile(ns), by_tile(g1), r2, e2, out_g.reshape(1, d))


def _peer(h, g, wq, k1, k2, u_tab, v_tab, layer, out_g, norm_out):
    q = _matmul(h, wq, norm_g=g)
    return _peer_dense(h, g, u_tab, v_tab, layer, _peer_select(q, k1, k2), out_g, norm_out)


def kernel(x_prompt, x_sample, state_b_k, state_b_v, state_c_s, state_d_re, state_d_im, norm1_g, norm2_g, final_g, w_in_even, w_out_even, a_ws, a_bs, b_sink, w_in_odd, w_out_odd, c_norm_g, d_a_re, d_a_im, d_log_dt, d_b_re, d_b_im, d_c_re, d_c_im, d_d, d_glu_w, d_glu_b, peer_wq, peer_k1, peer_k2, peer_u, peer_v):
    seq = x_prompt.shape[1]
    n_batch, n_new = x_sample.shape[:2]
    past = PAST_LEN
    hp = x_prompt.reshape(seq, D_MODEL)
    hs = x_sample.reshape(n_batch * n_new, D_MODEL)
    bf = lambda a: a.astype(BF16)

    def pad_tokens(a):
        a3 = a.reshape(n_batch, n_new, a.shape[-1])
        return jnp.pad(a3, ((0, 0), (0, T_PAD - n_new), (0, 0)))

    def unpad_tokens(a3):
        return a3[:, :n_new].reshape(n_batch * n_new, a3.shape[-1])

    w_in = bf(w_in_even[0])
    w_out = bf(w_out_even[0])
    zp = _matmul(hp, w_in, norm_g=norm1_g[0])
    zs = _matmul(hs, w_in, norm_g=norm1_g[0])
    bs_full = jnp.broadcast_to(a_bs[0][:, :, None], (A_GROUPS, CHUNK, LANES))
    yp = _even_prompt(zp, a_ws[0], bs_full, b_sink[0])
    ws_small = jnp.tril(a_ws[0][:, :n_new, :n_new]).transpose(2, 1, 0)
    wa = jnp.repeat(jnp.pad(ws_small, ((0, 0), (0, T_PAD - n_new), (0, 0))), LANES, axis=-1)
    wb = jnp.repeat(jnp.pad(a_bs[0][:, :n_new].T, ((0, T_PAD - n_new), (0, 0))), LANES, axis=-1)
    win = state_b_k.shape[2]
    kbuf = state_b_k[0].reshape(n_batch, win, B_KV_HEADS * B_DH)
    vbuf = state_b_v[0].reshape(n_batch, win, B_KV_HEADS * B_DH)
    ys3, av3 = _even_sample(pad_tokens(zs), kbuf, vbuf, wa, wb, b_sink[0])
    hp = _matmul(yp, w_out, resid=hp)
    hs = _matmul(unpad_tokens(ys3), w_out, resid=hs)

    k_off = 2 * A_WIDTH + B_HEADS * B_DH
    v_off = k_off + B_KV_HEADS * B_DH
    kv_shape = (1, -1, win, B_KV_HEADS, B_DH)
    a_v_sample = av3[:, :n_new].reshape(1, n_batch, n_new, A_GROUPS, A_WIDTH // A_GROUPS)
    b_k_prompt = zp[seq - win:, k_off:v_off].reshape(kv_shape)
    b_v_prompt = zp[seq - win:, v_off:].reshape(kv_shape)
    k_new = zs[:, k_off:v_off].reshape(n_batch, n_new, B_KV_HEADS * B_DH)
    v_new = zs[:, v_off:].reshape(n_batch, n_new, B_KV_HEADS * B_DH)
    b_k_sample = jnp.concatenate([kbuf, k_new], axis=1)[:, -win:].reshape(kv_shape)
    b_v_sample = jnp.concatenate([vbuf, v_new], axis=1)[:, -win:].reshape(kv_shape)

    wq = bf(peer_wq[0])
    hs, u_b, v_b = _peer(hs, norm2_g[0], wq, peer_k1[0], peer_k2[0], peer_u, peer_v, 0, final_g, False)
    hp = _peer(hp, norm2_g[0], wq, peer_k1[0], peer_k2[0], u_b, v_b, 0, final_g, False)

    w_in = bf(w_in_odd[0])
    w_out = bf(w_out_odd[0])
    c_width = C_HEADS * C_DV
    zp = _matmul(hp, w_in, norm_g=norm1_g[1])
    zs = _matmul(hs, w_in, norm_g=norm1_g[1])
    cos_p, sin_p = _rope_tables(jnp.arange(seq))
    cos_s, sin_s = _rope_tables(past + jnp.arange(T_PAD))
    ycp, c_s_prompt = _retention_prompt(zp, cos_p, sin_p, _retention_tables(CHUNK, CHUNK), c_norm_g[0])
    zs3 = pad_tokens(zs)
    ycs3, c_s_sample = _retention_sample(zs3, cos_s, sin_s, _retention_tables(T_PAD, n_new), c_norm_g[0],
                                         state_c_s[0])
    par = _s5_params(d_a_re[0], d_a_im[0], d_log_dt[0], d_b_re[0], d_b_im[0], d_c_re[0], d_c_im[0], d_d[0])
    u_off = 3 * c_width
    ydp_pre, d_prompt = _s5_prompt(zp, u_off // c_width, par)
    us_ts = zs[:, u_off:].reshape(n_batch, n_new, D_ROWS, D_ROW_CH).transpose(1, 2, 0, 3)
    x_re = state_d_re[0].reshape(n_batch, D_ROWS, D_ROW_STATE).transpose(1, 0, 2)
    x_im = state_d_im[0].reshape(n_batch, D_ROWS, D_ROW_STATE).transpose(1, 0, 2)
    yds_ts, s_re, s_im = _s5_sample(us_ts, x_re, x_im, par)
    glu_w = bf(d_glu_w[0])
    ydp = _glu(ydp_pre, glu_w, d_glu_b[0])
    yds = _glu(yds_ts.transpose(2, 0, 1, 3).reshape(n_batch * n_new, c_width), glu_w, d_glu_b[0])
    hp = _matmul(ycp, w_out[:c_width], x2=ydp, w2=w_out[c_width:], resid=hp)
    hs = _matmul(unpad_tokens(ycs3), w_out[:c_width], x2=yds, w2=w_out[c_width:], resid=hs)

    wq = bf(peer_wq[1])
    y_sample, u_b, v_b = _peer(hs, norm2_g[1], wq, peer_k1[1], peer_k2[1], peer_u, peer_v, 1, final_g, True)
    y_prompt = _peer(hp, norm2_g[1], wq, peer_k1[1], peer_k2[1], u_b, v_b, 1, final_g, True)
    y_prompt = y_prompt.reshape(x_prompt.shape)
    y_sample = y_sample.reshape(x_sample.shape)

    d_shape = (1, -1, D_GROUPS, D_STATE)
    return (y_prompt, y_sample, a_v_sample, b_k_prompt, b_v_prompt, b_k_sample, b_v_sample,
            c_s_prompt.reshape(1, 1, C_HEADS, C_DK, C_DV), c_s_sample[None],
            d_prompt[0].reshape(d_shape), d_prompt[1].reshape(d_shape),
            s_re.transpose(1, 0, 2).reshape(d_shape), s_im.transpose(1, 0, 2).reshape(d_shape))
```

```python
import functools
import math

import jax
import jax.numpy as jnp
from jax import lax
from jax.experimental import pallas as pl
from jax.experimental.pallas import tpu as pltpu

F32 = jnp.float32
BF16 = jnp.bfloat16
HIGHEST = lax.Precision.HIGHEST

EPS = 1e-6
NEG_BIG = -1e30
NEG_INF = float("-inf")

D_MODEL = 2048
PAST_LEN = 8192
LANES = 128
SUBLANES = 8
VMEM_LIMIT = 56 * 1024 * 1024

CHUNK = 128
A_GROUPS = 8
A_WIDTH = 1024
B_HEADS = 16
B_KV_HEADS = 4
B_GQA = 4
B_DH = 64
C_HEADS = 8
C_DK = 64
C_DV = 128
RET_DECAY_EXP0 = 5.0
ROPE_BASE = 10000.0
D_GROUPS = 64
D_STATE = 64
D_GROUP_CH = 16
D_ROWS = 8
D_ROW_STATE = 512
D_ROW_CH = 128
PEER_HEADS = 8
PEER_NKEYS = 128
PEER_TOPK = 16
PEER_HALF = 128


def _params(semantics):
    return pltpu.CompilerParams(dimension_semantics=semantics, vmem_limit_bytes=VMEM_LIMIT)


def _gelu(x):
    return 0.5 * x * (1.0 + lax.erf(x * (1.0 / math.sqrt(2.0))))


def _rmsnorm(x, g):
    return x * lax.rsqrt(jnp.mean(x * x, axis=-1, keepdims=True) + EPS) * g


def _dot_nt(a, b, precision=None):
    return lax.dot_general(a, b, (((1,), (1,)), ((), ())), precision=precision,
                           preferred_element_type=F32)


def _dot_tn(a, b, precision=None):
    return lax.dot_general(a, b, (((0,), (0,)), ((), ())), precision=precision,
                           preferred_element_type=F32)


def _mm_body(*refs, has_norm, has_pair, has_resid):
    it = iter(refs)
    x_ref, w_ref = next(it), next(it)
    g_ref = next(it) if has_norm else None
    x2_ref, w2_ref = (next(it), next(it)) if has_pair else (None, None)
    r_ref = next(it) if has_resid else None
    o_ref = next(it)
    x = x_ref[...]
    if has_norm:
        x = _rmsnorm(x, g_ref[...])
    xb = x.astype(BF16)
    x2b = x2_ref[...].astype(BF16) if has_pair else None
    n = o_ref.shape[1]
    for j in range(n // MM_COLS):
        cols = slice(j * MM_COLS, (j + 1) * MM_COLS)
        acc = jnp.dot(xb, w_ref[:, cols], preferred_element_type=F32)
        if has_pair:
            acc = acc + jnp.dot(x2b, w2_ref[:, cols], preferred_element_type=F32)
        if has_resid:
            acc = acc + r_ref[:, cols]
        o_ref[:, cols] = acc


MM_COLS = 512


def _matmul(x, w, *, norm_g=None, x2=None, w2=None, resid=None, tm=512):
    m, k = x.shape
    n = w.shape[1]
    assert m % tm == 0 and n % MM_COLS == 0
    resident = pl.Buffered(1)
    args = [x, w]
    specs = [pl.BlockSpec((tm, k), lambda i: (i, 0)),
             pl.BlockSpec((k, n), lambda i: (0, 0), pipeline_mode=resident)]
    if norm_g is not None:
        args.append(norm_g.reshape(1, k))
        specs.append(pl.BlockSpec((1, k), lambda i: (0, 0)))
    if x2 is not None:
        k2 = x2.shape[1]
        args += [x2, w2]
        specs += [pl.BlockSpec((tm, k2), lambda i: (i, 0)),
                  pl.BlockSpec((k2, n), lambda i: (0, 0), pipeline_mode=resident)]
    if resid is not None:
        args.append(resid)
        specs.append(pl.BlockSpec((tm, n), lambda i: (i, 0)))
    body = functools.partial(_mm_body, has_norm=norm_g is not None, has_pair=x2 is not None,
                             has_resid=resid is not None)
    return pl.pallas_call(
        body, grid=(m // tm,), in_specs=specs,
        out_specs=pl.BlockSpec((tm, n), lambda i: (i, 0)),
        out_shape=jax.ShapeDtypeStruct((m, n), F32),
        compiler_params=_params(("parallel",)), name="matmul",
    )(*args)


def _sink_column(sink_ref, kvh, rows_per_head, n_rows):
    grp = lax.broadcasted_iota(jnp.int32, (n_rows, 1), 0) // rows_per_head
    sk = jnp.full((n_rows, 1), sink_ref[kvh * B_GQA + B_GQA - 1], F32)
    for g in range(B_GQA - 2, -1, -1):
        sk = jnp.where(grp == g, sink_ref[kvh * B_GQA + g], sk)
    return sk


def _even_prompt_body(sink_ref, au_ref, av_ref, q_ref, kvc_ref, kvp_ref, ws_ref, bs_ref, o_ref):
    blk = pl.program_id(0)
    au = _gelu(au_ref[...])
    av = _gelu(av_ref[...])
    row = lax.broadcasted_iota(jnp.int32, (CHUNK, CHUNK), 0)
    col = lax.broadcasted_iota(jnp.int32, (CHUNK, CHUNK), 1)
    causal = row >= col
    for g in range(A_GROUPS):
        lanes = slice(g * LANES, (g + 1) * LANES)
        w = jnp.where(causal, ws_ref[g], 0.0).astype(BF16)
        mixed = jnp.dot(w, av[:, lanes].astype(BF16), preferred_element_type=F32) + bs_ref[g]
        o_ref[:, lanes] = au[:, lanes] * mixed

    q = q_ref[...]
    kvc = kvc_ref[...]
    kvp = kvp_ref[...]
    n_rows = B_GQA * CHUNK
    qi = lax.broadcasted_iota(jnp.int32, (n_rows, 2 * CHUNK), 0) % CHUNK
    kc = lax.broadcasted_iota(jnp.int32, (n_rows, 2 * CHUNK), 1)
    dist = qi + CHUNK - kc
    allowed = (dist >= 0) & (dist < CHUNK) & ((kc >= CHUNK) | (blk > 0))
    outs = []
    for kvh in range(B_KV_HEADS):
        ks = slice(kvh * B_DH, (kvh + 1) * B_DH)
        vs = slice(B_KV_HEADS * B_DH + kvh * B_DH, B_KV_HEADS * B_DH + (kvh + 1) * B_DH)
        kk = jnp.concatenate([kvp[:, ks], kvc[:, ks]], axis=0).astype(BF16)
        vv = jnp.concatenate([kvp[:, vs], kvc[:, vs]], axis=0).astype(BF16)
        q4 = jnp.concatenate(
            [q[:, (kvh * B_GQA + g) * B_DH:(kvh * B_GQA + g + 1) * B_DH] for g in range(B_GQA)], axis=0)
        s = _dot_nt(q4.astype(BF16), kk) * (B_DH ** -0.5)
        s = jnp.where(allowed, s, NEG_BIG)
        sk = _sink_column(sink_ref, kvh, CHUNK, n_rows)
        mx = jnp.maximum(jnp.max(s, axis=-1, keepdims=True), sk)
        p = jnp.exp(s - mx)
        p = p / (jnp.sum(p, axis=-1, keepdims=True) + jnp.exp(sk - mx))
        o = jnp.dot(p.astype(BF16), vv, preferred_element_type=F32)
        outs += [o[g * CHUNK:(g + 1) * CHUNK] for g in range(B_GQA)]
    o_ref[:, A_WIDTH:] = jnp.concatenate(outs, axis=1)


def _even_prompt(z, ws, bs_full, sink):
    t = z.shape[0]
    nb = t // CHUNK
    wide = A_WIDTH
    kvw = 2 * B_KV_HEADS * B_DH
    kv_blk = (2 * A_WIDTH + B_HEADS * B_DH) // kvw
    return pl.pallas_call(
        _even_prompt_body, grid=(nb,),
        in_specs=[
            pl.BlockSpec(memory_space=pltpu.SMEM),
            pl.BlockSpec((CHUNK, wide), lambda i: (i, 0)),
            pl.BlockSpec((CHUNK, wide), lambda i: (i, 1)),
            pl.BlockSpec((CHUNK, wide), lambda i: (i, 2)),
            pl.BlockSpec((CHUNK, kvw), lambda i: (i, kv_blk)),
            pl.BlockSpec((CHUNK, kvw), lambda i: (jnp.maximum(i - 1, 0), kv_blk)),
            pl.BlockSpec((A_GROUPS, CHUNK, CHUNK), lambda i: (0, 0, 0)),
            pl.BlockSpec((A_GROUPS, CHUNK, LANES), lambda i: (0, 0, 0)),
        ],
        out_specs=pl.BlockSpec((CHUNK, D_MODEL), lambda i: (i, 0)),
        out_shape=jax.ShapeDtypeStruct((t, D_MODEL), F32),
        compiler_params=_params(("parallel",)), name="even_prompt",
    )(sink, z, z, z, z, z, ws, bs_full)


EVEN_SAMPLE_BATCH = 8
T_PAD = 8


def _even_sample_body(sink_ref, au_ref, av_ref, q_ref, kv_ref, kb_ref, vb_ref, wa_ref, wb_ref,
                      y_ref, avo_ref):
    kv_width = B_KV_HEADS * B_DH
    n_rows = B_HEADS * T_PAD
    tq = lax.broadcasted_iota(jnp.int32, (n_rows, CHUNK), 0) % T_PAD
    kc = lax.broadcasted_iota(jnp.int32, (n_rows, CHUNK), 1)
    buf_allowed = kc > tq
    tq1 = lax.broadcasted_iota(jnp.int32, (n_rows, 1), 0) % T_PAD
    head_of_row = lax.broadcasted_iota(jnp.int32, (n_rows, 1), 0) // T_PAD
    sk = jnp.full((n_rows, 1), sink_ref[B_HEADS - 1], F32)
    for hd in range(B_HEADS - 2, -1, -1):
        sk = jnp.where(head_of_row == hd, sink_ref[hd], sk)
    lane_kvh = lax.broadcasted_iota(jnp.int32, (T_PAD, kv_width), 1) // B_DH
    scale = B_DH ** -0.5
    n_new = wa_ref.shape[0]
    for b in range(EVEN_SAMPLE_BATCH):
        au = _gelu(au_ref[b])
        av = _gelu(av_ref[b])
        avo_ref[b] = av
        mixed = wb_ref[...]
        for j in range(n_new):
            mixed = mixed + wa_ref[j] * av[j:j + 1, :]
        y_ref[b, :, :A_WIDTH] = au * mixed

        q = q_ref[b]
        kv = kv_ref[b]
        blocks = []
        for kvh in range(B_KV_HEADS):
            for g in range(B_GQA):
                col = (kvh * B_GQA + g) * B_DH
                qg = jnp.concatenate([q[:, col:col + B_DH]] * B_KV_HEADS, axis=1)
                blocks.append(jnp.where(lane_kvh == kvh, qg, 0.0))
        qbd = jnp.concatenate(blocks, axis=0).astype(BF16)
        s_buf = _dot_nt(qbd, kb_ref[b].astype(BF16)) * scale
        s_buf = jnp.where(buf_allowed, s_buf, NEG_BIG)
        qbd_r = qbd.astype(F32)
        s_new = []
        for j in range(n_new):
            kj = kv[j:j + 1, :kv_width].astype(BF16).astype(F32)
            sj = jnp.sum(qbd_r * kj, axis=-1, keepdims=True) * scale
            s_new.append(jnp.where(tq1 >= j, sj, NEG_BIG))
        mx = jnp.maximum(jnp.max(s_buf, axis=-1, keepdims=True), sk)
        for sj in s_new:
            mx = jnp.maximum(mx, sj)
        p_buf = jnp.exp(s_buf - mx)
        p_new = [jnp.exp(sj - mx) for sj in s_new]
        den = jnp.sum(p_buf, axis=-1, keepdims=True) + jnp.exp(sk - mx)
        for pj in p_new:
            den = den + pj
        inv = 1.0 / den
        o = jnp.dot((p_buf * inv).astype(BF16), vb_ref[b].astype(BF16), preferred_element_type=F32)
        for j in range(n_new):
            vj = kv[j:j + 1, kv_width:].astype(BF16).astype(F32)
            o = o + (p_new[j] * inv).astype(BF16).astype(F32) * vj
        outs = []
        for kvh in range(B_KV_HEADS):
            for g in range(B_GQA):
                row = (kvh * B_GQA + g) * T_PAD
                outs.append(o[row:row + T_PAD, kvh * B_DH:(kvh + 1) * B_DH])
        y_ref[b, :, A_WIDTH:] = jnp.concatenate(outs, axis=1)


def _even_sample(z3, kbuf, vbuf, wa, wb, sink):
    nb = z3.shape[0]
    bb = EVEN_SAMPLE_BATCH
    kvw = 2 * B_KV_HEADS * B_DH
    kv_blk = (2 * A_WIDTH + B_HEADS * B_DH) // kvw
    win = kbuf.shape[1]
    return pl.pallas_call(
        _even_sample_body, grid=(nb // bb,),
        in_specs=[
            pl.BlockSpec(memory_space=pltpu.SMEM),
            pl.BlockSpec((bb, T_PAD, A_WIDTH), lambda i: (i, 0, 0)),
            pl.BlockSpec((bb, T_PAD, A_WIDTH), lambda i: (i, 0, 1)),
            pl.BlockSpec((bb, T_PAD, A_WIDTH), lambda i: (i, 0, 2)),
            pl.BlockSpec((bb, T_PAD, kvw), lambda i: (i, 0, kv_blk)),
            pl.BlockSpec((bb, win, kvw // 2), lambda i: (i, 0, 0)),
            pl.BlockSpec((bb, win, kvw // 2), lambda i: (i, 0, 0)),
            pl.BlockSpec(wa.shape, lambda i: (0, 0, 0)),
            pl.BlockSpec(wb.shape, lambda i: (0, 0)),
        ],
        out_specs=[
            pl.BlockSpec((bb, T_PAD, D_MODEL), lambda i: (i, 0, 0)),
            pl.BlockSpec((bb, T_PAD, A_WIDTH), lambda i: (i, 0, 0)),
        ],
        out_shape=[
            jax.ShapeDtypeStruct((nb, T_PAD, D_MODEL), F32),
            jax.ShapeDtypeStruct((nb, T_PAD, A_WIDTH), F32),
        ],
        compiler_params=_params(("parallel",)), name="even_sample",
    )(sink, z3, z3, z3, z3, kbuf, vbuf, wa, wb)


def _rope(x, cos_f, sin_s):
    width = x.shape[-1]
    half = C_DK // 2
    lane = lax.broadcasted_iota(jnp.int32, x.shape, 1) % C_DK
    swapped = jnp.where(lane < half, pltpu.roll(x, width - half, 1), pltpu.roll(x, half, 1))
    return x * cos_f + swapped * sin_s


def _retention_chunk(qh, kh, vh, st, decay, qdec, kdec, sdec):
    qb, vb = qh.astype(BF16), vh.astype(BF16)
    scores = _dot_nt(qb, kh.astype(BF16)) * decay
    o = jnp.dot(scores.astype(BF16), vb, preferred_element_type=F32)
    o = o + jnp.dot(qb, st.astype(BF16), preferred_element_type=F32) * qdec
    new_st = sdec * st + _dot_tn((kh * kdec).astype(BF16), vb)
    return o, new_st


def _groupnorm_gate(o, gain, gate):
    mu = jnp.mean(o, axis=-1, keepdims=True)
    var = jnp.mean(jnp.square(o - mu), axis=-1, keepdims=True)
    return (o - mu) * lax.rsqrt(var + EPS) * gain * (gate * jax.nn.sigmoid(gate))


def _retention_body(qk_ref, v_ref, g_ref, cos_ref, sin_ref, decay_ref, qdec_ref, kdec_ref, sdec_ref,
                    gain_ref, o_ref, st_ref, state):
    @pl.when(pl.program_id(0) == 0)
    def _():
        state[...] = jnp.zeros_like(state)

    qk = qk_ref[...]
    width = C_HEADS * C_DK
    q = _rope(qk[:, :width], cos_ref[...], sin_ref[...]) * (C_DK ** -0.5)
    k = _rope(qk[:, width:], cos_ref[...], sin_ref[...])
    v = v_ref[...]
    g = g_ref[...]
    for h in range(C_HEADS):
        qh = q[:, h * C_DK:(h + 1) * C_DK]
        kh = k[:, h * C_DK:(h + 1) * C_DK]
        vh = v[:, h * C_DV:(h + 1) * C_DV]
        st = state[h]
        o, new_st = _retention_chunk(qh, kh, vh, st, decay_ref[h], qdec_ref[h], kdec_ref[h], sdec_ref[h])
        state[h] = new_st
        st_ref[h] = new_st
        lanes = slice(h * C_DV, (h + 1) * C_DV)
        o_ref[:, lanes] = _groupnorm_gate(o, gain_ref[:, lanes], g[:, lanes])


def _retention_prompt(z, cos_f, sin_s, tabs, gain):
    t = z.shape[0]
    nc = t // CHUNK
    decay, qdec, kdec, sdec = tabs
    width = C_HEADS * C_DV
    const3 = lambda i: (0, 0, 0)
    return pl.pallas_call(
        _retention_body, grid=(nc,),
        in_specs=[
            pl.BlockSpec((CHUNK, width), lambda i: (i, 0)),
            pl.BlockSpec((CHUNK, width), lambda i: (i, 1)),
            pl.BlockSpec((CHUNK, width), lambda i: (i, 2)),
            pl.BlockSpec((CHUNK, C_HEADS * C_DK), lambda i: (i, 0)),
            pl.BlockSpec((CHUNK, C_HEADS * C_DK), lambda i: (i, 0)),
            pl.BlockSpec(decay.shape, const3),
            pl.BlockSpec(qdec.shape, const3),
            pl.BlockSpec(kdec.shape, const3),
            pl.BlockSpec(sdec.shape, const3),
            pl.BlockSpec((1, width), lambda i: (0, 0)),
        ],
        out_specs=[
            pl.BlockSpec((CHUNK, width), lambda i: (i, 0)),
            pl.BlockSpec((C_HEADS, C_DK, C_DV), const3),
        ],
        out_shape=[
            jax.ShapeDtypeStruct((t, width), F32),
            jax.ShapeDtypeStruct((C_HEADS, C_DK, C_DV), F32),
        ],
        scratch_shapes=[pltpu.VMEM((C_HEADS, C_DK, C_DV), F32)],
        compiler_params=_params(("arbitrary",)), name="retention_prompt",
    )(z, z, z, cos_f, sin_s, decay, qdec, kdec, sdec, gain.reshape(1, width))


RET_SAMPLE_BATCH = 8


def _retention_sample_body(qk_ref, v_ref, g_ref, cos_ref, sin_ref, decay_ref, qdec_ref, kdec_ref,
                           sdec_ref, gain_ref, st_in_ref, o_ref, st_ref):
    width = C_HEADS * C_DK
    lane_head = lax.broadcasted_iota(jnp.int32, (T_PAD, width), 1) // C_DK
    for b in range(RET_SAMPLE_BATCH):
        qk = qk_ref[b]
        q = _rope(qk[:, :width], cos_ref[...], sin_ref[...]) * (C_DK ** -0.5)
        k = _rope(qk[:, width:], cos_ref[...], sin_ref[...])
        v = v_ref[b]
        g = g_ref[b]
        vb = v.astype(BF16)
        qbd = jnp.concatenate([jnp.where(lane_head == h, q, 0.0) for h in range(C_HEADS)],
                              axis=0).astype(BF16)
        st = st_in_ref[b]
        scores = _dot_nt(qbd, k.astype(BF16)) * decay_ref[...]
        o_intra = jnp.dot(scores.astype(BF16), vb, preferred_element_type=F32)
        o_cross = jnp.dot(qbd, st.astype(BF16), preferred_element_type=F32) * qdec_ref[...]
        kd = (k * kdec_ref[...]).astype(BF16)
        for h in range(C_HEADS):
            rows = slice(h * T_PAD, (h + 1) * T_PAD)
            lanes = slice(h * C_DV, (h + 1) * C_DV)
            keys = slice(h * C_DK, (h + 1) * C_DK)
            o = o_intra[rows, lanes] + o_cross[rows, :]
            o_ref[b, :, lanes] = _groupnorm_gate(o, gain_ref[:, lanes], g[:, lanes])
            st_ref[b, keys, :] = sdec_ref[keys, :] * st[keys, :] + _dot_tn(kd[:, keys], vb[:, lanes])


def _retention_sample(z3, cos_f, sin_s, tabs, gain, st_in):
    nb = z3.shape[0]
    bb = RET_SAMPLE_BATCH
    decay, qdec, kdec, sdec = tabs
    width = C_HEADS * C_DV
    decay_s = decay.reshape(C_HEADS * T_PAD, T_PAD)
    qdec_s = qdec.reshape(C_HEADS * T_PAD, C_DV)
    kdec_s = kdec.transpose(1, 0, 2).reshape(T_PAD, C_HEADS * C_DK)
    sdec_s = sdec.reshape(C_HEADS * C_DK, C_DV)
    st2 = st_in.reshape(nb, C_HEADS * C_DK, C_DV)
    c2 = lambda i: (0, 0)
    out, st_out = pl.pallas_call(
        _retention_sample_body, grid=(nb // bb,),
        in_specs=[
            pl.BlockSpec((bb, T_PAD, width), lambda i: (i, 0, 0)),
            pl.BlockSpec((bb, T_PAD, width), lambda i: (i, 0, 1)),
            pl.BlockSpec((bb, T_PAD, width), lambda i: (i, 0, 2)),
            pl.BlockSpec((T_PAD, C_HEADS * C_DK), c2),
            pl.BlockSpec((T_PAD, C_HEADS * C_DK), c2),
            pl.BlockSpec(decay_s.shape, c2),
            pl.BlockSpec(qdec_s.shape, c2),
            pl.BlockSpec(kdec_s.shape, c2),
            pl.BlockSpec(sdec_s.shape, c2),
            pl.BlockSpec((1, width), c2),
            pl.BlockSpec((bb, C_HEADS * C_DK, C_DV), lambda i: (i, 0, 0)),
        ],
        out_specs=[
            pl.BlockSpec((bb, T_PAD, width), lambda i: (i, 0, 0)),
            pl.BlockSpec((bb, C_HEADS * C_DK, C_DV), lambda i: (i, 0, 0)),
        ],
        out_shape=[
            jax.ShapeDtypeStruct((nb, T_PAD, width), F32),
            jax.ShapeDtypeStruct((nb, C_HEADS * C_DK, C_DV), F32),
        ],
        compiler_params=_params(("parallel",)), name="retention_sample",
    )(z3, z3, z3, cos_f, sin_s, decay_s, qdec_s, kdec_s, sdec_s, gain.reshape(1, width), st2)
    return out, st_out.reshape(st_in.shape)


def _retention_tables(length, n_valid):
    log_gamma = jnp.log1p(-jnp.exp2(-RET_DECAY_EXP0 - jnp.arange(C_HEADS, dtype=F32)))
    i = jnp.arange(length, dtype=F32)
    valid = (jnp.arange(length) < n_valid)
    rel = i[:, None] - i[None, :]
    decay = jnp.where(rel >= 0, jnp.exp(jnp.maximum(rel, 0.0)[None] * log_gamma[:, None, None]), 0.0)
    decay = jnp.where(valid[None, None, :], decay, 0.0)
    q_dec = jnp.exp((i[None, :] + 1.0) * log_gamma[:, None])
    k_dec = jnp.where(valid[None, :], jnp.exp((n_valid - 1.0 - i)[None, :] * log_gamma[:, None]), 0.0)
    s_dec = jnp.exp(n_valid * log_gamma)
    qdec = jnp.broadcast_to(q_dec[:, :, None], (C_HEADS, length, C_DV))
    kdec = jnp.broadcast_to(k_dec[:, :, None], (C_HEADS, length, C_DK))
    sdec = jnp.broadcast_to(s_dec[:, None, None], (C_HEADS, C_DK, C_DV))
    return decay, qdec, kdec, sdec


def _rope_tables(pos):
    half = C_DK // 2
    freqs = ROPE_BASE ** (-jnp.arange(half, dtype=F32) / half)
    ang = pos.astype(F32)[:, None] * freqs[None, :]
    cos, sin = jnp.cos(ang), jnp.sin(ang)
    cos_f = jnp.tile(jnp.concatenate([cos, cos], axis=1), (1, C_HEADS))
    sin_s = jnp.tile(jnp.concatenate([-sin, sin], axis=1), (1, C_HEADS))
    return cos_f, sin_s


def _s5_discretize(are_ref, aim_ref, ldt_ref):
    a_re, a_im = are_ref[...], aim_ref[...]
    dt = jnp.exp(ldt_ref[...])
    mag = jnp.exp(a_re * dt)
    ab_re = mag * jnp.cos(a_im * dt)
    ab_im = mag * jnp.sin(a_im * dt)
    num_re, num_im = ab_re - 1.0, ab_im
    den = a_re * a_re + a_im * a_im
    co_re = (num_re * a_re + num_im * a_im) / den
    co_im = (num_im * a_re - num_re * a_im) / den
    return ab_re, ab_im, co_re, co_im


S5_CHUNK = 256


def _s5_prompt_body(u_ref, are_ref, aim_ref, ldt_ref, bre_ref, bim_ref, cre_ref, cim_ref, dd_ref,
                    y_ref, st_ref, hre, him, state, disc):
    n = S5_CHUNK

    @pl.when(pl.program_id(0) == 0)
    def _():
        ab_re, ab_im, co_re, co_im = _s5_discretize(are_ref, aim_ref, ldt_ref)
        disc[0] = ab_re
        disc[1] = ab_im
        disc[2] = co_re
        disc[3] = co_im
        state[...] = jnp.zeros_like(state)

    n_chunks = D_ROW_STATE // LANES
    for s in range(D_ROWS):
        rows = pl.ds(s, n, stride=D_ROWS)
        u_s = u_ref[:, s * D_ROW_CH:(s + 1) * D_ROW_CH]
        u_b = u_s.astype(BF16)
        raw_re = jnp.dot(u_b, bre_ref[s], preferred_element_type=F32)
        raw_im = jnp.dot(u_b, bim_ref[s], preferred_element_type=F32)
        co_re = disc[2, s:s + 1, :]
        co_im = disc[3, s:s + 1, :]
        bu_re = co_re * raw_re - co_im * raw_im
        bu_im = co_re * raw_im + co_im * raw_re
        for c in range(n_chunks):
            hre[c, rows, :] = bu_re[:, c * LANES:(c + 1) * LANES]
            him[c, rows, :] = bu_im[:, c * LANES:(c + 1) * LANES]

    ab_re = disc[0]
    ab_im = disc[1]

    def step(t, carry):
        h_re, h_im = carry
        rows = pl.ds(pl.multiple_of(t * D_ROWS, D_ROWS), D_ROWS)
        bu_re = jnp.concatenate([hre[c, rows, :] for c in range(n_chunks)], axis=1)
        bu_im = jnp.concatenate([him[c, rows, :] for c in range(n_chunks)], axis=1)
        n_re = ab_re * h_re - ab_im * h_im + bu_re
        n_im = ab_re * h_im + ab_im * h_re + bu_im
        for c in range(n_chunks):
            hre[c, rows, :] = n_re[:, c * LANES:(c + 1) * LANES]
            him[c, rows, :] = n_im[:, c * LANES:(c + 1) * LANES]
        return n_re, n_im

    h_re, h_im = lax.fori_loop(0, n, step, (state[0], state[1]), unroll=4)
    state[0] = h_re
    state[1] = h_im
    st_ref[0] = h_re
    st_ref[1] = h_im

    for s in range(D_ROWS):
        rows = pl.ds(s, n, stride=D_ROWS)
        hist_re = jnp.concatenate([hre[c, rows, :] for c in range(n_chunks)], axis=1)
        hist_im = jnp.concatenate([him[c, rows, :] for c in range(n_chunks)], axis=1)
        cols = slice(s * D_ROW_CH, (s + 1) * D_ROW_CH)
        y = (jnp.dot(hist_re.astype(BF16), cre_ref[s], preferred_element_type=F32)
             - jnp.dot(hist_im.astype(BF16), cim_ref[s], preferred_element_type=F32)
             + dd_ref[s:s + 1, :] * u_ref[:, cols])
        y_ref[:, cols] = _gelu(y)


def _s5_prompt(z, u_col_block, par):
    a_re, a_im, ldt, b_re, b_im, c_re, c_im, dd = par
    t = z.shape[0]
    blk = S5_CHUNK * D_ROWS
    width = D_ROWS * D_ROW_CH
    c2 = lambda i: (0, 0)
    c3 = lambda i: (0, 0, 0)
    return pl.pallas_call(
        _s5_prompt_body, grid=(t // S5_CHUNK,),
        in_specs=[
            pl.BlockSpec((S5_CHUNK, width), lambda i: (i, u_col_block)),
            pl.BlockSpec(a_re.shape, c2), pl.BlockSpec(a_im.shape, c2), pl.BlockSpec(ldt.shape, c2),
            pl.BlockSpec(b_re.shape, c3), pl.BlockSpec(b_im.shape, c3),
            pl.BlockSpec(c_re.shape, c3), pl.BlockSpec(c_im.shape, c3),
            pl.BlockSpec(dd.shape, c2),
        ],
        out_specs=[
            pl.BlockSpec((S5_CHUNK, width), lambda i: (i, 0)),
            pl.BlockSpec((2, D_ROWS, D_ROW_STATE), c3),
        ],
        out_shape=[
            jax.ShapeDtypeStruct((t, width), F32),
            jax.ShapeDtypeStruct((2, D_ROWS, D_ROW_STATE), F32),
        ],
        scratch_shapes=[
            pltpu.VMEM((D_ROW_STATE // LANES, blk, LANES), F32),
            pltpu.VMEM((D_ROW_STATE // LANES, blk, LANES), F32),
            pltpu.VMEM((2, D_ROWS, D_ROW_STATE), F32), pltpu.VMEM((4, D_ROWS, D_ROW_STATE), F32),
        ],
        compiler_params=_params(("arbitrary",)), name="s5_prompt",
    )(z, a_re, a_im, ldt, b_re.astype(BF16), b_im.astype(BF16), c_re.astype(BF16), c_im.astype(BF16), dd)


def _s5_sample_body(u_ref, xre_ref, xim_ref, are_ref, aim_ref, ldt_ref, bre_ref, bim_ref, cre_ref,
                    cim_ref, dd_ref, y_ref, sre_ref, sim_ref):
    ab_re, ab_im, co_re, co_im = _s5_discretize(are_ref, aim_ref, ldt_ref)
    n_t = u_ref.shape[0]
    for s in range(D_ROWS):
        a_r, a_i = ab_re[s:s + 1, :], ab_im[s:s + 1, :]
        c_r, c_i = co_re[s:s + 1, :], co_im[s:s + 1, :]
        h_re, h_im = xre_ref[s], xim_ref[s]
        for t in range(n_t):
            u = u_ref[t, s]
            raw_re = jnp.dot(u, bre_ref[s], precision=HIGHEST, preferred_element_type=F32)
            raw_im = jnp.dot(u, bim_ref[s], precision=HIGHEST, preferred_element_type=F32)
            bu_re = c_r * raw_re - c_i * raw_im
            bu_im = c_r * raw_im + c_i * raw_re
            h_re, h_im = a_r * h_re - a_i * h_im + bu_re, a_r * h_im + a_i * h_re + bu_im
            y = (jnp.dot(h_re, cre_ref[s], precision=HIGHEST, preferred_element_type=F32)
                 - jnp.dot(h_im, cim_ref[s], precision=HIGHEST, preferred_element_type=F32)
                 + dd_ref[s:s + 1, :] * u)
            y_ref[t, s] = _gelu(y)
        sre_ref[s] = h_re
        sim_ref[s] = h_im


def _s5_sample(u_ts, x_re, x_im, par):
    a_re, a_im, ldt, b_re, b_im, c_re, c_im, dd = par
    return pl.pallas_call(
        _s5_sample_body,
        out_shape=[
            jax.ShapeDtypeStruct(u_ts.shape, F32),
            jax.ShapeDtypeStruct(x_re.shape, F32),
            jax.ShapeDtypeStruct(x_im.shape, F32),
        ],
        compiler_params=pltpu.CompilerParams(vmem_limit_bytes=VMEM_LIMIT), name="s5_sample",
    )(u_ts, x_re, x_im, a_re, a_im, ldt, b_re, b_im, c_re, c_im, dd)


def _s5_params(a_re, a_im, log_dt, b_re, b_im, c_re, c_im, dd):
    eye = jnp.eye(D_ROWS, dtype=F32)

    def rows(a):
        return a.reshape(D_ROWS, D_ROW_STATE)

    def b_blocks(b):
        b4 = b.reshape(D_ROWS, D_ROWS, D_STATE, D_GROUP_CH).transpose(0, 1, 3, 2)
        return jnp.einsum('sgcp,gh->sgchp', b4, eye).reshape(D_ROWS, D_ROW_CH, D_ROW_STATE)

    def c_blocks(c):
        c4 = c.reshape(D_ROWS, D_ROWS, D_GROUP_CH, D_STATE)
        return jnp.einsum('sgcp,gh->sgphc', c4, eye).reshape(D_ROWS, D_ROW_STATE, D_ROW_CH)

    ldt = jnp.broadcast_to(log_dt[:, None], (D_GROUPS, D_STATE))
    return (rows(a_re), rows(a_im), rows(ldt), b_blocks(b_re), b_blocks(b_im),
            c_blocks(c_re), c_blocks(c_im), dd.reshape(D_ROWS, D_ROW_CH))


def _glu_body(x_ref, w_ref, b_ref, o_ref):
    x = x_ref[...]
    gate = jnp.dot(x.astype(BF16), w_ref[...], preferred_element_type=F32) + b_ref[...]
    o_ref[...] = x * jax.nn.sigmoid(gate)


def _glu(x, w, b, tm=512):
    m, k = x.shape
    return pl.pallas_call(
        _glu_body, grid=(m // tm,),
        in_specs=[pl.BlockSpec((tm, k), lambda i: (i, 0)), pl.BlockSpec((k, k), lambda i: (0, 0)),
                  pl.BlockSpec((1, k), lambda i: (0, 0))],
        out_specs=pl.BlockSpec((tm, k), lambda i: (i, 0)),
        out_shape=jax.ShapeDtypeStruct((m, k), F32),
        compiler_params=_params(("parallel",)), name="glu",
    )(x, w, b.reshape(1, k))


PEER_SEL_TOKENS = 256


def _extract_top(s, n_out, exact, want_rank=False):
    n_rows = s.shape[0]
    rid = lax.broadcasted_iota(jnp.int32, s.shape, 0).astype(F32) if exact else None
    rank = jnp.full(s.shape, float(n_out), F32) if want_rank else None
    vals = []
    for r in range(n_out):
        m = jnp.max(s, axis=0, keepdims=True)
        if exact:
            first = jnp.min(jnp.where(s == m, rid, float(n_rows)), axis=0, keepdims=True)
            hit = rid == first
        else:
            hit = s == m
        if want_rank:
            rank = jnp.where(hit, float(r), rank)
        s = jnp.where(hit, NEG_INF, s)
        vals.append(m)
    return vals, s, rank


def _removed_count(s):
    return jnp.sum(jnp.where(s == NEG_INF, 1.0, 0.0), axis=0, keepdims=True)


_CAND_COUNTS = [PEER_TOPK // (a + 1) for a in range(PEER_TOPK)]


def _candidates(v1, v2):
    v1_all = _stack_rows(v1)
    v2_all = _stack_rows(v2)
    rid = lax.broadcasted_iota(jnp.int32, (SUBLANES, v1[0].shape[1]), 0)
    tiles = [v1[0] + v2_all]
    n_pad = 0
    a = 1
    while _CAND_COUNTS[a] > 1:
        tiles.append(jnp.where(rid < _CAND_COUNTS[a], v1[a] + v2_all[:SUBLANES], NEG_INF))
        n_pad += SUBLANES - _CAND_COUNTS[a]
        a += 1
    assert PEER_TOPK - a == SUBLANES
    tiles.append(v1_all[a:] + v2[0])
    return jnp.concatenate(tiles, axis=0), n_pad


def _stack_rows(rows):
    n = len(rows)
    rid = lax.broadcasted_iota(jnp.int32, (n, rows[0].shape[1]), 0)
    out = jnp.broadcast_to(rows[n - 1], rid.shape)
    for i in range(n - 2, -1, -1):
        out = jnp.where(rid == i, rows[i], out)
    return out


def _selected_per_first_key(removed, rank1):
    counts = [jnp.sum(removed[:PEER_TOPK], axis=0, keepdims=True)]
    row = PEER_TOPK
    a = 1
    while _CAND_COUNTS[a] > 1:
        counts.append(jnp.sum(removed[row:row + SUBLANES], axis=0, keepdims=True))
        row += SUBLANES
        a += 1
    for i in range(PEER_TOPK - a):
        counts.append(removed[row + i:row + i + 1])
    n_sel = jnp.zeros(rank1.shape, F32)
    for a, cnt in enumerate(counts):
        n_sel = jnp.where(rank1 == float(a), cnt, n_sel)
    return n_sel


def _peer_select_body(q_ref, k1_ref, k2_ref, ns_ref, g1_ref, r2_ref, e2_ref):
    def head(h, exact):
        q1 = q_ref[:, (2 * h) * PEER_HALF:(2 * h + 1) * PEER_HALF]
        q2 = q_ref[:, (2 * h + 1) * PEER_HALF:(2 * h + 2) * PEER_HALF]
        s1 = _dot_nt(k1_ref[h], q1, HIGHEST)
        s2 = _dot_nt(k2_ref[h], q2, HIGHEST)
        v1, left1, rank1 = _extract_top(s1, PEER_TOPK, exact, want_rank=True)
        v2, left2, rank2 = _extract_top(s2, PEER_TOPK, exact, want_rank=True)
        cand, n_pad = _candidates(v1, v2)
        top, left_c, _ = _extract_top(cand, PEER_TOPK, exact)
        mx = top[0]
        z = jnp.exp(top[0] - mx)
        for kk in range(1, PEER_TOPK):
            z = z + jnp.exp(top[kk] - mx)
        g1 = jnp.where(rank1 < PEER_TOPK, jnp.exp(s1 - v1[0]), 0.0) / z
        e2 = jnp.where(rank2 < PEER_TOPK, jnp.exp(s2 - v2[0]), 0.0)
        removed = jnp.where((left_c == NEG_INF) & (cand != NEG_INF), 1.0, 0.0)
        n_sel = _selected_per_first_key(removed, rank1)
        for j in range(PEER_SEL_TOKENS // LANES):
            lanes = slice(j * LANES, (j + 1) * LANES)
            ns_ref[j, h] = n_sel[:, lanes]
            g1_ref[j, h] = g1[:, lanes]
            r2_ref[j, h] = rank2[:, lanes]
            e2_ref[j, h] = e2[:, lanes]
        if exact:
            return None
        ok = ((_removed_count(left1) == PEER_TOPK) & (_removed_count(left2) == PEER_TOPK)
              & (_removed_count(left_c) == PEER_TOPK + n_pad))
        return jnp.where(ok, 0.0, 1.0)

    repeated = [jnp.max(head(h, False)) for h in range(PEER_HEADS)]
    for h in range(PEER_HEADS):
        @pl.when(repeated[h] > 0.0)
        def _():
            head(h, True)


def _peer_select(q, k1, k2):
    t = q.shape[0]
    tb = PEER_SEL_TOKENS
    nj = tb // LANES
    big = pl.BlockSpec((nj, PEER_HEADS, PEER_NKEYS, LANES), lambda i: (i, 0, 0, 0))
    shape = jax.ShapeDtypeStruct((t // LANES, PEER_HEADS, PEER_NKEYS, LANES), F32)
    return pl.pallas_call(
        _peer_select_body, grid=(t // tb,),
        in_specs=[
            pl.BlockSpec((tb, q.shape[1]), lambda i: (i, 0)),
            pl.BlockSpec(k1.shape, lambda i: (0, 0, 0)),
            pl.BlockSpec(k2.shape, lambda i: (0, 0, 0)),
        ],
        out_specs=[big, big, big, big],
        out_shape=[shape, shape, shape, shape],
        compiler_params=_params(("parallel",)), name="peer_select",
    )(q, k1, k2)


PEER_TOKENS = 512
PEER_EXPERT_TILE = 512


def _peer_dense_body(h_ref, g_ref, u_ref, v_ref, ns_ref, g1_ref, r2_ref, e2_ref, og_ref, o_ref, *rest,
                     norm_out, emit_bf16):
    if emit_bf16:
        ub_ref, vb_ref, xn, act_s, coef_s, acc = rest
    else:
        xn, act_s, coef_s, acc = rest
    e = pl.program_id(1)
    n_chunks = PEER_TOKENS // LANES

    @pl.when(e == 0)
    def _():
        xn[...] = _rmsnorm(h_ref[...], g_ref[...]).astype(BF16)
        acc[...] = jnp.zeros_like(acc)

    te = PEER_EXPERT_TILE
    slabs = te // PEER_NKEYS

    def one_tile(part):
        tile_rows = slice(part * te, (part + 1) * te)
        u_tile = u_ref[tile_rows, :].astype(BF16)
        if emit_bf16:
            ub_ref[tile_rows, :] = u_tile
        act = _dot_nt(u_tile, xn[...])
        for j in range(n_chunks):
            act_s[j] = act[:, j * LANES:(j + 1) * LANES]

        def chunk(j, carry):
            for c in range(slabs):
                key1 = part * slabs + c
                gate = jnp.zeros((PEER_NKEYS, LANES), F32)
                for h in range(PEER_HEADS):
                    sel = r2_ref[j, h] < ns_ref[j, h, key1:key1 + 1, :]
                    gate = gate + jnp.where(sel, e2_ref[j, h] * g1_ref[j, h, key1:key1 + 1, :], 0.0)
                rows = slice(c * PEER_NKEYS, (c + 1) * PEER_NKEYS)
                coef_s[j, rows, :] = (gate * _gelu(act_s[j, rows, :])).astype(BF16)
            return carry

        lax.fori_loop(0, n_chunks, chunk, 0)
        coef = jnp.concatenate([coef_s[j] for j in range(n_chunks)], axis=1)
        v_tile = v_ref[tile_rows, :].astype(BF16)
        if emit_bf16:
            vb_ref[tile_rows, :] = v_tile
        acc[...] += _dot_tn(coef, v_tile)

    for part in range(u_ref.shape[0] // te):
        one_tile(part)

    @pl.when(e == pl.num_programs(1) - 1)
    def _():
        out = h_ref[...] + acc[...]
        o_ref[...] = _rmsnorm(out, og_ref[...]) if norm_out else out


def _peer_dense(h, g, u_tab, v_tab, layer, sel, out_g, norm_out):
    ns, g1, r2, e2 = sel
    t, d = h.shape
    emit_bf16 = u_tab.ndim == 3
    n_exp = u_tab.shape[-2]
    tb = PEER_TOKENS
    nj = tb // LANES
    slabs = PEER_EXPERT_TILE // PEER_NKEYS
    o_spec = pl.BlockSpec((tb, d), lambda i, e: (i, 0))
    o_shape = jax.ShapeDtypeStruct((t, d), F32)
    if emit_bf16:
        assert t == tb, "every table tile must be visited exactly once when the casts are emitted"
        te = PEER_EXPERT_TILE
        n_steps = n_exp // te
        tab_spec = pl.BlockSpec((None, te, d), lambda i, e: (layer, e, 0))

        def by_tile(a):
            a5 = a.reshape(a.shape[0], PEER_HEADS, n_steps, slabs, LANES)
            return a5.transpose(2, 0, 1, 3, 4)

        ns, g1 = by_tile(ns), by_tile(g1)
        key_spec = pl.BlockSpec((None, nj, PEER_HEADS, slabs, LANES), lambda i, e: (e, i, 0, 0, 0))
    else:
        te = SUBLANES // slabs * PEER_EXPERT_TILE
        n_steps = n_exp // te
        tab_spec = pl.BlockSpec((te, d), lambda i, e: (e, 0))
        key_spec = pl.BlockSpec((nj, PEER_HEADS, SUBLANES, LANES), lambda i, e: (i, 0, e, 0))
    cast_spec = pl.BlockSpec((te, d), lambda i, e: (e, 0))
    cast_shape = jax.ShapeDtypeStruct((n_exp, d), BF16)
    tok_spec = pl.BlockSpec((nj, PEER_HEADS, PEER_NKEYS, LANES), lambda i, e: (i, 0, 0, 0))
    te_scratch = PEER_EXPERT_TILE
    return pl.pallas_call(
        functools.partial(_peer_dense_body, norm_out=norm_out, emit_bf16=emit_bf16),
        grid=(t // tb, n_steps),
        in_specs=[
            pl.BlockSpec((tb, d), lambda i, e: (i, 0)),
            pl.BlockSpec((1, d), lambda i, e: (0, 0)),
            tab_spec, tab_spec,
            key_spec, key_spec, tok_spec, tok_spec,
            pl.BlockSpec((1, d), lambda i, e: (0, 0)),
        ],
        out_specs=[o_spec, cast_spec, cast_spec] if emit_bf16 else o_spec,
        out_shape=[o_shape, cast_shape, cast_shape] if emit_bf16 else o_shape,
        scratch_shapes=[pltpu.VMEM((tb, d), BF16), pltpu.VMEM((nj, te_scratch, LANES), F32),
                        pltpu.VMEM((nj, te_scratch, LANES), BF16), pltpu.VMEM((tb, d), F32)],
        compiler_params=_params(("parallel", "arbitrary")), name="peer_dense",
    )(h, g.reshape(1, d), u_tab, v_tab, ns, g1, r2, e2, out_g.reshape(1, d))


def _peer(h, g, wq, k1, k2, u_tab, v_tab, layer, out_g, norm_out):
    q = _matmul(h, wq, norm_g=g)
    return _peer_dense(h, g, u_tab, v_tab, layer, _peer_select(q, k1, k2), out_g, norm_out)


def kernel(x_prompt, x_sample, state_b_k, state_b_v, state_c_s, state_d_re, state_d_im, norm1_g, norm2_g, final_g, w_in_even, w_out_even, a_ws, a_bs, b_sink, w_in_odd, w_out_odd, c_norm_g, d_a_re, d_a_im, d_log_dt, d_b_re, d_b_im, d_c_re, d_c_im, d_d, d_glu_w, d_glu_b, peer_wq, peer_k1, peer_k2, peer_u, peer_v):
    seq = x_prompt.shape[1]
    n_batch, n_new = x_sample.shape[:2]
    past = PAST_LEN
    hp = x_prompt.reshape(seq, D_MODEL)
    hs = x_sample.reshape(n_batch * n_new, D_MODEL)
    bf = lambda a: a.astype(BF16)

    def pad_tokens(a):
        a3 = a.reshape(n_batch, n_new, a.shape[-1])
        return jnp.pad(a3, ((0, 0), (0, T_PAD - n_new), (0, 0)))

    def unpad_tokens(a3):
        return a3[:, :n_new].reshape(n_batch * n_new, a3.shape[-1])

    w_in = bf(w_in_even[0])
    w_out = bf(w_out_even[0])
    zp = _matmul(hp, w_in, norm_g=norm1_g[0])
    zs = _matmul(hs, w_in, norm_g=norm1_g[0])
    bs_full = jnp.broadcast_to(a_bs[0][:, :, None], (A_GROUPS, CHUNK, LANES))
    yp = _even_prompt(zp, a_ws[0], bs_full, b_sink[0])
    ws_small = jnp.tril(a_ws[0][:, :n_new, :n_new]).transpose(2, 1, 0)
    wa = jnp.repeat(jnp.pad(ws_small, ((0, 0), (0, T_PAD - n_new), (0, 0))), LANES, axis=-1)
    wb = jnp.repeat(jnp.pad(a_bs[0][:, :n_new].T, ((0, T_PAD - n_new), (0, 0))), LANES, axis=-1)
    win = state_b_k.shape[2]
    kbuf = state_b_k[0].reshape(n_batch, win, B_KV_HEADS * B_DH)
    vbuf = state_b_v[0].reshape(n_batch, win, B_KV_HEADS * B_DH)
    ys3, av3 = _even_sample(pad_tokens(zs), kbuf, vbuf, wa, wb, b_sink[0])
    hp = _matmul(yp, w_out, resid=hp)
    hs = _matmul(unpad_tokens(ys3), w_out, resid=hs)

    k_off = 2 * A_WIDTH + B_HEADS * B_DH
    v_off = k_off + B_KV_HEADS * B_DH
    kv_shape = (1, -1, win, B_KV_HEADS, B_DH)
    a_v_sample = av3[:, :n_new].reshape(1, n_batch, n_new, A_GROUPS, A_WIDTH // A_GROUPS)
    b_k_prompt = zp[seq - win:, k_off:v_off].reshape(kv_shape)
    b_v_prompt = zp[seq - win:, v_off:].reshape(kv_shape)
    k_new = zs[:, k_off:v_off].reshape(n_batch, n_new, B_KV_HEADS * B_DH)
    v_new = zs[:, v_off:].reshape(n_batch, n_new, B_KV_HEADS * B_DH)
    b_k_sample = jnp.concatenate([kbuf, k_new], axis=1)[:, -win:].reshape(kv_shape)
    b_v_sample = jnp.concatenate([vbuf, v_new], axis=1)[:, -win:].reshape(kv_shape)

    wq = bf(peer_wq[0])
    hs, u_b, v_b = _peer(hs, norm2_g[0], wq, peer_k1[0], peer_k2[0], peer_u, peer_v, 0, final_g, False)
    hp = _peer(hp, norm2_g[0], wq, peer_k1[0], peer_k2[0], u_b, v_b, 0, final_g, False)

    w_in = bf(w_in_odd[0])
    w_out = bf(w_out_odd[0])
    c_width = C_HEADS * C_DV
    zp = _matmul(hp, w_in, norm_g=norm1_g[1])
    zs = _matmul(hs, w_in, norm_g=norm1_g[1])
    cos_p, sin_p = _rope_tables(jnp.arange(seq))
    cos_s, sin_s = _rope_tables(past + jnp.arange(T_PAD))
    ycp, c_s_prompt = _retention_prompt(zp, cos_p, sin_p, _retention_tables(CHUNK, CHUNK), c_norm_g[0])
    zs3 = pad_tokens(zs)
    ycs3, c_s_sample = _retention_sample(zs3, cos_s, sin_s, _retention_tables(T_PAD, n_new), c_norm_g[0],
                                         state_c_s[0])
    par = _s5_params(d_a_re[0], d_a_im[0], d_log_dt[0], d_b_re[0], d_b_im[0], d_c_re[0], d_c_im[0], d_d[0])
    u_off = 3 * c_width
    ydp_pre, d_prompt = _s5_prompt(zp, u_off // c_width, par)
    us_ts = zs[:, u_off:].reshape(n_batch, n_new, D_ROWS, D_ROW_CH).transpose(1, 2, 0, 3)
    x_re = state_d_re[0].reshape(n_batch, D_ROWS, D_ROW_STATE).transpose(1, 0, 2)
    x_im = state_d_im[0].reshape(n_batch, D_ROWS, D_ROW_STATE).transpose(1, 0, 2)
    yds_ts, s_re, s_im = _s5_sample(us_ts, x_re, x_im, par)
    glu_w = bf(d_glu_w[0])
    ydp = _glu(ydp_pre, glu_w, d_glu_b[0])
    yds = _glu(yds_ts.transpose(2, 0, 1, 3).reshape(n_batch * n_new, c_width), glu_w, d_glu_b[0])
    hp = _matmul(ycp, w_out[:c_width], x2=ydp, w2=w_out[c_width:], resid=hp)
    hs = _matmul(unpad_tokens(ycs3), w_out[:c_width], x2=yds, w2=w_out[c_width:], resid=hs)

    wq = bf(peer_wq[1])
    y_sample, u_b, v_b = _peer(hs, norm2_g[1], wq, peer_k1[1], peer_k2[1], peer_u, peer_v, 1, final_g, True)
    y_prompt = _peer(hp, norm2_g[1], wq, peer_k1[1], peer_k2[1], u_b, v_b, 1, final_g, True)
    y_prompt = y_prompt.reshape(x_prompt.shape)
    y_sample = y_sample.reshape(x_sample.shape)

    d_shape = (1, -1, D_GROUPS, D_STATE)
    return (y_prompt, y_sample, a_v_sample, b_k_prompt, b_v_prompt, b_k_sample, b_v_sample,
            c_s_prompt.reshape(1, 1, C_HEADS, C_DK, C_DV), c_s_sample[None],
            d_prompt[0].reshape(d_shape), d_prompt[1].reshape(d_shape),
            s_re.transpose(1, 0, 2).reshape(d_shape), s_im.transpose(1, 0, 2).reshape(d_shape))
```

```python
import functools
import math

import jax
import jax.numpy as jnp
from jax import lax
from jax.experimental import pallas as pl
from jax.experimental.pallas import tpu as pltpu

F32 = jnp.float32
BF16 = jnp.bfloat16
HIGHEST = lax.Precision.HIGHEST

EPS = 1e-6
NEG_BIG = -1e30
NEG_INF = float("-inf")

D_MODEL = 2048
PAST_LEN = 8192
LANES = 128
SUBLANES = 8
VMEM_LIMIT = 56 * 1024 * 1024

CHUNK = 128
A_GROUPS = 8
A_WIDTH = 1024
B_HEADS = 16
B_KV_HEADS = 4
B_GQA = 4
B_DH = 64
C_HEADS = 8
C_DK = 64
C_DV = 128
RET_DECAY_EXP0 = 5.0
ROPE_BASE = 10000.0
D_GROUPS = 64
D_STATE = 64
D_GROUP_CH = 16
D_ROWS = 8
D_ROW_STATE = 512
D_ROW_CH = 128
PEER_HEADS = 8
PEER_NKEYS = 128
PEER_TOPK = 16
PEER_HALF = 128


def _params(semantics):
    return pltpu.CompilerParams(dimension_semantics=semantics, vmem_limit_bytes=VMEM_LIMIT)


def _gelu(x):
    return 0.5 * x * (1.0 + lax.erf(x * (1.0 / math.sqrt(2.0))))


def _rmsnorm(x, g):
    return x * lax.rsqrt(jnp.mean(x * x, axis=-1, keepdims=True) + EPS) * g


def _dot_nt(a, b, precision=None):
    return lax.dot_general(a, b, (((1,), (1,)), ((), ())), precision=precision,
                           preferred_element_type=F32)


def _dot_tn(a, b, precision=None):
    return lax.dot_general(a, b, (((0,), (0,)), ((), ())), precision=precision,
                           preferred_element_type=F32)


def _mm_body(*refs, has_norm, has_pair, has_resid):
    it = iter(refs)
    x_ref, w_ref = next(it), next(it)
    g_ref = next(it) if has_norm else None
    x2_ref, w2_ref = (next(it), next(it)) if has_pair else (None, None)
    r_ref = next(it) if has_resid else None
    o_ref = next(it)
    x = x_ref[...]
    if has_norm:
        x = _rmsnorm(x, g_ref[...])
    xb = x.astype(BF16)
    x2b = x2_ref[...].astype(BF16) if has_pair else None
    n = o_ref.shape[1]
    for j in range(n // MM_COLS):
        cols = slice(j * MM_COLS, (j + 1) * MM_COLS)
        acc = jnp.dot(xb, w_ref[:, cols], preferred_element_type=F32)
        if has_pair:
            acc = acc + jnp.dot(x2b, w2_ref[:, cols], preferred_element_type=F32)
        if has_resid:
            acc = acc + r_ref[:, cols]
        o_ref[:, cols] = acc


MM_COLS = 512


def _matmul(x, w, *, norm_g=None, x2=None, w2=None, resid=None, tm=512):
    m, k = x.shape
    n = w.shape[1]
    assert m % tm == 0 and n % MM_COLS == 0
    resident = pl.Buffered(1)
    args = [x, w]
    specs = [pl.BlockSpec((tm, k), lambda i: (i, 0)),
             pl.BlockSpec((k, n), lambda i: (0, 0), pipeline_mode=resident)]
    if norm_g is not None:
        args.append(norm_g.reshape(1, k))
        specs.append(pl.BlockSpec((1, k), lambda i: (0, 0)))
    if x2 is not None:
        k2 = x2.shape[1]
        args += [x2, w2]
        specs += [pl.BlockSpec((tm, k2), lambda i: (i, 0)),
                  pl.BlockSpec((k2, n), lambda i: (0, 0), pipeline_mode=resident)]
    if resid is not None:
        args.append(resid)
        specs.append(pl.BlockSpec((tm, n), lambda i: (i, 0)))
    body = functools.partial(_mm_body, has_norm=norm_g is not None, has_pair=x2 is not None,
                             has_resid=resid is not None)
    return pl.pallas_call(
        body, grid=(m // tm,), in_specs=specs,
        out_specs=pl.BlockSpec((tm, n), lambda i: (i, 0)),
        out_shape=jax.ShapeDtypeStruct((m, n), F32),
        compiler_params=_params(("parallel",)), name="matmul",
    )(*args)


def _sink_column(sink_ref, kvh, rows_per_head, n_rows):
    grp = lax.broadcasted_iota(jnp.int32, (n_rows, 1), 0) // rows_per_head
    sk = jnp.full((n_rows, 1), sink_ref[kvh * B_GQA + B_GQA - 1], F32)
    for g in range(B_GQA - 2, -1, -1):
        sk = jnp.where(grp == g, sink_ref[kvh * B_GQA + g], sk)
    return sk


def _even_prompt_body(sink_ref, au_ref, av_ref, q_ref, kvc_ref, kvp_ref, ws_ref, bs_ref, o_ref):
    blk = pl.program_id(0)
    au = _gelu(au_ref[...])
    av = _gelu(av_ref[...])
    row = lax.broadcasted_iota(jnp.int32, (CHUNK, CHUNK), 0)
    col = lax.broadcasted_iota(jnp.int32, (CHUNK, CHUNK), 1)
    causal = row >= col
    for g in range(A_GROUPS):
        lanes = slice(g * LANES, (g + 1) * LANES)
        w = jnp.where(causal, ws_ref[g], 0.0).astype(BF16)
        mixed = jnp.dot(w, av[:, lanes].astype(BF16), preferred_element_type=F32) + bs_ref[g]
        o_ref[:, lanes] = au[:, lanes] * mixed

    q = q_ref[...]
    kvc = kvc_ref[...]
    kvp = kvp_ref[...]
    n_rows = B_GQA * CHUNK
    qi = lax.broadcasted_iota(jnp.int32, (n_rows, 2 * CHUNK), 0) % CHUNK
    kc = lax.broadcasted_iota(jnp.int32, (n_rows, 2 * CHUNK), 1)
    dist = qi + CHUNK - kc
    allowed = (dist >= 0) & (dist < CHUNK) & ((kc >= CHUNK) | (blk > 0))
    outs = []
    for kvh in range(B_KV_HEADS):
        ks = slice(kvh * B_DH, (kvh + 1) * B_DH)
        vs = slice(B_KV_HEADS * B_DH + kvh * B_DH, B_KV_HEADS * B_DH + (kvh + 1) * B_DH)
        kk = jnp.concatenate([kvp[:, ks], kvc[:, ks]], axis=0).astype(BF16)
        vv = jnp.concatenate([kvp[:, vs], kvc[:, vs]], axis=0).astype(BF16)
        q4 = jnp.concatenate(
            [q[:, (kvh * B_GQA + g) * B_DH:(kvh * B_GQA + g + 1) * B_DH] for g in range(B_GQA)], axis=0)
        s = _dot_nt(q4.astype(BF16), kk) * (B_DH ** -0.5)
        s = jnp.where(allowed, s, NEG_BIG)
        sk = _sink_column(sink_ref, kvh, CHUNK, n_rows)
        mx = jnp.maximum(jnp.max(s, axis=-1, keepdims=True), sk)
        p = jnp.exp(s - mx)
        p = p / (jnp.sum(p, axis=-1, keepdims=True) + jnp.exp(sk - mx))
        o = jnp.dot(p.astype(BF16), vv, preferred_element_type=F32)
        outs += [o[g * CHUNK:(g + 1) * CHUNK] for g in range(B_GQA)]
    o_ref[:, A_WIDTH:] = jnp.concatenate(outs, axis=1)


def _even_prompt(z, ws, bs_full, sink):
    t = z.shape[0]
    nb = t // CHUNK
    wide = A_WIDTH
    kvw = 2 * B_KV_HEADS * B_DH
    kv_blk = (2 * A_WIDTH + B_HEADS * B_DH) // kvw
    return pl.pallas_call(
        _even_prompt_body, grid=(nb,),
        in_specs=[
            pl.BlockSpec(memory_space=pltpu.SMEM),
            pl.BlockSpec((CHUNK, wide), lambda i: (i, 0)),
            pl.BlockSpec((CHUNK, wide), lambda i: (i, 1)),
            pl.BlockSpec((CHUNK, wide), lambda i: (i, 2)),
            pl.BlockSpec((CHUNK, kvw), lambda i: (i, kv_blk)),
            pl.BlockSpec((CHUNK, kvw), lambda i: (jnp.maximum(i - 1, 0), kv_blk)),
            pl.BlockSpec((A_GROUPS, CHUNK, CHUNK), lambda i: (0, 0, 0)),
            pl.BlockSpec((A_GROUPS, CHUNK, LANES), lambda i: (0, 0, 0)),
        ],
        out_specs=pl.BlockSpec((CHUNK, D_MODEL), lambda i: (i, 0)),
        out_shape=jax.ShapeDtypeStruct((t, D_MODEL), F32),
        compiler_params=_params(("parallel",)), name="even_prompt",
    )(sink, z, z, z, z, z, ws, bs_full)


EVEN_SAMPLE_BATCH = 8
T_PAD = 8


def _even_sample_body(sink_ref, au_ref, av_ref, q_ref, kv_ref, kb_ref, vb_ref, wa_ref, wb_ref,
                      y_ref, avo_ref):
    kv_width = B_KV_HEADS * B_DH
    n_rows = B_HEADS * T_PAD
    tq = lax.broadcasted_iota(jnp.int32, (n_rows, CHUNK), 0) % T_PAD
    kc = lax.broadcasted_iota(jnp.int32, (n_rows, CHUNK), 1)
    buf_allowed = kc > tq
    tq1 = lax.broadcasted_iota(jnp.int32, (n_rows, 1), 0) % T_PAD
    head_of_row = lax.broadcasted_iota(jnp.int32, (n_rows, 1), 0) // T_PAD
    sk = jnp.full((n_rows, 1), sink_ref[B_HEADS - 1], F32)
    for hd in range(B_HEADS - 2, -1, -1):
        sk = jnp.where(head_of_row == hd, sink_ref[hd], sk)
    lane_kvh = lax.broadcasted_iota(jnp.int32, (T_PAD, kv_width), 1) // B_DH
    scale = B_DH ** -0.5
    n_new = wa_ref.shape[0]
    for b in range(EVEN_SAMPLE_BATCH):
        au = _gelu(au_ref[b])
        av = _gelu(av_ref[b])
        avo_ref[b] = av
        mixed = wb_ref[...]
        for j in range(n_new):
            mixed = mixed + wa_ref[j] * av[j:j + 1, :]
        y_ref[b, :, :A_WIDTH] = au * mixed

        q = q_ref[b]
        kv = kv_ref[b]
        blocks = []
        for kvh in range(B_KV_HEADS):
            for g in range(B_GQA):
                col = (kvh * B_GQA + g) * B_DH
                qg = jnp.concatenate([q[:, col:col + B_DH]] * B_KV_HEADS, axis=1)
                blocks.append(jnp.where(lane_kvh == kvh, qg, 0.0))
        qbd = jnp.concatenate(blocks, axis=0).astype(BF16)
        s_buf = _dot_nt(qbd, kb_ref[b].astype(BF16)) * scale
        s_buf = jnp.where(buf_allowed, s_buf, NEG_BIG)
        qbd_r = qbd.astype(F32)
        s_new = []
        for j in range(n_new):
            kj = kv[j:j + 1, :kv_width].astype(BF16).astype(F32)
            sj = jnp.sum(qbd_r * kj, axis=-1, keepdims=True) * scale
            s_new.append(jnp.where(tq1 >= j, sj, NEG_BIG))
        mx = jnp.maximum(jnp.max(s_buf, axis=-1, keepdims=True), sk)
        for sj in s_new:
            mx = jnp.maximum(mx, sj)
        p_buf = jnp.exp(s_buf - mx)
        p_new = [jnp.exp(sj - mx) for sj in s_new]
        den = jnp.sum(p_buf, axis=-1, keepdims=True) + jnp.exp(sk - mx)
        for pj in p_new:
            den = den + pj
        inv = 1.0 / den
        o = jnp.dot((p_buf * inv).astype(BF16), vb_ref[b].astype(BF16), preferred_element_type=F32)
        for j in range(n_new):
            vj = kv[j:j + 1, kv_width:].astype(BF16).astype(F32)
            o = o + (p_new[j] * inv).astype(BF16).astype(F32) * vj
        outs = []
        for kvh in range(B_KV_HEADS):
            for g in range(B_GQA):
                row = (kvh * B_GQA + g) * T_PAD
                outs.append(o[row:row + T_PAD, kvh * B_DH:(kvh + 1) * B_DH])
        y_ref[b, :, A_WIDTH:] = jnp.concatenate(outs, axis=1)


def _even_sample(z3, kbuf, vbuf, wa, wb, sink):
    nb = z3.shape[0]
    bb = EVEN_SAMPLE_BATCH
    kvw = 2 * B_KV_HEADS * B_DH
    kv_blk = (2 * A_WIDTH + B_HEADS * B_DH) // kvw
    win = kbuf.shape[1]
    return pl.pallas_call(
        _even_sample_body, grid=(nb // bb,),
        in_specs=[
            pl.BlockSpec(memory_space=pltpu.SMEM),
            pl.BlockSpec((bb, T_PAD, A_WIDTH), lambda i: (i, 0, 0)),
            pl.BlockSpec((bb, T_PAD, A_WIDTH), lambda i: (i, 0, 1)),
            pl.BlockSpec((bb, T_PAD, A_WIDTH), lambda i: (i, 0, 2)),
            pl.BlockSpec((bb, T_PAD, kvw), lambda i: (i, 0, kv_blk)),
            pl.BlockSpec((bb, win, kvw // 2), lambda i: (i, 0, 0)),
            pl.BlockSpec((bb, win, kvw // 2), lambda i: (i, 0, 0)),
            pl.BlockSpec(wa.shape, lambda i: (0, 0, 0)),
            pl.BlockSpec(wb.shape, lambda i: (0, 0)),
        ],
        out_specs=[
            pl.BlockSpec((bb, T_PAD, D_MODEL), lambda i: (i, 0, 0)),
            pl.BlockSpec((bb, T_PAD, A_WIDTH), lambda i: (i, 0, 0)),
        ],
        out_shape=[
            jax.ShapeDtypeStruct((nb, T_PAD, D_MODEL), F32),
            jax.ShapeDtypeStruct((nb, T_PAD, A_WIDTH), F32),
        ],
        compiler_params=_params(("parallel",)), name="even_sample",
    )(sink, z3, z3, z3, z3, kbuf, vbuf, wa, wb)


def _rope(x, cos_f, sin_s):
    width = x.shape[-1]
    half = C_DK // 2
    lane = lax.broadcasted_iota(jnp.int32, x.shape, 1) % C_DK
    swapped = jnp.where(lane < half, pltpu.roll(x, width - half, 1), pltpu.roll(x, half, 1))
    return x * cos_f + swapped * sin_s


def _retention_chunk(qh, kh, vh, st, decay, qdec, kdec, sdec):
    qb, vb = qh.astype(BF16), vh.astype(BF16)
    scores = _dot_nt(qb, kh.astype(BF16)) * decay
    o = jnp.dot(scores.astype(BF16), vb, preferred_element_type=F32)
    o = o + jnp.dot(qb, st.astype(BF16), preferred_element_type=F32) * qdec
    new_st = sdec * st + _dot_tn((kh * kdec).astype(BF16), vb)
    return o, new_st


def _groupnorm_gate(o, gain, gate):
    mu = jnp.mean(o, axis=-1, keepdims=True)
    var = jnp.mean(jnp.square(o - mu), axis=-1, keepdims=True)
    return (o - mu) * lax.rsqrt(var + EPS) * gain * (gate * jax.nn.sigmoid(gate))


def _retention_body(qk_ref, v_ref, g_ref, cos_ref, sin_ref, decay_ref, qdec_ref, kdec_ref, sdec_ref,
                    gain_ref, o_ref, st_ref, state):
    @pl.when(pl.program_id(0) == 0)
    def _():
        state[...] = jnp.zeros_like(state)

    qk = qk_ref[...]
    width = C_HEADS * C_DK
    q = _rope(qk[:, :width], cos_ref[...], sin_ref[...]) * (C_DK ** -0.5)
    k = _rope(qk[:, width:], cos_ref[...], sin_ref[...])
    v = v_ref[...]
    g = g_ref[...]
    for h in range(C_HEADS):
        qh = q[:, h * C_DK:(h + 1) * C_DK]
        kh = k[:, h * C_DK:(h + 1) * C_DK]
        vh = v[:, h * C_DV:(h + 1) * C_DV]
        st = state[h]
        o, new_st = _retention_chunk(qh, kh, vh, st, decay_ref[h], qdec_ref[h], kdec_ref[h], sdec_ref[h])
        state[h] = new_st
        st_ref[h] = new_st
        lanes = slice(h * C_DV, (h + 1) * C_DV)
        o_ref[:, lanes] = _groupnorm_gate(o, gain_ref[:, lanes], g[:, lanes])


def _retention_prompt(z, cos_f, sin_s, tabs, gain):
    t = z.shape[0]
    nc = t // CHUNK
    decay, qdec, kdec, sdec = tabs
    width = C_HEADS * C_DV
    const3 = lambda i: (0, 0, 0)
    return pl.pallas_call(
        _retention_body, grid=(nc,),
        in_specs=[
            pl.BlockSpec((CHUNK, width), lambda i: (i, 0)),
            pl.BlockSpec((CHUNK, width), lambda i: (i, 1)),
            pl.BlockSpec((CHUNK, width), lambda i: (i, 2)),
            pl.BlockSpec((CHUNK, C_HEADS * C_DK), lambda i: (i, 0)),
            pl.BlockSpec((CHUNK, C_HEADS * C_DK), lambda i: (i, 0)),
            pl.BlockSpec(decay.shape, const3),
            pl.BlockSpec(qdec.shape, const3),
            pl.BlockSpec(kdec.shape, const3),
            pl.BlockSpec(sdec.shape, const3),
            pl.BlockSpec((1, width), lambda i: (0, 0)),
        ],
        out_specs=[
            pl.BlockSpec((CHUNK, width), lambda i: (i, 0)),
            pl.BlockSpec((C_HEADS, C_DK, C_DV), const3),
        ],
        out_shape=[
            jax.ShapeDtypeStruct((t, width), F32),
            jax.ShapeDtypeStruct((C_HEADS, C_DK, C_DV), F32),
        ],
        scratch_shapes=[pltpu.VMEM((C_HEADS, C_DK, C_DV), F32)],
        compiler_params=_params(("arbitrary",)), name="retention_prompt",
    )(z, z, z, cos_f, sin_s, decay, qdec, kdec, sdec, gain.reshape(1, width))


RET_SAMPLE_BATCH = 8


def _retention_sample_body(qk_ref, v_ref, g_ref, cos_ref, sin_ref, decay_ref, qdec_ref, kdec_ref,
                           sdec_ref, gain_ref, st_in_ref, o_ref, st_ref):
    width = C_HEADS * C_DK
    lane_head = lax.broadcasted_iota(jnp.int32, (T_PAD, width), 1) // C_DK
    for b in range(RET_SAMPLE_BATCH):
        qk = qk_ref[b]
        q = _rope(qk[:, :width], cos_ref[...], sin_ref[...]) * (C_DK ** -0.5)
        k = _rope(qk[:, width:], cos_ref[...], sin_ref[...])
        v = v_ref[b]
        g = g_ref[b]
        vb = v.astype(BF16)
        qbd = jnp.concatenate([jnp.where(lane_head == h, q, 0.0) for h in range(C_HEADS)],
                              axis=0).astype(BF16)
        st = st_in_ref[b]
        scores = _dot_nt(qbd, k.astype(BF16)) * decay_ref[...]
        o_intra = jnp.dot(scores.astype(BF16), vb, preferred_element_type=F32)
        o_cross = jnp.dot(qbd, st.astype(BF16), preferred_element_type=F32) * qdec_ref[...]
        kd = (k * kdec_ref[...]).astype(BF16)
        for h in range(C_HEADS):
            rows = slice(h * T_PAD, (h + 1) * T_PAD)
            lanes = slice(h * C_DV, (h + 1) * C_DV)
            keys = slice(h * C_DK, (h + 1) * C_DK)
            o = o_intra[rows, lanes] + o_cross[rows, :]
            o_ref[b, :, lanes] = _groupnorm_gate(o, gain_ref[:, lanes], g[:, lanes])
            st_ref[b, keys, :] = sdec_ref[keys, :] * st[keys, :] + _dot_tn(kd[:, keys], vb[:, lanes])


def _retention_sample(z3, cos_f, sin_s, tabs, gain, st_in):
    nb = z3.shape[0]
    bb = RET_SAMPLE_BATCH
    decay, qdec, kdec, sdec = tabs
    width = C_HEADS * C_DV
    decay_s = decay.reshape(C_HEADS * T_PAD, T_PAD)
    qdec_s = qdec.reshape(C_HEADS * T_PAD, C_DV)
    kdec_s = kdec.transpose(1, 0, 2).reshape(T_PAD, C_HEADS * C_DK)
    sdec_s = sdec.reshape(C_HEADS * C_DK, C_DV)
    st2 = st_in.reshape(nb, C_HEADS * C_DK, C_DV)
    c2 = lambda i: (0, 0)
    out, st_out = pl.pallas_call(
        _retention_sample_body, grid=(nb // bb,),
        in_specs=[
            pl.BlockSpec((bb, T_PAD, width), lambda i: (i, 0, 0)),
            pl.BlockSpec((bb, T_PAD, width), lambda i: (i, 0, 1)),
            pl.BlockSpec((bb, T_PAD, width), lambda i: (i, 0, 2)),
            pl.BlockSpec((T_PAD, C_HEADS * C_DK), c2),
            pl.BlockSpec((T_PAD, C_HEADS * C_DK), c2),
            pl.BlockSpec(decay_s.shape, c2),
            pl.BlockSpec(qdec_s.shape, c2),
            pl.BlockSpec(kdec_s.shape, c2),
            pl.BlockSpec(sdec_s.shape, c2),
            pl.BlockSpec((1, width), c2),
            pl.BlockSpec((bb, C_HEADS * C_DK, C_DV), lambda i: (i, 0, 0)),
        ],
        out_specs=[
            pl.BlockSpec((bb, T_PAD, width), lambda i: (i, 0, 0)),
            pl.BlockSpec((bb, C_HEADS * C_DK, C_DV), lambda i: (i, 0, 0)),
        ],
        out_shape=[
            jax.ShapeDtypeStruct((nb, T_PAD, width), F32),
            jax.ShapeDtypeStruct((nb, C_HEADS * C_DK, C_DV), F32),
        ],
        compiler_params=_params(("parallel",)), name="retention_sample",
    )(z3, z3, z3, cos_f, sin_s, decay_s, qdec_s, kdec_s, sdec_s, gain.reshape(1, width), st2)
    return out, st_out.reshape(st_in.shape)


def _retention_tables(length, n_valid):
    log_gamma = jnp.log1p(-jnp.exp2(-RET_DECAY_EXP0 - jnp.arange(C_HEADS, dtype=F32)))
    i = jnp.arange(length, dtype=F32)
    valid = (jnp.arange(length) < n_valid)
    rel = i[:, None] - i[None, :]
    decay = jnp.where(rel >= 0, jnp.exp(jnp.maximum(rel, 0.0)[None] * log_gamma[:, None, None]), 0.0)
    decay = jnp.where(valid[None, None, :], decay, 0.0)
    q_dec = jnp.exp((i[None, :] + 1.0) * log_gamma[:, None])
    k_dec = jnp.where(valid[None, :], jnp.exp((n_valid - 1.0 - i)[None, :] * log_gamma[:, None]), 0.0)
    s_dec = jnp.exp(n_valid * log_gamma)
    qdec = jnp.broadcast_to(q_dec[:, :, None], (C_HEADS, length, C_DV))
    kdec = jnp.broadcast_to(k_dec[:, :, None], (C_HEADS, length, C_DK))
    sdec = jnp.broadcast_to(s_dec[:, None, None], (C_HEADS, C_DK, C_DV))
    return decay, qdec, kdec, sdec


def _rope_tables(pos):
    half = C_DK // 2
    freqs = ROPE_BASE ** (-jnp.arange(half, dtype=F32) / half)
    ang = pos.astype(F32)[:, None] * freqs[None, :]
    cos, sin = jnp.cos(ang), jnp.sin(ang)
    cos_f = jnp.tile(jnp.concatenate([cos, cos], axis=1), (1, C_HEADS))
    sin_s = jnp.tile(jnp.concatenate([-sin, sin], axis=1), (1, C_HEADS))
    return cos_f, sin_s


def _s5_discretize(are_ref, aim_ref, ldt_ref):
    a_re, a_im = are_ref[...], aim_ref[...]
    dt = jnp.exp(ldt_ref[...])
    mag = jnp.exp(a_re * dt)
    ab_re = mag * jnp.cos(a_im * dt)
    ab_im = mag * jnp.sin(a_im * dt)
    num_re, num_im = ab_re - 1.0, ab_im
    den = a_re * a_re + a_im * a_im
    co_re = (num_re * a_re + num_im * a_im) / den
    co_im = (num_im * a_re - num_re * a_im) / den
    return ab_re, ab_im, co_re, co_im


S5_CHUNK = 256


def _s5_prompt_body(u_ref, are_ref, aim_ref, ldt_ref, bre_ref, bim_ref, cre_ref, cim_ref, dd_ref,
                    y_ref, st_ref, hre, him, state, disc):
    n = S5_CHUNK

    @pl.when(pl.program_id(0) == 0)
    def _():
        ab_re, ab_im, co_re, co_im = _s5_discretize(are_ref, aim_ref, ldt_ref)
        disc[0] = ab_re
        disc[1] = ab_im
        disc[2] = co_re
        disc[3] = co_im
        state[...] = jnp.zeros_like(state)

    n_chunks = D_ROW_STATE // LANES
    for s in range(D_ROWS):
        rows = pl.ds(s, n, stride=D_ROWS)
        u_s = u_ref[:, s * D_ROW_CH:(s + 1) * D_ROW_CH]
        u_b = u_s.astype(BF16)
        raw_re = jnp.dot(u_b, bre_ref[s], preferred_element_type=F32)
        raw_im = jnp.dot(u_b, bim_ref[s], preferred_element_type=F32)
        co_re = disc[2, s:s + 1, :]
        co_im = disc[3, s:s + 1, :]
        bu_re = co_re * raw_re - co_im * raw_im
        bu_im = co_re * raw_im + co_im * raw_re
        for c in range(n_chunks):
            hre[c, rows, :] = bu_re[:, c * LANES:(c + 1) * LANES]
            him[c, rows, :] = bu_im[:, c * LANES:(c + 1) * LANES]

    ab_re = disc[0]
    ab_im = disc[1]

    def step(t, carry):
        h_re, h_im = carry
        rows = pl.ds(pl.multiple_of(t * D_ROWS, D_ROWS), D_ROWS)
        bu_re = jnp.concatenate([hre[c, rows, :] for c in range(n_chunks)], axis=1)
        bu_im = jnp.concatenate([him[c, rows, :] for c in range(n_chunks)], axis=1)
        n_re = ab_re * h_re - ab_im * h_im + bu_re
        n_im = ab_re * h_im + ab_im * h_re + bu_im
        for c in range(n_chunks):
            hre[c, rows, :] = n_re[:, c * LANES:(c + 1) * LANES]
            him[c, rows, :] = n_im[:, c * LANES:(c + 1) * LANES]
        return n_re, n_im

    h_re, h_im = lax.fori_loop(0, n, step, (state[0], state[1]), unroll=4)
    state[0] = h_re
    state[1] = h_im
    st_ref[0] = h_re
    st_ref[1] = h_im

    for s in range(D_ROWS):
        rows = pl.ds(s, n, stride=D_ROWS)
        hist_re = jnp.concatenate([hre[c, rows, :] for c in range(n_chunks)], axis=1)
        hist_im = jnp.concatenate([him[c, rows, :] for c in range(n_chunks)], axis=1)
        cols = slice(s * D_ROW_CH, (s + 1) * D_ROW_CH)
        y = (jnp.dot(hist_re.astype(BF16), cre_ref[s], preferred_element_type=F32)
             - jnp.dot(hist_im.astype(BF16), cim_ref[s], preferred_element_type=F32)
             + dd_ref[s:s + 1, :] * u_ref[:, cols])
        y_ref[:, cols] = _gelu(y)


def _s5_prompt(z, u_col_block, par):
    a_re, a_im, ldt, b_re, b_im, c_re, c_im, dd = par
    t = z.shape[0]
    blk = S5_CHUNK * D_ROWS
    width = D_ROWS * D_ROW_CH
    c2 = lambda i: (0, 0)
    c3 = lambda i: (0, 0, 0)
    return pl.pallas_call(
        _s5_prompt_body, grid=(t // S5_CHUNK,),
        in_specs=[
            pl.BlockSpec((S5_CHUNK, width), lambda i: (i, u_col_block)),
            pl.BlockSpec(a_re.shape, c2), pl.BlockSpec(a_im.shape, c2), pl.BlockSpec(ldt.shape, c2),
            pl.BlockSpec(b_re.shape, c3), pl.BlockSpec(b_im.shape, c3),
            pl.BlockSpec(c_re.shape, c3), pl.BlockSpec(c_im.shape, c3),
            pl.BlockSpec(dd.shape, c2),
        ],
        out_specs=[
            pl.BlockSpec((S5_CHUNK, width), lambda i: (i, 0)),
            pl.BlockSpec((2, D_ROWS, D_ROW_STATE), c3),
        ],
        out_shape=[
            jax.ShapeDtypeStruct((t, width), F32),
            jax.ShapeDtypeStruct((2, D_ROWS, D_ROW_STATE), F32),
        ],
        scratch_shapes=[
            pltpu.VMEM((D_ROW_STATE // LANES, blk, LANES), F32),
            pltpu.VMEM((D_ROW_STATE // LANES, blk, LANES), F32),
            pltpu.VMEM((2, D_ROWS, D_ROW_STATE), F32), pltpu.VMEM((4, D_ROWS, D_ROW_STATE), F32),
        ],
        compiler_params=_params(("arbitrary",)), name="s5_prompt",
    )(z, a_re, a_im, ldt, b_re.astype(BF16), b_im.astype(BF16), c_re.astype(BF16), c_im.astype(BF16), dd)


def _s5_sample_body(u_ref, xre_ref, xim_ref, are_ref, aim_ref, ldt_ref, bre_ref, bim_ref, cre_ref,
                    cim_ref, dd_ref, y_ref, sre_ref, sim_ref):
    ab_re, ab_im, co_re, co_im = _s5_discretize(are_ref, aim_ref, ldt_ref)
    n_t = u_ref.shape[0]
    for s in range(D_ROWS):
        a_r, a_i = ab_re[s:s + 1, :], ab_im[s:s + 1, :]
        c_r, c_i = co_re[s:s + 1, :], co_im[s:s + 1, :]
        h_re, h_im = xre_ref[s], xim_ref[s]
        for t in range(n_t):
            u = u_ref[t, s]
            raw_re = jnp.dot(u, bre_ref[s], precision=HIGHEST, preferred_element_type=F32)
            raw_im = jnp.dot(u, bim_ref[s], precision=HIGHEST, preferred_element_type=F32)
            bu_re = c_r * raw_re - c_i * raw_im
            bu_im = c_r * raw_im + c_i * raw_re
            h_re, h_im = a_r * h_re - a_i * h_im + bu_re, a_r * h_im + a_i * h_re + bu_im
            y = (jnp.dot(h_re, cre_ref[s], precision=HIGHEST, preferred_element_type=F32)
                 - jnp.dot(h_im, cim_ref[s], precision=HIGHEST, preferred_element_type=F32)
                 + dd_ref[s:s + 1, :] * u)
            y_ref[t, s] = _gelu(y)
        sre_ref[s] = h_re
        sim_ref[s] = h_im


def _s5_sample(u_ts, x_re, x_im, par):
    a_re, a_im, ldt, b_re, b_im, c_re, c_im, dd = par
    return pl.pallas_call(
        _s5_sample_body,
        out_shape=[
            jax.ShapeDtypeStruct(u_ts.shape, F32),
            jax.ShapeDtypeStruct(x_re.shape, F32),
            jax.ShapeDtypeStruct(x_im.shape, F32),
        ],
        compiler_params=pltpu.CompilerParams(vmem_limit_bytes=VMEM_LIMIT), name="s5_sample",
    )(u_ts, x_re, x_im, a_re, a_im, ldt, b_re, b_im, c_re, c_im, dd)


def _s5_params(a_re, a_im, log_dt, b_re, b_im, c_re, c_im, dd):
    eye = jnp.eye(D_ROWS, dtype=F32)

    def rows(a):
        return a.reshape(D_ROWS, D_ROW_STATE)

    def b_blocks(b):
        b4 = b.reshape(D_ROWS, D_ROWS, D_STATE, D_GROUP_CH).transpose(0, 1, 3, 2)
        return jnp.einsum('sgcp,gh->sgchp', b4, eye).reshape(D_ROWS, D_ROW_CH, D_ROW_STATE)

    def c_blocks(c):
        c4 = c.reshape(D_ROWS, D_ROWS, D_GROUP_CH, D_STATE)
        return jnp.einsum('sgcp,gh->sgphc', c4, eye).reshape(D_ROWS, D_ROW_STATE, D_ROW_CH)

    ldt = jnp.broadcast_to(log_dt[:, None], (D_GROUPS, D_STATE))
    return (rows(a_re), rows(a_im), rows(ldt), b_blocks(b_re), b_blocks(b_im),
            c_blocks(c_re), c_blocks(c_im), dd.reshape(D_ROWS, D_ROW_CH))


def _glu_body(x_ref, w_ref, b_ref, o_ref):
    x = x_ref[...]
    gate = jnp.dot(x.astype(BF16), w_ref[...], preferred_element_type=F32) + b_ref[...]
    o_ref[...] = x * jax.nn.sigmoid(gate)


def _glu(x, w, b, tm=512):
    m, k = x.shape
    return pl.pallas_call(
        _glu_body, grid=(m // tm,),
        in_specs=[pl.BlockSpec((tm, k), lambda i: (i, 0)), pl.BlockSpec((k, k), lambda i: (0, 0)),
                  pl.BlockSpec((1, k), lambda i: (0, 0))],
        out_specs=pl.BlockSpec((tm, k), lambda i: (i, 0)),
        out_shape=jax.ShapeDtypeStruct((m, k), F32),
        compiler_params=_params(("parallel",)), name="glu",
    )(x, w, b.reshape(1, k))


PEER_SEL_TOKENS = 256


def _extract_top(s, n_out, exact, want_rank=False):
    n_rows = s.shape[0]
    rid = lax.broadcasted_iota(jnp.int32, s.shape, 0).astype(F32) if exact else None
    rank = jnp.full(s.shape, float(n_out), F32) if want_rank else None
    vals = []
    for r in range(n_out):
        m = jnp.max(s, axis=0, keepdims=True)
        if exact:
            first = jnp.min(jnp.where(s == m, rid, float(n_rows)), axis=0, keepdims=True)
            hit = rid == first
        else:
            hit = s == m
        if want_rank:
            rank = jnp.where(hit, float(r), rank)
        s = jnp.where(hit, NEG_INF, s)
        vals.append(m)
    return vals, s, rank


def _removed_count(s):
    return jnp.sum(jnp.where(s == NEG_INF, 1.0, 0.0), axis=0, keepdims=True)


_CAND_COUNTS = [PEER_TOPK // (a + 1) for a in range(PEER_TOPK)]


def _candidates(v1, v2):
    v1_all = _stack_rows(v1)
    v2_all = _stack_rows(v2)
    rid = lax.broadcasted_iota(jnp.int32, (SUBLANES, v1[0].shape[1]), 0)
    tiles = [v1[0] + v2_all]
    n_pad = 0
    a = 1
    while _CAND_COUNTS[a] > 1:
        tiles.append(jnp.where(rid < _CAND_COUNTS[a], v1[a] + v2_all[:SUBLANES], NEG_INF))
        n_pad += SUBLANES - _CAND_COUNTS[a]
        a += 1
    assert PEER_TOPK - a == SUBLANES
    tiles.append(v1_all[a:] + v2[0])
    return jnp.concatenate(tiles, axis=0), n_pad


def _stack_rows(rows):
    n = len(rows)
    rid = lax.broadcasted_iota(jnp.int32, (n, rows[0].shape[1]), 0)
    out = jnp.broadcast_to(rows[n - 1], rid.shape)
    for i in range(n - 2, -1, -1):
        out = jnp.where(rid == i, rows[i], out)
    return out


def _selected_per_first_key(removed, rank1):
    counts = [jnp.sum(removed[:PEER_TOPK], axis=0, keepdims=True)]
    row = PEER_TOPK
    a = 1
    while _CAND_COUNTS[a] > 1:
        counts.append(jnp.sum(removed[row:row + SUBLANES], axis=0, keepdims=True))
        row += SUBLANES
        a += 1
    for i in range(PEER_TOPK - a):
        counts.append(removed[row + i:row + i + 1])
    n_sel = jnp.zeros(rank1.shape, F32)
    for a, cnt in enumerate(counts):
        n_sel = jnp.where(rank1 == float(a), cnt, n_sel)
    return n_sel


def _peer_select_body(q_ref, k1_ref, k2_ref, ns_ref, g1_ref, r2_ref, e2_ref):
    def head(h, exact):
        q1 = q_ref[:, (2 * h) * PEER_HALF:(2 * h + 1) * PEER_HALF]
        q2 = q_ref[:, (2 * h + 1) * PEER_HALF:(2 * h + 2) * PEER_HALF]
        s1 = _dot_nt(k1_ref[h], q1, HIGHEST)
        s2 = _dot_nt(k2_ref[h], q2, HIGHEST)
        v1, left1, rank1 = _extract_top(s1, PEER_TOPK, exact, want_rank=True)
        v2, left2, rank2 = _extract_top(s2, PEER_TOPK, exact, want_rank=True)
        cand, n_pad = _candidates(v1, v2)
        top, left_c, _ = _extract_top(cand, PEER_TOPK, exact)
        mx = top[0]
        z = jnp.exp(top[0] - mx)
        for kk in range(1, PEER_TOPK):
            z = z + jnp.exp(top[kk] - mx)
        g1 = jnp.where(rank1 < PEER_TOPK, jnp.exp(s1 - v1[0]), 0.0) / z
        e2 = jnp.where(rank2 < PEER_TOPK, jnp.exp(s2 - v2[0]), 0.0)
        removed = jnp.where((left_c == NEG_INF) & (cand != NEG_INF), 1.0, 0.0)
        n_sel = _selected_per_first_key(removed, rank1)
        for j in range(PEER_SEL_TOKENS // LANES):
            lanes = slice(j * LANES, (j + 1) * LANES)
            ns_ref[j, h] = n_sel[:, lanes]
            g1_ref[j, h] = g1[:, lanes]
            r2_ref[j, h] = rank2[:, lanes]
            e2_ref[j, h] = e2[:, lanes]
        if exact:
            return None
        ok = ((_removed_count(left1) == PEER_TOPK) & (_removed_count(left2) == PEER_TOPK)
              & (_removed_count(left_c) == PEER_TOPK + n_pad))
        return jnp.where(ok, 0.0, 1.0)

    repeated = [jnp.max(head(h, False)) for h in range(PEER_HEADS)]
    for h in range(PEER_HEADS):
        @pl.when(repeated[h] > 0.0)
        def _():
            head(h, True)


def _peer_select(q, k1, k2):
    t = q.shape[0]
    tb = PEER_SEL_TOKENS
    nj = tb // LANES
    big = pl.BlockSpec((nj, PEER_HEADS, PEER_NKEYS, LANES), lambda i: (i, 0, 0, 0))
    shape = jax.ShapeDtypeStruct((t // LANES, PEER_HEADS, PEER_NKEYS, LANES), F32)
    return pl.pallas_call(
        _peer_select_body, grid=(t // tb,),
        in_specs=[
            pl.BlockSpec((tb, q.shape[1]), lambda i: (i, 0)),
            pl.BlockSpec(k1.shape, lambda i: (0, 0, 0)),
            pl.BlockSpec(k2.shape, lambda i: (0, 0, 0)),
        ],
        out_specs=[big, big, big, big],
        out_shape=[shape, shape, shape, shape],
        compiler_params=_params(("parallel",)), name="peer_select",
    )(q, k1, k2)


PEER_TOKENS = 512
PEER_EXPERT_TILE = 512
BF16_ROWS = 16
GATE_GROUP = 4


def _peer_dense_body(h_ref, g_ref, u_ref, v_ref, ns_ref, g1_ref, r2_ref, e2_ref, og_ref, o_ref, *rest,
                     norm_out, emit_bf16):
    if emit_bf16:
        ub_ref, vb_ref, xn, act_s, coef_s, acc = rest
    else:
        xn, act_s, coef_s, acc = rest
    e = pl.program_id(1)
    n_chunks = PEER_TOKENS // LANES

    @pl.when(e == 0)
    def _():
        xn[...] = _rmsnorm(h_ref[...], g_ref[...]).astype(BF16)
        acc[...] = jnp.zeros_like(acc)

    te = PEER_EXPERT_TILE
    slabs = te // PEER_NKEYS

    def one_tile(part):
        tile_rows = slice(part * te, (part + 1) * te)
        u_tile = u_ref[tile_rows, :].astype(BF16)
        if emit_bf16:
            ub_ref[tile_rows, :] = u_tile
        act = _dot_nt(u_tile, xn[...])
        for j in range(n_chunks):
            act_s[j] = act[:, j * LANES:(j + 1) * LANES]

        def chunk(j, carry):
            packed = (PEER_NKEYS // BF16_ROWS, BF16_ROWS, LANES)
            for c0 in range(0, slabs, GATE_GROUP):
                gates = [jnp.zeros(packed, BF16) for _ in range(GATE_GROUP)]
                for h in range(PEER_HEADS):
                    r2 = r2_ref[j, h].reshape(packed).astype(BF16)
                    e2 = e2_ref[j, h].reshape(packed).astype(BF16)
                    for k in range(GATE_GROUP):
                        key1 = part * slabs + c0 + k
                        ns = jnp.broadcast_to(ns_ref[j, h, key1:key1 + 1, :], packed[1:]).astype(BF16)
                        g1 = jnp.broadcast_to(g1_ref[j, h, key1:key1 + 1, :], packed[1:]).astype(BF16)
                        gates[k] = gates[k] + jnp.where(r2 < ns[None], e2 * g1[None], jnp.zeros((), BF16))
                for k in range(GATE_GROUP):
                    rows = slice((c0 + k) * PEER_NKEYS, (c0 + k + 1) * PEER_NKEYS)
                    act = _gelu(act_s[j, rows, :]).reshape(packed).astype(BF16)
                    coef_s[j, rows, :] = (gates[k] * act).reshape(PEER_NKEYS, LANES)
            return carry

        lax.fori_loop(0, n_chunks, chunk, 0)
        coef = jnp.concatenate([coef_s[j] for j in range(n_chunks)], axis=1)
        v_tile = v_ref[tile_rows, :].astype(BF16)
        if emit_bf16:
            vb_ref[tile_rows, :] = v_tile
        acc[...] += _dot_tn(coef, v_tile)

    for part in range(u_ref.shape[0] // te):
        one_tile(part)

    @pl.when(e == pl.num_programs(1) - 1)
    def _():
        out = h_ref[...] + acc[...]
        o_ref[...] = _rmsnorm(out, og_ref[...]) if norm_out else out


def _peer_dense(h, g, u_tab, v_tab, layer, sel, out_g, norm_out):
    ns, g1, r2, e2 = sel
    t, d = h.shape
    emit_bf16 = u_tab.ndim == 3
    n_exp = u_tab.shape[-2]
    tb = PEER_TOKENS
    nj = tb // LANES
    slabs = PEER_EXPERT_TILE // PEER_NKEYS
    o_spec = pl.BlockSpec((tb, d), lambda i, e: (i, 0))
    o_shape = jax.ShapeDtypeStruct((t, d), F32)
    if emit_bf16:
        assert t == tb, "every table tile must be visited exactly once when the casts are emitted"
        te = PEER_EXPERT_TILE
        n_steps = n_exp // te
        tab_spec = pl.BlockSpec((None, te, d), lambda i, e: (layer, e, 0))

        def by_tile(a):
            a5 = a.reshape(a.shape[0], PEER_HEADS, n_steps, slabs, LANES)
            return a5.transpose(2, 0, 1, 3, 4)

        ns, g1 = by_tile(ns), by_tile(g1)
        key_spec = pl.BlockSpec((None, nj, PEER_HEADS, slabs, LANES), lambda i, e: (e, i, 0, 0, 0))
    else:
        te = SUBLANES // slabs * PEER_EXPERT_TILE
        n_steps = n_exp // te
        tab_spec = pl.BlockSpec((te, d), lambda i, e: (e, 0))
        key_spec = pl.BlockSpec((nj, PEER_HEADS, SUBLANES, LANES), lambda i, e: (i, 0, e, 0))
    cast_spec = pl.BlockSpec((te, d), lambda i, e: (e, 0))
    cast_shape = jax.ShapeDtypeStruct((n_exp, d), BF16)
    tok_spec = pl.BlockSpec((nj, PEER_HEADS, PEER_NKEYS, LANES), lambda i, e: (i, 0, 0, 0))
    te_scratch = PEER_EXPERT_TILE
    return pl.pallas_call(
        functools.partial(_peer_dense_body, norm_out=norm_out, emit_bf16=emit_bf16),
        grid=(t // tb, n_steps),
        in_specs=[
            pl.BlockSpec((tb, d), lambda i, e: (i, 0)),
            pl.BlockSpec((1, d), lambda i, e: (0, 0)),
            tab_spec, tab_spec,
            key_spec, key_spec, tok_spec, tok_spec,
            pl.BlockSpec((1, d), lambda i, e: (0, 0)),
        ],
        out_specs=[o_spec, cast_spec, cast_spec] if emit_bf16 else o_spec,
        out_shape=[o_shape, cast_shape, cast_shape] if emit_bf16 else o_shape,
        scratch_shapes=[pltpu.VMEM((tb, d), BF16), pltpu.VMEM((nj, te_scratch, LANES), F32),
                        pltpu.VMEM((nj, te_scratch, LANES), BF16), pltpu.VMEM((tb, d), F32)],
        compiler_params=_params(("parallel", "arbitrary")), name="peer_dense",
    )(h, g.reshape(1, d), u_tab, v_tab, ns, g1, r2, e2, out_g.reshape(1, d))


def _peer(h, g, wq, k1, k2, u_tab, v_tab, layer, out_g, norm_out):
    q = _matmul(h, wq, norm_g=g)
    return _peer_dense(h, g, u_tab, v_tab, layer, _peer_select(q, k1, k2), out_g, norm_out)


def kernel(x_prompt, x_sample, state_b_k, state_b_v, state_c_s, state_d_re, state_d_im, norm1_g, norm2_g, final_g, w_in_even, w_out_even, a_ws, a_bs, b_sink, w_in_odd, w_out_odd, c_norm_g, d_a_re, d_a_im, d_log_dt, d_b_re, d_b_im, d_c_re, d_c_im, d_d, d_glu_w, d_glu_b, peer_wq, peer_k1, peer_k2, peer_u, peer_v):
    seq = x_prompt.shape[1]
    n_batch, n_new = x_sample.shape[:2]
    past = PAST_LEN
    hp = x_prompt.reshape(seq, D_MODEL)
    hs = x_sample.reshape(n_batch * n_new, D_MODEL)
    bf = lambda a: a.astype(BF16)

    def pad_tokens(a):
        a3 = a.reshape(n_batch, n_new, a.shape[-1])
        return jnp.pad(a3, ((0, 0), (0, T_PAD - n_new), (0, 0)))

    def unpad_tokens(a3):
        return a3[:, :n_new].reshape(n_batch * n_new, a3.shape[-1])

    w_in = bf(w_in_even[0])
    w_out = bf(w_out_even[0])
    zp = _matmul(hp, w_in, norm_g=norm1_g[0])
    zs = _matmul(hs, w_in, norm_g=norm1_g[0])
    bs_full = jnp.broadcast_to(a_bs[0][:, :, None], (A_GROUPS, CHUNK, LANES))
    yp = _even_prompt(zp, a_ws[0], bs_full, b_sink[0])
    ws_small = jnp.tril(a_ws[0][:, :n_new, :n_new]).transpose(2, 1, 0)
    wa = jnp.repeat(jnp.pad(ws_small, ((0, 0), (0, T_PAD - n_new), (0, 0))), LANES, axis=-1)
    wb = jnp.repeat(jnp.pad(a_bs[0][:, :n_new].T, ((0, T_PAD - n_new), (0, 0))), LANES, axis=-1)
    win = state_b_k.shape[2]
    kbuf = state_b_k[0].reshape(n_batch, win, B_KV_HEADS * B_DH)
    vbuf = state_b_v[0].reshape(n_batch, win, B_KV_HEADS * B_DH)
    ys3, av3 = _even_sample(pad_tokens(zs), kbuf, vbuf, wa, wb, b_sink[0])
    hp = _matmul(yp, w_out, resid=hp)
    hs = _matmul(unpad_tokens(ys3), w_out, resid=hs)

    k_off = 2 * A_WIDTH + B_HEADS * B_DH
    v_off = k_off + B_KV_HEADS * B_DH
    kv_shape = (1, -1, win, B_KV_HEADS, B_DH)
    a_v_sample = av3[:, :n_new].reshape(1, n_batch, n_new, A_GROUPS, A_WIDTH // A_GROUPS)
    b_k_prompt = zp[seq - win:, k_off:v_off].reshape(kv_shape)
    b_v_prompt = zp[seq - win:, v_off:].reshape(kv_shape)
    k_new = zs[:, k_off:v_off].reshape(n_batch, n_new, B_KV_HEADS * B_DH)
    v_new = zs[:, v_off:].reshape(n_batch, n_new, B_KV_HEADS * B_DH)
    b_k_sample = jnp.concatenate([kbuf, k_new], axis=1)[:, -win:].reshape(kv_shape)
    b_v_sample = jnp.concatenate([vbuf, v_new], axis=1)[:, -win:].reshape(kv_shape)

    wq = bf(peer_wq[0])
    hs, u_b, v_b = _peer(hs, norm2_g[0], wq, peer_k1[0], peer_k2[0], peer_u, peer_v, 0, final_g, False)
    hp = _peer(hp, norm2_g[0], wq, peer_k1[0], peer_k2[0], u_b, v_b, 0, final_g, False)

    w_in = bf(w_in_odd[0])
    w_out = bf(w_out_odd[0])
    c_width = C_HEADS * C_DV
    zp = _matmul(hp, w_in, norm_g=norm1_g[1])
    zs = _matmul(hs, w_in, norm_g=norm1_g[1])
    cos_p, sin_p = _rope_tables(jnp.arange(seq))
    cos_s, sin_s = _rope_tables(past + jnp.arange(T_PAD))
    ycp, c_s_prompt = _retention_prompt(zp, cos_p, sin_p, _retention_tables(CHUNK, CHUNK), c_norm_g[0])
    zs3 = pad_tokens(zs)
    ycs3, c_s_sample = _retention_sample(zs3, cos_s, sin_s, _retention_tables(T_PAD, n_new), c_norm_g[0],
                                         state_c_s[0])
    par = _s5_params(d_a_re[0], d_a_im[0], d_log_dt[0], d_b_re[0], d_b_im[0], d_c_re[0], d_c_im[0], d_d[0])
    u_off = 3 * c_width
    ydp_pre, d_prompt = _s5_prompt(zp, u_off // c_width, par)
    us_ts = zs[:, u_off:].reshape(n_batch, n_new, D_ROWS, D_ROW_CH).transpose(1, 2, 0, 3)
    x_re = state_d_re[0].reshape(n_batch, D_ROWS, D_ROW_STATE).transpose(1, 0, 2)
    x_im = state_d_im[0].reshape(n_batch, D_ROWS, D_ROW_STATE).transpose(1, 0, 2)
    yds_ts, s_re, s_im = _s5_sample(us_ts, x_re, x_im, par)
    glu_w = bf(d_glu_w[0])
    ydp = _glu(ydp_pre, glu_w, d_glu_b[0])
    yds = _glu(yds_ts.transpose(2, 0, 1, 3).reshape(n_batch * n_new, c_width), glu_w, d_glu_b[0])
    hp = _matmul(ycp, w_out[:c_width], x2=ydp, w2=w_out[c_width:], resid=hp)
    hs = _matmul(unpad_tokens(ycs3), w_out[:c_width], x2=yds, w2=w_out[c_width:], resid=hs)

    wq = bf(peer_wq[1])
    y_sample, u_b, v_b = _peer(hs, norm2_g[1], wq, peer_k1[1], peer_k2[1], peer_u, peer_v, 1, final_g, True)
    y_prompt = _peer(hp, norm2_g[1], wq, peer_k1[1], peer_k2[1], u_b, v_b, 1, final_g, True)
    y_prompt = y_prompt.reshape(x_prompt.shape)
    y_sample = y_sample.reshape(x_sample.shape)

    d_shape = (1, -1, D_GROUPS, D_STATE)
    return (y_prompt, y_sample, a_v_sample, b_k_prompt, b_v_prompt, b_k_sample, b_v_sample,
            c_s_prompt.reshape(1, 1, C_HEADS, C_DK, C_DV), c_s_sample[None],
            d_prompt[0].reshape(d_shape), d_prompt[1].reshape(d_shape),
            s_re.transpose(1, 0, 2).reshape(d_shape), s_im.transpose(1, 0, 2).reshape(d_shape))
```

```python
import functools
import math

import jax
import jax.numpy as jnp
from jax import lax
from jax.experimental import pallas as pl
from jax.experimental.pallas import tpu as pltpu

F32 = jnp.float32
BF16 = jnp.bfloat16
HIGHEST = lax.Precision.HIGHEST

EPS = 1e-6
NEG_BIG = -1e30
NEG_INF = float("-inf")

D_MODEL = 2048
PAST_LEN = 8192
LANES = 128
SUBLANES = 8
VMEM_LIMIT = 56 * 1024 * 1024

CHUNK = 128
A_GROUPS = 8
A_WIDTH = 1024
B_HEADS = 16
B_KV_HEADS = 4
B_GQA = 4
B_DH = 64
C_HEADS = 8
C_DK = 64
C_DV = 128
RET_DECAY_EXP0 = 5.0
ROPE_BASE = 10000.0
D_GROUPS = 64
D_STATE = 64
D_GROUP_CH = 16
D_ROWS = 8
D_ROW_STATE = 512
D_ROW_CH = 128
PEER_HEADS = 8
PEER_NKEYS = 128
PEER_TOPK = 16
PEER_HALF = 128


def _params(semantics):
    return pltpu.CompilerParams(dimension_semantics=semantics, vmem_limit_bytes=VMEM_LIMIT)


def _gelu(x):
    return 0.5 * x * (1.0 + lax.erf(x * (1.0 / math.sqrt(2.0))))


def _rmsnorm(x, g):
    return x * lax.rsqrt(jnp.mean(x * x, axis=-1, keepdims=True) + EPS) * g


def _dot_nt(a, b, precision=None):
    return lax.dot_general(a, b, (((1,), (1,)), ((), ())), precision=precision,
                           preferred_element_type=F32)


def _dot_tn(a, b, precision=None):
    return lax.dot_general(a, b, (((0,), (0,)), ((), ())), precision=precision,
                           preferred_element_type=F32)


def _mm_body(*refs, has_norm, has_pair, has_resid):
    it = iter(refs)
    x_ref, w_ref = next(it), next(it)
    g_ref = next(it) if has_norm else None
    x2_ref, w2_ref = (next(it), next(it)) if has_pair else (None, None)
    r_ref = next(it) if has_resid else None
    o_ref = next(it)
    x = x_ref[...]
    if has_norm:
        x = _rmsnorm(x, g_ref[...])
    xb = x.astype(BF16)
    x2b = x2_ref[...].astype(BF16) if has_pair else None
    n = o_ref.shape[1]
    for j in range(n // MM_COLS):
        cols = slice(j * MM_COLS, (j + 1) * MM_COLS)
        acc = jnp.dot(xb, w_ref[:, cols], preferred_element_type=F32)
        if has_pair:
            acc = acc + jnp.dot(x2b, w2_ref[:, cols], preferred_element_type=F32)
        if has_resid:
            acc = acc + r_ref[:, cols]
        o_ref[:, cols] = acc


MM_COLS = 512


def _matmul(x, w, *, norm_g=None, x2=None, w2=None, resid=None, tm=512):
    m, k = x.shape
    n = w.shape[1]
    assert m % tm == 0 and n % MM_COLS == 0
    resident = pl.Buffered(1)
    args = [x, w]
    specs = [pl.BlockSpec((tm, k), lambda i: (i, 0)),
             pl.BlockSpec((k, n), lambda i: (0, 0), pipeline_mode=resident)]
    if norm_g is not None:
        args.append(norm_g.reshape(1, k))
        specs.append(pl.BlockSpec((1, k), lambda i: (0, 0)))
    if x2 is not None:
        k2 = x2.shape[1]
        args += [x2, w2]
        specs += [pl.BlockSpec((tm, k2), lambda i: (i, 0)),
                  pl.BlockSpec((k2, n), lambda i: (0, 0), pipeline_mode=resident)]
    if resid is not None:
        args.append(resid)
        specs.append(pl.BlockSpec((tm, n), lambda i: (i, 0)))
    body = functools.partial(_mm_body, has_norm=norm_g is not None, has_pair=x2 is not None,
                             has_resid=resid is not None)
    return pl.pallas_call(
        body, grid=(m // tm,), in_specs=specs,
        out_specs=pl.BlockSpec((tm, n), lambda i: (i, 0)),
        out_shape=jax.ShapeDtypeStruct((m, n), F32),
        compiler_params=_params(("parallel",)), name="matmul",
    )(*args)


def _sink_column(sink_ref, kvh, rows_per_head, n_rows):
    grp = lax.broadcasted_iota(jnp.int32, (n_rows, 1), 0) // rows_per_head
    sk = jnp.full((n_rows, 1), sink_ref[kvh * B_GQA + B_GQA - 1], F32)
    for g in range(B_GQA - 2, -1, -1):
        sk = jnp.where(grp == g, sink_ref[kvh * B_GQA + g], sk)
    return sk


def _even_prompt_body(sink_ref, au_ref, av_ref, q_ref, kvc_ref, kvp_ref, ws_ref, bs_ref, o_ref):
    blk = pl.program_id(0)
    au = _gelu(au_ref[...])
    av = _gelu(av_ref[...])
    row = lax.broadcasted_iota(jnp.int32, (CHUNK, CHUNK), 0)
    col = lax.broadcasted_iota(jnp.int32, (CHUNK, CHUNK), 1)
    causal = row >= col
    for g in range(A_GROUPS):
        lanes = slice(g * LANES, (g + 1) * LANES)
        w = jnp.where(causal, ws_ref[g], 0.0).astype(BF16)
        mixed = jnp.dot(w, av[:, lanes].astype(BF16), preferred_element_type=F32) + bs_ref[g]
        o_ref[:, lanes] = au[:, lanes] * mixed

    q = q_ref[...]
    kvc = kvc_ref[...]
    kvp = kvp_ref[...]
    n_rows = B_GQA * CHUNK
    qi = lax.broadcasted_iota(jnp.int32, (n_rows, 2 * CHUNK), 0) % CHUNK
    kc = lax.broadcasted_iota(jnp.int32, (n_rows, 2 * CHUNK), 1)
    dist = qi + CHUNK - kc
    allowed = (dist >= 0) & (dist < CHUNK) & ((kc >= CHUNK) | (blk > 0))
    outs = []
    for kvh in range(B_KV_HEADS):
        ks = slice(kvh * B_DH, (kvh + 1) * B_DH)
        vs = slice(B_KV_HEADS * B_DH + kvh * B_DH, B_KV_HEADS * B_DH + (kvh + 1) * B_DH)
        kk = jnp.concatenate([kvp[:, ks], kvc[:, ks]], axis=0).astype(BF16)
        vv = jnp.concatenate([kvp[:, vs], kvc[:, vs]], axis=0).astype(BF16)
        q4 = jnp.concatenate(
            [q[:, (kvh * B_GQA + g) * B_DH:(kvh * B_GQA + g + 1) * B_DH] for g in range(B_GQA)], axis=0)
        s = _dot_nt(q4.astype(BF16), kk) * (B_DH ** -0.5)
        s = jnp.where(allowed, s, NEG_BIG)
        sk = _sink_column(sink_ref, kvh, CHUNK, n_rows)
        mx = jnp.maximum(jnp.max(s, axis=-1, keepdims=True), sk)
        p = jnp.exp(s - mx)
        p = p / (jnp.sum(p, axis=-1, keepdims=True) + jnp.exp(sk - mx))
        o = jnp.dot(p.astype(BF16), vv, preferred_element_type=F32)
        outs += [o[g * CHUNK:(g + 1) * CHUNK] for g in range(B_GQA)]
    o_ref[:, A_WIDTH:] = jnp.concatenate(outs, axis=1)


def _even_prompt(z, ws, bs_full, sink):
    t = z.shape[0]
    nb = t // CHUNK
    wide = A_WIDTH
    kvw = 2 * B_KV_HEADS * B_DH
    kv_blk = (2 * A_WIDTH + B_HEADS * B_DH) // kvw
    return pl.pallas_call(
        _even_prompt_body, grid=(nb,),
        in_specs=[
            pl.BlockSpec(memory_space=pltpu.SMEM),
            pl.BlockSpec((CHUNK, wide), lambda i: (i, 0)),
            pl.BlockSpec((CHUNK, wide), lambda i: (i, 1)),
            pl.BlockSpec((CHUNK, wide), lambda i: (i, 2)),
            pl.BlockSpec((CHUNK, kvw), lambda i: (i, kv_blk)),
            pl.BlockSpec((CHUNK, kvw), lambda i: (jnp.maximum(i - 1, 0), kv_blk)),
            pl.BlockSpec((A_GROUPS, CHUNK, CHUNK), lambda i: (0, 0, 0)),
            pl.BlockSpec((A_GROUPS, CHUNK, LANES), lambda i: (0, 0, 0)),
        ],
        out_specs=pl.BlockSpec((CHUNK, D_MODEL), lambda i: (i, 0)),
        out_shape=jax.ShapeDtypeStruct((t, D_MODEL), F32),
        compiler_params=_params(("parallel",)), name="even_prompt",
    )(sink, z, z, z, z, z, ws, bs_full)


EVEN_SAMPLE_BATCH = 8
T_PAD = 8


def _even_sample_body(sink_ref, au_ref, av_ref, q_ref, kv_ref, kb_ref, vb_ref, wa_ref, wb_ref,
                      y_ref, avo_ref):
    kv_width = B_KV_HEADS * B_DH
    n_rows = B_HEADS * T_PAD
    tq = lax.broadcasted_iota(jnp.int32, (n_rows, CHUNK), 0) % T_PAD
    kc = lax.broadcasted_iota(jnp.int32, (n_rows, CHUNK), 1)
    buf_allowed = kc > tq
    tq1 = lax.broadcasted_iota(jnp.int32, (n_rows, 1), 0) % T_PAD
    head_of_row = lax.broadcasted_iota(jnp.int32, (n_rows, 1), 0) // T_PAD
    sk = jnp.full((n_rows, 1), sink_ref[B_HEADS - 1], F32)
    for hd in range(B_HEADS - 2, -1, -1):
        sk = jnp.where(head_of_row == hd, sink_ref[hd], sk)
    lane_kvh = lax.broadcasted_iota(jnp.int32, (T_PAD, kv_width), 1) // B_DH
    scale = B_DH ** -0.5
    n_new = wa_ref.shape[0]
    for b in range(EVEN_SAMPLE_BATCH):
        au = _gelu(au_ref[b])
        av = _gelu(av_ref[b])
        avo_ref[b] = av
        mixed = wb_ref[...]
        for j in range(n_new):
            mixed = mixed + wa_ref[j] * av[j:j + 1, :]
        y_ref[b, :, :A_WIDTH] = au * mixed

        q = q_ref[b]
        kv = kv_ref[b]
        blocks = []
        for kvh in range(B_KV_HEADS):
            for g in range(B_GQA):
                col = (kvh * B_GQA + g) * B_DH
                qg = jnp.concatenate([q[:, col:col + B_DH]] * B_KV_HEADS, axis=1)
                blocks.append(jnp.where(lane_kvh == kvh, qg, 0.0))
        qbd = jnp.concatenate(blocks, axis=0).astype(BF16)
        s_buf = _dot_nt(qbd, kb_ref[b].astype(BF16)) * scale
        s_buf = jnp.where(buf_allowed, s_buf, NEG_BIG)
        qbd_r = qbd.astype(F32)
        s_new = []
        for j in range(n_new):
            kj = kv[j:j + 1, :kv_width].astype(BF16).astype(F32)
            sj = jnp.sum(qbd_r * kj, axis=-1, keepdims=True) * scale
            s_new.append(jnp.where(tq1 >= j, sj, NEG_BIG))
        mx = jnp.maximum(jnp.max(s_buf, axis=-1, keepdims=True), sk)
        for sj in s_new:
            mx = jnp.maximum(mx, sj)
        p_buf = jnp.exp(s_buf - mx)
        p_new = [jnp.exp(sj - mx) for sj in s_new]
        den = jnp.sum(p_buf, axis=-1, keepdims=True) + jnp.exp(sk - mx)
        for pj in p_new:
            den = den + pj
        inv = 1.0 / den
        o = jnp.dot((p_buf * inv).astype(BF16), vb_ref[b].astype(BF16), preferred_element_type=F32)
        for j in range(n_new):
            vj = kv[j:j + 1, kv_width:].astype(BF16).astype(F32)
            o = o + (p_new[j] * inv).astype(BF16).astype(F32) * vj
        outs = []
        for kvh in range(B_KV_HEADS):
            for g in range(B_GQA):
                row = (kvh * B_GQA + g) * T_PAD
                outs.append(o[row:row + T_PAD, kvh * B_DH:(kvh + 1) * B_DH])
        y_ref[b, :, A_WIDTH:] = jnp.concatenate(outs, axis=1)


def _even_sample(z3, kbuf, vbuf, wa, wb, sink):
    nb = z3.shape[0]
    bb = EVEN_SAMPLE_BATCH
    kvw = 2 * B_KV_HEADS * B_DH
    kv_blk = (2 * A_WIDTH + B_HEADS * B_DH) // kvw
    win = kbuf.shape[1]
    return pl.pallas_call(
        _even_sample_body, grid=(nb // bb,),
        in_specs=[
            pl.BlockSpec(memory_space=pltpu.SMEM),
            pl.BlockSpec((bb, T_PAD, A_WIDTH), lambda i: (i, 0, 0)),
            pl.BlockSpec((bb, T_PAD, A_WIDTH), lambda i: (i, 0, 1)),
            pl.BlockSpec((bb, T_PAD, A_WIDTH), lambda i: (i, 0, 2)),
            pl.BlockSpec((bb, T_PAD, kvw), lambda i: (i, 0, kv_blk)),
            pl.BlockSpec((bb, win, kvw // 2), lambda i: (i, 0, 0)),
            pl.BlockSpec((bb, win, kvw // 2), lambda i: (i, 0, 0)),
            pl.BlockSpec(wa.shape, lambda i: (0, 0, 0)),
            pl.BlockSpec(wb.shape, lambda i: (0, 0)),
        ],
        out_specs=[
            pl.BlockSpec((bb, T_PAD, D_MODEL), lambda i: (i, 0, 0)),
            pl.BlockSpec((bb, T_PAD, A_WIDTH), lambda i: (i, 0, 0)),
        ],
        out_shape=[
            jax.ShapeDtypeStruct((nb, T_PAD, D_MODEL), F32),
            jax.ShapeDtypeStruct((nb, T_PAD, A_WIDTH), F32),
        ],
        compiler_params=_params(("parallel",)), name="even_sample",
    )(sink, z3, z3, z3, z3, kbuf, vbuf, wa, wb)


def _rope(x, cos_f, sin_s):
    width = x.shape[-1]
    half = C_DK // 2
    lane = lax.broadcasted_iota(jnp.int32, x.shape, 1) % C_DK
    swapped = jnp.where(lane < half, pltpu.roll(x, width - half, 1), pltpu.roll(x, half, 1))
    return x * cos_f + swapped * sin_s


def _retention_chunk(qh, kh, vh, st, decay, qdec, kdec, sdec):
    qb, vb = qh.astype(BF16), vh.astype(BF16)
    scores = _dot_nt(qb, kh.astype(BF16)) * decay
    o = jnp.dot(scores.astype(BF16), vb, preferred_element_type=F32)
    o = o + jnp.dot(qb, st.astype(BF16), preferred_element_type=F32) * qdec
    new_st = sdec * st + _dot_tn((kh * kdec).astype(BF16), vb)
    return o, new_st


def _groupnorm_gate(o, gain, gate):
    mu = jnp.mean(o, axis=-1, keepdims=True)
    var = jnp.mean(jnp.square(o - mu), axis=-1, keepdims=True)
    return (o - mu) * lax.rsqrt(var + EPS) * gain * (gate * jax.nn.sigmoid(gate))


def _retention_body(qk_ref, v_ref, g_ref, cos_ref, sin_ref, decay_ref, qdec_ref, kdec_ref, sdec_ref,
                    gain_ref, o_ref, st_ref, state):
    @pl.when(pl.program_id(0) == 0)
    def _():
        state[...] = jnp.zeros_like(state)

    qk = qk_ref[...]
    width = C_HEADS * C_DK
    q = _rope(qk[:, :width], cos_ref[...], sin_ref[...]) * (C_DK ** -0.5)
    k = _rope(qk[:, width:], cos_ref[...], sin_ref[...])
    v = v_ref[...]
    g = g_ref[...]
    for h in range(C_HEADS):
        qh = q[:, h * C_DK:(h + 1) * C_DK]
        kh = k[:, h * C_DK:(h + 1) * C_DK]
        vh = v[:, h * C_DV:(h + 1) * C_DV]
        st = state[h]
        o, new_st = _retention_chunk(qh, kh, vh, st, decay_ref[h], qdec_ref[h], kdec_ref[h], sdec_ref[h])
        state[h] = new_st
        st_ref[h] = new_st
        lanes = slice(h * C_DV, (h + 1) * C_DV)
        o_ref[:, lanes] = _groupnorm_gate(o, gain_ref[:, lanes], g[:, lanes])


def _retention_prompt(z, cos_f, sin_s, tabs, gain):
    t = z.shape[0]
    nc = t // CHUNK
    decay, qdec, kdec, sdec = tabs
    width = C_HEADS * C_DV
    const3 = lambda i: (0, 0, 0)
    return pl.pallas_call(
        _retention_body, grid=(nc,),
        in_specs=[
            pl.BlockSpec((CHUNK, width), lambda i: (i, 0)),
            pl.BlockSpec((CHUNK, width), lambda i: (i, 1)),
            pl.BlockSpec((CHUNK, width), lambda i: (i, 2)),
            pl.BlockSpec((CHUNK, C_HEADS * C_DK), lambda i: (i, 0)),
            pl.BlockSpec((CHUNK, C_HEADS * C_DK), lambda i: (i, 0)),
            pl.BlockSpec(decay.shape, const3),
            pl.BlockSpec(qdec.shape, const3),
            pl.BlockSpec(kdec.shape, const3),
            pl.BlockSpec(sdec.shape, const3),
            pl.BlockSpec((1, width), lambda i: (0, 0)),
        ],
        out_specs=[
            pl.BlockSpec((CHUNK, width), lambda i: (i, 0)),
            pl.BlockSpec((C_HEADS, C_DK, C_DV), const3),
        ],
        out_shape=[
            jax.ShapeDtypeStruct((t, width), F32),
            jax.ShapeDtypeStruct((C_HEADS, C_DK, C_DV), F32),
        ],
        scratch_shapes=[pltpu.VMEM((C_HEADS, C_DK, C_DV), F32)],
        compiler_params=_params(("arbitrary",)), name="retention_prompt",
    )(z, z, z, cos_f, sin_s, decay, qdec, kdec, sdec, gain.reshape(1, width))


RET_SAMPLE_BATCH = 8


def _retention_sample_body(qk_ref, v_ref, g_ref, cos_ref, sin_ref, decay_ref, qdec_ref, kdec_ref,
                           sdec_ref, gain_ref, st_in_ref, o_ref, st_ref):
    width = C_HEADS * C_DK
    lane_head = lax.broadcasted_iota(jnp.int32, (T_PAD, width), 1) // C_DK
    for b in range(RET_SAMPLE_BATCH):
        qk = qk_ref[b]
        q = _rope(qk[:, :width], cos_ref[...], sin_ref[...]) * (C_DK ** -0.5)
        k = _rope(qk[:, width:], cos_ref[...], sin_ref[...])
        v = v_ref[b]
        g = g_ref[b]
        vb = v.astype(BF16)
        qbd = jnp.concatenate([jnp.where(lane_head == h, q, 0.0) for h in range(C_HEADS)],
                              axis=0).astype(BF16)
        st = st_in_ref[b]
        scores = _dot_nt(qbd, k.astype(BF16)) * decay_ref[...]
        o_intra = jnp.dot(scores.astype(BF16), vb, preferred_element_type=F32)
        o_cross = jnp.dot(qbd, st.astype(BF16), preferred_element_type=F32) * qdec_ref[...]
        kd = (k * kdec_ref[...]).astype(BF16)
        for h in range(C_HEADS):
            rows = slice(h * T_PAD, (h + 1) * T_PAD)
            lanes = slice(h * C_DV, (h + 1) * C_DV)
            keys = slice(h * C_DK, (h + 1) * C_DK)
            o = o_intra[rows, lanes] + o_cross[rows, :]
            o_ref[b, :, lanes] = _groupnorm_gate(o, gain_ref[:, lanes], g[:, lanes])
            st_ref[b, keys, :] = sdec_ref[keys, :] * st[keys, :] + _dot_tn(kd[:, keys], vb[:, lanes])


def _retention_sample(z3, cos_f, sin_s, tabs, gain, st_in):
    nb = z3.shape[0]
    bb = RET_SAMPLE_BATCH
    decay, qdec, kdec, sdec = tabs
    width = C_HEADS * C_DV
    decay_s = decay.reshape(C_HEADS * T_PAD, T_PAD)
    qdec_s = qdec.reshape(C_HEADS * T_PAD, C_DV)
    kdec_s = kdec.transpose(1, 0, 2).reshape(T_PAD, C_HEADS * C_DK)
    sdec_s = sdec.reshape(C_HEADS * C_DK, C_DV)
    st2 = st_in.reshape(nb, C_HEADS * C_DK, C_DV)
    c2 = lambda i: (0, 0)
    out, st_out = pl.pallas_call(
        _retention_sample_body, grid=(nb // bb,),
        in_specs=[
            pl.BlockSpec((bb, T_PAD, width), lambda i: (i, 0, 0)),
            pl.BlockSpec((bb, T_PAD, width), lambda i: (i, 0, 1)),
            pl.BlockSpec((bb, T_PAD, width), lambda i: (i, 0, 2)),
            pl.BlockSpec((T_PAD, C_HEADS * C_DK), c2),
            pl.BlockSpec((T_PAD, C_HEADS * C_DK), c2),
            pl.BlockSpec(decay_s.shape, c2),
            pl.BlockSpec(qdec_s.shape, c2),
            pl.BlockSpec(kdec_s.shape, c2),
            pl.BlockSpec(sdec_s.shape, c2),
            pl.BlockSpec((1, width), c2),
            pl.BlockSpec((bb, C_HEADS * C_DK, C_DV), lambda i: (i, 0, 0)),
        ],
        out_specs=[
            pl.BlockSpec((bb, T_PAD, width), lambda i: (i, 0, 0)),
            pl.BlockSpec((bb, C_HEADS * C_DK, C_DV), lambda i: (i, 0, 0)),
        ],
        out_shape=[
            jax.ShapeDtypeStruct((nb, T_PAD, width), F32),
            jax.ShapeDtypeStruct((nb, C_HEADS * C_DK, C_DV), F32),
        ],
        compiler_params=_params(("parallel",)), name="retention_sample",
    )(z3, z3, z3, cos_f, sin_s, decay_s, qdec_s, kdec_s, sdec_s, gain.reshape(1, width), st2)
    return out, st_out.reshape(st_in.shape)


def _retention_tables(length, n_valid):
    log_gamma = jnp.log1p(-jnp.exp2(-RET_DECAY_EXP0 - jnp.arange(C_HEADS, dtype=F32)))
    i = jnp.arange(length, dtype=F32)
    valid = (jnp.arange(length) < n_valid)
    rel = i[:, None] - i[None, :]
    decay = jnp.where(rel >= 0, jnp.exp(jnp.maximum(rel, 0.0)[None] * log_gamma[:, None, None]), 0.0)
    decay = jnp.where(valid[None, None, :], decay, 0.0)
    q_dec = jnp.exp((i[None, :] + 1.0) * log_gamma[:, None])
    k_dec = jnp.where(valid[None, :], jnp.exp((n_valid - 1.0 - i)[None, :] * log_gamma[:, None]), 0.0)
    s_dec = jnp.exp(n_valid * log_gamma)
    qdec = jnp.broadcast_to(q_dec[:, :, None], (C_HEADS, length, C_DV))
    kdec = jnp.broadcast_to(k_dec[:, :, None], (C_HEADS, length, C_DK))
    sdec = jnp.broadcast_to(s_dec[:, None, None], (C_HEADS, C_DK, C_DV))
    return decay, qdec, kdec, sdec


def _rope_tables(pos):
    half = C_DK // 2
    freqs = ROPE_BASE ** (-jnp.arange(half, dtype=F32) / half)
    ang = pos.astype(F32)[:, None] * freqs[None, :]
    cos, sin = jnp.cos(ang), jnp.sin(ang)
    cos_f = jnp.tile(jnp.concatenate([cos, cos], axis=1), (1, C_HEADS))
    sin_s = jnp.tile(jnp.concatenate([-sin, sin], axis=1), (1, C_HEADS))
    return cos_f, sin_s


def _s5_discretize(are_ref, aim_ref, ldt_ref):
    a_re, a_im = are_ref[...], aim_ref[...]
    dt = jnp.exp(ldt_ref[...])
    mag = jnp.exp(a_re * dt)
    ab_re = mag * jnp.cos(a_im * dt)
    ab_im = mag * jnp.sin(a_im * dt)
    num_re, num_im = ab_re - 1.0, ab_im
    den = a_re * a_re + a_im * a_im
    co_re = (num_re * a_re + num_im * a_im) / den
    co_im = (num_im * a_re - num_re * a_im) / den
    return ab_re, ab_im, co_re, co_im


S5_CHUNK = 256


def _s5_prompt_body(u_ref, are_ref, aim_ref, ldt_ref, bre_ref, bim_ref, cre_ref, cim_ref, dd_ref,
                    y_ref, st_ref, hre, him, state, disc):
    n = S5_CHUNK

    @pl.when(pl.program_id(0) == 0)
    def _():
        ab_re, ab_im, co_re, co_im = _s5_discretize(are_ref, aim_ref, ldt_ref)
        disc[0] = ab_re
        disc[1] = ab_im
        disc[2] = co_re
        disc[3] = co_im
        state[...] = jnp.zeros_like(state)

    n_chunks = D_ROW_STATE // LANES
    for s in range(D_ROWS):
        rows = pl.ds(s, n, stride=D_ROWS)
        u_s = u_ref[:, s * D_ROW_CH:(s + 1) * D_ROW_CH]
        u_b = u_s.astype(BF16)
        raw_re = jnp.dot(u_b, bre_ref[s], preferred_element_type=F32)
        raw_im = jnp.dot(u_b, bim_ref[s], preferred_element_type=F32)
        co_re = disc[2, s:s + 1, :]
        co_im = disc[3, s:s + 1, :]
        bu_re = co_re * raw_re - co_im * raw_im
        bu_im = co_re * raw_im + co_im * raw_re
        for c in range(n_chunks):
            hre[c, rows, :] = bu_re[:, c * LANES:(c + 1) * LANES]
            him[c, rows, :] = bu_im[:, c * LANES:(c + 1) * LANES]

    ab_re = disc[0]
    ab_im = disc[1]

    def step(t, carry):
        h_re, h_im = carry
        rows = pl.ds(pl.multiple_of(t * D_ROWS, D_ROWS), D_ROWS)
        bu_re = jnp.concatenate([hre[c, rows, :] for c in range(n_chunks)], axis=1)
        bu_im = jnp.concatenate([him[c, rows, :] for c in range(n_chunks)], axis=1)
        n_re = ab_re * h_re - ab_im * h_im + bu_re
        n_im = ab_re * h_im + ab_im * h_re + bu_im
        for c in range(n_chunks):
            hre[c, rows, :] = n_re[:, c * LANES:(c + 1) * LANES]
            him[c, rows, :] = n_im[:, c * LANES:(c + 1) * LANES]
        return n_re, n_im

    h_re, h_im = lax.fori_loop(0, n, step, (state[0], state[1]), unroll=4)
    state[0] = h_re
    state[1] = h_im
    st_ref[0] = h_re
    st_ref[1] = h_im

    for s in range(D_ROWS):
        rows = pl.ds(s, n, stride=D_ROWS)
        hist_re = jnp.concatenate([hre[c, rows, :] for c in range(n_chunks)], axis=1)
        hist_im = jnp.concatenate([him[c, rows, :] for c in range(n_chunks)], axis=1)
        cols = slice(s * D_ROW_CH, (s + 1) * D_ROW_CH)
        y = (jnp.dot(hist_re.astype(BF16), cre_ref[s], preferred_element_type=F32)
             - jnp.dot(hist_im.astype(BF16), cim_ref[s], preferred_element_type=F32)
             + dd_ref[s:s + 1, :] * u_ref[:, cols])
        y_ref[:, cols] = _gelu(y)


def _s5_prompt(z, u_col_block, par):
    a_re, a_im, ldt, b_re, b_im, c_re, c_im, dd = par
    t = z.shape[0]
    blk = S5_CHUNK * D_ROWS
    width = D_ROWS * D_ROW_CH
    c2 = lambda i: (0, 0)
    c3 = lambda i: (0, 0, 0)
    return pl.pallas_call(
        _s5_prompt_body, grid=(t // S5_CHUNK,),
        in_specs=[
            pl.BlockSpec((S5_CHUNK, width), lambda i: (i, u_col_block)),
            pl.BlockSpec(a_re.shape, c2), pl.BlockSpec(a_im.shape, c2), pl.BlockSpec(ldt.shape, c2),
            pl.BlockSpec(b_re.shape, c3), pl.BlockSpec(b_im.shape, c3),
            pl.BlockSpec(c_re.shape, c3), pl.BlockSpec(c_im.shape, c3),
            pl.BlockSpec(dd.shape, c2),
        ],
        out_specs=[
            pl.BlockSpec((S5_CHUNK, width), lambda i: (i, 0)),
            pl.BlockSpec((2, D_ROWS, D_ROW_STATE), c3),
        ],
        out_shape=[
            jax.ShapeDtypeStruct((t, width), F32),
            jax.ShapeDtypeStruct((2, D_ROWS, D_ROW_STATE), F32),
        ],
        scratch_shapes=[
            pltpu.VMEM((D_ROW_STATE // LANES, blk, LANES), F32),
            pltpu.VMEM((D_ROW_STATE // LANES, blk, LANES), F32),
            pltpu.VMEM((2, D_ROWS, D_ROW_STATE), F32), pltpu.VMEM((4, D_ROWS, D_ROW_STATE), F32),
        ],
        compiler_params=_params(("arbitrary",)), name="s5_prompt",
    )(z, a_re, a_im, ldt, b_re.astype(BF16), b_im.astype(BF16), c_re.astype(BF16), c_im.astype(BF16), dd)


def _s5_sample_body(u_ref, xre_ref, xim_ref, are_ref, aim_ref, ldt_ref, bre_ref, bim_ref, cre_ref,
                    cim_ref, dd_ref, y_ref, sre_ref, sim_ref):
    ab_re, ab_im, co_re, co_im = _s5_discretize(are_ref, aim_ref, ldt_ref)
    n_t = u_ref.shape[0]
    for s in range(D_ROWS):
        a_r, a_i = ab_re[s:s + 1, :], ab_im[s:s + 1, :]
        c_r, c_i = co_re[s:s + 1, :], co_im[s:s + 1, :]
        h_re, h_im = xre_ref[s], xim_ref[s]
        for t in range(n_t):
            u = u_ref[t, s]
            raw_re = jnp.dot(u, bre_ref[s], precision=HIGHEST, preferred_element_type=F32)
            raw_im = jnp.dot(u, bim_ref[s], precision=HIGHEST, preferred_element_type=F32)
            bu_re = c_r * raw_re - c_i * raw_im
            bu_im = c_r * raw_im + c_i * raw_re
            h_re, h_im = a_r * h_re - a_i * h_im + bu_re, a_r * h_im + a_i * h_re + bu_im
            y = (jnp.dot(h_re, cre_ref[s], precision=HIGHEST, preferred_element_type=F32)
                 - jnp.dot(h_im, cim_ref[s], precision=HIGHEST, preferred_element_type=F32)
                 + dd_ref[s:s + 1, :] * u)
            y_ref[t, s] = _gelu(y)
        sre_ref[s] = h_re
        sim_ref[s] = h_im


def _s5_sample(u_ts, x_re, x_im, par):
    a_re, a_im, ldt, b_re, b_im, c_re, c_im, dd = par
    return pl.pallas_call(
        _s5_sample_body,
        out_shape=[
            jax.ShapeDtypeStruct(u_ts.shape, F32),
            jax.ShapeDtypeStruct(x_re.shape, F32),
            jax.ShapeDtypeStruct(x_im.shape, F32),
        ],
        compiler_params=pltpu.CompilerParams(vmem_limit_bytes=VMEM_LIMIT), name="s5_sample",
    )(u_ts, x_re, x_im, a_re, a_im, ldt, b_re, b_im, c_re, c_im, dd)


def _s5_params(a_re, a_im, log_dt, b_re, b_im, c_re, c_im, dd):
    eye = jnp.eye(D_ROWS, dtype=F32)

    def rows(a):
        return a.reshape(D_ROWS, D_ROW_STATE)

    def b_blocks(b):
        b4 = b.reshape(D_ROWS, D_ROWS, D_STATE, D_GROUP_CH).transpose(0, 1, 3, 2)
        return jnp.einsum('sgcp,gh->sgchp', b4, eye).reshape(D_ROWS, D_ROW_CH, D_ROW_STATE)

    def c_blocks(c):
        c4 = c.reshape(D_ROWS, D_ROWS, D_GROUP_CH, D_STATE)
        return jnp.einsum('sgcp,gh->sgphc', c4, eye).reshape(D_ROWS, D_ROW_STATE, D_ROW_CH)

    ldt = jnp.broadcast_to(log_dt[:, None], (D_GROUPS, D_STATE))
    return (rows(a_re), rows(a_im), rows(ldt), b_blocks(b_re), b_blocks(b_im),
            c_blocks(c_re), c_blocks(c_im), dd.reshape(D_ROWS, D_ROW_CH))


def _glu_body(x_ref, w_ref, b_ref, o_ref):
    x = x_ref[...]
    gate = jnp.dot(x.astype(BF16), w_ref[...], preferred_element_type=F32) + b_ref[...]
    o_ref[...] = x * jax.nn.sigmoid(gate)


def _glu(x, w, b, tm=512):
    m, k = x.shape
    return pl.pallas_call(
        _glu_body, grid=(m // tm,),
        in_specs=[pl.BlockSpec((tm, k), lambda i: (i, 0)), pl.BlockSpec((k, k), lambda i: (0, 0)),
                  pl.BlockSpec((1, k), lambda i: (0, 0))],
        out_specs=pl.BlockSpec((tm, k), lambda i: (i, 0)),
        out_shape=jax.ShapeDtypeStruct((m, k), F32),
        compiler_params=_params(("parallel",)), name="glu",
    )(x, w, b.reshape(1, k))


PEER_SEL_TOKENS = 256


def _extract_top(s, n_out, exact, want_rank=False):
    n_rows = s.shape[0]
    rid = lax.broadcasted_iota(jnp.int32, s.shape, 0).astype(F32) if exact else None
    rank = jnp.full(s.shape, float(n_out), F32) if want_rank else None
    vals = []
    for r in range(n_out):
        m = jnp.max(s, axis=0, keepdims=True)
        if exact:
            first = jnp.min(jnp.where(s == m, rid, float(n_rows)), axis=0, keepdims=True)
            hit = rid == first
        else:
            hit = s == m
        if want_rank:
            rank = jnp.where(hit, float(r), rank)
        s = jnp.where(hit, NEG_INF, s)
        vals.append(m)
    return vals, s, rank


def _removed_count(s):
    return jnp.sum(jnp.where(s == NEG_INF, 1.0, 0.0), axis=0, keepdims=True)


_CAND_COUNTS = [PEER_TOPK // (a + 1) for a in range(PEER_TOPK)]


def _candidates(v1, v2):
    v1_all = _stack_rows(v1)
    v2_all = _stack_rows(v2)
    rid = lax.broadcasted_iota(jnp.int32, (SUBLANES, v1[0].shape[1]), 0)
    tiles = [v1[0] + v2_all]
    n_pad = 0
    a = 1
    while _CAND_COUNTS[a] > 1:
        tiles.append(jnp.where(rid < _CAND_COUNTS[a], v1[a] + v2_all[:SUBLANES], NEG_INF))
        n_pad += SUBLANES - _CAND_COUNTS[a]
        a += 1
    assert PEER_TOPK - a == SUBLANES
    tiles.append(v1_all[a:] + v2[0])
    return jnp.concatenate(tiles, axis=0), n_pad


def _stack_rows(rows):
    n = len(rows)
    rid = lax.broadcasted_iota(jnp.int32, (n, rows[0].shape[1]), 0)
    out = jnp.broadcast_to(rows[n - 1], rid.shape)
    for i in range(n - 2, -1, -1):
        out = jnp.where(rid == i, rows[i], out)
    return out


def _selected_per_first_key(removed, rank1):
    counts = [jnp.sum(removed[:PEER_TOPK], axis=0, keepdims=True)]
    row = PEER_TOPK
    a = 1
    while _CAND_COUNTS[a] > 1:
        counts.append(jnp.sum(removed[row:row + SUBLANES], axis=0, keepdims=True))
        row += SUBLANES
        a += 1
    for i in range(PEER_TOPK - a):
        counts.append(removed[row + i:row + i + 1])
    n_sel = jnp.zeros(rank1.shape, F32)
    for a, cnt in enumerate(counts):
        n_sel = jnp.where(rank1 == float(a), cnt, n_sel)
    return n_sel


def _peer_select_body(q_ref, k1_ref, k2_ref, ns_ref, g1_ref, r2_ref, e2_ref):
    def head(h, exact):
        q1 = q_ref[:, (2 * h) * PEER_HALF:(2 * h + 1) * PEER_HALF]
        q2 = q_ref[:, (2 * h + 1) * PEER_HALF:(2 * h + 2) * PEER_HALF]
        s1 = _dot_nt(k1_ref[h], q1, HIGHEST)
        s2 = _dot_nt(k2_ref[h], q2, HIGHEST)
        v1, left1, rank1 = _extract_top(s1, PEER_TOPK, exact, want_rank=True)
        v2, left2, rank2 = _extract_top(s2, PEER_TOPK, exact, want_rank=True)
        cand, n_pad = _candidates(v1, v2)
        top, left_c, _ = _extract_top(cand, PEER_TOPK, exact)
        mx = top[0]
        z = jnp.exp(top[0] - mx)
        for kk in range(1, PEER_TOPK):
            z = z + jnp.exp(top[kk] - mx)
        g1 = jnp.where(rank1 < PEER_TOPK, jnp.exp(s1 - v1[0]), 0.0) / z
        e2 = jnp.where(rank2 < PEER_TOPK, jnp.exp(s2 - v2[0]), 0.0)
        removed = jnp.where((left_c == NEG_INF) & (cand != NEG_INF), 1.0, 0.0)
        n_sel = _selected_per_first_key(removed, rank1)
        for j in range(PEER_SEL_TOKENS // LANES):
            lanes = slice(j * LANES, (j + 1) * LANES)
            ns_ref[j, h] = n_sel[:, lanes]
            g1_ref[j, h] = g1[:, lanes]
            r2_ref[j, h] = rank2[:, lanes]
            e2_ref[j, h] = e2[:, lanes]
        if exact:
            return None
        ok = ((_removed_count(left1) == PEER_TOPK) & (_removed_count(left2) == PEER_TOPK)
              & (_removed_count(left_c) == PEER_TOPK + n_pad))
        return jnp.where(ok, 0.0, 1.0)

    repeated = [jnp.max(head(h, False)) for h in range(PEER_HEADS)]
    for h in range(PEER_HEADS):
        @pl.when(repeated[h] > 0.0)
        def _():
            head(h, True)


def _peer_select(q, k1, k2):
    t = q.shape[0]
    tb = PEER_SEL_TOKENS
    nj = tb // LANES
    big = pl.BlockSpec((nj, PEER_HEADS, PEER_NKEYS, LANES), lambda i: (i, 0, 0, 0))
    shape = jax.ShapeDtypeStruct((t // LANES, PEER_HEADS, PEER_NKEYS, LANES), F32)
    return pl.pallas_call(
        _peer_select_body, grid=(t // tb,),
        in_specs=[
            pl.BlockSpec((tb, q.shape[1]), lambda i: (i, 0)),
            pl.BlockSpec(k1.shape, lambda i: (0, 0, 0)),
            pl.BlockSpec(k2.shape, lambda i: (0, 0, 0)),
        ],
        out_specs=[big, big, big, big],
        out_shape=[shape, shape, shape, shape],
        compiler_params=_params(("parallel",)), name="peer_select",
    )(q, k1, k2)


PEER_TOKENS = 512
PEER_EXPERT_TILE = 512
BF16_ROWS = 16
GATE_GROUP = 4


def _peer_dense_body(h_ref, g_ref, u_ref, v_ref, ns_ref, g1_ref, r2_ref, e2_ref, og_ref, o_ref, *rest,
                     norm_out, emit_bf16):
    if emit_bf16:
        ub_ref, vb_ref, xn, act_s, coef_s, acc = rest
    else:
        xn, act_s, coef_s, acc = rest
    e = pl.program_id(1)
    n_chunks = PEER_TOKENS // LANES

    @pl.when(e == 0)
    def _():
        xn[...] = _rmsnorm(h_ref[...], g_ref[...]).astype(BF16)
        acc[...] = jnp.zeros_like(acc)

    te = PEER_EXPERT_TILE
    slabs = te // PEER_NKEYS

    def one_tile(part):
        tile_rows = slice(part * te, (part + 1) * te)
        u_tile = u_ref[tile_rows, :].astype(BF16)
        if emit_bf16:
            ub_ref[tile_rows, :] = u_tile
        act = _dot_nt(u_tile, xn[...])
        for j in range(n_chunks):
            act_s[j] = act[:, j * LANES:(j + 1) * LANES]

        def chunk(j, carry):
            packed = (PEER_NKEYS // BF16_ROWS, BF16_ROWS, LANES)
            for c0 in range(0, slabs, GATE_GROUP):
                gates = [jnp.zeros(packed, BF16) for _ in range(GATE_GROUP)]
                for h in range(PEER_HEADS):
                    r2 = r2_ref[j, h].reshape(packed).astype(BF16)
                    e2 = e2_ref[j, h].reshape(packed).astype(BF16)
                    for k in range(GATE_GROUP):
                        key1 = part * slabs + c0 + k
                        ns = jnp.broadcast_to(ns_ref[j, h, key1:key1 + 1, :], packed[1:]).astype(BF16)
                        g1 = jnp.broadcast_to(g1_ref[j, h, key1:key1 + 1, :], packed[1:]).astype(BF16)
                        gates[k] = gates[k] + jnp.where(r2 < ns[None], e2 * g1[None], jnp.zeros((), BF16))
                for k in range(GATE_GROUP):
                    rows = slice((c0 + k) * PEER_NKEYS, (c0 + k + 1) * PEER_NKEYS)
                    act = _gelu(act_s[j, rows, :]).reshape(packed).astype(BF16)
                    coef_s[j, rows, :] = (gates[k] * act).reshape(PEER_NKEYS, LANES)
            return carry

        for j in range(n_chunks):
            chunk(j, 0)
        coef = jnp.concatenate([coef_s[j] for j in range(n_chunks)], axis=1)
        v_tile = v_ref[tile_rows, :].astype(BF16)
        if emit_bf16:
            vb_ref[tile_rows, :] = v_tile
        acc[...] += _dot_tn(coef, v_tile)

    for part in range(u_ref.shape[0] // te):
        one_tile(part)

    @pl.when(e == pl.num_programs(1) - 1)
    def _():
        out = h_ref[...] + acc[...]
        o_ref[...] = _rmsnorm(out, og_ref[...]) if norm_out else out


def _peer_dense(h, g, u_tab, v_tab, layer, sel, out_g, norm_out):
    ns, g1, r2, e2 = sel
    t, d = h.shape
    emit_bf16 = u_tab.ndim == 3
    n_exp = u_tab.shape[-2]
    tb = PEER_TOKENS
    nj = tb // LANES
    slabs = PEER_EXPERT_TILE // PEER_NKEYS
    o_spec = pl.BlockSpec((tb, d), lambda i, e: (i, 0))
    o_shape = jax.ShapeDtypeStruct((t, d), F32)
    if emit_bf16:
        assert t == tb, "every table tile must be visited exactly once when the casts are emitted"
        te = PEER_EXPERT_TILE
        n_steps = n_exp // te
        tab_spec = pl.BlockSpec((None, te, d), lambda i, e: (layer, e, 0))

        def by_tile(a):
            a5 = a.reshape(a.shape[0], PEER_HEADS, n_steps, slabs, LANES)
            return a5.transpose(2, 0, 1, 3, 4)

        ns, g1 = by_tile(ns), by_tile(g1)
        key_spec = pl.BlockSpec((None, nj, PEER_HEADS, slabs, LANES), lambda i, e: (e, i, 0, 0, 0))
    else:
        te = SUBLANES // slabs * PEER_EXPERT_TILE
        n_steps = n_exp // te
        tab_spec = pl.BlockSpec((te, d), lambda i, e: (e, 0))
        key_spec = pl.BlockSpec((nj, PEER_HEADS, SUBLANES, LANES), lambda i, e: (i, 0, e, 0))
    cast_spec = pl.BlockSpec((te, d), lambda i, e: (e, 0))
    cast_shape = jax.ShapeDtypeStruct((n_exp, d), BF16)
    tok_spec = pl.BlockSpec((nj, PEER_HEADS, PEER_NKEYS, LANES), lambda i, e: (i, 0, 0, 0))
    te_scratch = PEER_EXPERT_TILE
    return pl.pallas_call(
        functools.partial(_peer_dense_body, norm_out=norm_out, emit_bf16=emit_bf16),
        grid=(t // tb, n_steps),
        in_specs=[
            pl.BlockSpec((tb, d), lambda i, e: (i, 0)),
            pl.BlockSpec((1, d), lambda i, e: (0, 0)),
            tab_spec, tab_spec,
            key_spec, key_spec, tok_spec, tok_spec,
            pl.BlockSpec((1, d), lambda i, e: (0, 0)),
        ],
        out_specs=[o_spec, cast_spec, cast_spec] if emit_bf16 else o_spec,
        out_shape=[o_shape, cast_shape, cast_shape] if emit_bf16 else o_shape,
        scratch_shapes=[pltpu.VMEM((tb, d), BF16), pltpu.VMEM((nj, te_scratch, LANES), F32),
                        pltpu.VMEM((nj, te_scratch, LANES), BF16), pltpu.VMEM((tb, d), F32)],
        compiler_params=_params(("parallel", "arbitrary")), name="peer_dense",
    )(h, g.reshape(1, d), u_tab, v_tab, ns, g1, r2, e2, out_g.reshape(1, d))


def _peer(h, g, wq, k1, k2, u_tab, v_tab, layer, out_g, norm_out):
    q = _matmul(h, wq, norm_g=g)
    return _peer_dense(h, g, u_tab, v_tab, layer, _peer_select(q, k1, k2), out_g, norm_out)


def kernel(x_prompt, x_sample, state_b_k, state_b_v, state_c_s, state_d_re, state_d_im, norm1_g, norm2_g, final_g, w_in_even, w_out_even, a_ws, a_bs, b_sink, w_in_odd, w_out_odd, c_norm_g, d_a_re, d_a_im, d_log_dt, d_b_re, d_b_im, d_c_re, d_c_im, d_d, d_glu_w, d_glu_b, peer_wq, peer_k1, peer_k2, peer_u, peer_v):
    seq = x_prompt.shape[1]
    n_batch, n_new = x_sample.shape[:2]
    past = PAST_LEN
    hp = x_prompt.reshape(seq, D_MODEL)
    hs = x_sample.reshape(n_batch * n_new, D_MODEL)
    bf = lambda a: a.astype(BF16)

    def pad_tokens(a):
        a3 = a.reshape(n_batch, n_new, a.shape[-1])
        return jnp.pad(a3, ((0, 0), (0, T_PAD - n_new), (0, 0)))

    def unpad_tokens(a3):
        return a3[:, :n_new].reshape(n_batch * n_new, a3.shape[-1])

    w_in = bf(w_in_even[0])
    w_out = bf(w_out_even[0])
    zp = _matmul(hp, w_in, norm_g=norm1_g[0])
    zs = _matmul(hs, w_in, norm_g=norm1_g[0])
    bs_full = jnp.broadcast_to(a_bs[0][:, :, None], (A_GROUPS, CHUNK, LANES))
    yp = _even_prompt(zp, a_ws[0], bs_full, b_sink[0])
    ws_small = jnp.tril(a_ws[0][:, :n_new, :n_new]).transpose(2, 1, 0)
    wa = jnp.repeat(jnp.pad(ws_small, ((0, 0), (0, T_PAD - n_new), (0, 0))), LANES, axis=-1)
    wb = jnp.repeat(jnp.pad(a_bs[0][:, :n_new].T, ((0, T_PAD - n_new), (0, 0))), LANES, axis=-1)
    win = state_b_k.shape[2]
    kbuf = state_b_k[0].reshape(n_batch, win, B_KV_HEADS * B_DH)
    vbuf = state_b_v[0].reshape(n_batch, win, B_KV_HEADS * B_DH)
    ys3, av3 = _even_sample(pad_tokens(zs), kbuf, vbuf, wa, wb, b_sink[0])
    hp = _matmul(yp, w_out, resid=hp)
    hs = _matmul(unpad_tokens(ys3), w_out, resid=hs)

    k_off = 2 * A_WIDTH + B_HEADS * B_DH
    v_off = k_off + B_KV_HEADS * B_DH
    kv_shape = (1, -1, win, B_KV_HEADS, B_DH)
    a_v_sample = av3[:, :n_new].reshape(1, n_batch, n_new, A_GROUPS, A_WIDTH // A_GROUPS)
    b_k_prompt = zp[seq - win:, k_off:v_off].reshape(kv_shape)
    b_v_prompt = zp[seq - win:, v_off:].reshape(kv_shape)
    k_new = zs[:, k_off:v_off].reshape(n_batch, n_new, B_KV_HEADS * B_DH)
    v_new = zs[:, v_off:].reshape(n_batch, n_new, B_KV_HEADS * B_DH)
    b_k_sample = jnp.concatenate([kbuf, k_new], axis=1)[:, -win:].reshape(kv_shape)
    b_v_sample = jnp.concatenate([vbuf, v_new], axis=1)[:, -win:].reshape(kv_shape)

    wq = bf(peer_wq[0])
    hs, u_b, v_b = _peer(hs, norm2_g[0], wq, peer_k1[0], peer_k2[0], peer_u, peer_v, 0, final_g, False)
    hp = _peer(hp, norm2_g[0], wq, peer_k1[0], peer_k2[0], u_b, v_b, 0, final_g, False)

    w_in = bf(w_in_odd[0])
    w_out = bf(w_out_odd[0])
    c_width = C_HEADS * C_DV
    zp = _matmul(hp, w_in, norm_g=norm1_g[1])
    zs = _matmul(hs, w_in, norm_g=norm1_g[1])
    cos_p, sin_p = _rope_tables(jnp.arange(seq))
    cos_s, sin_s = _rope_tables(past + jnp.arange(T_PAD))
    ycp, c_s_prompt = _retention_prompt(zp, cos_p, sin_p, _retention_tables(CHUNK, CHUNK), c_norm_g[0])
    zs3 = pad_tokens(zs)
    ycs3, c_s_sample = _retention_sample(zs3, cos_s, sin_s, _retention_tables(T_PAD, n_new), c_norm_g[0],
                                         state_c_s[0])
    par = _s5_params(d_a_re[0], d_a_im[0], d_log_dt[0], d_b_re[0], d_b_im[0], d_c_re[0], d_c_im[0], d_d[0])
    u_off = 3 * c_width
    ydp_pre, d_prompt = _s5_prompt(zp, u_off // c_width, par)
    us_ts = zs[:, u_off:].reshape(n_batch, n_new, D_ROWS, D_ROW_CH).transpose(1, 2, 0, 3)
    x_re = state_d_re[0].reshape(n_batch, D_ROWS, D_ROW_STATE).transpose(1, 0, 2)
    x_im = state_d_im[0].reshape(n_batch, D_ROWS, D_ROW_STATE).transpose(1, 0, 2)
    yds_ts, s_re, s_im = _s5_sample(us_ts, x_re, x_im, par)
    glu_w = bf(d_glu_w[0])
    ydp = _glu(ydp_pre, glu_w, d_glu_b[0])
    yds = _glu(yds_ts.transpose(2, 0, 1, 3).reshape(n_batch * n_new, c_width), glu_w, d_glu_b[0])
    hp = _matmul(ycp, w_out[:c_width], x2=ydp, w2=w_out[c_width:], resid=hp)
    hs = _matmul(unpad_tokens(ycs3), w_out[:c_width], x2=yds, w2=w_out[c_width:], resid=hs)

    wq = bf(peer_wq[1])
    y_sample, u_b, v_b = _peer(hs, norm2_g[1], wq, peer_k1[1], peer_k2[1], peer_u, peer_v, 1, final_g, True)
    y_prompt = _peer(hp, norm2_g[1], wq, peer_k1[1], peer_k2[1], u_b, v_b, 1, final_g, True)
    y_prompt = y_prompt.reshape(x_prompt.shape)
    y_sample = y_sample.reshape(x_sample.shape)

    d_shape = (1, -1, D_GROUPS, D_STATE)
    return (y_prompt, y_sample, a_v_sample, b_k_prompt, b_v_prompt, b_k_sample, b_v_sample,
            c_s_prompt.reshape(1, 1, C_HEADS, C_DK, C_DV), c_s_sample[None],
            d_prompt[0].reshape(d_shape), d_prompt[1].reshape(d_shape),
            s_re.transpose(1, 0, 2).reshape(d_shape), s_im.transpose(1, 0, 2).reshape(d_shape))
```

```python
import functools
import math

import jax
import jax.numpy as jnp
from jax import lax
from jax.experimental import pallas as pl
from jax.experimental.pallas import tpu as pltpu

F32 = jnp.float32
BF16 = jnp.bfloat16
HIGHEST = lax.Precision.HIGHEST

EPS = 1e-6
NEG_BIG = -1e30
NEG_INF = float("-inf")

D_MODEL = 2048
PAST_LEN = 8192
LANES = 128
SUBLANES = 8
VMEM_LIMIT = 56 * 1024 * 1024

CHUNK = 128
A_GROUPS = 8
A_WIDTH = 1024
B_HEADS = 16
B_KV_HEADS = 4
B_GQA = 4
B_DH = 64
C_HEADS = 8
C_DK = 64
C_DV = 128
RET_DECAY_EXP0 = 5.0
ROPE_BASE = 10000.0
D_GROUPS = 64
D_STATE = 64
D_GROUP_CH = 16
D_ROWS = 8
D_ROW_STATE = 512
D_ROW_CH = 128
PEER_HEADS = 8
PEER_NKEYS = 128
PEER_TOPK = 16
PEER_HALF = 128


def _params(semantics):
    return pltpu.CompilerParams(dimension_semantics=semantics, vmem_limit_bytes=VMEM_LIMIT)


def _gelu(x):
    return 0.5 * x * (1.0 + lax.erf(x * (1.0 / math.sqrt(2.0))))


def _rmsnorm(x, g):
    return x * lax.rsqrt(jnp.mean(x * x, axis=-1, keepdims=True) + EPS) * g


def _dot_nt(a, b, precision=None):
    return lax.dot_general(a, b, (((1,), (1,)), ((), ())), precision=precision,
                           preferred_element_type=F32)


def _dot_tn(a, b, precision=None):
    return lax.dot_general(a, b, (((0,), (0,)), ((), ())), precision=precision,
                           preferred_element_type=F32)


def _mm_body(*refs, has_norm, has_pair, has_resid):
    it = iter(refs)
    x_ref, w_ref = next(it), next(it)
    g_ref = next(it) if has_norm else None
    x2_ref, w2_ref = (next(it), next(it)) if has_pair else (None, None)
    r_ref = next(it) if has_resid else None
    o_ref = next(it)
    x = x_ref[...]
    if has_norm:
        x = _rmsnorm(x, g_ref[...])
    xb = x.astype(BF16)
    x2b = x2_ref[...].astype(BF16) if has_pair else None
    n = o_ref.shape[1]
    for j in range(n // MM_COLS):
        cols = slice(j * MM_COLS, (j + 1) * MM_COLS)
        acc = jnp.dot(xb, w_ref[:, cols], preferred_element_type=F32)
        if has_pair:
            acc = acc + jnp.dot(x2b, w2_ref[:, cols], preferred_element_type=F32)
        if has_resid:
            acc = acc + r_ref[:, cols]
        o_ref[:, cols] = acc


MM_COLS = 512


def _matmul(x, w, *, norm_g=None, x2=None, w2=None, resid=None, tm=512):
    m, k = x.shape
    n = w.shape[1]
    assert m % tm == 0 and n % MM_COLS == 0
    resident = pl.Buffered(1)
    args = [x, w]
    specs = [pl.BlockSpec((tm, k), lambda i: (i, 0)),
             pl.BlockSpec((k, n), lambda i: (0, 0), pipeline_mode=resident)]
    if norm_g is not None:
        args.append(norm_g.reshape(1, k))
        specs.append(pl.BlockSpec((1, k), lambda i: (0, 0)))
    if x2 is not None:
        k2 = x2.shape[1]
        args += [x2, w2]
        specs += [pl.BlockSpec((tm, k2), lambda i: (i, 0)),
                  pl.BlockSpec((k2, n), lambda i: (0, 0), pipeline_mode=resident)]
    if resid is not None:
        args.append(resid)
        specs.append(pl.BlockSpec((tm, n), lambda i: (i, 0)))
    body = functools.partial(_mm_body, has_norm=norm_g is not None, has_pair=x2 is not None,
                             has_resid=resid is not None)
    return pl.pallas_call(
        body, grid=(m // tm,), in_specs=specs,
        out_specs=pl.BlockSpec((tm, n), lambda i: (i, 0)),
        out_shape=jax.ShapeDtypeStruct((m, n), F32),
        compiler_params=_params(("parallel",)), name="matmul",
    )(*args)


def _sink_column(sink_ref, kvh, rows_per_head, n_rows):
    grp = lax.broadcasted_iota(jnp.int32, (n_rows, 1), 0) // rows_per_head
    sk = jnp.full((n_rows, 1), sink_ref[kvh * B_GQA + B_GQA - 1], F32)
    for g in range(B_GQA - 2, -1, -1):
        sk = jnp.where(grp == g, sink_ref[kvh * B_GQA + g], sk)
    return sk


def _even_prompt_body(sink_ref, au_ref, av_ref, q_ref, kvc_ref, kvp_ref, ws_ref, bs_ref, o_ref):
    blk = pl.program_id(0)
    au = _gelu(au_ref[...])
    av = _gelu(av_ref[...])
    row = lax.broadcasted_iota(jnp.int32, (CHUNK, CHUNK), 0)
    col = lax.broadcasted_iota(jnp.int32, (CHUNK, CHUNK), 1)
    causal = row >= col
    for g in range(A_GROUPS):
        lanes = slice(g * LANES, (g + 1) * LANES)
        w = jnp.where(causal, ws_ref[g], 0.0).astype(BF16)
        mixed = jnp.dot(w, av[:, lanes].astype(BF16), preferred_element_type=F32) + bs_ref[g]
        o_ref[:, lanes] = au[:, lanes] * mixed

    q = q_ref[...]
    kvc = kvc_ref[...]
    kvp = kvp_ref[...]
    n_rows = B_GQA * CHUNK
    qi = lax.broadcasted_iota(jnp.int32, (n_rows, 2 * CHUNK), 0) % CHUNK
    kc = lax.broadcasted_iota(jnp.int32, (n_rows, 2 * CHUNK), 1)
    dist = qi + CHUNK - kc
    allowed = (dist >= 0) & (dist < CHUNK) & ((kc >= CHUNK) | (blk > 0))
    outs = []
    for kvh in range(B_KV_HEADS):
        ks = slice(kvh * B_DH, (kvh + 1) * B_DH)
        vs = slice(B_KV_HEADS * B_DH + kvh * B_DH, B_KV_HEADS * B_DH + (kvh + 1) * B_DH)
        kk = jnp.concatenate([kvp[:, ks], kvc[:, ks]], axis=0).astype(BF16)
        vv = jnp.concatenate([kvp[:, vs], kvc[:, vs]], axis=0).astype(BF16)
        q4 = jnp.concatenate(
            [q[:, (kvh * B_GQA + g) * B_DH:(kvh * B_GQA + g + 1) * B_DH] for g in range(B_GQA)], axis=0)
        s = _dot_nt(q4.astype(BF16), kk) * (B_DH ** -0.5)
        s = jnp.where(allowed, s, NEG_BIG)
        sk = _sink_column(sink_ref, kvh, CHUNK, n_rows)
        mx = jnp.maximum(jnp.max(s, axis=-1, keepdims=True), sk)
        p = jnp.exp(s - mx)
        p = p / (jnp.sum(p, axis=-1, keepdims=True) + jnp.exp(sk - mx))
        o = jnp.dot(p.astype(BF16), vv, preferred_element_type=F32)
        outs += [o[g * CHUNK:(g + 1) * CHUNK] for g in range(B_GQA)]
    o_ref[:, A_WIDTH:] = jnp.concatenate(outs, axis=1)


def _even_prompt(z, ws, bs_full, sink):
    t = z.shape[0]
    nb = t // CHUNK
    wide = A_WIDTH
    kvw = 2 * B_KV_HEADS * B_DH
    kv_blk = (2 * A_WIDTH + B_HEADS * B_DH) // kvw
    return pl.pallas_call(
        _even_prompt_body, grid=(nb,),
        in_specs=[
            pl.BlockSpec(memory_space=pltpu.SMEM),
            pl.BlockSpec((CHUNK, wide), lambda i: (i, 0)),
            pl.BlockSpec((CHUNK, wide), lambda i: (i, 1)),
            pl.BlockSpec((CHUNK, wide), lambda i: (i, 2)),
            pl.BlockSpec((CHUNK, kvw), lambda i: (i, kv_blk)),
            pl.BlockSpec((CHUNK, kvw), lambda i: (jnp.maximum(i - 1, 0), kv_blk)),
            pl.BlockSpec((A_GROUPS, CHUNK, CHUNK), lambda i: (0, 0, 0)),
            pl.BlockSpec((A_GROUPS, CHUNK, LANES), lambda i: (0, 0, 0)),
        ],
        out_specs=pl.BlockSpec((CHUNK, D_MODEL), lambda i: (i, 0)),
        out_shape=jax.ShapeDtypeStruct((t, D_MODEL), F32),
        compiler_params=_params(("parallel",)), name="even_prompt",
    )(sink, z, z, z, z, z, ws, bs_full)


EVEN_SAMPLE_BATCH = 8
T_PAD = 8


def _even_sample_body(sink_ref, au_ref, av_ref, q_ref, kv_ref, kb_ref, vb_ref, wa_ref, wb_ref,
                      y_ref, avo_ref):
    kv_width = B_KV_HEADS * B_DH
    n_rows = B_HEADS * T_PAD
    tq = lax.broadcasted_iota(jnp.int32, (n_rows, CHUNK), 0) % T_PAD
    kc = lax.broadcasted_iota(jnp.int32, (n_rows, CHUNK), 1)
    buf_allowed = kc > tq
    tq1 = lax.broadcasted_iota(jnp.int32, (n_rows, 1), 0) % T_PAD
    head_of_row = lax.broadcasted_iota(jnp.int32, (n_rows, 1), 0) // T_PAD
    sk = jnp.full((n_rows, 1), sink_ref[B_HEADS - 1], F32)
    for hd in range(B_HEADS - 2, -1, -1):
        sk = jnp.where(head_of_row == hd, sink_ref[hd], sk)
    lane_kvh = lax.broadcasted_iota(jnp.int32, (T_PAD, kv_width), 1) // B_DH
    scale = B_DH ** -0.5
    n_new = wa_ref.shape[0]
    for b in range(EVEN_SAMPLE_BATCH):
        au = _gelu(au_ref[b])
        av = _gelu(av_ref[b])
        avo_ref[b] = av
        mixed = wb_ref[...]
        for j in range(n_new):
            mixed = mixed + wa_ref[j] * av[j:j + 1, :]
        y_ref[b, :, :A_WIDTH] = au * mixed

        q = q_ref[b]
        kv = kv_ref[b]
        blocks = []
        for kvh in range(B_KV_HEADS):
            for g in range(B_GQA):
                col = (kvh * B_GQA + g) * B_DH
                qg = jnp.concatenate([q[:, col:col + B_DH]] * B_KV_HEADS, axis=1)
                blocks.append(jnp.where(lane_kvh == kvh, qg, 0.0))
        qbd = jnp.concatenate(blocks, axis=0).astype(BF16)
        s_buf = _dot_nt(qbd, kb_ref[b].astype(BF16)) * scale
        s_buf = jnp.where(buf_allowed, s_buf, NEG_BIG)
        qbd_r = qbd.astype(F32)
        s_new = []
        for j in range(n_new):
            kj = kv[j:j + 1, :kv_width].astype(BF16).astype(F32)
            sj = jnp.sum(qbd_r * kj, axis=-1, keepdims=True) * scale
            s_new.append(jnp.where(tq1 >= j, sj, NEG_BIG))
        mx = jnp.maximum(jnp.max(s_buf, axis=-1, keepdims=True), sk)
        for sj in s_new:
            mx = jnp.maximum(mx, sj)
        p_buf = jnp.exp(s_buf - mx)
        p_new = [jnp.exp(sj - mx) for sj in s_new]
        den = jnp.sum(p_buf, axis=-1, keepdims=True) + jnp.exp(sk - mx)
        for pj in p_new:
            den = den + pj
        inv = 1.0 / den
        o = jnp.dot((p_buf * inv).astype(BF16), vb_ref[b].astype(BF16), preferred_element_type=F32)
        for j in range(n_new):
            vj = kv[j:j + 1, kv_width:].astype(BF16).astype(F32)
            o = o + (p_new[j] * inv).astype(BF16).astype(F32) * vj
        outs = []
        for kvh in range(B_KV_HEADS):
            for g in range(B_GQA):
                row = (kvh * B_GQA + g) * T_PAD
                outs.append(o[row:row + T_PAD, kvh * B_DH:(kvh + 1) * B_DH])
        y_ref[b, :, A_WIDTH:] = jnp.concatenate(outs, axis=1)


def _even_sample(z3, kbuf, vbuf, wa, wb, sink):
    nb = z3.shape[0]
    bb = EVEN_SAMPLE_BATCH
    kvw = 2 * B_KV_HEADS * B_DH
    kv_blk = (2 * A_WIDTH + B_HEADS * B_DH) // kvw
    win = kbuf.shape[1]
    return pl.pallas_call(
        _even_sample_body, grid=(nb // bb,),
        in_specs=[
            pl.BlockSpec(memory_space=pltpu.SMEM),
            pl.BlockSpec((bb, T_PAD, A_WIDTH), lambda i: (i, 0, 0)),
            pl.BlockSpec((bb, T_PAD, A_WIDTH), lambda i: (i, 0, 1)),
            pl.BlockSpec((bb, T_PAD, A_WIDTH), lambda i: (i, 0, 2)),
            pl.BlockSpec((bb, T_PAD, kvw), lambda i: (i, 0, kv_blk)),
            pl.BlockSpec((bb, win, kvw // 2), lambda i: (i, 0, 0)),
            pl.BlockSpec((bb, win, kvw // 2), lambda i: (i, 0, 0)),
            pl.BlockSpec(wa.shape, lambda i: (0, 0, 0)),
            pl.BlockSpec(wb.shape, lambda i: (0, 0)),
        ],
        out_specs=[
            pl.BlockSpec((bb, T_PAD, D_MODEL), lambda i: (i, 0, 0)),
            pl.BlockSpec((bb, T_PAD, A_WIDTH), lambda i: (i, 0, 0)),
        ],
        out_shape=[
            jax.ShapeDtypeStruct((nb, T_PAD, D_MODEL), F32),
            jax.ShapeDtypeStruct((nb, T_PAD, A_WIDTH), F32),
        ],
        compiler_params=_params(("parallel",)), name="even_sample",
    )(sink, z3, z3, z3, z3, kbuf, vbuf, wa, wb)


def _rope(x, cos_f, sin_s):
    width = x.shape[-1]
    half = C_DK // 2
    lane = lax.broadcasted_iota(jnp.int32, x.shape, 1) % C_DK
    swapped = jnp.where(lane < half, pltpu.roll(x, width - half, 1), pltpu.roll(x, half, 1))
    return x * cos_f + swapped * sin_s


def _retention_chunk(qh, kh, vh, st, decay, qdec, kdec, sdec):
    qb, vb = qh.astype(BF16), vh.astype(BF16)
    scores = _dot_nt(qb, kh.astype(BF16)) * decay
    o = jnp.dot(scores.astype(BF16), vb, preferred_element_type=F32)
    o = o + jnp.dot(qb, st.astype(BF16), preferred_element_type=F32) * qdec
    new_st = sdec * st + _dot_tn((kh * kdec).astype(BF16), vb)
    return o, new_st


def _groupnorm_gate(o, gain, gate):
    mu = jnp.mean(o, axis=-1, keepdims=True)
    var = jnp.mean(jnp.square(o - mu), axis=-1, keepdims=True)
    return (o - mu) * lax.rsqrt(var + EPS) * gain * (gate * jax.nn.sigmoid(gate))


def _retention_body(qk_ref, v_ref, g_ref, cos_ref, sin_ref, decay_ref, qdec_ref, kdec_ref, sdec_ref,
                    gain_ref, o_ref, st_ref, state):
    @pl.when(pl.program_id(0) == 0)
    def _():
        state[...] = jnp.zeros_like(state)

    qk = qk_ref[...]
    width = C_HEADS * C_DK
    q = _rope(qk[:, :width], cos_ref[...], sin_ref[...]) * (C_DK ** -0.5)
    k = _rope(qk[:, width:], cos_ref[...], sin_ref[...])
    v = v_ref[...]
    g = g_ref[...]
    for h in range(C_HEADS):
        qh = q[:, h * C_DK:(h + 1) * C_DK]
        kh = k[:, h * C_DK:(h + 1) * C_DK]
        vh = v[:, h * C_DV:(h + 1) * C_DV]
        st = state[h]
        o, new_st = _retention_chunk(qh, kh, vh, st, decay_ref[h], qdec_ref[h], kdec_ref[h], sdec_ref[h])
        state[h] = new_st
        st_ref[h] = new_st
        lanes = slice(h * C_DV, (h + 1) * C_DV)
        o_ref[:, lanes] = _groupnorm_gate(o, gain_ref[:, lanes], g[:, lanes])


def _retention_prompt(z, cos_f, sin_s, tabs, gain):
    t = z.shape[0]
    nc = t // CHUNK
    decay, qdec, kdec, sdec = tabs
    width = C_HEADS * C_DV
    const3 = lambda i: (0, 0, 0)
    return pl.pallas_call(
        _retention_body, grid=(nc,),
        in_specs=[
            pl.BlockSpec((CHUNK, width), lambda i: (i, 0)),
            pl.BlockSpec((CHUNK, width), lambda i: (i, 1)),
            pl.BlockSpec((CHUNK, width), lambda i: (i, 2)),
            pl.BlockSpec((CHUNK, C_HEADS * C_DK), lambda i: (i, 0)),
            pl.BlockSpec((CHUNK, C_HEADS * C_DK), lambda i: (i, 0)),
            pl.BlockSpec(decay.shape, const3),
            pl.BlockSpec(qdec.shape, const3),
            pl.BlockSpec(kdec.shape, const3),
            pl.BlockSpec(sdec.shape, const3),
            pl.BlockSpec((1, width), lambda i: (0, 0)),
        ],
        out_specs=[
            pl.BlockSpec((CHUNK, width), lambda i: (i, 0)),
            pl.BlockSpec((C_HEADS, C_DK, C_DV), const3),
        ],
        out_shape=[
            jax.ShapeDtypeStruct((t, width), F32),
            jax.ShapeDtypeStruct((C_HEADS, C_DK, C_DV), F32),
        ],
        scratch_shapes=[pltpu.VMEM((C_HEADS, C_DK, C_DV), F32)],
        compiler_params=_params(("arbitrary",)), name="retention_prompt",
    )(z, z, z, cos_f, sin_s, decay, qdec, kdec, sdec, gain.reshape(1, width))


RET_SAMPLE_BATCH = 8


def _retention_sample_body(qk_ref, v_ref, g_ref, cos_ref, sin_ref, decay_ref, qdec_ref, kdec_ref,
                           sdec_ref, gain_ref, st_in_ref, o_ref, st_ref):
    width = C_HEADS * C_DK
    lane_head = lax.broadcasted_iota(jnp.int32, (T_PAD, width), 1) // C_DK
    for b in range(RET_SAMPLE_BATCH):
        qk = qk_ref[b]
        q = _rope(qk[:, :width], cos_ref[...], sin_ref[...]) * (C_DK ** -0.5)
        k = _rope(qk[:, width:], cos_ref[...], sin_ref[...])
        v = v_ref[b]
        g = g_ref[b]
        vb = v.astype(BF16)
        qbd = jnp.concatenate([jnp.where(lane_head == h, q, 0.0) for h in range(C_HEADS)],
                              axis=0).astype(BF16)
        st = st_in_ref[b]
        scores = _dot_nt(qbd, k.astype(BF16)) * decay_ref[...]
        o_intra = jnp.dot(scores.astype(BF16), vb, preferred_element_type=F32)
        o_cross = jnp.dot(qbd, st.astype(BF16), preferred_element_type=F32) * qdec_ref[...]
        kd = (k * kdec_ref[...]).astype(BF16)
        for h in range(C_HEADS):
            rows = slice(h * T_PAD, (h + 1) * T_PAD)
            lanes = slice(h * C_DV, (h + 1) * C_DV)
            keys = slice(h * C_DK, (h + 1) * C_DK)
            o = o_intra[rows, lanes] + o_cross[rows, :]
            o_ref[b, :, lanes] = _groupnorm_gate(o, gain_ref[:, lanes], g[:, lanes])
            st_ref[b, keys, :] = sdec_ref[keys, :] * st[keys, :] + _dot_tn(kd[:, keys], vb[:, lanes])


def _retention_sample(z3, cos_f, sin_s, tabs, gain, st_in):
    nb = z3.shape[0]
    bb = RET_SAMPLE_BATCH
    decay, qdec, kdec, sdec = tabs
    width = C_HEADS * C_DV
    decay_s = decay.reshape(C_HEADS * T_PAD, T_PAD)
    qdec_s = qdec.reshape(C_HEADS * T_PAD, C_DV)
    kdec_s = kdec.transpose(1, 0, 2).reshape(T_PAD, C_HEADS * C_DK)
    sdec_s = sdec.reshape(C_HEADS * C_DK, C_DV)
    st2 = st_in.reshape(nb, C_HEADS * C_DK, C_DV)
    c2 = lambda i: (0, 0)
    out, st_out = pl.pallas_call(
        _retention_sample_body, grid=(nb // bb,),
        in_specs=[
            pl.BlockSpec((bb, T_PAD, width), lambda i: (i, 0, 0)),
            pl.BlockSpec((bb, T_PAD, width), lambda i: (i, 0, 1)),
            pl.BlockSpec((bb, T_PAD, width), lambda i: (i, 0, 2)),
            pl.BlockSpec((T_PAD, C_HEADS * C_DK), c2),
            pl.BlockSpec((T_PAD, C_HEADS * C_DK), c2),
            pl.BlockSpec(decay_s.shape, c2),
            pl.BlockSpec(qdec_s.shape, c2),
            pl.BlockSpec(kdec_s.shape, c2),
            pl.BlockSpec(sdec_s.shape, c2),
            pl.BlockSpec((1, width), c2),
            pl.BlockSpec((bb, C_HEADS * C_DK, C_DV), lambda i: (i, 0, 0)),
        ],
        out_specs=[
            pl.BlockSpec((bb, T_PAD, width), lambda i: (i, 0, 0)),
            pl.BlockSpec((bb, C_HEADS * C_DK, C_DV), lambda i: (i, 0, 0)),
        ],
        out_shape=[
            jax.ShapeDtypeStruct((nb, T_PAD, width), F32),
            jax.ShapeDtypeStruct((nb, C_HEADS * C_DK, C_DV), F32),
        ],
        compiler_params=_params(("parallel",)), name="retention_sample",
    )(z3, z3, z3, cos_f, sin_s, decay_s, qdec_s, kdec_s, sdec_s, gain.reshape(1, width), st2)
    return out, st_out.reshape(st_in.shape)


def _retention_tables(length, n_valid):
    log_gamma = jnp.log1p(-jnp.exp2(-RET_DECAY_EXP0 - jnp.arange(C_HEADS, dtype=F32)))
    i = jnp.arange(length, dtype=F32)
    valid = (jnp.arange(length) < n_valid)
    rel = i[:, None] - i[None, :]
    decay = jnp.where(rel >= 0, jnp.exp(jnp.maximum(rel, 0.0)[None] * log_gamma[:, None, None]), 0.0)
    decay = jnp.where(valid[None, None, :], decay, 0.0)
    q_dec = jnp.exp((i[None, :] + 1.0) * log_gamma[:, None])
    k_dec = jnp.where(valid[None, :], jnp.exp((n_valid - 1.0 - i)[None, :] * log_gamma[:, None]), 0.0)
    s_dec = jnp.exp(n_valid * log_gamma)
    qdec = jnp.broadcast_to(q_dec[:, :, None], (C_HEADS, length, C_DV))
    kdec = jnp.broadcast_to(k_dec[:, :, None], (C_HEADS, length, C_DK))
    sdec = jnp.broadcast_to(s_dec[:, None, None], (C_HEADS, C_DK, C_DV))
    return decay, qdec, kdec, sdec


def _rope_tables(pos):
    half = C_DK // 2
    freqs = ROPE_BASE ** (-jnp.arange(half, dtype=F32) / half)
    ang = pos.astype(F32)[:, None] * freqs[None, :]
    cos, sin = jnp.cos(ang), jnp.sin(ang)
    cos_f = jnp.tile(jnp.concatenate([cos, cos], axis=1), (1, C_HEADS))
    sin_s = jnp.tile(jnp.concatenate([-sin, sin], axis=1), (1, C_HEADS))
    return cos_f, sin_s


def _s5_discretize(are_ref, aim_ref, ldt_ref):
    a_re, a_im = are_ref[...], aim_ref[...]
    dt = jnp.exp(ldt_ref[...])
    mag = jnp.exp(a_re * dt)
    ab_re = mag * jnp.cos(a_im * dt)
    ab_im = mag * jnp.sin(a_im * dt)
    num_re, num_im = ab_re - 1.0, ab_im
    den = a_re * a_re + a_im * a_im
    co_re = (num_re * a_re + num_im * a_im) / den
    co_im = (num_im * a_re - num_re * a_im) / den
    return ab_re, ab_im, co_re, co_im


S5_CHUNK = 256


def _s5_prompt_body(u_ref, are_ref, aim_ref, ldt_ref, bre_ref, bim_ref, cre_ref, cim_ref, dd_ref,
                    y_ref, st_ref, hre, him, state, disc):
    n = S5_CHUNK

    @pl.when(pl.program_id(0) == 0)
    def _():
        ab_re, ab_im, co_re, co_im = _s5_discretize(are_ref, aim_ref, ldt_ref)
        disc[0] = ab_re
        disc[1] = ab_im
        disc[2] = co_re
        disc[3] = co_im
        state[...] = jnp.zeros_like(state)

    n_chunks = D_ROW_STATE // LANES
    for s in range(D_ROWS):
        rows = pl.ds(s, n, stride=D_ROWS)
        u_s = u_ref[:, s * D_ROW_CH:(s + 1) * D_ROW_CH]
        u_b = u_s.astype(BF16)
        raw_re = jnp.dot(u_b, bre_ref[s], preferred_element_type=F32)
        raw_im = jnp.dot(u_b, bim_ref[s], preferred_element_type=F32)
        co_re = disc[2, s:s + 1, :]
        co_im = disc[3, s:s + 1, :]
        bu_re = co_re * raw_re - co_im * raw_im
        bu_im = co_re * raw_im + co_im * raw_re
        for c in range(n_chunks):
            hre[c, rows, :] = bu_re[:, c * LANES:(c + 1) * LANES]
            him[c, rows, :] = bu_im[:, c * LANES:(c + 1) * LANES]

    ab_re = disc[0]
    ab_im = disc[1]

    def step(t, carry):
        h_re, h_im = carry
        rows = pl.ds(pl.multiple_of(t * D_ROWS, D_ROWS), D_ROWS)
        bu_re = jnp.concatenate([hre[c, rows, :] for c in range(n_chunks)], axis=1)
        bu_im = jnp.concatenate([him[c, rows, :] for c in range(n_chunks)], axis=1)
        n_re = ab_re * h_re - ab_im * h_im + bu_re
        n_im = ab_re * h_im + ab_im * h_re + bu_im
        for c in range(n_chunks):
            hre[c, rows, :] = n_re[:, c * LANES:(c + 1) * LANES]
            him[c, rows, :] = n_im[:, c * LANES:(c + 1) * LANES]
        return n_re, n_im

    h_re, h_im = lax.fori_loop(0, n, step, (state[0], state[1]), unroll=4)
    state[0] = h_re
    state[1] = h_im
    st_ref[0] = h_re
    st_ref[1] = h_im

    for s in range(D_ROWS):
        rows = pl.ds(s, n, stride=D_ROWS)
        hist_re = jnp.concatenate([hre[c, rows, :] for c in range(n_chunks)], axis=1)
        hist_im = jnp.concatenate([him[c, rows, :] for c in range(n_chunks)], axis=1)
        cols = slice(s * D_ROW_CH, (s + 1) * D_ROW_CH)
        y = (jnp.dot(hist_re.astype(BF16), cre_ref[s], preferred_element_type=F32)
             - jnp.dot(hist_im.astype(BF16), cim_ref[s], preferred_element_type=F32)
             + dd_ref[s:s + 1, :] * u_ref[:, cols])
        y_ref[:, cols] = _gelu(y)


def _s5_prompt(z, u_col_block, par):
    a_re, a_im, ldt, b_re, b_im, c_re, c_im, dd = par
    t = z.shape[0]
    blk = S5_CHUNK * D_ROWS
    width = D_ROWS * D_ROW_CH
    c2 = lambda i: (0, 0)
    c3 = lambda i: (0, 0, 0)
    return pl.pallas_call(
        _s5_prompt_body, grid=(t // S5_CHUNK,),
        in_specs=[
            pl.BlockSpec((S5_CHUNK, width), lambda i: (i, u_col_block)),
            pl.BlockSpec(a_re.shape, c2), pl.BlockSpec(a_im.shape, c2), pl.BlockSpec(ldt.shape, c2),
            pl.BlockSpec(b_re.shape, c3), pl.BlockSpec(b_im.shape, c3),
            pl.BlockSpec(c_re.shape, c3), pl.BlockSpec(c_im.shape, c3),
            pl.BlockSpec(dd.shape, c2),
        ],
        out_specs=[
            pl.BlockSpec((S5_CHUNK, width), lambda i: (i, 0)),
            pl.BlockSpec((2, D_ROWS, D_ROW_STATE), c3),
        ],
        out_shape=[
            jax.ShapeDtypeStruct((t, width), F32),
            jax.ShapeDtypeStruct((2, D_ROWS, D_ROW_STATE), F32),
        ],
        scratch_shapes=[
            pltpu.VMEM((D_ROW_STATE // LANES, blk, LANES), F32),
            pltpu.VMEM((D_ROW_STATE // LANES, blk, LANES), F32),
            pltpu.VMEM((2, D_ROWS, D_ROW_STATE), F32), pltpu.VMEM((4, D_ROWS, D_ROW_STATE), F32),
        ],
        compiler_params=_params(("arbitrary",)), name="s5_prompt",
    )(z, a_re, a_im, ldt, b_re.astype(BF16), b_im.astype(BF16), c_re.astype(BF16), c_im.astype(BF16), dd)


def _s5_sample_body(u_ref, xre_ref, xim_ref, are_ref, aim_ref, ldt_ref, bre_ref, bim_ref, cre_ref,
                    cim_ref, dd_ref, y_ref, sre_ref, sim_ref):
    ab_re, ab_im, co_re, co_im = _s5_discretize(are_ref, aim_ref, ldt_ref)
    n_t = u_ref.shape[0]
    for s in range(D_ROWS):
        a_r, a_i = ab_re[s:s + 1, :], ab_im[s:s + 1, :]
        c_r, c_i = co_re[s:s + 1, :], co_im[s:s + 1, :]
        h_re, h_im = xre_ref[s], xim_ref[s]
        for t in range(n_t):
            u = u_ref[t, s]
            raw_re = jnp.dot(u, bre_ref[s], precision=HIGHEST, preferred_element_type=F32)
            raw_im = jnp.dot(u, bim_ref[s], precision=HIGHEST, preferred_element_type=F32)
            bu_re = c_r * raw_re - c_i * raw_im
            bu_im = c_r * raw_im + c_i * raw_re
            h_re, h_im = a_r * h_re - a_i * h_im + bu_re, a_r * h_im + a_i * h_re + bu_im
            y = (jnp.dot(h_re, cre_ref[s], precision=HIGHEST, preferred_element_type=F32)
                 - jnp.dot(h_im, cim_ref[s], precision=HIGHEST, preferred_element_type=F32)
                 + dd_ref[s:s + 1, :] * u)
            y_ref[t, s] = _gelu(y)
        sre_ref[s] = h_re
        sim_ref[s] = h_im


def _s5_sample(u_ts, x_re, x_im, par):
    a_re, a_im, ldt, b_re, b_im, c_re, c_im, dd = par
    return pl.pallas_call(
        _s5_sample_body,
        out_shape=[
            jax.ShapeDtypeStruct(u_ts.shape, F32),
            jax.ShapeDtypeStruct(x_re.shape, F32),
            jax.ShapeDtypeStruct(x_im.shape, F32),
        ],
        compiler_params=pltpu.CompilerParams(vmem_limit_bytes=VMEM_LIMIT), name="s5_sample",
    )(u_ts, x_re, x_im, a_re, a_im, ldt, b_re, b_im, c_re, c_im, dd)


def _s5_params(a_re, a_im, log_dt, b_re, b_im, c_re, c_im, dd):
    eye = jnp.eye(D_ROWS, dtype=F32)

    def rows(a):
        return a.reshape(D_ROWS, D_ROW_STATE)

    def b_blocks(b):
        b4 = b.reshape(D_ROWS, D_ROWS, D_STATE, D_GROUP_CH).transpose(0, 1, 3, 2)
        return jnp.einsum('sgcp,gh->sgchp', b4, eye).reshape(D_ROWS, D_ROW_CH, D_ROW_STATE)

    def c_blocks(c):
        c4 = c.reshape(D_ROWS, D_ROWS, D_GROUP_CH, D_STATE)
        return jnp.einsum('sgcp,gh->sgphc', c4, eye).reshape(D_ROWS, D_ROW_STATE, D_ROW_CH)

    ldt = jnp.broadcast_to(log_dt[:, None], (D_GROUPS, D_STATE))
    return (rows(a_re), rows(a_im), rows(ldt), b_blocks(b_re), b_blocks(b_im),
            c_blocks(c_re), c_blocks(c_im), dd.reshape(D_ROWS, D_ROW_CH))


def _glu_body(x_ref, w_ref, b_ref, o_ref):
    x = x_ref[...]
    gate = jnp.dot(x.astype(BF16), w_ref[...], preferred_element_type=F32) + b_ref[...]
    o_ref[...] = x * jax.nn.sigmoid(gate)


def _glu(x, w, b, tm=512):
    m, k = x.shape
    return pl.pallas_call(
        _glu_body, grid=(m // tm,),
        in_specs=[pl.BlockSpec((tm, k), lambda i: (i, 0)), pl.BlockSpec((k, k), lambda i: (0, 0)),
                  pl.BlockSpec((1, k), lambda i: (0, 0))],
        out_specs=pl.BlockSpec((tm, k), lambda i: (i, 0)),
        out_shape=jax.ShapeDtypeStruct((m, k), F32),
        compiler_params=_params(("parallel",)), name="glu",
    )(x, w, b.reshape(1, k))


PEER_SEL_TOKENS = 256


def _extract_top(s, n_out, exact, want_rank=False):
    n_rows = s.shape[0]
    rid = lax.broadcasted_iota(jnp.int32, s.shape, 0).astype(F32) if exact else None
    rank = jnp.full(s.shape, float(n_out), F32) if want_rank else None
    vals = []
    for r in range(n_out):
        m = jnp.max(s, axis=0, keepdims=True)
        if exact:
            first = jnp.min(jnp.where(s == m, rid, float(n_rows)), axis=0, keepdims=True)
            hit = rid == first
        else:
            hit = s == m
        if want_rank:
            rank = jnp.where(hit, float(r), rank)
        s = jnp.where(hit, NEG_INF, s)
        vals.append(m)
    return vals, s, rank


def _removed_count(s):
    return jnp.sum(jnp.where(s == NEG_INF, 1.0, 0.0), axis=0, keepdims=True)


_CAND_COUNTS = [PEER_TOPK // (a + 1) for a in range(PEER_TOPK)]


def _candidates(v1, v2):
    v1_all = _stack_rows(v1)
    v2_all = _stack_rows(v2)
    rid = lax.broadcasted_iota(jnp.int32, (SUBLANES, v1[0].shape[1]), 0)
    tiles = [v1[0] + v2_all]
    n_pad = 0
    a = 1
    while _CAND_COUNTS[a] > 1:
        tiles.append(jnp.where(rid < _CAND_COUNTS[a], v1[a] + v2_all[:SUBLANES], NEG_INF))
        n_pad += SUBLANES - _CAND_COUNTS[a]
        a += 1
    assert PEER_TOPK - a == SUBLANES
    tiles.append(v1_all[a:] + v2[0])
    return jnp.concatenate(tiles, axis=0), n_pad


def _stack_rows(rows):
    n = len(rows)
    rid = lax.broadcasted_iota(jnp.int32, (n, rows[0].shape[1]), 0)
    out = jnp.broadcast_to(rows[n - 1], rid.shape)
    for i in range(n - 2, -1, -1):
        out = jnp.where(rid == i, rows[i], out)
    return out


def _selected_per_first_key(removed, tag, tags):
    counts = [jnp.sum(removed[:PEER_TOPK], axis=0, keepdims=True)]
    row = PEER_TOPK
    a = 1
    while _CAND_COUNTS[a] > 1:
        counts.append(jnp.sum(removed[row:row + SUBLANES], axis=0, keepdims=True))
        row += SUBLANES
        a += 1
    for i in range(PEER_TOPK - a):
        counts.append(removed[row + i:row + i + 1])
    n_sel = jnp.zeros(tag.shape, F32)
    for a, cnt in enumerate(counts):
        n_sel = jnp.where(tag == tags[a], cnt, n_sel)
    return n_sel


def _peer_select_body(q_ref, k1_ref, k2_ref, ns_ref, g1_ref, r2_ref, e2_ref):
    def head(h, exact):
        q1 = q_ref[:, (2 * h) * PEER_HALF:(2 * h + 1) * PEER_HALF]
        q2 = q_ref[:, (2 * h + 1) * PEER_HALF:(2 * h + 2) * PEER_HALF]
        s1 = _dot_nt(k1_ref[h], q1, HIGHEST)
        s2 = _dot_nt(k2_ref[h], q2, HIGHEST)
        v1, left1, rank1 = _extract_top(s1, PEER_TOPK, exact, want_rank=exact)
        v2, left2, rank2 = _extract_top(s2, PEER_TOPK, exact, want_rank=True)
        tag1, tags1 = (rank1, [float(a) for a in range(PEER_TOPK)]) if exact else (s1, v1)
        cand, n_pad = _candidates(v1, v2)
        top, left_c, _ = _extract_top(cand, PEER_TOPK, exact)
        mx = top[0]
        z = jnp.exp(top[0] - mx)
        for kk in range(1, PEER_TOPK):
            z = z + jnp.exp(top[kk] - mx)
        g1 = jnp.where(left1 == NEG_INF, jnp.exp(s1 - v1[0]), 0.0) / z
        e2 = jnp.where(rank2 < PEER_TOPK, jnp.exp(s2 - v2[0]), 0.0)
        removed = jnp.where((left_c == NEG_INF) & (cand != NEG_INF), 1.0, 0.0)
        n_sel = _selected_per_first_key(removed, tag1, tags1)
        for j in range(PEER_SEL_TOKENS // LANES):
            lanes = slice(j * LANES, (j + 1) * LANES)
            ns_ref[j, h] = n_sel[:, lanes]
            g1_ref[j, h] = g1[:, lanes]
            r2_ref[j, h] = rank2[:, lanes]
            e2_ref[j, h] = e2[:, lanes]
        if exact:
            return None
        ok = ((_removed_count(left1) == PEER_TOPK) & (_removed_count(left2) == PEER_TOPK)
              & (_removed_count(left_c) == PEER_TOPK + n_pad))
        return jnp.where(ok, 0.0, 1.0)

    repeated = [jnp.max(head(h, False)) for h in range(PEER_HEADS)]
    for h in range(PEER_HEADS):
        @pl.when(repeated[h] > 0.0)
        def _():
            head(h, True)


def _peer_select(q, k1, k2):
    t = q.shape[0]
    tb = PEER_SEL_TOKENS
    nj = tb // LANES
    big = pl.BlockSpec((nj, PEER_HEADS, PEER_NKEYS, LANES), lambda i: (i, 0, 0, 0))
    shape = jax.ShapeDtypeStruct((t // LANES, PEER_HEADS, PEER_NKEYS, LANES), F32)
    return pl.pallas_call(
        _peer_select_body, grid=(t // tb,),
        in_specs=[
            pl.BlockSpec((tb, q.shape[1]), lambda i: (i, 0)),
            pl.BlockSpec(k1.shape, lambda i: (0, 0, 0)),
            pl.BlockSpec(k2.shape, lambda i: (0, 0, 0)),
        ],
        out_specs=[big, big, big, big],
        out_shape=[shape, shape, shape, shape],
        compiler_params=_params(("parallel",)), name="peer_select",
    )(q, k1, k2)


PEER_TOKENS = 512
PEER_EXPERT_TILE = 512
BF16_ROWS = 16
GATE_GROUP = 4


def _peer_dense_body(h_ref, g_ref, u_ref, v_ref, ns_ref, g1_ref, r2_ref, e2_ref, og_ref, o_ref, *rest,
                     norm_out, emit_bf16):
    if emit_bf16:
        ub_ref, vb_ref, xn, act_s, coef_s, acc = rest
    else:
        xn, act_s, coef_s, acc = rest
    e = pl.program_id(1)
    n_chunks = PEER_TOKENS // LANES

    @pl.when(e == 0)
    def _():
        xn[...] = _rmsnorm(h_ref[...], g_ref[...]).astype(BF16)
        acc[...] = jnp.zeros_like(acc)

    te = PEER_EXPERT_TILE
    slabs = te // PEER_NKEYS

    def one_tile(part):
        tile_rows = slice(part * te, (part + 1) * te)
        u_tile = u_ref[tile_rows, :].astype(BF16)
        if emit_bf16:
            ub_ref[tile_rows, :] = u_tile
        act = _dot_nt(u_tile, xn[...])
        for j in range(n_chunks):
            act_s[j] = act[:, j * LANES:(j + 1) * LANES]

        def chunk(j, carry):
            packed = (PEER_NKEYS // BF16_ROWS, BF16_ROWS, LANES)
            for c0 in range(0, slabs, GATE_GROUP):
                gates = [jnp.zeros(packed, BF16) for _ in range(GATE_GROUP)]
                for h in range(PEER_HEADS):
                    r2 = r2_ref[j, h].reshape(packed).astype(BF16)
                    e2 = e2_ref[j, h].reshape(packed).astype(BF16)
                    for k in range(GATE_GROUP):
                        key1 = part * slabs + c0 + k
                        ns = jnp.broadcast_to(ns_ref[j, h, key1:key1 + 1, :], packed[1:]).astype(BF16)
                        g1 = jnp.broadcast_to(g1_ref[j, h, key1:key1 + 1, :], packed[1:]).astype(BF16)
                        gates[k] = gates[k] + jnp.where(r2 < ns[None], e2 * g1[None], jnp.zeros((), BF16))
                for k in range(GATE_GROUP):
                    rows = slice((c0 + k) * PEER_NKEYS, (c0 + k + 1) * PEER_NKEYS)
                    act = _gelu(act_s[j, rows, :]).reshape(packed).astype(BF16)
                    coef_s[j, rows, :] = (gates[k] * act).reshape(PEER_NKEYS, LANES)
            return carry

        for j in range(n_chunks):
            chunk(j, 0)
        coef = jnp.concatenate([coef_s[j] for j in range(n_chunks)], axis=1)
        v_tile = v_ref[tile_rows, :].astype(BF16)
        if emit_bf16:
            vb_ref[tile_rows, :] = v_tile
        acc[...] += _dot_tn(coef, v_tile)

    for part in range(u_ref.shape[0] // te):
        one_tile(part)

    @pl.when(e == pl.num_programs(1) - 1)
    def _():
        out = h_ref[...] + acc[...]
        o_ref[...] = _rmsnorm(out, og_ref[...]) if norm_out else out


def _peer_dense(h, g, u_tab, v_tab, layer, sel, out_g, norm_out):
    ns, g1, r2, e2 = sel
    t, d = h.shape
    emit_bf16 = u_tab.ndim == 3
    n_exp = u_tab.shape[-2]
    tb = PEER_TOKENS
    nj = tb // LANES
    slabs = PEER_EXPERT_TILE // PEER_NKEYS
    o_spec = pl.BlockSpec((tb, d), lambda i, e: (i, 0))
    o_shape = jax.ShapeDtypeStruct((t, d), F32)
    if emit_bf16:
        assert t == tb, "every table tile must be visited exactly once when the casts are emitted"
        te = PEER_EXPERT_TILE
        n_steps = n_exp // te
        tab_spec = pl.BlockSpec((None, te, d), lambda i, e: (layer, e, 0))

        def by_tile(a):
            a5 = a.reshape(a.shape[0], PEER_HEADS, n_steps, slabs, LANES)
            return a5.transpose(2, 0, 1, 3, 4)

        ns, g1 = by_tile(ns), by_tile(g1)
        key_spec = pl.BlockSpec((None, nj, PEER_HEADS, slabs, LANES), lambda i, e: (e, i, 0, 0, 0))
    else:
        te = SUBLANES // slabs * PEER_EXPERT_TILE
        n_steps = n_exp // te
        tab_spec = pl.BlockSpec((te, d), lambda i, e: (e, 0))
        key_spec = pl.BlockSpec((nj, PEER_HEADS, SUBLANES, LANES), lambda i, e: (i, 0, e, 0))
    cast_spec = pl.BlockSpec((te, d), lambda i, e: (e, 0))
    cast_shape = jax.ShapeDtypeStruct((n_exp, d), BF16)
    tok_spec = pl.BlockSpec((nj, PEER_HEADS, PEER_NKEYS, LANES), lambda i, e: (i, 0, 0, 0))
    te_scratch = PEER_EXPERT_TILE
    return pl.pallas_call(
        functools.partial(_peer_dense_body, norm_out=norm_out, emit_bf16=emit_bf16),
        grid=(t // tb, n_steps),
        in_specs=[
            pl.BlockSpec((tb, d), lambda i, e: (i, 0)),
            pl.BlockSpec((1, d), lambda i, e: (0, 0)),
            tab_spec, tab_spec,
            key_spec, key_spec, tok_spec, tok_spec,
            pl.BlockSpec((1, d), lambda i, e: (0, 0)),
        ],
        out_specs=[o_spec, cast_spec, cast_spec] if emit_bf16 else o_spec,
        out_shape=[o_shape, cast_shape, cast_shape] if emit_bf16 else o_shape,
        scratch_shapes=[pltpu.VMEM((tb, d), BF16), pltpu.VMEM((nj, te_scratch, LANES), F32),
                        pltpu.VMEM((nj, te_scratch, LANES), BF16), pltpu.VMEM((tb, d), F32)],
        compiler_params=_params(("parallel", "arbitrary")), name="peer_dense",
    )(h, g.reshape(1, d), u_tab, v_tab, ns, g1, r2, e2, out_g.reshape(1, d))


def _peer(h, g, wq, k1, k2, u_tab, v_tab, layer, out_g, norm_out):
    q = _matmul(h, wq, norm_g=g)
    return _peer_dense(h, g, u_tab, v_tab, layer, _peer_select(q, k1, k2), out_g, norm_out)


def kernel(x_prompt, x_sample, state_b_k, state_b_v, state_c_s, state_d_re, state_d_im, norm1_g, norm2_g, final_g, w_in_even, w_out_even, a_ws, a_bs, b_sink, w_in_odd, w_out_odd, c_norm_g, d_a_re, d_a_im, d_log_dt, d_b_re, d_b_im, d_c_re, d_c_im, d_d, d_glu_w, d_glu_b, peer_wq, peer_k1, peer_k2, peer_u, peer_v):
    seq = x_prompt.shape[1]
    n_batch, n_new = x_sample.shape[:2]
    past = PAST_LEN
    hp = x_prompt.reshape(seq, D_MODEL)
    hs = x_sample.reshape(n_batch * n_new, D_MODEL)
    bf = lambda a: a.astype(BF16)

    def pad_tokens(a):
        a3 = a.reshape(n_batch, n_new, a.shape[-1])
        return jnp.pad(a3, ((0, 0), (0, T_PAD - n_new), (0, 0)))

    def unpad_tokens(a3):
        return a3[:, :n_new].reshape(n_batch * n_new, a3.shape[-1])

    w_in = bf(w_in_even[0])
    w_out = bf(w_out_even[0])
    zp = _matmul(hp, w_in, norm_g=norm1_g[0])
    zs = _matmul(hs, w_in, norm_g=norm1_g[0])
    bs_full = jnp.broadcast_to(a_bs[0][:, :, None], (A_GROUPS, CHUNK, LANES))
    yp = _even_prompt(zp, a_ws[0], bs_full, b_sink[0])
    ws_small = jnp.tril(a_ws[0][:, :n_new, :n_new]).transpose(2, 1, 0)
    wa = jnp.repeat(jnp.pad(ws_small, ((0, 0), (0, T_PAD - n_new), (0, 0))), LANES, axis=-1)
    wb = jnp.repeat(jnp.pad(a_bs[0][:, :n_new].T, ((0, T_PAD - n_new), (0, 0))), LANES, axis=-1)
    win = state_b_k.shape[2]
    kbuf = state_b_k[0].reshape(n_batch, win, B_KV_HEADS * B_DH)
    vbuf = state_b_v[0].reshape(n_batch, win, B_KV_HEADS * B_DH)
    ys3, av3 = _even_sample(pad_tokens(zs), kbuf, vbuf, wa, wb, b_sink[0])
    hp = _matmul(yp, w_out, resid=hp)
    hs = _matmul(unpad_tokens(ys3), w_out, resid=hs)

    k_off = 2 * A_WIDTH + B_HEADS * B_DH
    v_off = k_off + B_KV_HEADS * B_DH
    kv_shape = (1, -1, win, B_KV_HEADS, B_DH)
    a_v_sample = av3[:, :n_new].reshape(1, n_batch, n_new, A_GROUPS, A_WIDTH // A_GROUPS)
    b_k_prompt = zp[seq - win:, k_off:v_off].reshape(kv_shape)
    b_v_prompt = zp[seq - win:, v_off:].reshape(kv_shape)
    k_new = zs[:, k_off:v_off].reshape(n_batch, n_new, B_KV_HEADS * B_DH)
    v_new = zs[:, v_off:].reshape(n_batch, n_new, B_KV_HEADS * B_DH)
    b_k_sample = jnp.concatenate([kbuf, k_new], axis=1)[:, -win:].reshape(kv_shape)
    b_v_sample = jnp.concatenate([vbuf, v_new], axis=1)[:, -win:].reshape(kv_shape)

    wq = bf(peer_wq[0])
    hs, u_b, v_b = _peer(hs, norm2_g[0], wq, peer_k1[0], peer_k2[0], peer_u, peer_v, 0, final_g, False)
    hp = _peer(hp, norm2_g[0], wq, peer_k1[0], peer_k2[0], u_b, v_b, 0, final_g, False)

    w_in = bf(w_in_odd[0])
    w_out = bf(w_out_odd[0])
    c_width = C_HEADS * C_DV
    zp = _matmul(hp, w_in, norm_g=norm1_g[1])
    zs = _matmul(hs, w_in, norm_g=norm1_g[1])
    cos_p, sin_p = _rope_tables(jnp.arange(seq))
    cos_s, sin_s = _rope_tables(past + jnp.arange(T_PAD))
    ycp, c_s_prompt = _retention_prompt(zp, cos_p, sin_p, _retention_tables(CHUNK, CHUNK), c_norm_g[0])
    zs3 = pad_tokens(zs)
    ycs3, c_s_sample = _retention_sample(zs3, cos_s, sin_s, _retention_tables(T_PAD, n_new), c_norm_g[0],
                                         state_c_s[0])
    par = _s5_params(d_a_re[0], d_a_im[0], d_log_dt[0], d_b_re[0], d_b_im[0], d_c_re[0], d_c_im[0], d_d[0])
    u_off = 3 * c_width
    ydp_pre, d_prompt = _s5_prompt(zp, u_off // c_width, par)
    us_ts = zs[:, u_off:].reshape(n_batch, n_new, D_ROWS, D_ROW_CH).transpose(1, 2, 0, 3)
    x_re = state_d_re[0].reshape(n_batch, D_ROWS, D_ROW_STATE).transpose(1, 0, 2)
    x_im = state_d_im[0].reshape(n_batch, D_ROWS, D_ROW_STATE).transpose(1, 0, 2)
    yds_ts, s_re, s_im = _s5_sample(us_ts, x_re, x_im, par)
    glu_w = bf(d_glu_w[0])
    ydp = _glu(ydp_pre, glu_w, d_glu_b[0])
    yds = _glu(yds_ts.transpose(2, 0, 1, 3).reshape(n_batch * n_new, c_width), glu_w, d_glu_b[0])
    hp = _matmul(ycp, w_out[:c_width], x2=ydp, w2=w_out[c_width:], resid=hp)
    hs = _matmul(unpad_tokens(ycs3), w_out[:c_width], x2=yds, w2=w_out[c_width:], resid=hs)

    wq = bf(peer_wq[1])
    y_sample, u_b, v_b = _peer(hs, norm2_g[1], wq, peer_k1[1], peer_k2[1], peer_u, peer_v, 1, final_g, True)
    y_prompt = _peer(hp, norm2_g[1], wq, peer_k1[1], peer_k2[1], u_b, v_b, 1, final_g, True)
    y_prompt = y_prompt.reshape(x_prompt.shape)
    y_sample = y_sample.reshape(x_sample.shape)

    d_shape = (1, -1, D_GROUPS, D_STATE)
    return (y_prompt, y_sample, a_v_sample, b_k_prompt, b_v_prompt, b_k_sample, b_v_sample,
            c_s_prompt.reshape(1, 1, C_HEADS, C_DK, C_DV), c_s_sample[None],
            d_prompt[0].reshape(d_shape), d_prompt[1].reshape(d_shape),
            s_re.transpose(1, 0, 2).reshape(d_shape), s_im.transpose(1, 0, 2).reshape(d_shape))
```

```python
import functools
import math

import jax
import jax.numpy as jnp
from jax import lax
from jax.experimental import pallas as pl
from jax.experimental.pallas import tpu as pltpu

F32 = jnp.float32
BF16 = jnp.bfloat16
HIGHEST = lax.Precision.HIGHEST

EPS = 1e-6
NEG_BIG = -1e30
NEG_INF = float("-inf")

D_MODEL = 2048
PAST_LEN = 8192
LANES = 128
SUBLANES = 8
VMEM_LIMIT = 56 * 1024 * 1024

CHUNK = 128
A_GROUPS = 8
A_WIDTH = 1024
B_HEADS = 16
B_KV_HEADS = 4
B_GQA = 4
B_DH = 64
C_HEADS = 8
C_DK = 64
C_DV = 128
RET_DECAY_EXP0 = 5.0
ROPE_BASE = 10000.0
D_GROUPS = 64
D_STATE = 64
D_GROUP_CH = 16
D_ROWS = 8
D_ROW_STATE = 512
D_ROW_CH = 128
PEER_HEADS = 8
PEER_NKEYS = 128
PEER_TOPK = 16
PEER_HALF = 128


def _params(semantics):
    return pltpu.CompilerParams(dimension_semantics=semantics, vmem_limit_bytes=VMEM_LIMIT)


def _gelu(x):
    return 0.5 * x * (1.0 + lax.erf(x * (1.0 / math.sqrt(2.0))))


def _rmsnorm(x, g):
    return x * lax.rsqrt(jnp.mean(x * x, axis=-1, keepdims=True) + EPS) * g


def _dot_nt(a, b, precision=None):
    return lax.dot_general(a, b, (((1,), (1,)), ((), ())), precision=precision,
                           preferred_element_type=F32)


def _dot_tn(a, b, precision=None):
    return lax.dot_general(a, b, (((0,), (0,)), ((), ())), precision=precision,
                           preferred_element_type=F32)


def _mm_body(*refs, has_norm, has_pair, has_resid):
    it = iter(refs)
    x_ref, w_ref = next(it), next(it)
    g_ref = next(it) if has_norm else None
    x2_ref, w2_ref = (next(it), next(it)) if has_pair else (None, None)
    r_ref = next(it) if has_resid else None
    o_ref = next(it)
    x = x_ref[...]
    if has_norm:
        x = _rmsnorm(x, g_ref[...])
    xb = x.astype(BF16)
    x2b = x2_ref[...].astype(BF16) if has_pair else None
    n = o_ref.shape[1]
    for j in range(n // MM_COLS):
        cols = slice(j * MM_COLS, (j + 1) * MM_COLS)
        acc = jnp.dot(xb, w_ref[:, cols], preferred_element_type=F32)
        if has_pair:
            acc = acc + jnp.dot(x2b, w2_ref[:, cols], preferred_element_type=F32)
        if has_resid:
            acc = acc + r_ref[:, cols]
        o_ref[:, cols] = acc


MM_COLS = 512


def _matmul(x, w, *, norm_g=None, x2=None, w2=None, resid=None, tm=512):
    m, k = x.shape
    n = w.shape[1]
    assert m % tm == 0 and n % MM_COLS == 0
    resident = pl.Buffered(1)
    args = [x, w]
    specs = [pl.BlockSpec((tm, k), lambda i: (i, 0)),
             pl.BlockSpec((k, n), lambda i: (0, 0), pipeline_mode=resident)]
    if norm_g is not None:
        args.append(norm_g.reshape(1, k))
        specs.append(pl.BlockSpec((1, k), lambda i: (0, 0)))
    if x2 is not None:
        k2 = x2.shape[1]
        args += [x2, w2]
        specs += [pl.BlockSpec((tm, k2), lambda i: (i, 0)),
                  pl.BlockSpec((k2, n), lambda i: (0, 0), pipeline_mode=resident)]
    if resid is not None:
        args.append(resid)
        specs.append(pl.BlockSpec((tm, n), lambda i: (i, 0)))
    body = functools.partial(_mm_body, has_norm=norm_g is not None, has_pair=x2 is not None,
                             has_resid=resid is not None)
    return pl.pallas_call(
        body, grid=(m // tm,), in_specs=specs,
        out_specs=pl.BlockSpec((tm, n), lambda i: (i, 0)),
        out_shape=jax.ShapeDtypeStruct((m, n), F32),
        compiler_params=_params(("parallel",)), name="matmul",
    )(*args)


def _sink_column(sink_ref, kvh, rows_per_head, n_rows):
    grp = lax.broadcasted_iota(jnp.int32, (n_rows, 1), 0) // rows_per_head
    sk = jnp.full((n_rows, 1), sink_ref[kvh * B_GQA + B_GQA - 1], F32)
    for g in range(B_GQA - 2, -1, -1):
        sk = jnp.where(grp == g, sink_ref[kvh * B_GQA + g], sk)
    return sk


def _even_prompt_body(sink_ref, au_ref, av_ref, q_ref, kvc_ref, kvp_ref, ws_ref, bs_ref, o_ref):
    blk = pl.program_id(0)
    au = _gelu(au_ref[...])
    av = _gelu(av_ref[...])
    row = lax.broadcasted_iota(jnp.int32, (CHUNK, CHUNK), 0)
    col = lax.broadcasted_iota(jnp.int32, (CHUNK, CHUNK), 1)
    causal = row >= col
    for g in range(A_GROUPS):
        lanes = slice(g * LANES, (g + 1) * LANES)
        w = jnp.where(causal, ws_ref[g], 0.0).astype(BF16)
        mixed = jnp.dot(w, av[:, lanes].astype(BF16), preferred_element_type=F32) + bs_ref[g]
        o_ref[:, lanes] = au[:, lanes] * mixed

    q = q_ref[...]
    kvc = kvc_ref[...]
    kvp = kvp_ref[...]
    n_rows = B_GQA * CHUNK
    qi = lax.broadcasted_iota(jnp.int32, (n_rows, 2 * CHUNK), 0) % CHUNK
    kc = lax.broadcasted_iota(jnp.int32, (n_rows, 2 * CHUNK), 1)
    dist = qi + CHUNK - kc
    allowed = (dist >= 0) & (dist < CHUNK) & ((kc >= CHUNK) | (blk > 0))
    outs = []
    for kvh in range(B_KV_HEADS):
        ks = slice(kvh * B_DH, (kvh + 1) * B_DH)
        vs = slice(B_KV_HEADS * B_DH + kvh * B_DH, B_KV_HEADS * B_DH + (kvh + 1) * B_DH)
        kk = jnp.concatenate([kvp[:, ks], kvc[:, ks]], axis=0).astype(BF16)
        vv = jnp.concatenate([kvp[:, vs], kvc[:, vs]], axis=0).astype(BF16)
        q4 = jnp.concatenate(
            [q[:, (kvh * B_GQA + g) * B_DH:(kvh * B_GQA + g + 1) * B_DH] for g in range(B_GQA)], axis=0)
        s = _dot_nt(q4.astype(BF16), kk) * (B_DH ** -0.5)
        s = jnp.where(allowed, s, NEG_BIG)
        sk = _sink_column(sink_ref, kvh, CHUNK, n_rows)
        mx = jnp.maximum(jnp.max(s, axis=-1, keepdims=True), sk)
        p = jnp.exp(s - mx)
        p = p / (jnp.sum(p, axis=-1, keepdims=True) + jnp.exp(sk - mx))
        o = jnp.dot(p.astype(BF16), vv, preferred_element_type=F32)
        outs += [o[g * CHUNK:(g + 1) * CHUNK] for g in range(B_GQA)]
    o_ref[:, A_WIDTH:] = jnp.concatenate(outs, axis=1)


def _even_prompt(z, ws, bs_full, sink):
    t = z.shape[0]
    nb = t // CHUNK
    wide = A_WIDTH
    kvw = 2 * B_KV_HEADS * B_DH
    kv_blk = (2 * A_WIDTH + B_HEADS * B_DH) // kvw
    return pl.pallas_call(
        _even_prompt_body, grid=(nb,),
        in_specs=[
            pl.BlockSpec(memory_space=pltpu.SMEM),
            pl.BlockSpec((CHUNK, wide), lambda i: (i, 0)),
            pl.BlockSpec((CHUNK, wide), lambda i: (i, 1)),
            pl.BlockSpec((CHUNK, wide), lambda i: (i, 2)),
            pl.BlockSpec((CHUNK, kvw), lambda i: (i, kv_blk)),
            pl.BlockSpec((CHUNK, kvw), lambda i: (jnp.maximum(i - 1, 0), kv_blk)),
            pl.BlockSpec((A_GROUPS, CHUNK, CHUNK), lambda i: (0, 0, 0)),
            pl.BlockSpec((A_GROUPS, CHUNK, LANES), lambda i: (0, 0, 0)),
        ],
        out_specs=pl.BlockSpec((CHUNK, D_MODEL), lambda i: (i, 0)),
        out_shape=jax.ShapeDtypeStruct((t, D_MODEL), F32),
        compiler_params=_params(("parallel",)), name="even_prompt",
    )(sink, z, z, z, z, z, ws, bs_full)


EVEN_SAMPLE_BATCH = 8
T_PAD = 8


def _even_sample_body(sink_ref, au_ref, av_ref, q_ref, kv_ref, kb_ref, vb_ref, wa_ref, wb_ref,
                      y_ref, avo_ref):
    kv_width = B_KV_HEADS * B_DH
    n_rows = B_HEADS * T_PAD
    tq = lax.broadcasted_iota(jnp.int32, (n_rows, CHUNK), 0) % T_PAD
    kc = lax.broadcasted_iota(jnp.int32, (n_rows, CHUNK), 1)
    buf_allowed = kc > tq
    tq1 = lax.broadcasted_iota(jnp.int32, (n_rows, 1), 0) % T_PAD
    head_of_row = lax.broadcasted_iota(jnp.int32, (n_rows, 1), 0) // T_PAD
    sk = jnp.full((n_rows, 1), sink_ref[B_HEADS - 1], F32)
    for hd in range(B_HEADS - 2, -1, -1):
        sk = jnp.where(head_of_row == hd, sink_ref[hd], sk)
    lane_kvh = lax.broadcasted_iota(jnp.int32, (T_PAD, kv_width), 1) // B_DH
    scale = B_DH ** -0.5
    n_new = wa_ref.shape[0]
    for b in range(EVEN_SAMPLE_BATCH):
        au = _gelu(au_ref[b])
        av = _gelu(av_ref[b])
        avo_ref[b] = av
        mixed = wb_ref[...]
        for j in range(n_new):
            mixed = mixed + wa_ref[j] * av[j:j + 1, :]
        y_ref[b, :, :A_WIDTH] = au * mixed

        q = q_ref[b]
        kv = kv_ref[b]
        blocks = []
        for kvh in range(B_KV_HEADS):
            for g in range(B_GQA):
                col = (kvh * B_GQA + g) * B_DH
                qg = jnp.concatenate([q[:, col:col + B_DH]] * B_KV_HEADS, axis=1)
                blocks.append(jnp.where(lane_kvh == kvh, qg, 0.0))
        qbd = jnp.concatenate(blocks, axis=0).astype(BF16)
        s_buf = _dot_nt(qbd, kb_ref[b].astype(BF16)) * scale
        s_buf = jnp.where(buf_allowed, s_buf, NEG_BIG)
        qbd_r = qbd.astype(F32)
        s_new = []
        for j in range(n_new):
            kj = kv[j:j + 1, :kv_width].astype(BF16).astype(F32)
            sj = jnp.sum(qbd_r * kj, axis=-1, keepdims=True) * scale
            s_new.append(jnp.where(tq1 >= j, sj, NEG_BIG))
        mx = jnp.maximum(jnp.max(s_buf, axis=-1, keepdims=True), sk)
        for sj in s_new:
            mx = jnp.maximum(mx, sj)
        p_buf = jnp.exp(s_buf - mx)
        p_new = [jnp.exp(sj - mx) for sj in s_new]
        den = jnp.sum(p_buf, axis=-1, keepdims=True) + jnp.exp(sk - mx)
        for pj in p_new:
            den = den + pj
        inv = 1.0 / den
        o = jnp.dot((p_buf * inv).astype(BF16), vb_ref[b].astype(BF16), preferred_element_type=F32)
        for j in range(n_new):
            vj = kv[j:j + 1, kv_width:].astype(BF16).astype(F32)
            o = o + (p_new[j] * inv).astype(BF16).astype(F32) * vj
        outs = []
        for kvh in range(B_KV_HEADS):
            for g in range(B_GQA):
                row = (kvh * B_GQA + g) * T_PAD
                outs.append(o[row:row + T_PAD, kvh * B_DH:(kvh + 1) * B_DH])
        y_ref[b, :, A_WIDTH:] = jnp.concatenate(outs, axis=1)


def _even_sample(z3, kbuf, vbuf, wa, wb, sink):
    nb = z3.shape[0]
    bb = EVEN_SAMPLE_BATCH
    kvw = 2 * B_KV_HEADS * B_DH
    kv_blk = (2 * A_WIDTH + B_HEADS * B_DH) // kvw
    win = kbuf.shape[1]
    return pl.pallas_call(
        _even_sample_body, grid=(nb // bb,),
        in_specs=[
            pl.BlockSpec(memory_space=pltpu.SMEM),
            pl.BlockSpec((bb, T_PAD, A_WIDTH), lambda i: (i, 0, 0)),
            pl.BlockSpec((bb, T_PAD, A_WIDTH), lambda i: (i, 0, 1)),
            pl.BlockSpec((bb, T_PAD, A_WIDTH), lambda i: (i, 0, 2)),
            pl.BlockSpec((bb, T_PAD, kvw), lambda i: (i, 0, kv_blk)),
            pl.BlockSpec((bb, win, kvw // 2), lambda i: (i, 0, 0)),
            pl.BlockSpec((bb, win, kvw // 2), lambda i: (i, 0, 0)),
            pl.BlockSpec(wa.shape, lambda i: (0, 0, 0)),
            pl.BlockSpec(wb.shape, lambda i: (0, 0)),
        ],
        out_specs=[
            pl.BlockSpec((bb, T_PAD, D_MODEL), lambda i: (i, 0, 0)),
            pl.BlockSpec((bb, T_PAD, A_WIDTH), lambda i: (i, 0, 0)),
        ],
        out_shape=[
            jax.ShapeDtypeStruct((nb, T_PAD, D_MODEL), F32),
            jax.ShapeDtypeStruct((nb, T_PAD, A_WIDTH), F32),
        ],
        compiler_params=_params(("parallel",)), name="even_sample",
    )(sink, z3, z3, z3, z3, kbuf, vbuf, wa, wb)


def _rope(x, cos_f, sin_s):
    width = x.shape[-1]
    half = C_DK // 2
    lane = lax.broadcasted_iota(jnp.int32, x.shape, 1) % C_DK
    swapped = jnp.where(lane < half, pltpu.roll(x, width - half, 1), pltpu.roll(x, half, 1))
    return x * cos_f + swapped * sin_s


def _retention_chunk(qh, kh, vh, st, decay, qdec, kdec, sdec):
    qb, vb = qh.astype(BF16), vh.astype(BF16)
    scores = _dot_nt(qb, kh.astype(BF16)) * decay
    o = jnp.dot(scores.astype(BF16), vb, preferred_element_type=F32)
    o = o + jnp.dot(qb, st.astype(BF16), preferred_element_type=F32) * qdec
    new_st = sdec * st + _dot_tn((kh * kdec).astype(BF16), vb)
    return o, new_st


def _groupnorm_gate(o, gain, gate):
    mu = jnp.mean(o, axis=-1, keepdims=True)
    var = jnp.mean(jnp.square(o - mu), axis=-1, keepdims=True)
    return (o - mu) * lax.rsqrt(var + EPS) * gain * (gate * jax.nn.sigmoid(gate))


def _retention_body(qk_ref, v_ref, g_ref, cos_ref, sin_ref, decay_ref, qdec_ref, kdec_ref, sdec_ref,
                    gain_ref, o_ref, st_ref, state):
    @pl.when(pl.program_id(0) == 0)
    def _():
        state[...] = jnp.zeros_like(state)

    qk = qk_ref[...]
    width = C_HEADS * C_DK
    q = _rope(qk[:, :width], cos_ref[...], sin_ref[...]) * (C_DK ** -0.5)
    k = _rope(qk[:, width:], cos_ref[...], sin_ref[...])
    v = v_ref[...]
    g = g_ref[...]
    for h in range(C_HEADS):
        qh = q[:, h * C_DK:(h + 1) * C_DK]
        kh = k[:, h * C_DK:(h + 1) * C_DK]
        vh = v[:, h * C_DV:(h + 1) * C_DV]
        st = state[h]
        o, new_st = _retention_chunk(qh, kh, vh, st, decay_ref[h], qdec_ref[h], kdec_ref[h], sdec_ref[h])
        state[h] = new_st
        st_ref[h] = new_st
        lanes = slice(h * C_DV, (h + 1) * C_DV)
        o_ref[:, lanes] = _groupnorm_gate(o, gain_ref[:, lanes], g[:, lanes])


def _retention_prompt(z, cos_f, sin_s, tabs, gain):
    t = z.shape[0]
    nc = t // CHUNK
    decay, qdec, kdec, sdec = tabs
    width = C_HEADS * C_DV
    const3 = lambda i: (0, 0, 0)
    return pl.pallas_call(
        _retention_body, grid=(nc,),
        in_specs=[
            pl.BlockSpec((CHUNK, width), lambda i: (i, 0)),
            pl.BlockSpec((CHUNK, width), lambda i: (i, 1)),
            pl.BlockSpec((CHUNK, width), lambda i: (i, 2)),
            pl.BlockSpec((CHUNK, C_HEADS * C_DK), lambda i: (i, 0)),
            pl.BlockSpec((CHUNK, C_HEADS * C_DK), lambda i: (i, 0)),
            pl.BlockSpec(decay.shape, const3),
            pl.BlockSpec(qdec.shape, const3),
            pl.BlockSpec(kdec.shape, const3),
            pl.BlockSpec(sdec.shape, const3),
            pl.BlockSpec((1, width), lambda i: (0, 0)),
        ],
        out_specs=[
            pl.BlockSpec((CHUNK, width), lambda i: (i, 0)),
            pl.BlockSpec((C_HEADS, C_DK, C_DV), const3),
        ],
        out_shape=[
            jax.ShapeDtypeStruct((t, width), F32),
            jax.ShapeDtypeStruct((C_HEADS, C_DK, C_DV), F32),
        ],
        scratch_shapes=[pltpu.VMEM((C_HEADS, C_DK, C_DV), F32)],
        compiler_params=_params(("arbitrary",)), name="retention_prompt",
    )(z, z, z, cos_f, sin_s, decay, qdec, kdec, sdec, gain.reshape(1, width))


RET_SAMPLE_BATCH = 8


def _retention_sample_body(qk_ref, v_ref, g_ref, cos_ref, sin_ref, decay_ref, qdec_ref, kdec_ref,
                           sdec_ref, gain_ref, st_in_ref, o_ref, st_ref):
    width = C_HEADS * C_DK
    lane_head = lax.broadcasted_iota(jnp.int32, (T_PAD, width), 1) // C_DK
    for b in range(RET_SAMPLE_BATCH):
        qk = qk_ref[b]
        q = _rope(qk[:, :width], cos_ref[...], sin_ref[...]) * (C_DK ** -0.5)
        k = _rope(qk[:, width:], cos_ref[...], sin_ref[...])
        v = v_ref[b]
        g = g_ref[b]
        vb = v.astype(BF16)
        qbd = jnp.concatenate([jnp.where(lane_head == h, q, 0.0) for h in range(C_HEADS)],
                              axis=0).astype(BF16)
        st = st_in_ref[b]
        scores = _dot_nt(qbd, k.astype(BF16)) * decay_ref[...]
        o_intra = jnp.dot(scores.astype(BF16), vb, preferred_element_type=F32)
        o_cross = jnp.dot(qbd, st.astype(BF16), preferred_element_type=F32) * qdec_ref[...]
        kd = (k * kdec_ref[...]).astype(BF16)
        for h in range(C_HEADS):
            rows = slice(h * T_PAD, (h + 1) * T_PAD)
            lanes = slice(h * C_DV, (h + 1) * C_DV)
            keys = slice(h * C_DK, (h + 1) * C_DK)
            o = o_intra[rows, lanes] + o_cross[rows, :]
            o_ref[b, :, lanes] = _groupnorm_gate(o, gain_ref[:, lanes], g[:, lanes])
            st_ref[b, keys, :] = sdec_ref[keys, :] * st[keys, :] + _dot_tn(kd[:, keys], vb[:, lanes])


def _retention_sample(z3, cos_f, sin_s, tabs, gain, st_in):
    nb = z3.shape[0]
    bb = RET_SAMPLE_BATCH
    decay, qdec, kdec, sdec = tabs
    width = C_HEADS * C_DV
    decay_s = decay.reshape(C_HEADS * T_PAD, T_PAD)
    qdec_s = qdec.reshape(C_HEADS * T_PAD, C_DV)
    kdec_s = kdec.transpose(1, 0, 2).reshape(T_PAD, C_HEADS * C_DK)
    sdec_s = sdec.reshape(C_HEADS * C_DK, C_DV)
    st2 = st_in.reshape(nb, C_HEADS * C_DK, C_DV)
    c2 = lambda i: (0, 0)
    out, st_out = pl.pallas_call(
        _retention_sample_body, grid=(nb // bb,),
        in_specs=[
            pl.BlockSpec((bb, T_PAD, width), lambda i: (i, 0, 0)),
            pl.BlockSpec((bb, T_PAD, width), lambda i: (i, 0, 1)),
            pl.BlockSpec((bb, T_PAD, width), lambda i: (i, 0, 2)),
            pl.BlockSpec((T_PAD, C_HEADS * C_DK), c2),
            pl.BlockSpec((T_PAD, C_HEADS * C_DK), c2),
            pl.BlockSpec(decay_s.shape, c2),
            pl.BlockSpec(qdec_s.shape, c2),
            pl.BlockSpec(kdec_s.shape, c2),
            pl.BlockSpec(sdec_s.shape, c2),
            pl.BlockSpec((1, width), c2),
            pl.BlockSpec((bb, C_HEADS * C_DK, C_DV), lambda i: (i, 0, 0)),
        ],
        out_specs=[
            pl.BlockSpec((bb, T_PAD, width), lambda i: (i, 0, 0)),
            pl.BlockSpec((bb, C_HEADS * C_DK, C_DV), lambda i: (i, 0, 0)),
        ],
        out_shape=[
            jax.ShapeDtypeStruct((nb, T_PAD, width), F32),
            jax.ShapeDtypeStruct((nb, C_HEADS * C_DK, C_DV), F32),
        ],
        compiler_params=_params(("parallel",)), name="retention_sample",
    )(z3, z3, z3, cos_f, sin_s, decay_s, qdec_s, kdec_s, sdec_s, gain.reshape(1, width), st2)
    return out, st_out.reshape(st_in.shape)


def _retention_tables(length, n_valid):
    log_gamma = jnp.log1p(-jnp.exp2(-RET_DECAY_EXP0 - jnp.arange(C_HEADS, dtype=F32)))
    i = jnp.arange(length, dtype=F32)
    valid = (jnp.arange(length) < n_valid)
    rel = i[:, None] - i[None, :]
    decay = jnp.where(rel >= 0, jnp.exp(jnp.maximum(rel, 0.0)[None] * log_gamma[:, None, None]), 0.0)
    decay = jnp.where(valid[None, None, :], decay, 0.0)
    q_dec = jnp.exp((i[None, :] + 1.0) * log_gamma[:, None])
    k_dec = jnp.where(valid[None, :], jnp.exp((n_valid - 1.0 - i)[None, :] * log_gamma[:, None]), 0.0)
    s_dec = jnp.exp(n_valid * log_gamma)
    qdec = jnp.broadcast_to(q_dec[:, :, None], (C_HEADS, length, C_DV))
    kdec = jnp.broadcast_to(k_dec[:, :, None], (C_HEADS, length, C_DK))
    sdec = jnp.broadcast_to(s_dec[:, None, None], (C_HEADS, C_DK, C_DV))
    return decay, qdec, kdec, sdec


def _rope_tables(pos):
    half = C_DK // 2
    freqs = ROPE_BASE ** (-jnp.arange(half, dtype=F32) / half)
    ang = pos.astype(F32)[:, None] * freqs[None, :]
    cos, sin = jnp.cos(ang), jnp.sin(ang)
    cos_f = jnp.tile(jnp.concatenate([cos, cos], axis=1), (1, C_HEADS))
    sin_s = jnp.tile(jnp.concatenate([-sin, sin], axis=1), (1, C_HEADS))
    return cos_f, sin_s


def _s5_discretize(are_ref, aim_ref, ldt_ref):
    a_re, a_im = are_ref[...], aim_ref[...]
    dt = jnp.exp(ldt_ref[...])
    mag = jnp.exp(a_re * dt)
    ab_re = mag * jnp.cos(a_im * dt)
    ab_im = mag * jnp.sin(a_im * dt)
    num_re, num_im = ab_re - 1.0, ab_im
    den = a_re * a_re + a_im * a_im
    co_re = (num_re * a_re + num_im * a_im) / den
    co_im = (num_im * a_re - num_re * a_im) / den
    return ab_re, ab_im, co_re, co_im


S5_CHUNK = 256


def _s5_prompt_body(u_ref, are_ref, aim_ref, ldt_ref, bre_ref, bim_ref, cre_ref, cim_ref, dd_ref,
                    y_ref, st_ref, hre, him, state, disc):
    n = S5_CHUNK

    @pl.when(pl.program_id(0) == 0)
    def _():
        ab_re, ab_im, co_re, co_im = _s5_discretize(are_ref, aim_ref, ldt_ref)
        disc[0] = ab_re
        disc[1] = ab_im
        disc[2] = co_re
        disc[3] = co_im
        state[...] = jnp.zeros_like(state)

    n_chunks = D_ROW_STATE // LANES
    for s in range(D_ROWS):
        rows = pl.ds(s, n, stride=D_ROWS)
        u_s = u_ref[:, s * D_ROW_CH:(s + 1) * D_ROW_CH]
        u_b = u_s.astype(BF16)
        raw_re = jnp.dot(u_b, bre_ref[s], preferred_element_type=F32)
        raw_im = jnp.dot(u_b, bim_ref[s], preferred_element_type=F32)
        co_re = disc[2, s:s + 1, :]
        co_im = disc[3, s:s + 1, :]
        bu_re = co_re * raw_re - co_im * raw_im
        bu_im = co_re * raw_im + co_im * raw_re
        for c in range(n_chunks):
            hre[c, rows, :] = bu_re[:, c * LANES:(c + 1) * LANES]
            him[c, rows, :] = bu_im[:, c * LANES:(c + 1) * LANES]

    ab_re = disc[0]
    ab_im = disc[1]

    def step(t, carry):
        h_re, h_im = carry
        rows = pl.ds(pl.multiple_of(t * D_ROWS, D_ROWS), D_ROWS)
        bu_re = jnp.concatenate([hre[c, rows, :] for c in range(n_chunks)], axis=1)
        bu_im = jnp.concatenate([him[c, rows, :] for c in range(n_chunks)], axis=1)
        n_re = ab_re * h_re - ab_im * h_im + bu_re
        n_im = ab_re * h_im + ab_im * h_re + bu_im
        for c in range(n_chunks):
            hre[c, rows, :] = n_re[:, c * LANES:(c + 1) * LANES]
            him[c, rows, :] = n_im[:, c * LANES:(c + 1) * LANES]
        return n_re, n_im

    h_re, h_im = lax.fori_loop(0, n, step, (state[0], state[1]), unroll=4)
    state[0] = h_re
    state[1] = h_im
    st_ref[0] = h_re
    st_ref[1] = h_im

    for s in range(D_ROWS):
        rows = pl.ds(s, n, stride=D_ROWS)
        hist_re = jnp.concatenate([hre[c, rows, :] for c in range(n_chunks)], axis=1)
        hist_im = jnp.concatenate([him[c, rows, :] for c in range(n_chunks)], axis=1)
        cols = slice(s * D_ROW_CH, (s + 1) * D_ROW_CH)
        y = (jnp.dot(hist_re.astype(BF16), cre_ref[s], preferred_element_type=F32)
             - jnp.dot(hist_im.astype(BF16), cim_ref[s], preferred_element_type=F32)
             + dd_ref[s:s + 1, :] * u_ref[:, cols])
        y_ref[:, cols] = _gelu(y)


def _s5_prompt(z, u_col_block, par):
    a_re, a_im, ldt, b_re, b_im, c_re, c_im, dd = par
    t = z.shape[0]
    blk = S5_CHUNK * D_ROWS
    width = D_ROWS * D_ROW_CH
    c2 = lambda i: (0, 0)
    c3 = lambda i: (0, 0, 0)
    return pl.pallas_call(
        _s5_prompt_body, grid=(t // S5_CHUNK,),
        in_specs=[
            pl.BlockSpec((S5_CHUNK, width), lambda i: (i, u_col_block)),
            pl.BlockSpec(a_re.shape, c2), pl.BlockSpec(a_im.shape, c2), pl.BlockSpec(ldt.shape, c2),
            pl.BlockSpec(b_re.shape, c3), pl.BlockSpec(b_im.shape, c3),
            pl.BlockSpec(c_re.shape, c3), pl.BlockSpec(c_im.shape, c3),
            pl.BlockSpec(dd.shape, c2),
        ],
        out_specs=[
            pl.BlockSpec((S5_CHUNK, width), lambda i: (i, 0)),
            pl.BlockSpec((2, D_ROWS, D_ROW_STATE), c3),
        ],
        out_shape=[
            jax.ShapeDtypeStruct((t, width), F32),
            jax.ShapeDtypeStruct((2, D_ROWS, D_ROW_STATE), F32),
        ],
        scratch_shapes=[
            pltpu.VMEM((D_ROW_STATE // LANES, blk, LANES), F32),
            pltpu.VMEM((D_ROW_STATE // LANES, blk, LANES), F32),
            pltpu.VMEM((2, D_ROWS, D_ROW_STATE), F32), pltpu.VMEM((4, D_ROWS, D_ROW_STATE), F32),
        ],
        compiler_params=_params(("arbitrary",)), name="s5_prompt",
    )(z, a_re, a_im, ldt, b_re.astype(BF16), b_im.astype(BF16), c_re.astype(BF16), c_im.astype(BF16), dd)


def _s5_sample_body(u_ref, xre_ref, xim_ref, are_ref, aim_ref, ldt_ref, bre_ref, bim_ref, cre_ref,
                    cim_ref, dd_ref, y_ref, sre_ref, sim_ref):
    ab_re, ab_im, co_re, co_im = _s5_discretize(are_ref, aim_ref, ldt_ref)
    n_t = u_ref.shape[0]
    for s in range(D_ROWS):
        a_r, a_i = ab_re[s:s + 1, :], ab_im[s:s + 1, :]
        c_r, c_i = co_re[s:s + 1, :], co_im[s:s + 1, :]
        h_re, h_im = xre_ref[s], xim_ref[s]
        for t in range(n_t):
            u = u_ref[t, s]
            raw_re = jnp.dot(u, bre_ref[s], precision=HIGHEST, preferred_element_type=F32)
            raw_im = jnp.dot(u, bim_ref[s], precision=HIGHEST, preferred_element_type=F32)
            bu_re = c_r * raw_re - c_i * raw_im
            bu_im = c_r * raw_im + c_i * raw_re
            h_re, h_im = a_r * h_re - a_i * h_im + bu_re, a_r * h_im + a_i * h_re + bu_im
            y = (jnp.dot(h_re, cre_ref[s], precision=HIGHEST, preferred_element_type=F32)
                 - jnp.dot(h_im, cim_ref[s], precision=HIGHEST, preferred_element_type=F32)
                 + dd_ref[s:s + 1, :] * u)
            y_ref[t, s] = _gelu(y)
        sre_ref[s] = h_re
        sim_ref[s] = h_im


def _s5_sample(u_ts, x_re, x_im, par):
    a_re, a_im, ldt, b_re, b_im, c_re, c_im, dd = par
    return pl.pallas_call(
        _s5_sample_body,
        out_shape=[
            jax.ShapeDtypeStruct(u_ts.shape, F32),
            jax.ShapeDtypeStruct(x_re.shape, F32),
            jax.ShapeDtypeStruct(x_im.shape, F32),
        ],
        compiler_params=pltpu.CompilerParams(vmem_limit_bytes=VMEM_LIMIT), name="s5_sample",
    )(u_ts, x_re, x_im, a_re, a_im, ldt, b_re, b_im, c_re, c_im, dd)


def _s5_params(a_re, a_im, log_dt, b_re, b_im, c_re, c_im, dd):
    eye = jnp.eye(D_ROWS, dtype=F32)

    def rows(a):
        return a.reshape(D_ROWS, D_ROW_STATE)

    def b_blocks(b):
        b4 = b.reshape(D_ROWS, D_ROWS, D_STATE, D_GROUP_CH).transpose(0, 1, 3, 2)
        return jnp.einsum('sgcp,gh->sgchp', b4, eye).reshape(D_ROWS, D_ROW_CH, D_ROW_STATE)

    def c_blocks(c):
        c4 = c.reshape(D_ROWS, D_ROWS, D_GROUP_CH, D_STATE)
        return jnp.einsum('sgcp,gh->sgphc', c4, eye).reshape(D_ROWS, D_ROW_STATE, D_ROW_CH)

    ldt = jnp.broadcast_to(log_dt[:, None], (D_GROUPS, D_STATE))
    return (rows(a_re), rows(a_im), rows(ldt), b_blocks(b_re), b_blocks(b_im),
            c_blocks(c_re), c_blocks(c_im), dd.reshape(D_ROWS, D_ROW_CH))


def _glu_body(x_ref, w_ref, b_ref, o_ref):
    x = x_ref[...]
    gate = jnp.dot(x.astype(BF16), w_ref[...], preferred_element_type=F32) + b_ref[...]
    o_ref[...] = x * jax.nn.sigmoid(gate)


def _glu(x, w, b, tm=512):
    m, k = x.shape
    return pl.pallas_call(
        _glu_body, grid=(m // tm,),
        in_specs=[pl.BlockSpec((tm, k), lambda i: (i, 0)), pl.BlockSpec((k, k), lambda i: (0, 0)),
                  pl.BlockSpec((1, k), lambda i: (0, 0))],
        out_specs=pl.BlockSpec((tm, k), lambda i: (i, 0)),
        out_shape=jax.ShapeDtypeStruct((m, k), F32),
        compiler_params=_params(("parallel",)), name="glu",
    )(x, w, b.reshape(1, k))


PEER_SEL_TOKENS = 256


def _extract_top(s, n_out, exact, want_rank=False):
    n_rows = s.shape[0]
    rid = lax.broadcasted_iota(jnp.int32, s.shape, 0).astype(F32) if exact else None
    rank = jnp.full(s.shape, float(n_out), F32) if want_rank else None
    vals = []
    for r in range(n_out):
        m = jnp.max(s, axis=0, keepdims=True)
        if exact:
            first = jnp.min(jnp.where(s == m, rid, float(n_rows)), axis=0, keepdims=True)
            hit = rid == first
        else:
            hit = s == m
        if want_rank:
            rank = jnp.where(hit, float(r), rank)
        s = jnp.where(hit, NEG_INF, s)
        vals.append(m)
    return vals, s, rank


def _removed_count(s):
    return jnp.sum(jnp.where(s == NEG_INF, 1.0, 0.0), axis=0, keepdims=True)


_CAND_COUNTS = [PEER_TOPK // (a + 1) for a in range(PEER_TOPK)]


def _candidates(v1, v2):
    v1_all = _stack_rows(v1)
    v2_all = _stack_rows(v2)
    rid = lax.broadcasted_iota(jnp.int32, (SUBLANES, v1[0].shape[1]), 0)
    tiles = [v1[0] + v2_all]
    n_pad = 0
    a = 1
    while _CAND_COUNTS[a] > 1:
        tiles.append(jnp.where(rid < _CAND_COUNTS[a], v1[a] + v2_all[:SUBLANES], NEG_INF))
        n_pad += SUBLANES - _CAND_COUNTS[a]
        a += 1
    assert PEER_TOPK - a == SUBLANES
    tiles.append(v1_all[a:] + v2[0])
    return jnp.concatenate(tiles, axis=0), n_pad


def _stack_rows(rows):
    n = len(rows)
    rid = lax.broadcasted_iota(jnp.int32, (n, rows[0].shape[1]), 0)
    out = jnp.broadcast_to(rows[n - 1], rid.shape)
    for i in range(n - 2, -1, -1):
        out = jnp.where(rid == i, rows[i], out)
    return out


def _selected_per_first_key(removed, tag, tags):
    counts = [jnp.sum(removed[:PEER_TOPK], axis=0, keepdims=True)]
    row = PEER_TOPK
    a = 1
    while _CAND_COUNTS[a] > 1:
        counts.append(jnp.sum(removed[row:row + SUBLANES], axis=0, keepdims=True))
        row += SUBLANES
        a += 1
    for i in range(PEER_TOPK - a):
        counts.append(removed[row + i:row + i + 1])
    n_sel = jnp.zeros(tag.shape, F32)
    for a, cnt in enumerate(counts):
        n_sel = jnp.where(tag == tags[a], cnt, n_sel)
    return n_sel


def _peer_select_body(q_ref, k1_ref, k2_ref, ns_ref, g1_ref, r2_ref, e2_ref):
    def head(h, exact):
        q1 = q_ref[:, (2 * h) * PEER_HALF:(2 * h + 1) * PEER_HALF]
        q2 = q_ref[:, (2 * h + 1) * PEER_HALF:(2 * h + 2) * PEER_HALF]
        s1 = _dot_nt(k1_ref[h], q1, HIGHEST)
        s2 = _dot_nt(k2_ref[h], q2, HIGHEST)
        v1, left1, rank1 = _extract_top(s1, PEER_TOPK, exact, want_rank=exact)
        v2, left2, rank2 = _extract_top(s2, PEER_TOPK, exact, want_rank=True)
        tag1, tags1 = (rank1, [float(a) for a in range(PEER_TOPK)]) if exact else (s1, v1)
        cand, n_pad = _candidates(v1, v2)
        top, left_c, _ = _extract_top(cand, PEER_TOPK, exact)
        mx = top[0]
        z = jnp.exp(top[0] - mx)
        for kk in range(1, PEER_TOPK):
            z = z + jnp.exp(top[kk] - mx)
        g1 = jnp.where(left1 == NEG_INF, jnp.exp(s1 - v1[0]), 0.0) / z
        e2 = jnp.where(rank2 < PEER_TOPK, jnp.exp(s2 - v2[0]), 0.0)
        removed = jnp.where((left_c == NEG_INF) & (cand != NEG_INF), 1.0, 0.0)
        n_sel = _selected_per_first_key(removed, tag1, tags1)
        for j in range(PEER_SEL_TOKENS // LANES):
            lanes = slice(j * LANES, (j + 1) * LANES)
            ns_ref[j, h] = n_sel[:, lanes]
            g1_ref[j, h] = g1[:, lanes]
            r2_ref[j, h] = rank2[:, lanes]
            e2_ref[j, h] = e2[:, lanes]
        if exact:
            return None
        ok = ((_removed_count(left1) == PEER_TOPK) & (_removed_count(left2) == PEER_TOPK)
              & (_removed_count(left_c) == PEER_TOPK + n_pad))
        return jnp.where(ok, 0.0, 1.0)

    repeated = [jnp.max(head(h, False)) for h in range(PEER_HEADS)]
    for h in range(PEER_HEADS):
        @pl.when(repeated[h] > 0.0)
        def _():
            head(h, True)


def _peer_select(q, k1, k2):
    t = q.shape[0]
    tb = PEER_SEL_TOKENS
    nj = tb // LANES
    big = pl.BlockSpec((nj, PEER_HEADS, PEER_NKEYS, LANES), lambda i: (i, 0, 0, 0))
    shape = jax.ShapeDtypeStruct((t // LANES, PEER_HEADS, PEER_NKEYS, LANES), F32)
    return pl.pallas_call(
        _peer_select_body, grid=(t // tb,),
        in_specs=[
            pl.BlockSpec((tb, q.shape[1]), lambda i: (i, 0)),
            pl.BlockSpec(k1.shape, lambda i: (0, 0, 0)),
            pl.BlockSpec(k2.shape, lambda i: (0, 0, 0)),
        ],
        out_specs=[big, big, big, big],
        out_shape=[shape, shape, shape, shape],
        compiler_params=_params(("parallel",)), name="peer_select",
    )(q, k1, k2)


PEER_TOKENS = 512
PEER_EXPERT_TILE = 512
BF16_ROWS = 16
GATE_GROUP = 4


def _peer_dense_body(h_ref, g_ref, u_ref, v_ref, ns_ref, g1_ref, r2_ref, e2_ref, og_ref, o_ref, *rest,
                     norm_out, emit_bf16):
    if emit_bf16:
        ub_ref, vb_ref, xn, act_s, coef_s, acc = rest
    else:
        xn, act_s, coef_s, acc = rest
    e = pl.program_id(1)
    n_chunks = PEER_TOKENS // LANES

    @pl.when(e == 0)
    def _():
        xn[...] = _rmsnorm(h_ref[...], g_ref[...]).T.astype(BF16)
        acc[...] = jnp.zeros_like(acc)

    te = PEER_EXPERT_TILE
    slabs = te // PEER_NKEYS

    def one_tile(part):
        tile_rows = slice(part * te, (part + 1) * te)
        u_tile = u_ref[tile_rows, :].astype(BF16)
        if emit_bf16:
            ub_ref[tile_rows, :] = u_tile
        act = jnp.dot(u_tile, xn[...], preferred_element_type=F32)
        for j in range(n_chunks):
            act_s[j] = act[:, j * LANES:(j + 1) * LANES]

        def chunk(j, carry):
            packed = (PEER_NKEYS // BF16_ROWS, BF16_ROWS, LANES)
            for c0 in range(0, slabs, GATE_GROUP):
                gates = [jnp.zeros(packed, BF16) for _ in range(GATE_GROUP)]
                for h in range(PEER_HEADS):
                    r2 = r2_ref[j, h].reshape(packed).astype(BF16)
                    e2 = e2_ref[j, h].reshape(packed).astype(BF16)
                    for k in range(GATE_GROUP):
                        key1 = part * slabs + c0 + k
                        ns = jnp.broadcast_to(ns_ref[j, h, key1:key1 + 1, :], packed[1:]).astype(BF16)
                        g1 = jnp.broadcast_to(g1_ref[j, h, key1:key1 + 1, :], packed[1:]).astype(BF16)
                        gates[k] = gates[k] + jnp.where(r2 < ns[None], e2 * g1[None], jnp.zeros((), BF16))
                for k in range(GATE_GROUP):
                    rows = slice((c0 + k) * PEER_NKEYS, (c0 + k + 1) * PEER_NKEYS)
                    act = _gelu(act_s[j, rows, :]).reshape(packed).astype(BF16)
                    coef_s[j, rows, :] = (gates[k] * act).reshape(PEER_NKEYS, LANES)
            return carry

        for j in range(n_chunks):
            chunk(j, 0)
        coef = jnp.concatenate([coef_s[j] for j in range(n_chunks)], axis=1)
        v_tile = v_ref[tile_rows, :].astype(BF16)
        if emit_bf16:
            vb_ref[tile_rows, :] = v_tile
        acc[...] += _dot_tn(coef, v_tile)

    for part in range(u_ref.shape[0] // te):
        one_tile(part)

    @pl.when(e == pl.num_programs(1) - 1)
    def _():
        out = h_ref[...] + acc[...]
        o_ref[...] = _rmsnorm(out, og_ref[...]) if norm_out else out


def _peer_dense(h, g, u_tab, v_tab, layer, sel, out_g, norm_out):
    ns, g1, r2, e2 = sel
    t, d = h.shape
    emit_bf16 = u_tab.ndim == 3
    n_exp = u_tab.shape[-2]
    tb = PEER_TOKENS
    nj = tb // LANES
    slabs = PEER_EXPERT_TILE // PEER_NKEYS
    o_spec = pl.BlockSpec((tb, d), lambda i, e: (i, 0))
    o_shape = jax.ShapeDtypeStruct((t, d), F32)
    if emit_bf16:
        assert t == tb, "every table tile must be visited exactly once when the casts are emitted"
        te = PEER_EXPERT_TILE
        n_steps = n_exp // te
        tab_spec = pl.BlockSpec((None, te, d), lambda i, e: (layer, e, 0))

        def by_tile(a):
            a5 = a.reshape(a.shape[0], PEER_HEADS, n_steps, slabs, LANES)
            return a5.transpose(2, 0, 1, 3, 4)

        ns, g1 = by_tile(ns), by_tile(g1)
        key_spec = pl.BlockSpec((None, nj, PEER_HEADS, slabs, LANES), lambda i, e: (e, i, 0, 0, 0))
    else:
        te = SUBLANES // slabs * PEER_EXPERT_TILE
        n_steps = n_exp // te
        tab_spec = pl.BlockSpec((te, d), lambda i, e: (e, 0))
        key_spec = pl.BlockSpec((nj, PEER_HEADS, SUBLANES, LANES), lambda i, e: (i, 0, e, 0))
    cast_spec = pl.BlockSpec((te, d), lambda i, e: (e, 0))
    cast_shape = jax.ShapeDtypeStruct((n_exp, d), BF16)
    tok_spec = pl.BlockSpec((nj, PEER_HEADS, PEER_NKEYS, LANES), lambda i, e: (i, 0, 0, 0))
    te_scratch = PEER_EXPERT_TILE
    return pl.pallas_call(
        functools.partial(_peer_dense_body, norm_out=norm_out, emit_bf16=emit_bf16),
        grid=(t // tb, n_steps),
        in_specs=[
            pl.BlockSpec((tb, d), lambda i, e: (i, 0)),
            pl.BlockSpec((1, d), lambda i, e: (0, 0)),
            tab_spec, tab_spec,
            key_spec, key_spec, tok_spec, tok_spec,
            pl.BlockSpec((1, d), lambda i, e: (0, 0)),
        ],
        out_specs=[o_spec, cast_spec, cast_spec] if emit_bf16 else o_spec,
        out_shape=[o_shape, cast_shape, cast_shape] if emit_bf16 else o_shape,
        scratch_shapes=[pltpu.VMEM((d, tb), BF16), pltpu.VMEM((nj, te_scratch, LANES), F32),
                        pltpu.VMEM((nj, te_scratch, LANES), BF16), pltpu.VMEM((tb, d), F32)],
        compiler_params=_params(("parallel", "arbitrary")), name="peer_dense",
    )(h, g.reshape(1, d), u_tab, v_tab, ns, g1, r2, e2, out_g.reshape(1, d))


def _peer(h, g, wq, k1, k2, u_tab, v_tab, layer, out_g, norm_out):
    q = _matmul(h, wq, norm_g=g)
    return _peer_dense(h, g, u_tab, v_tab, layer, _peer_select(q, k1, k2), out_g, norm_out)


def kernel(x_prompt, x_sample, state_b_k, state_b_v, state_c_s, state_d_re, state_d_im, norm1_g, norm2_g, final_g, w_in_even, w_out_even, a_ws, a_bs, b_sink, w_in_odd, w_out_odd, c_norm_g, d_a_re, d_a_im, d_log_dt, d_b_re, d_b_im, d_c_re, d_c_im, d_d, d_glu_w, d_glu_b, peer_wq, peer_k1, peer_k2, peer_u, peer_v):
    seq = x_prompt.shape[1]
    n_batch, n_new = x_sample.shape[:2]
    past = PAST_LEN
    hp = x_prompt.reshape(seq, D_MODEL)
    hs = x_sample.reshape(n_batch * n_new, D_MODEL)
    bf = lambda a: a.astype(BF16)

    def pad_tokens(a):
        a3 = a.reshape(n_batch, n_new, a.shape[-1])
        return jnp.pad(a3, ((0, 0), (0, T_PAD - n_new), (0, 0)))

    def unpad_tokens(a3):
        return a3[:, :n_new].reshape(n_batch * n_new, a3.shape[-1])

    w_in = bf(w_in_even[0])
    w_out = bf(w_out_even[0])
    zp = _matmul(hp, w_in, norm_g=norm1_g[0])
    zs = _matmul(hs, w_in, norm_g=norm1_g[0])
    bs_full = jnp.broadcast_to(a_bs[0][:, :, None], (A_GROUPS, CHUNK, LANES))
    yp = _even_prompt(zp, a_ws[0], bs_full, b_sink[0])
    ws_small = jnp.tril(a_ws[0][:, :n_new, :n_new]).transpose(2, 1, 0)
    wa = jnp.repeat(jnp.pad(ws_small, ((0, 0), (0, T_PAD - n_new), (0, 0))), LANES, axis=-1)
    wb = jnp.repeat(jnp.pad(a_bs[0][:, :n_new].T, ((0, T_PAD - n_new), (0, 0))), LANES, axis=-1)
    win = state_b_k.shape[2]
    kbuf = state_b_k[0].reshape(n_batch, win, B_KV_HEADS * B_DH)
    vbuf = state_b_v[0].reshape(n_batch, win, B_KV_HEADS * B_DH)
    ys3, av3 = _even_sample(pad_tokens(zs), kbuf, vbuf, wa, wb, b_sink[0])
    hp = _matmul(yp, w_out, resid=hp)
    hs = _matmul(unpad_tokens(ys3), w_out, resid=hs)

    k_off = 2 * A_WIDTH + B_HEADS * B_DH
    v_off = k_off + B_KV_HEADS * B_DH
    kv_shape = (1, -1, win, B_KV_HEADS, B_DH)
    a_v_sample = av3[:, :n_new].reshape(1, n_batch, n_new, A_GROUPS, A_WIDTH // A_GROUPS)
    b_k_prompt = zp[seq - win:, k_off:v_off].reshape(kv_shape)
    b_v_prompt = zp[seq - win:, v_off:].reshape(kv_shape)
    k_new = zs[:, k_off:v_off].reshape(n_batch, n_new, B_KV_HEADS * B_DH)
    v_new = zs[:, v_off:].reshape(n_batch, n_new, B_KV_HEADS * B_DH)
    b_k_sample = jnp.concatenate([kbuf, k_new], axis=1)[:, -win:].reshape(kv_shape)
    b_v_sample = jnp.concatenate([vbuf, v_new], axis=1)[:, -win:].reshape(kv_shape)

    wq = bf(peer_wq[0])
    hs, u_b, v_b = _peer(hs, norm2_g[0], wq, peer_k1[0], peer_k2[0], peer_u, peer_v, 0, final_g, False)
    hp = _peer(hp, norm2_g[0], wq, peer_k1[0], peer_k2[0], u_b, v_b, 0, final_g, False)

    w_in = bf(w_in_odd[0])
    w_out = bf(w_out_odd[0])
    c_width = C_HEADS * C_DV
    zp = _matmul(hp, w_in, norm_g=norm1_g[1])
    zs = _matmul(hs, w_in, norm_g=norm1_g[1])
    cos_p, sin_p = _rope_tables(jnp.arange(seq))
    cos_s, sin_s = _rope_tables(past + jnp.arange(T_PAD))
    ycp, c_s_prompt = _retention_prompt(zp, cos_p, sin_p, _retention_tables(CHUNK, CHUNK), c_norm_g[0])
    zs3 = pad_tokens(zs)
    ycs3, c_s_sample = _retention_sample(zs3, cos_s, sin_s, _retention_tables(T_PAD, n_new), c_norm_g[0],
                                         state_c_s[0])
    par = _s5_params(d_a_re[0], d_a_im[0], d_log_dt[0], d_b_re[0], d_b_im[0], d_c_re[0], d_c_im[0], d_d[0])
    u_off = 3 * c_width
    ydp_pre, d_prompt = _s5_prompt(zp, u_off // c_width, par)
    us_ts = zs[:, u_off:].reshape(n_batch, n_new, D_ROWS, D_ROW_CH).transpose(1, 2, 0, 3)
    x_re = state_d_re[0].reshape(n_batch, D_ROWS, D_ROW_STATE).transpose(1, 0, 2)
    x_im = state_d_im[0].reshape(n_batch, D_ROWS, D_ROW_STATE).transpose(1, 0, 2)
    yds_ts, s_re, s_im = _s5_sample(us_ts, x_re, x_im, par)
    glu_w = bf(d_glu_w[0])
    ydp = _glu(ydp_pre, glu_w, d_glu_b[0])
    yds = _glu(yds_ts.transpose(2, 0, 1, 3).reshape(n_batch * n_new, c_width), glu_w, d_glu_b[0])
    hp = _matmul(ycp, w_out[:c_width], x2=ydp, w2=w_out[c_width:], resid=hp)
    hs = _matmul(unpad_tokens(ycs3), w_out[:c_width], x2=yds, w2=w_out[c_width:], resid=hs)

    wq = bf(peer_wq[1])
    y_sample, u_b, v_b = _peer(hs, norm2_g[1], wq, peer_k1[1], peer_k2[1], peer_u, peer_v, 1, final_g, True)
    y_prompt = _peer(hp, norm2_g[1], wq, peer_k1[1], peer_k2[1], u_b, v_b, 1, final_g, True)
    y_prompt = y_prompt.reshape(x_prompt.shape)
    y_sample = y_sample.reshape(x_sample.shape)

    d_shape = (1, -1, D_GROUPS, D_STATE)
    return (y_prompt, y_sample, a_v_sample, b_k_prompt, b_v_prompt, b_k_sample, b_v_sample,
            c_s_prompt.reshape(1, 1, C_HEADS, C_DK, C_DV), c_s_sample[None],
            d_prompt[0].reshape(d_shape), d_prompt[1].reshape(d_shape),
            s_re.transpose(1, 0, 2).reshape(d_shape), s_im.transpose(1, 0, 2).reshape(d_shape))
```

```python
import functools
import math

import jax
import jax.numpy as jnp
from jax import lax
from jax.experimental import pallas as pl
from jax.experimental.pallas import tpu as pltpu

F32 = jnp.float32
BF16 = jnp.bfloat16
HIGHEST = lax.Precision.HIGHEST

EPS = 1e-6
NEG_BIG = -1e30
NEG_INF = float("-inf")

D_MODEL = 2048
PAST_LEN = 8192
LANES = 128
SUBLANES = 8
VMEM_LIMIT = 56 * 1024 * 1024

CHUNK = 128
A_GROUPS = 8
A_WIDTH = 1024
B_HEADS = 16
B_KV_HEADS = 4
B_GQA = 4
B_DH = 64
C_HEADS = 8
C_DK = 64
C_DV = 128
RET_DECAY_EXP0 = 5.0
ROPE_BASE = 10000.0
D_GROUPS = 64
D_STATE = 64
D_GROUP_CH = 16
D_ROWS = 8
D_ROW_STATE = 512
D_ROW_CH = 128
PEER_HEADS = 8
PEER_NKEYS = 128
PEER_TOPK = 16
PEER_HALF = 128


def _params(semantics):
    return pltpu.CompilerParams(dimension_semantics=semantics, vmem_limit_bytes=VMEM_LIMIT)


def _gelu(x):
    return 0.5 * x * (1.0 + lax.erf(x * (1.0 / math.sqrt(2.0))))


def _rmsnorm(x, g):
    return x * lax.rsqrt(jnp.mean(x * x, axis=-1, keepdims=True) + EPS) * g


def _dot_nt(a, b, precision=None):
    return lax.dot_general(a, b, (((1,), (1,)), ((), ())), precision=precision,
                           preferred_element_type=F32)


def _dot_tn(a, b, precision=None):
    return lax.dot_general(a, b, (((0,), (0,)), ((), ())), precision=precision,
                           preferred_element_type=F32)


def _mm_body(*refs, has_norm, has_pair, has_resid):
    it = iter(refs)
    x_ref, w_ref = next(it), next(it)
    g_ref = next(it) if has_norm else None
    x2_ref, w2_ref = (next(it), next(it)) if has_pair else (None, None)
    r_ref = next(it) if has_resid else None
    o_ref = next(it)
    x = x_ref[...]
    if has_norm:
        x = _rmsnorm(x, g_ref[...])
    xb = x.astype(BF16)
    x2b = x2_ref[...].astype(BF16) if has_pair else None
    n = o_ref.shape[1]
    for j in range(n // MM_COLS):
        cols = slice(j * MM_COLS, (j + 1) * MM_COLS)
        acc = jnp.dot(xb, w_ref[:, cols], preferred_element_type=F32)
        if has_pair:
            acc = acc + jnp.dot(x2b, w2_ref[:, cols], preferred_element_type=F32)
        if has_resid:
            acc = acc + r_ref[:, cols]
        o_ref[:, cols] = acc


MM_COLS = 512


def _matmul(x, w, *, norm_g=None, x2=None, w2=None, resid=None, tm=512):
    m, k = x.shape
    n = w.shape[1]
    assert m % tm == 0 and n % MM_COLS == 0
    resident = pl.Buffered(1)
    args = [x, w]
    specs = [pl.BlockSpec((tm, k), lambda i: (i, 0)),
             pl.BlockSpec((k, n), lambda i: (0, 0), pipeline_mode=resident)]
    if norm_g is not None:
        args.append(norm_g.reshape(1, k))
        specs.append(pl.BlockSpec((1, k), lambda i: (0, 0)))
    if x2 is not None:
        k2 = x2.shape[1]
        args += [x2, w2]
        specs += [pl.BlockSpec((tm, k2), lambda i: (i, 0)),
                  pl.BlockSpec((k2, n), lambda i: (0, 0), pipeline_mode=resident)]
    if resid is not None:
        args.append(resid)
        specs.append(pl.BlockSpec((tm, n), lambda i: (i, 0)))
    body = functools.partial(_mm_body, has_norm=norm_g is not None, has_pair=x2 is not None,
                             has_resid=resid is not None)
    return pl.pallas_call(
        body, grid=(m // tm,), in_specs=specs,
        out_specs=pl.BlockSpec((tm, n), lambda i: (i, 0)),
        out_shape=jax.ShapeDtypeStruct((m, n), F32),
        compiler_params=_params(("parallel",)), name="matmul",
    )(*args)


def _sink_column(sink_ref, kvh, rows_per_head, n_rows):
    grp = lax.broadcasted_iota(jnp.int32, (n_rows, 1), 0) // rows_per_head
    sk = jnp.full((n_rows, 1), sink_ref[kvh * B_GQA + B_GQA - 1], F32)
    for g in range(B_GQA - 2, -1, -1):
        sk = jnp.where(grp == g, sink_ref[kvh * B_GQA + g], sk)
    return sk


def _even_prompt_body(sink_ref, au_ref, av_ref, q_ref, kvc_ref, kvp_ref, ws_ref, bs_ref, o_ref):
    blk = pl.program_id(0)
    au = _gelu(au_ref[...])
    av = _gelu(av_ref[...])
    row = lax.broadcasted_iota(jnp.int32, (CHUNK, CHUNK), 0)
    col = lax.broadcasted_iota(jnp.int32, (CHUNK, CHUNK), 1)
    causal = row >= col
    for g in range(A_GROUPS):
        lanes = slice(g * LANES, (g + 1) * LANES)
        w = jnp.where(causal, ws_ref[g], 0.0).astype(BF16)
        mixed = jnp.dot(w, av[:, lanes].astype(BF16), preferred_element_type=F32) + bs_ref[g]
        o_ref[:, lanes] = au[:, lanes] * mixed

    q = q_ref[...]
    kvc = kvc_ref[...]
    kvp = kvp_ref[...]
    n_rows = B_GQA * CHUNK
    qi = lax.broadcasted_iota(jnp.int32, (n_rows, 2 * CHUNK), 0) % CHUNK
    kc = lax.broadcasted_iota(jnp.int32, (n_rows, 2 * CHUNK), 1)
    dist = qi + CHUNK - kc
    allowed = (dist >= 0) & (dist < CHUNK) & ((kc >= CHUNK) | (blk > 0))
    outs = []
    for kvh in range(B_KV_HEADS):
        ks = slice(kvh * B_DH, (kvh + 1) * B_DH)
        vs = slice(B_KV_HEADS * B_DH + kvh * B_DH, B_KV_HEADS * B_DH + (kvh + 1) * B_DH)
        kk = jnp.concatenate([kvp[:, ks], kvc[:, ks]], axis=0).astype(BF16)
        vv = jnp.concatenate([kvp[:, vs], kvc[:, vs]], axis=0).astype(BF16)
        q4 = jnp.concatenate(
            [q[:, (kvh * B_GQA + g) * B_DH:(kvh * B_GQA + g + 1) * B_DH] for g in range(B_GQA)], axis=0)
        s = _dot_nt(q4.astype(BF16), kk) * (B_DH ** -0.5)
        s = jnp.where(allowed, s, NEG_BIG)
        sk = _sink_column(sink_ref, kvh, CHUNK, n_rows)
        mx = jnp.maximum(jnp.max(s, axis=-1, keepdims=True), sk)
        p = jnp.exp(s - mx)
        p = p / (jnp.sum(p, axis=-1, keepdims=True) + jnp.exp(sk - mx))
        o = jnp.dot(p.astype(BF16), vv, preferred_element_type=F32)
        outs += [o[g * CHUNK:(g + 1) * CHUNK] for g in range(B_GQA)]
    o_ref[:, A_WIDTH:] = jnp.concatenate(outs, axis=1)


def _even_prompt(z, ws, bs_full, sink):
    t = z.shape[0]
    nb = t // CHUNK
    wide = A_WIDTH
    kvw = 2 * B_KV_HEADS * B_DH
    kv_blk = (2 * A_WIDTH + B_HEADS * B_DH) // kvw
    return pl.pallas_call(
        _even_prompt_body, grid=(nb,),
        in_specs=[
            pl.BlockSpec(memory_space=pltpu.SMEM),
            pl.BlockSpec((CHUNK, wide), lambda i: (i, 0)),
            pl.BlockSpec((CHUNK, wide), lambda i: (i, 1)),
            pl.BlockSpec((CHUNK, wide), lambda i: (i, 2)),
            pl.BlockSpec((CHUNK, kvw), lambda i: (i, kv_blk)),
            pl.BlockSpec((CHUNK, kvw), lambda i: (jnp.maximum(i - 1, 0), kv_blk)),
            pl.BlockSpec((A_GROUPS, CHUNK, CHUNK), lambda i: (0, 0, 0)),
            pl.BlockSpec((A_GROUPS, CHUNK, LANES), lambda i: (0, 0, 0)),
        ],
        out_specs=pl.BlockSpec((CHUNK, D_MODEL), lambda i: (i, 0)),
        out_shape=jax.ShapeDtypeStruct((t, D_MODEL), F32),
        compiler_params=_params(("parallel",)), name="even_prompt",
    )(sink, z, z, z, z, z, ws, bs_full)


EVEN_SAMPLE_BATCH = 8
T_PAD = 8


def _even_sample_body(sink_ref, au_ref, av_ref, q_ref, kv_ref, kb_ref, vb_ref, wa_ref, wb_ref,
                      y_ref, avo_ref):
    kv_width = B_KV_HEADS * B_DH
    n_rows = B_HEADS * T_PAD
    tq = lax.broadcasted_iota(jnp.int32, (n_rows, CHUNK), 0) % T_PAD
    kc = lax.broadcasted_iota(jnp.int32, (n_rows, CHUNK), 1)
    buf_allowed = kc > tq
    tq1 = lax.broadcasted_iota(jnp.int32, (n_rows, 1), 0) % T_PAD
    head_of_row = lax.broadcasted_iota(jnp.int32, (n_rows, 1), 0) // T_PAD
    sk = jnp.full((n_rows, 1), sink_ref[B_HEADS - 1], F32)
    for hd in range(B_HEADS - 2, -1, -1):
        sk = jnp.where(head_of_row == hd, sink_ref[hd], sk)
    lane_kvh = lax.broadcasted_iota(jnp.int32, (T_PAD, kv_width), 1) // B_DH
    scale = B_DH ** -0.5
    n_new = wa_ref.shape[0]
    for b in range(EVEN_SAMPLE_BATCH):
        au = _gelu(au_ref[b])
        av = _gelu(av_ref[b])
        avo_ref[b] = av
        mixed = wb_ref[...]
        for j in range(n_new):
            mixed = mixed + wa_ref[j] * av[j:j + 1, :]
        y_ref[b, :, :A_WIDTH] = au * mixed

        q = q_ref[b]
        kv = kv_ref[b]
        blocks = []
        for kvh in range(B_KV_HEADS):
            for g in range(B_GQA):
                col = (kvh * B_GQA + g) * B_DH
                qg = jnp.concatenate([q[:, col:col + B_DH]] * B_KV_HEADS, axis=1)
                blocks.append(jnp.where(lane_kvh == kvh, qg, 0.0))
        qbd = jnp.concatenate(blocks, axis=0).astype(BF16)
        s_buf = _dot_nt(qbd, kb_ref[b].astype(BF16)) * scale
        s_buf = jnp.where(buf_allowed, s_buf, NEG_BIG)
        qbd_r = qbd.astype(F32)
        s_new = []
        for j in range(n_new):
            kj = kv[j:j + 1, :kv_width].astype(BF16).astype(F32)
            sj = jnp.sum(qbd_r * kj, axis=-1, keepdims=True) * scale
            s_new.append(jnp.where(tq1 >= j, sj, NEG_BIG))
        mx = jnp.maximum(jnp.max(s_buf, axis=-1, keepdims=True), sk)
        for sj in s_new:
            mx = jnp.maximum(mx, sj)
        p_buf = jnp.exp(s_buf - mx)
        p_new = [jnp.exp(sj - mx) for sj in s_new]
        den = jnp.sum(p_buf, axis=-1, keepdims=True) + jnp.exp(sk - mx)
        for pj in p_new:
            den = den + pj
        inv = 1.0 / den
        o = jnp.dot((p_buf * inv).astype(BF16), vb_ref[b].astype(BF16), preferred_element_type=F32)
        for j in range(n_new):
            vj = kv[j:j + 1, kv_width:].astype(BF16).astype(F32)
            o = o + (p_new[j] * inv).astype(BF16).astype(F32) * vj
        outs = []
        for kvh in range(B_KV_HEADS):
            for g in range(B_GQA):
                row = (kvh * B_GQA + g) * T_PAD
                outs.append(o[row:row + T_PAD, kvh * B_DH:(kvh + 1) * B_DH])
        y_ref[b, :, A_WIDTH:] = jnp.concatenate(outs, axis=1)


def _even_sample(z3, kbuf, vbuf, wa, wb, sink):
    nb = z3.shape[0]
    bb = EVEN_SAMPLE_BATCH
    kvw = 2 * B_KV_HEADS * B_DH
    kv_blk = (2 * A_WIDTH + B_HEADS * B_DH) // kvw
    win = kbuf.shape[1]
    return pl.pallas_call(
        _even_sample_body, grid=(nb // bb,),
        in_specs=[
            pl.BlockSpec(memory_space=pltpu.SMEM),
            pl.BlockSpec((bb, T_PAD, A_WIDTH), lambda i: (i, 0, 0)),
            pl.BlockSpec((bb, T_PAD, A_WIDTH), lambda i: (i, 0, 1)),
            pl.BlockSpec((bb, T_PAD, A_WIDTH), lambda i: (i, 0, 2)),
            pl.BlockSpec((bb, T_PAD, kvw), lambda i: (i, 0, kv_blk)),
            pl.BlockSpec((bb, win, kvw // 2), lambda i: (i, 0, 0)),
            pl.BlockSpec((bb, win, kvw // 2), lambda i: (i, 0, 0)),
            pl.BlockSpec(wa.shape, lambda i: (0, 0, 0)),
            pl.BlockSpec(wb.shape, lambda i: (0, 0)),
        ],
        out_specs=[
            pl.BlockSpec((bb, T_PAD, D_MODEL), lambda i: (i, 0, 0)),
            pl.BlockSpec((bb, T_PAD, A_WIDTH), lambda i: (i, 0, 0)),
        ],
        out_shape=[
            jax.ShapeDtypeStruct((nb, T_PAD, D_MODEL), F32),
            jax.ShapeDtypeStruct((nb, T_PAD, A_WIDTH), F32),
        ],
        compiler_params=_params(("parallel",)), name="even_sample",
    )(sink, z3, z3, z3, z3, kbuf, vbuf, wa, wb)


def _rope(x, cos_f, sin_s):
    width = x.shape[-1]
    half = C_DK // 2
    lane = lax.broadcasted_iota(jnp.int32, x.shape, 1) % C_DK
    swapped = jnp.where(lane < half, pltpu.roll(x, width - half, 1), pltpu.roll(x, half, 1))
    return x * cos_f + swapped * sin_s


def _retention_chunk(qh, kh, vh, st, decay, qdec, kdec, sdec):
    qb, vb = qh.astype(BF16), vh.astype(BF16)
    scores = _dot_nt(qb, kh.astype(BF16)) * decay
    o = jnp.dot(scores.astype(BF16), vb, preferred_element_type=F32)
    o = o + jnp.dot(qb, st.astype(BF16), preferred_element_type=F32) * qdec
    new_st = sdec * st + _dot_tn((kh * kdec).astype(BF16), vb)
    return o, new_st


def _groupnorm_gate(o, gain, gate):
    mu = jnp.mean(o, axis=-1, keepdims=True)
    var = jnp.mean(jnp.square(o - mu), axis=-1, keepdims=True)
    return (o - mu) * lax.rsqrt(var + EPS) * gain * (gate * jax.nn.sigmoid(gate))


def _retention_body(qk_ref, v_ref, g_ref, cos_ref, sin_ref, decay_ref, qdec_ref, kdec_ref, sdec_ref,
                    gain_ref, o_ref, st_ref, state):
    @pl.when(pl.program_id(0) == 0)
    def _():
        state[...] = jnp.zeros_like(state)

    qk = qk_ref[...]
    width = C_HEADS * C_DK
    q = _rope(qk[:, :width], cos_ref[...], sin_ref[...]) * (C_DK ** -0.5)
    k = _rope(qk[:, width:], cos_ref[...], sin_ref[...])
    v = v_ref[...]
    g = g_ref[...]
    for h in range(C_HEADS):
        qh = q[:, h * C_DK:(h + 1) * C_DK]
        kh = k[:, h * C_DK:(h + 1) * C_DK]
        vh = v[:, h * C_DV:(h + 1) * C_DV]
        st = state[h]
        o, new_st = _retention_chunk(qh, kh, vh, st, decay_ref[h], qdec_ref[h], kdec_ref[h], sdec_ref[h])
        state[h] = new_st
        st_ref[h] = new_st
        lanes = slice(h * C_DV, (h + 1) * C_DV)
        o_ref[:, lanes] = _groupnorm_gate(o, gain_ref[:, lanes], g[:, lanes])


def _retention_prompt(z, cos_f, sin_s, tabs, gain):
    t = z.shape[0]
    nc = t // CHUNK
    decay, qdec, kdec, sdec = tabs
    width = C_HEADS * C_DV
    const3 = lambda i: (0, 0, 0)
    return pl.pallas_call(
        _retention_body, grid=(nc,),
        in_specs=[
            pl.BlockSpec((CHUNK, width), lambda i: (i, 0)),
            pl.BlockSpec((CHUNK, width), lambda i: (i, 1)),
            pl.BlockSpec((CHUNK, width), lambda i: (i, 2)),
            pl.BlockSpec((CHUNK, C_HEADS * C_DK), lambda i: (i, 0)),
            pl.BlockSpec((CHUNK, C_HEADS * C_DK), lambda i: (i, 0)),
            pl.BlockSpec(decay.shape, const3),
            pl.BlockSpec(qdec.shape, const3),
            pl.BlockSpec(kdec.shape, const3),
            pl.BlockSpec(sdec.shape, const3),
            pl.BlockSpec((1, width), lambda i: (0, 0)),
        ],
        out_specs=[
            pl.BlockSpec((CHUNK, width), lambda i: (i, 0)),
            pl.BlockSpec((C_HEADS, C_DK, C_DV), const3),
        ],
        out_shape=[
            jax.ShapeDtypeStruct((t, width), F32),
            jax.ShapeDtypeStruct((C_HEADS, C_DK, C_DV), F32),
        ],
        scratch_shapes=[pltpu.VMEM((C_HEADS, C_DK, C_DV), F32)],
        compiler_params=_params(("arbitrary",)), name="retention_prompt",
    )(z, z, z, cos_f, sin_s, decay, qdec, kdec, sdec, gain.reshape(1, width))


RET_SAMPLE_BATCH = 8


def _retention_sample_body(qk_ref, v_ref, g_ref, cos_ref, sin_ref, decay_ref, qdec_ref, kdec_ref,
                           sdec_ref, gain_ref, st_in_ref, o_ref, st_ref):
    width = C_HEADS * C_DK
    lane_head = lax.broadcasted_iota(jnp.int32, (T_PAD, width), 1) // C_DK
    for b in range(RET_SAMPLE_BATCH):
        qk = qk_ref[b]
        q = _rope(qk[:, :width], cos_ref[...], sin_ref[...]) * (C_DK ** -0.5)
        k = _rope(qk[:, width:], cos_ref[...], sin_ref[...])
        v = v_ref[b]
        g = g_ref[b]
        vb = v.astype(BF16)
        qbd = jnp.concatenate([jnp.where(lane_head == h, q, 0.0) for h in range(C_HEADS)],
                              axis=0).astype(BF16)
        st = st_in_ref[b]
        scores = _dot_nt(qbd, k.astype(BF16)) * decay_ref[...]
        o_intra = jnp.dot(scores.astype(BF16), vb, preferred_element_type=F32)
        o_cross = jnp.dot(qbd, st.astype(BF16), preferred_element_type=F32) * qdec_ref[...]
        kd = (k * kdec_ref[...]).astype(BF16)
        for h in range(C_HEADS):
            rows = slice(h * T_PAD, (h + 1) * T_PAD)
            lanes = slice(h * C_DV, (h + 1) * C_DV)
            keys = slice(h * C_DK, (h + 1) * C_DK)
            o = o_intra[rows, lanes] + o_cross[rows, :]
            o_ref[b, :, lanes] = _groupnorm_gate(o, gain_ref[:, lanes], g[:, lanes])
            st_ref[b, keys, :] = sdec_ref[keys, :] * st[keys, :] + _dot_tn(kd[:, keys], vb[:, lanes])


def _retention_sample(z3, cos_f, sin_s, tabs, gain, st_in):
    nb = z3.shape[0]
    bb = RET_SAMPLE_BATCH
    decay, qdec, kdec, sdec = tabs
    width = C_HEADS * C_DV
    decay_s = decay.reshape(C_HEADS * T_PAD, T_PAD)
    qdec_s = qdec.reshape(C_HEADS * T_PAD, C_DV)
    kdec_s = kdec.transpose(1, 0, 2).reshape(T_PAD, C_HEADS * C_DK)
    sdec_s = sdec.reshape(C_HEADS * C_DK, C_DV)
    st2 = st_in.reshape(nb, C_HEADS * C_DK, C_DV)
    c2 = lambda i: (0, 0)
    out, st_out = pl.pallas_call(
        _retention_sample_body, grid=(nb // bb,),
        in_specs=[
            pl.BlockSpec((bb, T_PAD, width), lambda i: (i, 0, 0)),
            pl.BlockSpec((bb, T_PAD, width), lambda i: (i, 0, 1)),
            pl.BlockSpec((bb, T_PAD, width), lambda i: (i, 0, 2)),
            pl.BlockSpec((T_PAD, C_HEADS * C_DK), c2),
            pl.BlockSpec((T_PAD, C_HEADS * C_DK), c2),
            pl.BlockSpec(decay_s.shape, c2),
            pl.BlockSpec(qdec_s.shape, c2),
            pl.BlockSpec(kdec_s.shape, c2),
            pl.BlockSpec(sdec_s.shape, c2),
            pl.BlockSpec((1, width), c2),
            pl.BlockSpec((bb, C_HEADS * C_DK, C_DV), lambda i: (i, 0, 0)),
        ],
        out_specs=[
            pl.BlockSpec((bb, T_PAD, width), lambda i: (i, 0, 0)),
            pl.BlockSpec((bb, C_HEADS * C_DK, C_DV), lambda i: (i, 0, 0)),
        ],
        out_shape=[
            jax.ShapeDtypeStruct((nb, T_PAD, width), F32),
            jax.ShapeDtypeStruct((nb, C_HEADS * C_DK, C_DV), F32),
        ],
        compiler_params=_params(("parallel",)), name="retention_sample",
    )(z3, z3, z3, cos_f, sin_s, decay_s, qdec_s, kdec_s, sdec_s, gain.reshape(1, width), st2)
    return out, st_out.reshape(st_in.shape)


def _retention_tables(length, n_valid):
    log_gamma = jnp.log1p(-jnp.exp2(-RET_DECAY_EXP0 - jnp.arange(C_HEADS, dtype=F32)))
    i = jnp.arange(length, dtype=F32)
    valid = (jnp.arange(length) < n_valid)
    rel = i[:, None] - i[None, :]
    decay = jnp.where(rel >= 0, jnp.exp(jnp.maximum(rel, 0.0)[None] * log_gamma[:, None, None]), 0.0)
    decay = jnp.where(valid[None, None, :], decay, 0.0)
    q_dec = jnp.exp((i[None, :] + 1.0) * log_gamma[:, None])
    k_dec = jnp.where(valid[None, :], jnp.exp((n_valid - 1.0 - i)[None, :] * log_gamma[:, None]), 0.0)
    s_dec = jnp.exp(n_valid * log_gamma)
    qdec = jnp.broadcast_to(q_dec[:, :, None], (C_HEADS, length, C_DV))
    kdec = jnp.broadcast_to(k_dec[:, :, None], (C_HEADS, length, C_DK))
    sdec = jnp.broadcast_to(s_dec[:, None, None], (C_HEADS, C_DK, C_DV))
    return decay, qdec, kdec, sdec


def _rope_tables(pos):
    half = C_DK // 2
    freqs = ROPE_BASE ** (-jnp.arange(half, dtype=F32) / half)
    ang = pos.astype(F32)[:, None] * freqs[None, :]
    cos, sin = jnp.cos(ang), jnp.sin(ang)
    cos_f = jnp.tile(jnp.concatenate([cos, cos], axis=1), (1, C_HEADS))
    sin_s = jnp.tile(jnp.concatenate([-sin, sin], axis=1), (1, C_HEADS))
    return cos_f, sin_s


def _s5_discretize(are_ref, aim_ref, ldt_ref):
    a_re, a_im = are_ref[...], aim_ref[...]
    dt = jnp.exp(ldt_ref[...])
    mag = jnp.exp(a_re * dt)
    ab_re = mag * jnp.cos(a_im * dt)
    ab_im = mag * jnp.sin(a_im * dt)
    num_re, num_im = ab_re - 1.0, ab_im
    den = a_re * a_re + a_im * a_im
    co_re = (num_re * a_re + num_im * a_im) / den
    co_im = (num_im * a_re - num_re * a_im) / den
    return ab_re, ab_im, co_re, co_im


S5_CHUNK = 256


def _s5_prompt_body(u_ref, are_ref, aim_ref, ldt_ref, bre_ref, bim_ref, cre_ref, cim_ref, dd_ref,
                    gw_ref, gb_ref, y_ref, st_ref, hre, him, state, disc):
    n = S5_CHUNK

    @pl.when(pl.program_id(0) == 0)
    def _():
        ab_re, ab_im, co_re, co_im = _s5_discretize(are_ref, aim_ref, ldt_ref)
        disc[0] = ab_re
        disc[1] = ab_im
        disc[2] = co_re
        disc[3] = co_im
        state[...] = jnp.zeros_like(state)

    n_chunks = D_ROW_STATE // LANES
    for s in range(D_ROWS):
        rows = pl.ds(s, n, stride=D_ROWS)
        u_s = u_ref[:, s * D_ROW_CH:(s + 1) * D_ROW_CH]
        u_b = u_s.astype(BF16)
        raw_re = jnp.dot(u_b, bre_ref[s], preferred_element_type=F32)
        raw_im = jnp.dot(u_b, bim_ref[s], preferred_element_type=F32)
        co_re = disc[2, s:s + 1, :]
        co_im = disc[3, s:s + 1, :]
        bu_re = co_re * raw_re - co_im * raw_im
        bu_im = co_re * raw_im + co_im * raw_re
        for c in range(n_chunks):
            hre[c, rows, :] = bu_re[:, c * LANES:(c + 1) * LANES]
            him[c, rows, :] = bu_im[:, c * LANES:(c + 1) * LANES]

    ab_re = disc[0]
    ab_im = disc[1]

    def step(t, carry):
        h_re, h_im = carry
        rows = pl.ds(pl.multiple_of(t * D_ROWS, D_ROWS), D_ROWS)
        bu_re = jnp.concatenate([hre[c, rows, :] for c in range(n_chunks)], axis=1)
        bu_im = jnp.concatenate([him[c, rows, :] for c in range(n_chunks)], axis=1)
        n_re = ab_re * h_re - ab_im * h_im + bu_re
        n_im = ab_re * h_im + ab_im * h_re + bu_im
        for c in range(n_chunks):
            hre[c, rows, :] = n_re[:, c * LANES:(c + 1) * LANES]
            him[c, rows, :] = n_im[:, c * LANES:(c + 1) * LANES]
        return n_re, n_im

    h_re, h_im = lax.fori_loop(0, n, step, (state[0], state[1]), unroll=4)
    state[0] = h_re
    state[1] = h_im
    st_ref[0] = h_re
    st_ref[1] = h_im

    for s in range(D_ROWS):
        rows = pl.ds(s, n, stride=D_ROWS)
        hist_re = jnp.concatenate([hre[c, rows, :] for c in range(n_chunks)], axis=1)
        hist_im = jnp.concatenate([him[c, rows, :] for c in range(n_chunks)], axis=1)
        cols = slice(s * D_ROW_CH, (s + 1) * D_ROW_CH)
        y = (jnp.dot(hist_re.astype(BF16), cre_ref[s], preferred_element_type=F32)
             - jnp.dot(hist_im.astype(BF16), cim_ref[s], preferred_element_type=F32)
             + dd_ref[s:s + 1, :] * u_ref[:, cols])
        y_ref[:, cols] = _gelu(y)

    yd = y_ref[...]
    gate = jnp.dot(yd.astype(BF16), gw_ref[...], preferred_element_type=F32) + gb_ref[...]
    y_ref[...] = yd * jax.nn.sigmoid(gate)


def _s5_prompt(z, u_col_block, par, glu_w, glu_b):
    a_re, a_im, ldt, b_re, b_im, c_re, c_im, dd = par
    t = z.shape[0]
    blk = S5_CHUNK * D_ROWS
    width = D_ROWS * D_ROW_CH
    c2 = lambda i: (0, 0)
    c3 = lambda i: (0, 0, 0)
    return pl.pallas_call(
        _s5_prompt_body, grid=(t // S5_CHUNK,),
        in_specs=[
            pl.BlockSpec((S5_CHUNK, width), lambda i: (i, u_col_block)),
            pl.BlockSpec(a_re.shape, c2), pl.BlockSpec(a_im.shape, c2), pl.BlockSpec(ldt.shape, c2),
            pl.BlockSpec(b_re.shape, c3), pl.BlockSpec(b_im.shape, c3),
            pl.BlockSpec(c_re.shape, c3), pl.BlockSpec(c_im.shape, c3),
            pl.BlockSpec(dd.shape, c2),
            pl.BlockSpec(glu_w.shape, c2, pipeline_mode=pl.Buffered(1)),
            pl.BlockSpec((1, width), c2),
        ],
        out_specs=[
            pl.BlockSpec((S5_CHUNK, width), lambda i: (i, 0)),
            pl.BlockSpec((2, D_ROWS, D_ROW_STATE), c3),
        ],
        out_shape=[
            jax.ShapeDtypeStruct((t, width), F32),
            jax.ShapeDtypeStruct((2, D_ROWS, D_ROW_STATE), F32),
        ],
        scratch_shapes=[
            pltpu.VMEM((D_ROW_STATE // LANES, blk, LANES), F32),
            pltpu.VMEM((D_ROW_STATE // LANES, blk, LANES), F32),
            pltpu.VMEM((2, D_ROWS, D_ROW_STATE), F32), pltpu.VMEM((4, D_ROWS, D_ROW_STATE), F32),
        ],
        compiler_params=_params(("arbitrary",)), name="s5_prompt",
    )(z, a_re, a_im, ldt, b_re.astype(BF16), b_im.astype(BF16), c_re.astype(BF16), c_im.astype(BF16), dd,
      glu_w, glu_b.reshape(1, width))


def _s5_sample_body(u_ref, xre_ref, xim_ref, are_ref, aim_ref, ldt_ref, bre_ref, bim_ref, cre_ref,
                    cim_ref, dd_ref, y_ref, sre_ref, sim_ref):
    ab_re, ab_im, co_re, co_im = _s5_discretize(are_ref, aim_ref, ldt_ref)
    n_t = u_ref.shape[0]
    for s in range(D_ROWS):
        a_r, a_i = ab_re[s:s + 1, :], ab_im[s:s + 1, :]
        c_r, c_i = co_re[s:s + 1, :], co_im[s:s + 1, :]
        h_re, h_im = xre_ref[s], xim_ref[s]
        for t in range(n_t):
            u = u_ref[t, s]
            raw_re = jnp.dot(u, bre_ref[s], precision=HIGHEST, preferred_element_type=F32)
            raw_im = jnp.dot(u, bim_ref[s], precision=HIGHEST, preferred_element_type=F32)
            bu_re = c_r * raw_re - c_i * raw_im
            bu_im = c_r * raw_im + c_i * raw_re
            h_re, h_im = a_r * h_re - a_i * h_im + bu_re, a_r * h_im + a_i * h_re + bu_im
            y = (jnp.dot(h_re, cre_ref[s], precision=HIGHEST, preferred_element_type=F32)
                 - jnp.dot(h_im, cim_ref[s], precision=HIGHEST, preferred_element_type=F32)
                 + dd_ref[s:s + 1, :] * u)
            y_ref[t, s] = _gelu(y)
        sre_ref[s] = h_re
        sim_ref[s] = h_im


def _s5_sample(u_ts, x_re, x_im, par):
    a_re, a_im, ldt, b_re, b_im, c_re, c_im, dd = par
    return pl.pallas_call(
        _s5_sample_body,
        out_shape=[
            jax.ShapeDtypeStruct(u_ts.shape, F32),
            jax.ShapeDtypeStruct(x_re.shape, F32),
            jax.ShapeDtypeStruct(x_im.shape, F32),
        ],
        compiler_params=pltpu.CompilerParams(vmem_limit_bytes=VMEM_LIMIT), name="s5_sample",
    )(u_ts, x_re, x_im, a_re, a_im, ldt, b_re, b_im, c_re, c_im, dd)


def _s5_params(a_re, a_im, log_dt, b_re, b_im, c_re, c_im, dd):
    eye = jnp.eye(D_ROWS, dtype=F32)

    def rows(a):
        return a.reshape(D_ROWS, D_ROW_STATE)

    def b_blocks(b):
        b4 = b.reshape(D_ROWS, D_ROWS, D_STATE, D_GROUP_CH).transpose(0, 1, 3, 2)
        return jnp.einsum('sgcp,gh->sgchp', b4, eye).reshape(D_ROWS, D_ROW_CH, D_ROW_STATE)

    def c_blocks(c):
        c4 = c.reshape(D_ROWS, D_ROWS, D_GROUP_CH, D_STATE)
        return jnp.einsum('sgcp,gh->sgphc', c4, eye).reshape(D_ROWS, D_ROW_STATE, D_ROW_CH)

    ldt = jnp.broadcast_to(log_dt[:, None], (D_GROUPS, D_STATE))
    return (rows(a_re), rows(a_im), rows(ldt), b_blocks(b_re), b_blocks(b_im),
            c_blocks(c_re), c_blocks(c_im), dd.reshape(D_ROWS, D_ROW_CH))


def _glu_body(x_ref, w_ref, b_ref, o_ref):
    x = x_ref[...]
    gate = jnp.dot(x.astype(BF16), w_ref[...], preferred_element_type=F32) + b_ref[...]
    o_ref[...] = x * jax.nn.sigmoid(gate)


def _glu(x, w, b, tm=512):
    m, k = x.shape
    return pl.pallas_call(
        _glu_body, grid=(m // tm,),
        in_specs=[pl.BlockSpec((tm, k), lambda i: (i, 0)), pl.BlockSpec((k, k), lambda i: (0, 0)),
                  pl.BlockSpec((1, k), lambda i: (0, 0))],
        out_specs=pl.BlockSpec((tm, k), lambda i: (i, 0)),
        out_shape=jax.ShapeDtypeStruct((m, k), F32),
        compiler_params=_params(("parallel",)), name="glu",
    )(x, w, b.reshape(1, k))


PEER_SEL_TOKENS = 256


def _extract_top(s, n_out, exact, want_rank=False):
    n_rows = s.shape[0]
    rid = lax.broadcasted_iota(jnp.int32, s.shape, 0).astype(F32) if exact else None
    rank = jnp.full(s.shape, float(n_out), F32) if want_rank else None
    vals = []
    for r in range(n_out):
        m = jnp.max(s, axis=0, keepdims=True)
        if exact:
            first = jnp.min(jnp.where(s == m, rid, float(n_rows)), axis=0, keepdims=True)
            hit = rid == first
        else:
            hit = s == m
        if want_rank:
            rank = jnp.where(hit, float(r), rank)
        s = jnp.where(hit, NEG_INF, s)
        vals.append(m)
    return vals, s, rank


def _removed_count(s):
    return jnp.sum(jnp.where(s == NEG_INF, 1.0, 0.0), axis=0, keepdims=True)


_CAND_COUNTS = [PEER_TOPK // (a + 1) for a in range(PEER_TOPK)]


def _candidates(v1, v2):
    v1_all = _stack_rows(v1)
    v2_all = _stack_rows(v2)
    rid = lax.broadcasted_iota(jnp.int32, (SUBLANES, v1[0].shape[1]), 0)
    tiles = [v1[0] + v2_all]
    n_pad = 0
    a = 1
    while _CAND_COUNTS[a] > 1:
        tiles.append(jnp.where(rid < _CAND_COUNTS[a], v1[a] + v2_all[:SUBLANES], NEG_INF))
        n_pad += SUBLANES - _CAND_COUNTS[a]
        a += 1
    assert PEER_TOPK - a == SUBLANES
    tiles.append(v1_all[a:] + v2[0])
    return jnp.concatenate(tiles, axis=0), n_pad


def _stack_rows(rows):
    n = len(rows)
    rid = lax.broadcasted_iota(jnp.int32, (n, rows[0].shape[1]), 0)
    out = jnp.broadcast_to(rows[n - 1], rid.shape)
    for i in range(n - 2, -1, -1):
        out = jnp.where(rid == i, rows[i], out)
    return out


def _selected_per_first_key(removed, tag, tags):
    counts = [jnp.sum(removed[:PEER_TOPK], axis=0, keepdims=True)]
    row = PEER_TOPK
    a = 1
    while _CAND_COUNTS[a] > 1:
        counts.append(jnp.sum(removed[row:row + SUBLANES], axis=0, keepdims=True))
        row += SUBLANES
        a += 1
    for i in range(PEER_TOPK - a):
        counts.append(removed[row + i:row + i + 1])
    n_sel = jnp.zeros(tag.shape, F32)
    for a, cnt in enumerate(counts):
        n_sel = jnp.where(tag == tags[a], cnt, n_sel)
    return n_sel


def _peer_select_body(q_ref, k1_ref, k2_ref, ns_ref, g1_ref, r2_ref, e2_ref):
    def head(h, exact):
        q1 = q_ref[:, (2 * h) * PEER_HALF:(2 * h + 1) * PEER_HALF]
        q2 = q_ref[:, (2 * h + 1) * PEER_HALF:(2 * h + 2) * PEER_HALF]
        s1 = _dot_nt(k1_ref[h], q1, HIGHEST)
        s2 = _dot_nt(k2_ref[h], q2, HIGHEST)
        v1, left1, rank1 = _extract_top(s1, PEER_TOPK, exact, want_rank=exact)
        v2, left2, rank2 = _extract_top(s2, PEER_TOPK, exact, want_rank=True)
        tag1, tags1 = (rank1, [float(a) for a in range(PEER_TOPK)]) if exact else (s1, v1)
        cand, n_pad = _candidates(v1, v2)
        top, left_c, _ = _extract_top(cand, PEER_TOPK, exact)
        mx = top[0]
        z = jnp.exp(top[0] - mx)
        for kk in range(1, PEER_TOPK):
            z = z + jnp.exp(top[kk] - mx)
        g1 = jnp.where(left1 == NEG_INF, jnp.exp(s1 - v1[0]), 0.0) / z
        e2 = jnp.where(rank2 < PEER_TOPK, jnp.exp(s2 - v2[0]), 0.0)
        removed = jnp.where((left_c == NEG_INF) & (cand != NEG_INF), 1.0, 0.0)
        n_sel = _selected_per_first_key(removed, tag1, tags1)
        for j in range(PEER_SEL_TOKENS // LANES):
            lanes = slice(j * LANES, (j + 1) * LANES)
            ns_ref[j, h] = n_sel[:, lanes]
            g1_ref[j, h] = g1[:, lanes]
            r2_ref[j, h] = rank2[:, lanes]
            e2_ref[j, h] = e2[:, lanes]
        if exact:
            return None
        ok = ((_removed_count(left1) == PEER_TOPK) & (_removed_count(left2) == PEER_TOPK)
              & (_removed_count(left_c) == PEER_TOPK + n_pad))
        return jnp.where(ok, 0.0, 1.0)

    repeated = [jnp.max(head(h, False)) for h in range(PEER_HEADS)]
    for h in range(PEER_HEADS):
        @pl.when(repeated[h] > 0.0)
        def _():
            head(h, True)


def _peer_select(q, k1, k2):
    t = q.shape[0]
    tb = PEER_SEL_TOKENS
    nj = tb // LANES
    big = pl.BlockSpec((nj, PEER_HEADS, PEER_NKEYS, LANES), lambda i: (i, 0, 0, 0))
    shape = jax.ShapeDtypeStruct((t // LANES, PEER_HEADS, PEER_NKEYS, LANES), F32)
    return pl.pallas_call(
        _peer_select_body, grid=(t // tb,),
        in_specs=[
            pl.BlockSpec((tb, q.shape[1]), lambda i: (i, 0)),
            pl.BlockSpec(k1.shape, lambda i: (0, 0, 0)),
            pl.BlockSpec(k2.shape, lambda i: (0, 0, 0)),
        ],
        out_specs=[big, big, big, big],
        out_shape=[shape, shape, shape, shape],
        compiler_params=_params(("parallel",)), name="peer_select",
    )(q, k1, k2)


PEER_TOKENS = 512
PEER_EXPERT_TILE = 512
BF16_ROWS = 16
GATE_GROUP = 4


def _peer_dense_body(h_ref, g_ref, u_ref, v_ref, ns_ref, g1_ref, r2_ref, e2_ref, og_ref, o_ref, *rest,
                     norm_out, emit_bf16):
    if emit_bf16:
        ub_ref, vb_ref, xn, act_s, coef_s, acc = rest
    else:
        xn, act_s, coef_s, acc = rest
    e = pl.program_id(1)
    n_chunks = PEER_TOKENS // LANES

    @pl.when(e == 0)
    def _():
        xn[...] = _rmsnorm(h_ref[...], g_ref[...]).T.astype(BF16)
        acc[...] = jnp.zeros_like(acc)

    te = PEER_EXPERT_TILE
    slabs = te // PEER_NKEYS

    def one_tile(part):
        tile_rows = slice(part * te, (part + 1) * te)
        u_tile = u_ref[tile_rows, :].astype(BF16)
        if emit_bf16:
            ub_ref[tile_rows, :] = u_tile
        act = jnp.dot(u_tile, xn[...], preferred_element_type=F32)
        for j in range(n_chunks):
            act_s[j] = act[:, j * LANES:(j + 1) * LANES]

        def chunk(j, carry):
            packed = (PEER_NKEYS // BF16_ROWS, BF16_ROWS, LANES)
            for c0 in range(0, slabs, GATE_GROUP):
                gates = [jnp.zeros(packed, BF16) for _ in range(GATE_GROUP)]
                for h in range(PEER_HEADS):
                    r2 = r2_ref[j, h].reshape(packed).astype(BF16)
                    e2 = e2_ref[j, h].reshape(packed).astype(BF16)
                    for k in range(GATE_GROUP):
                        key1 = part * slabs + c0 + k
                        ns = jnp.broadcast_to(ns_ref[j, h, key1:key1 + 1, :], packed[1:]).astype(BF16)
                        g1 = jnp.broadcast_to(g1_ref[j, h, key1:key1 + 1, :], packed[1:]).astype(BF16)
                        gates[k] = gates[k] + jnp.where(r2 < ns[None], e2 * g1[None], jnp.zeros((), BF16))
                for k in range(GATE_GROUP):
                    rows = slice((c0 + k) * PEER_NKEYS, (c0 + k + 1) * PEER_NKEYS)
                    act = _gelu(act_s[j, rows, :]).reshape(packed).astype(BF16)
                    coef_s[j, rows, :] = (gates[k] * act).reshape(PEER_NKEYS, LANES)
            return carry

        for j in range(n_chunks):
            chunk(j, 0)
        coef = jnp.concatenate([coef_s[j] for j in range(n_chunks)], axis=1)
        v_tile = v_ref[tile_rows, :].astype(BF16)
        if emit_bf16:
            vb_ref[tile_rows, :] = v_tile
        acc[...] += _dot_tn(coef, v_tile)

    for part in range(u_ref.shape[0] // te):
        one_tile(part)

    @pl.when(e == pl.num_programs(1) - 1)
    def _():
        out = h_ref[...] + acc[...]
        o_ref[...] = _rmsnorm(out, og_ref[...]) if norm_out else out


def _peer_dense(h, g, u_tab, v_tab, layer, sel, out_g, norm_out):
    ns, g1, r2, e2 = sel
    t, d = h.shape
    emit_bf16 = u_tab.ndim == 3
    n_exp = u_tab.shape[-2]
    tb = PEER_TOKENS
    nj = tb // LANES
    slabs = PEER_EXPERT_TILE // PEER_NKEYS
    o_spec = pl.BlockSpec((tb, d), lambda i, e: (i, 0))
    o_shape = jax.ShapeDtypeStruct((t, d), F32)
    if emit_bf16:
        assert t == tb, "every table tile must be visited exactly once when the casts are emitted"
        te = PEER_EXPERT_TILE
        n_steps = n_exp // te
        tab_spec = pl.BlockSpec((None, te, d), lambda i, e: (layer, e, 0))

        def by_tile(a):
            a5 = a.reshape(a.shape[0], PEER_HEADS, n_steps, slabs, LANES)
            return a5.transpose(2, 0, 1, 3, 4)

        ns, g1 = by_tile(ns), by_tile(g1)
        key_spec = pl.BlockSpec((None, nj, PEER_HEADS, slabs, LANES), lambda i, e: (e, i, 0, 0, 0))
    else:
        te = SUBLANES // slabs * PEER_EXPERT_TILE
        n_steps = n_exp // te
        tab_spec = pl.BlockSpec((te, d), lambda i, e: (e, 0))
        key_spec = pl.BlockSpec((nj, PEER_HEADS, SUBLANES, LANES), lambda i, e: (i, 0, e, 0))
    cast_spec = pl.BlockSpec((te, d), lambda i, e: (e, 0))
    cast_shape = jax.ShapeDtypeStruct((n_exp, d), BF16)
    tok_spec = pl.BlockSpec((nj, PEER_HEADS, PEER_NKEYS, LANES), lambda i, e: (i, 0, 0, 0))
    te_scratch = PEER_EXPERT_TILE
    return pl.pallas_call(
        functools.partial(_peer_dense_body, norm_out=norm_out, emit_bf16=emit_bf16),
        grid=(t // tb, n_steps),
        in_specs=[
            pl.BlockSpec((tb, d), lambda i, e: (i, 0)),
            pl.BlockSpec((1, d), lambda i, e: (0, 0)),
            tab_spec, tab_spec,
            key_spec, key_spec, tok_spec, tok_spec,
            pl.BlockSpec((1, d), lambda i, e: (0, 0)),
        ],
        out_specs=[o_spec, cast_spec, cast_spec] if emit_bf16 else o_spec,
        out_shape=[o_shape, cast_shape, cast_shape] if emit_bf16 else o_shape,
        scratch_shapes=[pltpu.VMEM((d, tb), BF16), pltpu.VMEM((nj, te_scratch, LANES), F32),
                        pltpu.VMEM((nj, te_scratch, LANES), BF16), pltpu.VMEM((tb, d), F32)],
        compiler_params=_params(("parallel", "arbitrary")), name="peer_dense",
    )(h, g.reshape(1, d), u_tab, v_tab, ns, g1, r2, e2, out_g.reshape(1, d))


def _peer(h, g, wq, k1, k2, u_tab, v_tab, layer, out_g, norm_out):
    q = _matmul(h, wq, norm_g=g)
    return _peer_dense(h, g, u_tab, v_tab, layer, _peer_select(q, k1, k2), out_g, norm_out)


def kernel(x_prompt, x_sample, state_b_k, state_b_v, state_c_s, state_d_re, state_d_im, norm1_g, norm2_g, final_g, w_in_even, w_out_even, a_ws, a_bs, b_sink, w_in_odd, w_out_odd, c_norm_g, d_a_re, d_a_im, d_log_dt, d_b_re, d_b_im, d_c_re, d_c_im, d_d, d_glu_w, d_glu_b, peer_wq, peer_k1, peer_k2, peer_u, peer_v):
    seq = x_prompt.shape[1]
    n_batch, n_new = x_sample.shape[:2]
    past = PAST_LEN
    hp = x_prompt.reshape(seq, D_MODEL)
    hs = x_sample.reshape(n_batch * n_new, D_MODEL)
    bf = lambda a: a.astype(BF16)

    def pad_tokens(a):
        a3 = a.reshape(n_batch, n_new, a.shape[-1])
        return jnp.pad(a3, ((0, 0), (0, T_PAD - n_new), (0, 0)))

    def unpad_tokens(a3):
        return a3[:, :n_new].reshape(n_batch * n_new, a3.shape[-1])

    w_in = bf(w_in_even[0])
    w_out = bf(w_out_even[0])
    zp = _matmul(hp, w_in, norm_g=norm1_g[0])
    zs = _matmul(hs, w_in, norm_g=norm1_g[0])
    bs_full = jnp.broadcast_to(a_bs[0][:, :, None], (A_GROUPS, CHUNK, LANES))
    yp = _even_prompt(zp, a_ws[0], bs_full, b_sink[0])
    ws_small = jnp.tril(a_ws[0][:, :n_new, :n_new]).transpose(2, 1, 0)
    wa = jnp.repeat(jnp.pad(ws_small, ((0, 0), (0, T_PAD - n_new), (0, 0))), LANES, axis=-1)
    wb = jnp.repeat(jnp.pad(a_bs[0][:, :n_new].T, ((0, T_PAD - n_new), (0, 0))), LANES, axis=-1)
    win = state_b_k.shape[2]
    kbuf = state_b_k[0].reshape(n_batch, win, B_KV_HEADS * B_DH)
    vbuf = state_b_v[0].reshape(n_batch, win, B_KV_HEADS * B_DH)
    ys3, av3 = _even_sample(pad_tokens(zs), kbuf, vbuf, wa, wb, b_sink[0])
    hp = _matmul(yp, w_out, resid=hp)
    hs = _matmul(unpad_tokens(ys3), w_out, resid=hs)

    k_off = 2 * A_WIDTH + B_HEADS * B_DH
    v_off = k_off + B_KV_HEADS * B_DH
    kv_shape = (1, -1, win, B_KV_HEADS, B_DH)
    a_v_sample = av3[:, :n_new].reshape(1, n_batch, n_new, A_GROUPS, A_WIDTH // A_GROUPS)
    b_k_prompt = zp[seq - win:, k_off:v_off].reshape(kv_shape)
    b_v_prompt = zp[seq - win:, v_off:].reshape(kv_shape)
    k_new = zs[:, k_off:v_off].reshape(n_batch, n_new, B_KV_HEADS * B_DH)
    v_new = zs[:, v_off:].reshape(n_batch, n_new, B_KV_HEADS * B_DH)
    b_k_sample = jnp.concatenate([kbuf, k_new], axis=1)[:, -win:].reshape(kv_shape)
    b_v_sample = jnp.concatenate([vbuf, v_new], axis=1)[:, -win:].reshape(kv_shape)

    wq = bf(peer_wq[0])
    hs, u_b, v_b = _peer(hs, norm2_g[0], wq, peer_k1[0], peer_k2[0], peer_u, peer_v, 0, final_g, False)
    hp = _peer(hp, norm2_g[0], wq, peer_k1[0], peer_k2[0], u_b, v_b, 0, final_g, False)

    w_in = bf(w_in_odd[0])
    w_out = bf(w_out_odd[0])
    c_width = C_HEADS * C_DV
    zp = _matmul(hp, w_in, norm_g=norm1_g[1])
    zs = _matmul(hs, w_in, norm_g=norm1_g[1])
    cos_p, sin_p = _rope_tables(jnp.arange(seq))
    cos_s, sin_s = _rope_tables(past + jnp.arange(T_PAD))
    ycp, c_s_prompt = _retention_prompt(zp, cos_p, sin_p, _retention_tables(CHUNK, CHUNK), c_norm_g[0])
    zs3 = pad_tokens(zs)
    ycs3, c_s_sample = _retention_sample(zs3, cos_s, sin_s, _retention_tables(T_PAD, n_new), c_norm_g[0],
                                         state_c_s[0])
    par = _s5_params(d_a_re[0], d_a_im[0], d_log_dt[0], d_b_re[0], d_b_im[0], d_c_re[0], d_c_im[0], d_d[0])
    u_off = 3 * c_width
    glu_w = bf(d_glu_w[0])
    ydp, d_prompt = _s5_prompt(zp, u_off // c_width, par, glu_w, d_glu_b[0])
    us_ts = zs[:, u_off:].reshape(n_batch, n_new, D_ROWS, D_ROW_CH).transpose(1, 2, 0, 3)
    x_re = state_d_re[0].reshape(n_batch, D_ROWS, D_ROW_STATE).transpose(1, 0, 2)
    x_im = state_d_im[0].reshape(n_batch, D_ROWS, D_ROW_STATE).transpose(1, 0, 2)
    yds_ts, s_re, s_im = _s5_sample(us_ts, x_re, x_im, par)
    yds = _glu(yds_ts.transpose(2, 0, 1, 3).reshape(n_batch * n_new, c_width), glu_w, d_glu_b[0])
    hp = _matmul(ycp, w_out[:c_width], x2=ydp, w2=w_out[c_width:], resid=hp)
    hs = _matmul(unpad_tokens(ycs3), w_out[:c_width], x2=yds, w2=w_out[c_width:], resid=hs)

    wq = bf(peer_wq[1])
    y_sample, u_b, v_b = _peer(hs, norm2_g[1], wq, peer_k1[1], peer_k2[1], peer_u, peer_v, 1, final_g, True)
    y_prompt = _peer(hp, norm2_g[1], wq, peer_k1[1], peer_k2[1], u_b, v_b, 1, final_g, True)
    y_prompt = y_prompt.reshape(x_prompt.shape)
    y_sample = y_sample.reshape(x_sample.shape)

    d_shape = (1, -1, D_GROUPS, D_STATE)
    return (y_prompt, y_sample, a_v_sample, b_k_prompt, b_v_prompt, b_k_sample, b_v_sample,
            c_s_prompt.reshape(1, 1, C_HEADS, C_DK, C_DV), c_s_sample[None],
            d_prompt[0].reshape(d_shape), d_prompt[1].reshape(d_shape),
            s_re.transpose(1, 0, 2).reshape(d_shape), s_im.transpose(1, 0, 2).reshape(d_shape))
```

```python
import functools
import math

import jax
import jax.numpy as jnp
from jax import lax
from jax.experimental import pallas as pl
from jax.experimental.pallas import tpu as pltpu

F32 = jnp.float32
BF16 = jnp.bfloat16
HIGHEST = lax.Precision.HIGHEST

EPS = 1e-6
NEG_BIG = -1e30
NEG_INF = float("-inf")

D_MODEL = 2048
PAST_LEN = 8192
LANES = 128
SUBLANES = 8
VMEM_LIMIT = 56 * 1024 * 1024

CHUNK = 128
A_GROUPS = 8
A_WIDTH = 1024
B_HEADS = 16
B_KV_HEADS = 4
B_GQA = 4
B_DH = 64
C_HEADS = 8
C_DK = 64
C_DV = 128
RET_DECAY_EXP0 = 5.0
ROPE_BASE = 10000.0
D_GROUPS = 64
D_STATE = 64
D_GROUP_CH = 16
D_ROWS = 8
D_ROW_STATE = 512
D_ROW_CH = 128
PEER_HEADS = 8
PEER_NKEYS = 128
PEER_TOPK = 16
PEER_HALF = 128


def _params(semantics):
    return pltpu.CompilerParams(dimension_semantics=semantics, vmem_limit_bytes=VMEM_LIMIT)


def _gelu(x):
    return 0.5 * x * (1.0 + lax.erf(x * (1.0 / math.sqrt(2.0))))


def _rmsnorm(x, g):
    return x * lax.rsqrt(jnp.mean(x * x, axis=-1, keepdims=True) + EPS) * g


def _dot_nt(a, b, precision=None):
    return lax.dot_general(a, b, (((1,), (1,)), ((), ())), precision=precision,
                           preferred_element_type=F32)


def _dot_tn(a, b, precision=None):
    return lax.dot_general(a, b, (((0,), (0,)), ((), ())), precision=precision,
                           preferred_element_type=F32)


def _mm_body(*refs, has_norm, has_pair, has_resid):
    it = iter(refs)
    x_ref, w_ref = next(it), next(it)
    g_ref = next(it) if has_norm else None
    x2_ref, w2_ref = (next(it), next(it)) if has_pair else (None, None)
    r_ref = next(it) if has_resid else None
    o_ref = next(it)
    x = x_ref[...]
    if has_norm:
        x = _rmsnorm(x, g_ref[...])
    xb = x.astype(BF16)
    x2b = x2_ref[...].astype(BF16) if has_pair else None
    n = o_ref.shape[1]
    for j in range(n // MM_COLS):
        cols = slice(j * MM_COLS, (j + 1) * MM_COLS)
        acc = jnp.dot(xb, w_ref[:, cols], preferred_element_type=F32)
        if has_pair:
            acc = acc + jnp.dot(x2b, w2_ref[:, cols], preferred_element_type=F32)
        if has_resid:
            acc = acc + r_ref[:, cols]
        o_ref[:, cols] = acc


MM_COLS = 512


def _matmul(x, w, *, norm_g=None, x2=None, w2=None, resid=None, tm=512):
    m, k = x.shape
    n = w.shape[1]
    assert m % tm == 0 and n % MM_COLS == 0
    resident = pl.Buffered(1)
    args = [x, w]
    specs = [pl.BlockSpec((tm, k), lambda i: (i, 0)),
             pl.BlockSpec((k, n), lambda i: (0, 0), pipeline_mode=resident)]
    if norm_g is not None:
        args.append(norm_g.reshape(1, k))
        specs.append(pl.BlockSpec((1, k), lambda i: (0, 0)))
    if x2 is not None:
        k2 = x2.shape[1]
        args += [x2, w2]
        specs += [pl.BlockSpec((tm, k2), lambda i: (i, 0)),
                  pl.BlockSpec((k2, n), lambda i: (0, 0), pipeline_mode=resident)]
    if resid is not None:
        args.append(resid)
        specs.append(pl.BlockSpec((tm, n), lambda i: (i, 0)))
    body = functools.partial(_mm_body, has_norm=norm_g is not None, has_pair=x2 is not None,
                             has_resid=resid is not None)
    return pl.pallas_call(
        body, grid=(m // tm,), in_specs=specs,
        out_specs=pl.BlockSpec((tm, n), lambda i: (i, 0)),
        out_shape=jax.ShapeDtypeStruct((m, n), F32),
        compiler_params=_params(("parallel",)), name="matmul",
    )(*args)


def _sink_column(sink_ref, kvh, rows_per_head, n_rows):
    grp = lax.broadcasted_iota(jnp.int32, (n_rows, 1), 0) // rows_per_head
    sk = jnp.full((n_rows, 1), sink_ref[kvh * B_GQA + B_GQA - 1], F32)
    for g in range(B_GQA - 2, -1, -1):
        sk = jnp.where(grp == g, sink_ref[kvh * B_GQA + g], sk)
    return sk


def _even_prompt_body(sink_ref, au_ref, av_ref, q_ref, kvc_ref, kvp_ref, ws_ref, bs_ref, o_ref):
    blk = pl.program_id(0)
    au = _gelu(au_ref[...])
    av = _gelu(av_ref[...])
    row = lax.broadcasted_iota(jnp.int32, (CHUNK, CHUNK), 0)
    col = lax.broadcasted_iota(jnp.int32, (CHUNK, CHUNK), 1)
    causal = row >= col
    for g in range(A_GROUPS):
        lanes = slice(g * LANES, (g + 1) * LANES)
        w = jnp.where(causal, ws_ref[g], 0.0).astype(BF16)
        mixed = jnp.dot(w, av[:, lanes].astype(BF16), preferred_element_type=F32) + bs_ref[g]
        o_ref[:, lanes] = au[:, lanes] * mixed

    q = q_ref[...]
    kvc = kvc_ref[...]
    kvp = kvp_ref[...]
    n_rows = B_GQA * CHUNK
    qi = lax.broadcasted_iota(jnp.int32, (n_rows, 2 * CHUNK), 0) % CHUNK
    kc = lax.broadcasted_iota(jnp.int32, (n_rows, 2 * CHUNK), 1)
    dist = qi + CHUNK - kc
    allowed = (dist >= 0) & (dist < CHUNK) & ((kc >= CHUNK) | (blk > 0))
    outs = []
    for kvh in range(B_KV_HEADS):
        ks = slice(kvh * B_DH, (kvh + 1) * B_DH)
        vs = slice(B_KV_HEADS * B_DH + kvh * B_DH, B_KV_HEADS * B_DH + (kvh + 1) * B_DH)
        kk_t = jnp.concatenate([kvp[:, ks], kvc[:, ks]], axis=0).T.astype(BF16)
        vv = jnp.concatenate([kvp[:, vs], kvc[:, vs]], axis=0).astype(BF16)
        q4 = jnp.concatenate(
            [q[:, (kvh * B_GQA + g) * B_DH:(kvh * B_GQA + g + 1) * B_DH] for g in range(B_GQA)], axis=0)
        s = jnp.dot(q4.astype(BF16), kk_t, preferred_element_type=F32) * (B_DH ** -0.5)
        s = jnp.where(allowed, s, NEG_BIG)
        sk = _sink_column(sink_ref, kvh, CHUNK, n_rows)
        mx = jnp.maximum(jnp.max(s, axis=-1, keepdims=True), sk)
        p = jnp.exp(s - mx)
        p = p / (jnp.sum(p, axis=-1, keepdims=True) + jnp.exp(sk - mx))
        o = jnp.dot(p.astype(BF16), vv, preferred_element_type=F32)
        outs += [o[g * CHUNK:(g + 1) * CHUNK] for g in range(B_GQA)]
    o_ref[:, A_WIDTH:] = jnp.concatenate(outs, axis=1)


def _even_prompt(z, ws, bs_full, sink):
    t = z.shape[0]
    nb = t // CHUNK
    wide = A_WIDTH
    kvw = 2 * B_KV_HEADS * B_DH
    kv_blk = (2 * A_WIDTH + B_HEADS * B_DH) // kvw
    return pl.pallas_call(
        _even_prompt_body, grid=(nb,),
        in_specs=[
            pl.BlockSpec(memory_space=pltpu.SMEM),
            pl.BlockSpec((CHUNK, wide), lambda i: (i, 0)),
            pl.BlockSpec((CHUNK, wide), lambda i: (i, 1)),
            pl.BlockSpec((CHUNK, wide), lambda i: (i, 2)),
            pl.BlockSpec((CHUNK, kvw), lambda i: (i, kv_blk)),
            pl.BlockSpec((CHUNK, kvw), lambda i: (jnp.maximum(i - 1, 0), kv_blk)),
            pl.BlockSpec((A_GROUPS, CHUNK, CHUNK), lambda i: (0, 0, 0)),
            pl.BlockSpec((A_GROUPS, CHUNK, LANES), lambda i: (0, 0, 0)),
        ],
        out_specs=pl.BlockSpec((CHUNK, D_MODEL), lambda i: (i, 0)),
        out_shape=jax.ShapeDtypeStruct((t, D_MODEL), F32),
        compiler_params=_params(("parallel",)), name="even_prompt",
    )(sink, z, z, z, z, z, ws, bs_full)


EVEN_SAMPLE_BATCH = 8
T_PAD = 8


def _even_sample_body(sink_ref, au_ref, av_ref, q_ref, kv_ref, kb_ref, vb_ref, wa_ref, wb_ref,
                      y_ref, avo_ref):
    kv_width = B_KV_HEADS * B_DH
    n_rows = B_HEADS * T_PAD
    tq = lax.broadcasted_iota(jnp.int32, (n_rows, CHUNK), 0) % T_PAD
    kc = lax.broadcasted_iota(jnp.int32, (n_rows, CHUNK), 1)
    buf_allowed = kc > tq
    tq1 = lax.broadcasted_iota(jnp.int32, (n_rows, 1), 0) % T_PAD
    head_of_row = lax.broadcasted_iota(jnp.int32, (n_rows, 1), 0) // T_PAD
    sk = jnp.full((n_rows, 1), sink_ref[B_HEADS - 1], F32)
    for hd in range(B_HEADS - 2, -1, -1):
        sk = jnp.where(head_of_row == hd, sink_ref[hd], sk)
    lane_kvh = lax.broadcasted_iota(jnp.int32, (T_PAD, kv_width), 1) // B_DH
    scale = B_DH ** -0.5
    n_new = wa_ref.shape[0]
    for b in range(EVEN_SAMPLE_BATCH):
        au = _gelu(au_ref[b])
        av = _gelu(av_ref[b])
        avo_ref[b] = av
        mixed = wb_ref[...]
        for j in range(n_new):
            mixed = mixed + wa_ref[j] * av[j:j + 1, :]
        y_ref[b, :, :A_WIDTH] = au * mixed

        q = q_ref[b]
        kv = kv_ref[b]
        blocks = []
        for kvh in range(B_KV_HEADS):
            for g in range(B_GQA):
                col = (kvh * B_GQA + g) * B_DH
                qg = jnp.concatenate([q[:, col:col + B_DH]] * B_KV_HEADS, axis=1)
                blocks.append(jnp.where(lane_kvh == kvh, qg, 0.0))
        qbd = jnp.concatenate(blocks, axis=0).astype(BF16)
        s_buf = _dot_nt(qbd, kb_ref[b].astype(BF16)) * scale
        s_buf = jnp.where(buf_allowed, s_buf, NEG_BIG)
        qbd_r = qbd.astype(F32)
        s_new = []
        for j in range(n_new):
            kj = kv[j:j + 1, :kv_width].astype(BF16).astype(F32)
            sj = jnp.sum(qbd_r * kj, axis=-1, keepdims=True) * scale
            s_new.append(jnp.where(tq1 >= j, sj, NEG_BIG))
        mx = jnp.maximum(jnp.max(s_buf, axis=-1, keepdims=True), sk)
        for sj in s_new:
            mx = jnp.maximum(mx, sj)
        p_buf = jnp.exp(s_buf - mx)
        p_new = [jnp.exp(sj - mx) for sj in s_new]
        den = jnp.sum(p_buf, axis=-1, keepdims=True) + jnp.exp(sk - mx)
        for pj in p_new:
            den = den + pj
        inv = 1.0 / den
        o = jnp.dot((p_buf * inv).astype(BF16), vb_ref[b].astype(BF16), preferred_element_type=F32)
        for j in range(n_new):
            vj = kv[j:j + 1, kv_width:].astype(BF16).astype(F32)
            o = o + (p_new[j] * inv).astype(BF16).astype(F32) * vj
        outs = []
        for kvh in range(B_KV_HEADS):
            for g in range(B_GQA):
                row = (kvh * B_GQA + g) * T_PAD
                outs.append(o[row:row + T_PAD, kvh * B_DH:(kvh + 1) * B_DH])
        y_ref[b, :, A_WIDTH:] = jnp.concatenate(outs, axis=1)


def _even_sample(z3, kbuf, vbuf, wa, wb, sink):
    nb = z3.shape[0]
    bb = EVEN_SAMPLE_BATCH
    kvw = 2 * B_KV_HEADS * B_DH
    kv_blk = (2 * A_WIDTH + B_HEADS * B_DH) // kvw
    win = kbuf.shape[1]
    return pl.pallas_call(
        _even_sample_body, grid=(nb // bb,),
        in_specs=[
            pl.BlockSpec(memory_space=pltpu.SMEM),
            pl.BlockSpec((bb, T_PAD, A_WIDTH), lambda i: (i, 0, 0)),
            pl.BlockSpec((bb, T_PAD, A_WIDTH), lambda i: (i, 0, 1)),
            pl.BlockSpec((bb, T_PAD, A_WIDTH), lambda i: (i, 0, 2)),
            pl.BlockSpec((bb, T_PAD, kvw), lambda i: (i, 0, kv_blk)),
            pl.BlockSpec((bb, win, kvw // 2), lambda i: (i, 0, 0)),
            pl.BlockSpec((bb, win, kvw // 2), lambda i: (i, 0, 0)),
            pl.BlockSpec(wa.shape, lambda i: (0, 0, 0)),
            pl.BlockSpec(wb.shape, lambda i: (0, 0)),
        ],
        out_specs=[
            pl.BlockSpec((bb, T_PAD, D_MODEL), lambda i: (i, 0, 0)),
            pl.BlockSpec((bb, T_PAD, A_WIDTH), lambda i: (i, 0, 0)),
        ],
        out_shape=[
            jax.ShapeDtypeStruct((nb, T_PAD, D_MODEL), F32),
            jax.ShapeDtypeStruct((nb, T_PAD, A_WIDTH), F32),
        ],
        compiler_params=_params(("parallel",)), name="even_sample",
    )(sink, z3, z3, z3, z3, kbuf, vbuf, wa, wb)


def _rope(x, cos_f, sin_s):
    width = x.shape[-1]
    half = C_DK // 2
    lane = lax.broadcasted_iota(jnp.int32, x.shape, 1) % C_DK
    swapped = jnp.where(lane < half, pltpu.roll(x, width - half, 1), pltpu.roll(x, half, 1))
    return x * cos_f + swapped * sin_s


def _retention_chunk(qh, kh, vh, st, decay, qdec, kdec, sdec):
    qb, vb = qh.astype(BF16), vh.astype(BF16)
    scores = _dot_nt(qb, kh.astype(BF16)) * decay
    o = jnp.dot(scores.astype(BF16), vb, preferred_element_type=F32)
    o = o + jnp.dot(qb, st.astype(BF16), preferred_element_type=F32) * qdec
    new_st = sdec * st + _dot_tn((kh * kdec).astype(BF16), vb)
    return o, new_st


def _groupnorm_gate(o, gain, gate):
    mu = jnp.mean(o, axis=-1, keepdims=True)
    var = jnp.mean(jnp.square(o - mu), axis=-1, keepdims=True)
    return (o - mu) * lax.rsqrt(var + EPS) * gain * (gate * jax.nn.sigmoid(gate))


def _retention_body(qk_ref, v_ref, g_ref, cos_ref, sin_ref, decay_ref, qdec_ref, kdec_ref, sdec_ref,
                    gain_ref, o_ref, st_ref, state):
    @pl.when(pl.program_id(0) == 0)
    def _():
        state[...] = jnp.zeros_like(state)

    qk = qk_ref[...]
    width = C_HEADS * C_DK
    q = _rope(qk[:, :width], cos_ref[...], sin_ref[...]) * (C_DK ** -0.5)
    k = _rope(qk[:, width:], cos_ref[...], sin_ref[...])
    v = v_ref[...]
    g = g_ref[...]
    for h in range(C_HEADS):
        qh = q[:, h * C_DK:(h + 1) * C_DK]
        kh = k[:, h * C_DK:(h + 1) * C_DK]
        vh = v[:, h * C_DV:(h + 1) * C_DV]
        st = state[h]
        o, new_st = _retention_chunk(qh, kh, vh, st, decay_ref[h], qdec_ref[h], kdec_ref[h], sdec_ref[h])
        state[h] = new_st
        st_ref[h] = new_st
        lanes = slice(h * C_DV, (h + 1) * C_DV)
        o_ref[:, lanes] = _groupnorm_gate(o, gain_ref[:, lanes], g[:, lanes])


def _retention_prompt(z, cos_f, sin_s, tabs, gain):
    t = z.shape[0]
    nc = t // CHUNK
    decay, qdec, kdec, sdec = tabs
    width = C_HEADS * C_DV
    const3 = lambda i: (0, 0, 0)
    return pl.pallas_call(
        _retention_body, grid=(nc,),
        in_specs=[
            pl.BlockSpec((CHUNK, width), lambda i: (i, 0)),
            pl.BlockSpec((CHUNK, width), lambda i: (i, 1)),
            pl.BlockSpec((CHUNK, width), lambda i: (i, 2)),
            pl.BlockSpec((CHUNK, C_HEADS * C_DK), lambda i: (i, 0)),
            pl.BlockSpec((CHUNK, C_HEADS * C_DK), lambda i: (i, 0)),
            pl.BlockSpec(decay.shape, const3),
            pl.BlockSpec(qdec.shape, const3),
            pl.BlockSpec(kdec.shape, const3),
            pl.BlockSpec(sdec.shape, const3),
            pl.BlockSpec((1, width), lambda i: (0, 0)),
        ],
        out_specs=[
            pl.BlockSpec((CHUNK, width), lambda i: (i, 0)),
            pl.BlockSpec((C_HEADS, C_DK, C_DV), const3),
        ],
        out_shape=[
            jax.ShapeDtypeStruct((t, width), F32),
            jax.ShapeDtypeStruct((C_HEADS, C_DK, C_DV), F32),
        ],
        scratch_shapes=[pltpu.VMEM((C_HEADS, C_DK, C_DV), F32)],
        compiler_params=_params(("arbitrary",)), name="retention_prompt",
    )(z, z, z, cos_f, sin_s, decay, qdec, kdec, sdec, gain.reshape(1, width))


RET_SAMPLE_BATCH = 8


def _retention_sample_body(qk_ref, v_ref, g_ref, cos_ref, sin_ref, decay_ref, qdec_ref, kdec_ref,
                           sdec_ref, gain_ref, st_in_ref, o_ref, st_ref):
    width = C_HEADS * C_DK
    lane_head = lax.broadcasted_iota(jnp.int32, (T_PAD, width), 1) // C_DK
    for b in range(RET_SAMPLE_BATCH):
        qk = qk_ref[b]
        q = _rope(qk[:, :width], cos_ref[...], sin_ref[...]) * (C_DK ** -0.5)
        k = _rope(qk[:, width:], cos_ref[...], sin_ref[...])
        v = v_ref[b]
        g = g_ref[b]
        vb = v.astype(BF16)
        qbd = jnp.concatenate([jnp.where(lane_head == h, q, 0.0) for h in range(C_HEADS)],
                              axis=0).astype(BF16)
        st = st_in_ref[b]
        scores = _dot_nt(qbd, k.astype(BF16)) * decay_ref[...]
        o_intra = jnp.dot(scores.astype(BF16), vb, preferred_element_type=F32)
        o_cross = jnp.dot(qbd, st.astype(BF16), preferred_element_type=F32) * qdec_ref[...]
        kd = (k * kdec_ref[...]).astype(BF16)
        for h in range(C_HEADS):
            rows = slice(h * T_PAD, (h + 1) * T_PAD)
            lanes = slice(h * C_DV, (h + 1) * C_DV)
            keys = slice(h * C_DK, (h + 1) * C_DK)
            o = o_intra[rows, lanes] + o_cross[rows, :]
            o_ref[b, :, lanes] = _groupnorm_gate(o, gain_ref[:, lanes], g[:, lanes])
            st_ref[b, keys, :] = sdec_ref[keys, :] * st[keys, :] + _dot_tn(kd[:, keys], vb[:, lanes])


def _retention_sample(z3, cos_f, sin_s, tabs, gain, st_in):
    nb = z3.shape[0]
    bb = RET_SAMPLE_BATCH
    decay, qdec, kdec, sdec = tabs
    width = C_HEADS * C_DV
    decay_s = decay.reshape(C_HEADS * T_PAD, T_PAD)
    qdec_s = qdec.reshape(C_HEADS * T_PAD, C_DV)
    kdec_s = kdec.transpose(1, 0, 2).reshape(T_PAD, C_HEADS * C_DK)
    sdec_s = sdec.reshape(C_HEADS * C_DK, C_DV)
    st2 = st_in.reshape(nb, C_HEADS * C_DK, C_DV)
    c2 = lambda i: (0, 0)
    out, st_out = pl.pallas_call(
        _retention_sample_body, grid=(nb // bb,),
        in_specs=[
            pl.BlockSpec((bb, T_PAD, width), lambda i: (i, 0, 0)),
            pl.BlockSpec((bb, T_PAD, width), lambda i: (i, 0, 1)),
            pl.BlockSpec((bb, T_PAD, width), lambda i: (i, 0, 2)),
            pl.BlockSpec((T_PAD, C_HEADS * C_DK), c2),
            pl.BlockSpec((T_PAD, C_HEADS * C_DK), c2),
            pl.BlockSpec(decay_s.shape, c2),
            pl.BlockSpec(qdec_s.shape, c2),
            pl.BlockSpec(kdec_s.shape, c2),
            pl.BlockSpec(sdec_s.shape, c2),
            pl.BlockSpec((1, width), c2),
            pl.BlockSpec((bb, C_HEADS * C_DK, C_DV), lambda i: (i, 0, 0)),
        ],
        out_specs=[
            pl.BlockSpec((bb, T_PAD, width), lambda i: (i, 0, 0)),
            pl.BlockSpec((bb, C_HEADS * C_DK, C_DV), lambda i: (i, 0, 0)),
        ],
        out_shape=[
            jax.ShapeDtypeStruct((nb, T_PAD, width), F32),
            jax.ShapeDtypeStruct((nb, C_HEADS * C_DK, C_DV), F32),
        ],
        compiler_params=_params(("parallel",)), name="retention_sample",
    )(z3, z3, z3, cos_f, sin_s, decay_s, qdec_s, kdec_s, sdec_s, gain.reshape(1, width), st2)
    return out, st_out.reshape(st_in.shape)


def _retention_tables(length, n_valid):
    log_gamma = jnp.log1p(-jnp.exp2(-RET_DECAY_EXP0 - jnp.arange(C_HEADS, dtype=F32)))
    i = jnp.arange(length, dtype=F32)
    valid = (jnp.arange(length) < n_valid)
    rel = i[:, None] - i[None, :]
    decay = jnp.where(rel >= 0, jnp.exp(jnp.maximum(rel, 0.0)[None] * log_gamma[:, None, None]), 0.0)
    decay = jnp.where(valid[None, None, :], decay, 0.0)
    q_dec = jnp.exp((i[None, :] + 1.0) * log_gamma[:, None])
    k_dec = jnp.where(valid[None, :], jnp.exp((n_valid - 1.0 - i)[None, :] * log_gamma[:, None]), 0.0)
    s_dec = jnp.exp(n_valid * log_gamma)
    qdec = jnp.broadcast_to(q_dec[:, :, None], (C_HEADS, length, C_DV))
    kdec = jnp.broadcast_to(k_dec[:, :, None], (C_HEADS, length, C_DK))
    sdec = jnp.broadcast_to(s_dec[:, None, None], (C_HEADS, C_DK, C_DV))
    return decay, qdec, kdec, sdec


def _rope_tables(pos):
    half = C_DK // 2
    freqs = ROPE_BASE ** (-jnp.arange(half, dtype=F32) / half)
    ang = pos.astype(F32)[:, None] * freqs[None, :]
    cos, sin = jnp.cos(ang), jnp.sin(ang)
    cos_f = jnp.tile(jnp.concatenate([cos, cos], axis=1), (1, C_HEADS))
    sin_s = jnp.tile(jnp.concatenate([-sin, sin], axis=1), (1, C_HEADS))
    return cos_f, sin_s


def _s5_discretize(are_ref, aim_ref, ldt_ref):
    a_re, a_im = are_ref[...], aim_ref[...]
    dt = jnp.exp(ldt_ref[...])
    mag = jnp.exp(a_re * dt)
    ab_re = mag * jnp.cos(a_im * dt)
    ab_im = mag * jnp.sin(a_im * dt)
    num_re, num_im = ab_re - 1.0, ab_im
    den = a_re * a_re + a_im * a_im
    co_re = (num_re * a_re + num_im * a_im) / den
    co_im = (num_im * a_re - num_re * a_im) / den
    return ab_re, ab_im, co_re, co_im


S5_CHUNK = 256


def _s5_prompt_body(u_ref, are_ref, aim_ref, ldt_ref, bre_ref, bim_ref, cre_ref, cim_ref, dd_ref,
                    y_ref, st_ref, hre, him, state, disc):
    n = S5_CHUNK

    @pl.when(pl.program_id(0) == 0)
    def _():
        ab_re, ab_im, co_re, co_im = _s5_discretize(are_ref, aim_ref, ldt_ref)
        disc[0] = ab_re
        disc[1] = ab_im
        disc[2] = co_re
        disc[3] = co_im
        state[...] = jnp.zeros_like(state)

    n_chunks = D_ROW_STATE // LANES
    for s in range(D_ROWS):
        rows = pl.ds(s, n, stride=D_ROWS)
        u_s = u_ref[:, s * D_ROW_CH:(s + 1) * D_ROW_CH]
        u_b = u_s.astype(BF16)
        raw_re = jnp.dot(u_b, bre_ref[s], preferred_element_type=F32)
        raw_im = jnp.dot(u_b, bim_ref[s], preferred_element_type=F32)
        co_re = disc[2, s:s + 1, :]
        co_im = disc[3, s:s + 1, :]
        bu_re = co_re * raw_re - co_im * raw_im
        bu_im = co_re * raw_im + co_im * raw_re
        for c in range(n_chunks):
            hre[c, rows, :] = bu_re[:, c * LANES:(c + 1) * LANES]
            him[c, rows, :] = bu_im[:, c * LANES:(c + 1) * LANES]

    ab_re = disc[0]
    ab_im = disc[1]

    def step(t, carry):
        h_re, h_im = carry
        rows = pl.ds(pl.multiple_of(t * D_ROWS, D_ROWS), D_ROWS)
        bu_re = jnp.concatenate([hre[c, rows, :] for c in range(n_chunks)], axis=1)
        bu_im = jnp.concatenate([him[c, rows, :] for c in range(n_chunks)], axis=1)
        n_re = ab_re * h_re - ab_im * h_im + bu_re
        n_im = ab_re * h_im + ab_im * h_re + bu_im
        for c in range(n_chunks):
            hre[c, rows, :] = n_re[:, c * LANES:(c + 1) * LANES]
            him[c, rows, :] = n_im[:, c * LANES:(c + 1) * LANES]
        return n_re, n_im

    h_re, h_im = lax.fori_loop(0, n, step, (state[0], state[1]), unroll=4)
    state[0] = h_re
    state[1] = h_im
    st_ref[0] = h_re
    st_ref[1] = h_im

    for s in range(D_ROWS):
        rows = pl.ds(s, n, stride=D_ROWS)
        hist_re = jnp.concatenate([hre[c, rows, :] for c in range(n_chunks)], axis=1)
        hist_im = jnp.concatenate([him[c, rows, :] for c in range(n_chunks)], axis=1)
        cols = slice(s * D_ROW_CH, (s + 1) * D_ROW_CH)
        y = (jnp.dot(hist_re.astype(BF16), cre_ref[s], preferred_element_type=F32)
             - jnp.dot(hist_im.astype(BF16), cim_ref[s], preferred_element_type=F32)
             + dd_ref[s:s + 1, :] * u_ref[:, cols])
        y_ref[:, cols] = _gelu(y)


def _s5_prompt(z, u_col_block, par):
    a_re, a_im, ldt, b_re, b_im, c_re, c_im, dd = par
    t = z.shape[0]
    blk = S5_CHUNK * D_ROWS
    width = D_ROWS * D_ROW_CH
    c2 = lambda i: (0, 0)
    c3 = lambda i: (0, 0, 0)
    return pl.pallas_call(
        _s5_prompt_body, grid=(t // S5_CHUNK,),
        in_specs=[
            pl.BlockSpec((S5_CHUNK, width), lambda i: (i, u_col_block)),
            pl.BlockSpec(a_re.shape, c2), pl.BlockSpec(a_im.shape, c2), pl.BlockSpec(ldt.shape, c2),
            pl.BlockSpec(b_re.shape, c3), pl.BlockSpec(b_im.shape, c3),
            pl.BlockSpec(c_re.shape, c3), pl.BlockSpec(c_im.shape, c3),
            pl.BlockSpec(dd.shape, c2),
        ],
        out_specs=[
            pl.BlockSpec((S5_CHUNK, width), lambda i: (i, 0)),
            pl.BlockSpec((2, D_ROWS, D_ROW_STATE), c3),
        ],
        out_shape=[
            jax.ShapeDtypeStruct((t, width), F32),
            jax.ShapeDtypeStruct((2, D_ROWS, D_ROW_STATE), F32),
        ],
        scratch_shapes=[
            pltpu.VMEM((D_ROW_STATE // LANES, blk, LANES), F32),
            pltpu.VMEM((D_ROW_STATE // LANES, blk, LANES), F32),
            pltpu.VMEM((2, D_ROWS, D_ROW_STATE), F32), pltpu.VMEM((4, D_ROWS, D_ROW_STATE), F32),
        ],
        compiler_params=_params(("arbitrary",)), name="s5_prompt",
    )(z, a_re, a_im, ldt, b_re.astype(BF16), b_im.astype(BF16), c_re.astype(BF16), c_im.astype(BF16), dd)


def _s5_sample_body(u_ref, xre_ref, xim_ref, are_ref, aim_ref, ldt_ref, bre_ref, bim_ref, cre_ref,
                    cim_ref, dd_ref, y_ref, sre_ref, sim_ref):
    ab_re, ab_im, co_re, co_im = _s5_discretize(are_ref, aim_ref, ldt_ref)
    n_t = u_ref.shape[0]
    for s in range(D_ROWS):
        a_r, a_i = ab_re[s:s + 1, :], ab_im[s:s + 1, :]
        c_r, c_i = co_re[s:s + 1, :], co_im[s:s + 1, :]
        h_re, h_im = xre_ref[s], xim_ref[s]
        for t in range(n_t):
            u = u_ref[t, s]
            raw_re = jnp.dot(u, bre_ref[s], precision=HIGHEST, preferred_element_type=F32)
            raw_im = jnp.dot(u, bim_ref[s], precision=HIGHEST, preferred_element_type=F32)
            bu_re = c_r * raw_re - c_i * raw_im
            bu_im = c_r * raw_im + c_i * raw_re
            h_re, h_im = a_r * h_re - a_i * h_im + bu_re, a_r * h_im + a_i * h_re + bu_im
            y = (jnp.dot(h_re, cre_ref[s], precision=HIGHEST, preferred_element_type=F32)
                 - jnp.dot(h_im, cim_ref[s], precision=HIGHEST, preferred_element_type=F32)
                 + dd_ref[s:s + 1, :] * u)
            y_ref[t, s] = _gelu(y)
        sre_ref[s] = h_re
        sim_ref[s] = h_im


def _s5_sample(u_ts, x_re, x_im, par):
    a_re, a_im, ldt, b_re, b_im, c_re, c_im, dd = par
    return pl.pallas_call(
        _s5_sample_body,
        out_shape=[
            jax.ShapeDtypeStruct(u_ts.shape, F32),
            jax.ShapeDtypeStruct(x_re.shape, F32),
            jax.ShapeDtypeStruct(x_im.shape, F32),
        ],
        compiler_params=pltpu.CompilerParams(vmem_limit_bytes=VMEM_LIMIT), name="s5_sample",
    )(u_ts, x_re, x_im, a_re, a_im, ldt, b_re, b_im, c_re, c_im, dd)


def _s5_params(a_re, a_im, log_dt, b_re, b_im, c_re, c_im, dd):
    eye = jnp.eye(D_ROWS, dtype=F32)

    def rows(a):
        return a.reshape(D_ROWS, D_ROW_STATE)

    def b_blocks(b):
        b4 = b.reshape(D_ROWS, D_ROWS, D_STATE, D_GROUP_CH).transpose(0, 1, 3, 2)
        return jnp.einsum('sgcp,gh->sgchp', b4, eye).reshape(D_ROWS, D_ROW_CH, D_ROW_STATE)

    def c_blocks(c):
        c4 = c.reshape(D_ROWS, D_ROWS, D_GROUP_CH, D_STATE)
        return jnp.einsum('sgcp,gh->sgphc', c4, eye).reshape(D_ROWS, D_ROW_STATE, D_ROW_CH)

    ldt = jnp.broadcast_to(log_dt[:, None], (D_GROUPS, D_STATE))
    return (rows(a_re), rows(a_im), rows(ldt), b_blocks(b_re), b_blocks(b_im),
            c_blocks(c_re), c_blocks(c_im), dd.reshape(D_ROWS, D_ROW_CH))


def _glu_body(x_ref, w_ref, b_ref, o_ref):
    x = x_ref[...]
    gate = jnp.dot(x.astype(BF16), w_ref[...], preferred_element_type=F32) + b_ref[...]
    o_ref[...] = x * jax.nn.sigmoid(gate)


def _glu(x, w, b, tm=512):
    m, k = x.shape
    return pl.pallas_call(
        _glu_body, grid=(m // tm,),
        in_specs=[pl.BlockSpec((tm, k), lambda i: (i, 0)), pl.BlockSpec((k, k), lambda i: (0, 0)),
                  pl.BlockSpec((1, k), lambda i: (0, 0))],
        out_specs=pl.BlockSpec((tm, k), lambda i: (i, 0)),
        out_shape=jax.ShapeDtypeStruct((m, k), F32),
        compiler_params=_params(("parallel",)), name="glu",
    )(x, w, b.reshape(1, k))


PEER_SEL_TOKENS = 256


def _extract_top(s, n_out, exact, want_rank=False):
    n_rows = s.shape[0]
    rid = lax.broadcasted_iota(jnp.int32, s.shape, 0).astype(F32) if exact else None
    rank = jnp.full(s.shape, float(n_out), F32) if want_rank else None
    vals = []
    for r in range(n_out):
        m = jnp.max(s, axis=0, keepdims=True)
        if exact:
            first = jnp.min(jnp.where(s == m, rid, float(n_rows)), axis=0, keepdims=True)
            hit = rid == first
        else:
            hit = s == m
        if want_rank:
            rank = jnp.where(hit, float(r), rank)
        s = jnp.where(hit, NEG_INF, s)
        vals.append(m)
    return vals, s, rank


def _removed_count(s):
    return jnp.sum(jnp.where(s == NEG_INF, 1.0, 0.0), axis=0, keepdims=True)


_CAND_COUNTS = [PEER_TOPK // (a + 1) for a in range(PEER_TOPK)]


def _candidates(v1, v2):
    v1_all = _stack_rows(v1)
    v2_all = _stack_rows(v2)
    rid = lax.broadcasted_iota(jnp.int32, (SUBLANES, v1[0].shape[1]), 0)
    tiles = [v1[0] + v2_all]
    n_pad = 0
    a = 1
    while _CAND_COUNTS[a] > 1:
        tiles.append(jnp.where(rid < _CAND_COUNTS[a], v1[a] + v2_all[:SUBLANES], NEG_INF))
        n_pad += SUBLANES - _CAND_COUNTS[a]
        a += 1
    assert PEER_TOPK - a == SUBLANES
    tiles.append(v1_all[a:] + v2[0])
    return jnp.concatenate(tiles, axis=0), n_pad


def _stack_rows(rows):
    n = len(rows)
    rid = lax.broadcasted_iota(jnp.int32, (n, rows[0].shape[1]), 0)
    out = jnp.broadcast_to(rows[n - 1], rid.shape)
    for i in range(n - 2, -1, -1):
        out = jnp.where(rid == i, rows[i], out)
    return out


def _selected_per_first_key(removed, tag, tags):
    counts = [jnp.sum(removed[:PEER_TOPK], axis=0, keepdims=True)]
    row = PEER_TOPK
    a = 1
    while _CAND_COUNTS[a] > 1:
        counts.append(jnp.sum(removed[row:row + SUBLANES], axis=0, keepdims=True))
        row += SUBLANES
        a += 1
    for i in range(PEER_TOPK - a):
        counts.append(removed[row + i:row + i + 1])
    n_sel = jnp.zeros(tag.shape, F32)
    for a, cnt in enumerate(counts):
        n_sel = jnp.where(tag == tags[a], cnt, n_sel)
    return n_sel


def _peer_select_body(q_ref, k1_ref, k2_ref, ns_ref, g1_ref, r2_ref, e2_ref):
    def head(h, exact):
        q1 = q_ref[:, (2 * h) * PEER_HALF:(2 * h + 1) * PEER_HALF]
        q2 = q_ref[:, (2 * h + 1) * PEER_HALF:(2 * h + 2) * PEER_HALF]
        s1 = _dot_nt(k1_ref[h], q1, HIGHEST)
        s2 = _dot_nt(k2_ref[h], q2, HIGHEST)
        v1, left1, rank1 = _extract_top(s1, PEER_TOPK, exact, want_rank=exact)
        v2, left2, rank2 = _extract_top(s2, PEER_TOPK, exact, want_rank=True)
        tag1, tags1 = (rank1, [float(a) for a in range(PEER_TOPK)]) if exact else (s1, v1)
        cand, n_pad = _candidates(v1, v2)
        top, left_c, _ = _extract_top(cand, PEER_TOPK, exact)
        mx = top[0]
        z = jnp.exp(top[0] - mx)
        for kk in range(1, PEER_TOPK):
            z = z + jnp.exp(top[kk] - mx)
        g1 = jnp.where(left1 == NEG_INF, jnp.exp(s1 - v1[0]), 0.0) / z
        e2 = jnp.where(rank2 < PEER_TOPK, jnp.exp(s2 - v2[0]), 0.0)
        removed = jnp.where((left_c == NEG_INF) & (cand != NEG_INF), 1.0, 0.0)
        n_sel = _selected_per_first_key(removed, tag1, tags1)
        for j in range(PEER_SEL_TOKENS // LANES):
            lanes = slice(j * LANES, (j + 1) * LANES)
            ns_ref[j, h] = n_sel[:, lanes]
            g1_ref[j, h] = g1[:, lanes]
            r2_ref[j, h] = rank2[:, lanes]
            e2_ref[j, h] = e2[:, lanes]
        if exact:
            return None
        ok = ((_removed_count(left1) == PEER_TOPK) & (_removed_count(left2) == PEER_TOPK)
              & (_removed_count(left_c) == PEER_TOPK + n_pad))
        return jnp.where(ok, 0.0, 1.0)

    repeated = [jnp.max(head(h, False)) for h in range(PEER_HEADS)]
    for h in range(PEER_HEADS):
        @pl.when(repeated[h] > 0.0)
        def _():
            head(h, True)


def _peer_select(q, k1, k2):
    t = q.shape[0]
    tb = PEER_SEL_TOKENS
    nj = tb // LANES
    big = pl.BlockSpec((nj, PEER_HEADS, PEER_NKEYS, LANES), lambda i: (i, 0, 0, 0))
    shape = jax.ShapeDtypeStruct((t // LANES, PEER_HEADS, PEER_NKEYS, LANES), F32)
    return pl.pallas_call(
        _peer_select_body, grid=(t // tb,),
        in_specs=[
            pl.BlockSpec((tb, q.shape[1]), lambda i: (i, 0)),
            pl.BlockSpec(k1.shape, lambda i: (0, 0, 0)),
            pl.BlockSpec(k2.shape, lambda i: (0, 0, 0)),
        ],
        out_specs=[big, big, big, big],
        out_shape=[shape, shape, shape, shape],
        compiler_params=_params(("parallel",)), name="peer_select",
    )(q, k1, k2)


PEER_TOKENS = 512
PEER_EXPERT_TILE = 512
BF16_ROWS = 16
GATE_GROUP = 4


def _peer_dense_body(h_ref, g_ref, u_ref, v_ref, ns_ref, g1_ref, r2_ref, e2_ref, og_ref, o_ref, *rest,
                     norm_out, emit_bf16):
    if emit_bf16:
        ub_ref, vb_ref, xn, act_s, coef_s, acc = rest
    else:
        xn, act_s, coef_s, acc = rest
    e = pl.program_id(1)
    n_chunks = PEER_TOKENS // LANES

    @pl.when(e == 0)
    def _():
        xn[...] = _rmsnorm(h_ref[...], g_ref[...]).T.astype(BF16)
        acc[...] = jnp.zeros_like(acc)

    te = PEER_EXPERT_TILE
    slabs = te // PEER_NKEYS

    def one_tile(part):
        tile_rows = slice(part * te, (part + 1) * te)
        u_tile = u_ref[tile_rows, :].astype(BF16)
        if emit_bf16:
            ub_ref[tile_rows, :] = u_tile
        act = jnp.dot(u_tile, xn[...], preferred_element_type=F32)
        for j in range(n_chunks):
            act_s[j] = act[:, j * LANES:(j + 1) * LANES]

        def chunk(j, carry):
            packed = (PEER_NKEYS // BF16_ROWS, BF16_ROWS, LANES)
            for c0 in range(0, slabs, GATE_GROUP):
                gates = [jnp.zeros(packed, BF16) for _ in range(GATE_GROUP)]
                for h in range(PEER_HEADS):
                    r2 = r2_ref[j, h].reshape(packed).astype(BF16)
                    e2 = e2_ref[j, h].reshape(packed).astype(BF16)
                    for k in range(GATE_GROUP):
                        key1 = part * slabs + c0 + k
                        ns = jnp.broadcast_to(ns_ref[j, h, key1:key1 + 1, :], packed[1:]).astype(BF16)
                        g1 = jnp.broadcast_to(g1_ref[j, h, key1:key1 + 1, :], packed[1:]).astype(BF16)
                        gates[k] = gates[k] + jnp.where(r2 < ns[None], e2 * g1[None], jnp.zeros((), BF16))
                for k in range(GATE_GROUP):
                    rows = slice((c0 + k) * PEER_NKEYS, (c0 + k + 1) * PEER_NKEYS)
                    act = _gelu(act_s[j, rows, :]).reshape(packed).astype(BF16)
                    coef_s[j, rows, :] = (gates[k] * act).reshape(PEER_NKEYS, LANES)
            return carry

        for j in range(n_chunks):
            chunk(j, 0)
        coef = jnp.concatenate([coef_s[j] for j in range(n_chunks)], axis=1)
        v_tile = v_ref[tile_rows, :].astype(BF16)
        if emit_bf16:
            vb_ref[tile_rows, :] = v_tile
        acc[...] += _dot_tn(coef, v_tile)

    for part in range(u_ref.shape[0] // te):
        one_tile(part)

    @pl.when(e == pl.num_programs(1) - 1)
    def _():
        out = h_ref[...] + acc[...]
        o_ref[...] = _rmsnorm(out, og_ref[...]) if norm_out else out


def _peer_dense(h, g, u_tab, v_tab, layer, sel, out_g, norm_out):
    ns, g1, r2, e2 = sel
    t, d = h.shape
    emit_bf16 = u_tab.ndim == 3
    n_exp = u_tab.shape[-2]
    tb = PEER_TOKENS
    nj = tb // LANES
    slabs = PEER_EXPERT_TILE // PEER_NKEYS
    o_spec = pl.BlockSpec((tb, d), lambda i, e: (i, 0))
    o_shape = jax.ShapeDtypeStruct((t, d), F32)
    if emit_bf16:
        assert t == tb, "every table tile must be visited exactly once when the casts are emitted"
        te = PEER_EXPERT_TILE
        n_steps = n_exp // te
        tab_spec = pl.BlockSpec((None, te, d), lambda i, e: (layer, e, 0))

        def by_tile(a):
            a5 = a.reshape(a.shape[0], PEER_HEADS, n_steps, slabs, LANES)
            return a5.transpose(2, 0, 1, 3, 4)

        ns, g1 = by_tile(ns), by_tile(g1)
        key_spec = pl.BlockSpec((None, nj, PEER_HEADS, slabs, LANES), lambda i, e: (e, i, 0, 0, 0))
    else:
        te = SUBLANES // slabs * PEER_EXPERT_TILE
        n_steps = n_exp // te
        tab_spec = pl.BlockSpec((te, d), lambda i, e: (e, 0))
        key_spec = pl.BlockSpec((nj, PEER_HEADS, SUBLANES, LANES), lambda i, e: (i, 0, e, 0))
    cast_spec = pl.BlockSpec((te, d), lambda i, e: (e, 0))
    cast_shape = jax.ShapeDtypeStruct((n_exp, d), BF16)
    tok_spec = pl.BlockSpec((nj, PEER_HEADS, PEER_NKEYS, LANES), lambda i, e: (i, 0, 0, 0))
    te_scratch = PEER_EXPERT_TILE
    return pl.pallas_call(
        functools.partial(_peer_dense_body, norm_out=norm_out, emit_bf16=emit_bf16),
        grid=(t // tb, n_steps),
        in_specs=[
            pl.BlockSpec((tb, d), lambda i, e: (i, 0)),
            pl.BlockSpec((1, d), lambda i, e: (0, 0)),
            tab_spec, tab_spec,
            key_spec, key_spec, tok_spec, tok_spec,
            pl.BlockSpec((1, d), lambda i, e: (0, 0)),
        ],
        out_specs=[o_spec, cast_spec, cast_spec] if emit_bf16 else o_spec,
        out_shape=[o_shape, cast_shape, cast_shape] if emit_bf16 else o_shape,
        scratch_shapes=[pltpu.VMEM((d, tb), BF16), pltpu.VMEM((nj, te_scratch, LANES), F32),
                        pltpu.VMEM((nj, te_scratch, LANES), BF16), pltpu.VMEM((tb, d), F32)],
        compiler_params=_params(("parallel", "arbitrary")), name="peer_dense",
    )(h, g.reshape(1, d), u_tab, v_tab, ns, g1, r2, e2, out_g.reshape(1, d))


def _peer(h, g, wq, k1, k2, u_tab, v_tab, layer, out_g, norm_out):
    q = _matmul(h, wq, norm_g=g)
    return _peer_dense(h, g, u_tab, v_tab, layer, _peer_select(q, k1, k2), out_g, norm_out)


def kernel(x_prompt, x_sample, state_b_k, state_b_v, state_c_s, state_d_re, state_d_im, norm1_g, norm2_g, final_g, w_in_even, w_out_even, a_ws, a_bs, b_sink, w_in_odd, w_out_odd, c_norm_g, d_a_re, d_a_im, d_log_dt, d_b_re, d_b_im, d_c_re, d_c_im, d_d, d_glu_w, d_glu_b, peer_wq, peer_k1, peer_k2, peer_u, peer_v):
    seq = x_prompt.shape[1]
    n_batch, n_new = x_sample.shape[:2]
    past = PAST_LEN
    hp = x_prompt.reshape(seq, D_MODEL)
    hs = x_sample.reshape(n_batch * n_new, D_MODEL)
    bf = lambda a: a.astype(BF16)

    def pad_tokens(a):
        a3 = a.reshape(n_batch, n_new, a.shape[-1])
        return jnp.pad(a3, ((0, 0), (0, T_PAD - n_new), (0, 0)))

    def unpad_tokens(a3):
        return a3[:, :n_new].reshape(n_batch * n_new, a3.shape[-1])

    w_in = bf(w_in_even[0])
    w_out = bf(w_out_even[0])
    zp = _matmul(hp, w_in, norm_g=norm1_g[0])
    zs = _matmul(hs, w_in, norm_g=norm1_g[0])
    bs_full = jnp.broadcast_to(a_bs[0][:, :, None], (A_GROUPS, CHUNK, LANES))
    yp = _even_prompt(zp, a_ws[0], bs_full, b_sink[0])
    ws_small = jnp.tril(a_ws[0][:, :n_new, :n_new]).transpose(2, 1, 0)
    wa = jnp.repeat(jnp.pad(ws_small, ((0, 0), (0, T_PAD - n_new), (0, 0))), LANES, axis=-1)
    wb = jnp.repeat(jnp.pad(a_bs[0][:, :n_new].T, ((0, T_PAD - n_new), (0, 0))), LANES, axis=-1)
    win = state_b_k.shape[2]
    kbuf = state_b_k[0].reshape(n_batch, win, B_KV_HEADS * B_DH)
    vbuf = state_b_v[0].reshape(n_batch, win, B_KV_HEADS * B_DH)
    ys3, av3 = _even_sample(pad_tokens(zs), kbuf, vbuf, wa, wb, b_sink[0])
    hp = _matmul(yp, w_out, resid=hp)
    hs = _matmul(unpad_tokens(ys3), w_out, resid=hs)

    k_off = 2 * A_WIDTH + B_HEADS * B_DH
    v_off = k_off + B_KV_HEADS * B_DH
    kv_shape = (1, -1, win, B_KV_HEADS, B_DH)
    a_v_sample = av3[:, :n_new].reshape(1, n_batch, n_new, A_GROUPS, A_WIDTH // A_GROUPS)
    b_k_prompt = zp[seq - win:, k_off:v_off].reshape(kv_shape)
    b_v_prompt = zp[seq - win:, v_off:].reshape(kv_shape)
    k_new = zs[:, k_off:v_off].reshape(n_batch, n_new, B_KV_HEADS * B_DH)
    v_new = zs[:, v_off:].reshape(n_batch, n_new, B_KV_HEADS * B_DH)
    b_k_sample = jnp.concatenate([kbuf, k_new], axis=1)[:, -win:].reshape(kv_shape)
    b_v_sample = jnp.concatenate([vbuf, v_new], axis=1)[:, -win:].reshape(kv_shape)

    wq = bf(peer_wq[0])
    hs, u_b, v_b = _peer(hs, norm2_g[0], wq, peer_k1[0], peer_k2[0], peer_u, peer_v, 0, final_g, False)
    hp = _peer(hp, norm2_g[0], wq, peer_k1[0], peer_k2[0], u_b, v_b, 0, final_g, False)

    w_in = bf(w_in_odd[0])
    w_out = bf(w_out_odd[0])
    c_width = C_HEADS * C_DV
    zp = _matmul(hp, w_in, norm_g=norm1_g[1])
    zs = _matmul(hs, w_in, norm_g=norm1_g[1])
    cos_p, sin_p = _rope_tables(jnp.arange(seq))
    cos_s, sin_s = _rope_tables(past + jnp.arange(T_PAD))
    ycp, c_s_prompt = _retention_prompt(zp, cos_p, sin_p, _retention_tables(CHUNK, CHUNK), c_norm_g[0])
    zs3 = pad_tokens(zs)
    ycs3, c_s_sample = _retention_sample(zs3, cos_s, sin_s, _retention_tables(T_PAD, n_new), c_norm_g[0],
                                         state_c_s[0])
    par = _s5_params(d_a_re[0], d_a_im[0], d_log_dt[0], d_b_re[0], d_b_im[0], d_c_re[0], d_c_im[0], d_d[0])
    u_off = 3 * c_width
    ydp_pre, d_prompt = _s5_prompt(zp, u_off // c_width, par)
    us_ts = zs[:, u_off:].reshape(n_batch, n_new, D_ROWS, D_ROW_CH).transpose(1, 2, 0, 3)
    x_re = state_d_re[0].reshape(n_batch, D_ROWS, D_ROW_STATE).transpose(1, 0, 2)
    x_im = state_d_im[0].reshape(n_batch, D_ROWS, D_ROW_STATE).transpose(1, 0, 2)
    yds_ts, s_re, s_im = _s5_sample(us_ts, x_re, x_im, par)
    glu_w = bf(d_glu_w[0])
    ydp = _glu(ydp_pre, glu_w, d_glu_b[0])
    yds = _glu(yds_ts.transpose(2, 0, 1, 3).reshape(n_batch * n_new, c_width), glu_w, d_glu_b[0])
    hp = _matmul(ycp, w_out[:c_width], x2=ydp, w2=w_out[c_width:], resid=hp)
    hs = _matmul(unpad_tokens(ycs3), w_out[:c_width], x2=yds, w2=w_out[c_width:], resid=hs)

    wq = bf(peer_wq[1])
    y_sample, u_b, v_b = _peer(hs, norm2_g[1], wq, peer_k1[1], peer_k2[1], peer_u, peer_v, 1, final_g, True)
    y_prompt = _peer(hp, norm2_g[1], wq, peer_k1[1], peer_k2[1], u_b, v_b, 1, final_g, True)
    y_prompt = y_prompt.reshape(x_prompt.shape)
    y_sample = y_sample.reshape(x_sample.shape)

    d_shape = (1, -1, D_GROUPS, D_STATE)
    return (y_prompt, y_sample, a_v_sample, b_k_prompt, b_v_prompt, b_k_sample, b_v_sample,
            c_s_prompt.reshape(1, 1, C_HEADS, C_DK, C_DV), c_s_sample[None],
            d_prompt[0].reshape(d_shape), d_prompt[1].reshape(d_shape),
            s_re.transpose(1, 0, 2).reshape(d_shape), s_im.transpose(1, 0, 2).reshape(d_shape))
```
